```python
import math
import jax, jax.numpy as jnp
from jax import lax
import numpy as np

D_MODEL = 1024
BATCH = 16
SEQ = 2048
DEPTH = 4

HEAD_DIM = 64
N_HEADS = D_MODEL // HEAD_DIM
D_FF = 2816
NORM_EPS = 1e-6
N_MIXERS = 3

REL_BUCKETS = 32
REL_MAX_DIST = 1024

QBLK = 128

A_KV_HEADS = 2
A_WINDOW = 128

B_KV_GROUPS = 4
CMP_LEN = 32
CMP_STRIDE = 16
CMP_HIDDEN = 256
SLC_BLK = 64
SLC_TOPK = 16
B_WINDOW = 512
B_QCHUNK = 128

MOBA_BLK = 256
MOBA_TOPK = 3
MOBA_QCHUNK = 64

NEG = -1e30
TINY = 1e-30

A_IN = N_HEADS * HEAD_DIM + 2 * A_KV_HEADS * HEAD_DIM
B_IN = N_HEADS * HEAD_DIM + 6 * B_KV_GROUPS * HEAD_DIM + 3 * N_HEADS
C_IN = 3 * N_HEADS * HEAD_DIM

N_A_LAYERS = (DEPTH + 2) // 3
N_B_LAYERS = (DEPTH + 1) // 3
N_C_LAYERS = DEPTH // 3

kernel_name = "hybrid_swa_nsa_moba_macaron"


def rms_norm(x, g):
    xf = x.astype(jnp.float32)
    y = xf * lax.rsqrt(jnp.mean(xf * xf, axis=-1, keepdims=True) + NORM_EPS)
    return (y * g.astype(jnp.float32)).astype(x.dtype)


def swiglu(h, w_in, w_out):
    gate, up = jnp.split(h @ w_in, 2, axis=-1)
    return (jax.nn.silu(gate) * up) @ w_out


def rel_bucket(dist):
    dist = jnp.maximum(dist, 0)
    exact = REL_BUCKETS // 2
    d = jnp.maximum(dist, 1).astype(jnp.float32)
    log_b = exact + (jnp.log(d / exact) / math.log(REL_MAX_DIST / exact)
                     * (REL_BUCKETS - exact)).astype(jnp.int32)
    return jnp.where(dist < exact, dist, jnp.minimum(log_b, REL_BUCKETS - 1))


def masked_softmax(s, ok, sink=None):
    s = jnp.where(ok, s, NEG)
    m = jnp.max(s, axis=-1, keepdims=True)
    if sink is not None:
        m = jnp.maximum(m, sink)
    e = jnp.where(ok, jnp.exp(s - m), 0.0)
    denom = jnp.sum(e, axis=-1, keepdims=True)
    if sink is not None:
        denom = denom + jnp.exp(sink - m)
    return e / jnp.maximum(denom, TINY)


def banded_attention(q, k, v, rel_table, window, sinks=None):
    B, S, H, dh = q.shape
    G = k.shape[2]
    R = H // G
    n_prev = -(-(window - 1) // QBLK)
    pad = n_prev * QBLK
    slab = pad + QBLK
    kp = jnp.pad(k, ((0, 0), (pad, 0), (0, 0), (0, 0)))
    vp = jnp.pad(v, ((0, 0), (pad, 0), (0, 0), (0, 0)))
    n_blk = S // QBLK
    qb = q.reshape(B, n_blk, QBLK, G, R, dh).transpose(1, 0, 2, 3, 4, 5)
    qi = jnp.arange(QBLK)[:, None]
    kj = jnp.arange(slab)[None, :]
    dist = pad + qi - kj
    in_band = (dist >= 0) & (dist < window)
    bias = rel_table.astype(jnp.float32)[rel_bucket(dist)]
    bias = bias.reshape(QBLK, slab, G, R).transpose(2, 3, 0, 1)
    sink = None if sinks is None else sinks.astype(jnp.float32).reshape(G, R, 1, 1)
    scale = dh ** -0.5

    def block(args):
        n, q_n = args
        start = n * QBLK
        k_n = lax.dynamic_slice_in_dim(kp, start, slab, axis=1)
        v_n = lax.dynamic_slice_in_dim(vp, start, slab, axis=1)
        s = jnp.einsum("bqgrd,bkgd->bgrqk", q_n, k_n).astype(jnp.float32) * scale + bias
        ok = in_band & (start - pad + kj >= 0)
        p = masked_softmax(s, ok, sink)
        o = jnp.einsum("bgrqk,bkgd->bqgrd", p.astype(v.dtype), v_n)
        return o.reshape(B, QBLK, H, dh)

    out = lax.map(block, (jnp.arange(n_blk), qb))
    return out.transpose(1, 0, 2, 3, 4).reshape(B, S, H, dh)


def sliding_sink_attention(h, w_in, w_out, sinks, rel_table):
    B, S, _ = h.shape
    qw, kw = N_HEADS * HEAD_DIM, A_KV_HEADS * HEAD_DIM
    q, k, v = jnp.split(h @ w_in, [qw, qw + kw], axis=-1)
    q = q.reshape(B, S, N_HEADS, HEAD_DIM)
    k = k.reshape(B, S, A_KV_HEADS, HEAD_DIM)
    v = v.reshape(B, S, A_KV_HEADS, HEAD_DIM)
    o = banded_attention(q, k, v, rel_table, A_WINDOW, sinks)
    return o.reshape(B, S, -1) @ w_out


def compress_kv(kv, pos, w1, w2):
    B, S, G, dh = kv.shape
    n_cmp = (S - CMP_LEN) // CMP_STRIDE + 1
    idx = jnp.arange(n_cmp)[:, None] * CMP_STRIDE + jnp.arange(CMP_LEN)[None, :]
    blocks = kv[:, idx] + pos[:, None, :]
    flat = blocks.transpose(0, 1, 3, 2, 4).reshape(B, n_cmp, G, CMP_LEN * dh)
    return jax.nn.gelu(flat @ w1) @ w2


def selected_block_attention(q, k, v, sel_idx, rel_table):
    B, S, G, R, dh = q.shape
    n_sel = sel_idx.shape[-1]
    n_slc = S // SLC_BLK
    kb = k.reshape(B, n_slc, SLC_BLK, G, dh).transpose(0, 3, 1, 2, 4)
    vb = v.reshape(B, n_slc, SLC_BLK, G, dh).transpose(0, 3, 1, 2, 4)
    table_g = rel_table.astype(jnp.float32).reshape(REL_BUCKETS, G, R).transpose(1, 0, 2)
    g_ix = jnp.arange(G)[:, None, None]
    n_q = S // B_QCHUNK
    n_keys = n_sel * SLC_BLK
    scale = dh ** -0.5

    def chunk(i):
        b, c = i // n_q, i % n_q
        start = c * B_QCHUNK
        q_c = lax.dynamic_slice_in_dim(q[b], start, B_QCHUNK, axis=0)
        idx = lax.dynamic_slice_in_dim(sel_idx[b], start, B_QCHUNK, axis=1)
        k_sel = kb[b][g_ix, idx].reshape(G, B_QCHUNK, n_keys, dh)
        v_sel = vb[b][g_ix, idx].reshape(G, B_QCHUNK, n_keys, dh)
        pos = (idx[..., None] * SLC_BLK + jnp.arange(SLC_BLK)).reshape(G, B_QCHUNK, n_keys)
        t = start + jnp.arange(B_QCHUNK)
        dist = t[None, :, None] - pos
        bias = table_g[g_ix, rel_bucket(dist)].transpose(0, 3, 1, 2)
        s = jnp.einsum("qgrd,gqkd->grqk", q_c, k_sel).astype(jnp.float32) * scale + bias
        p = masked_softmax(s, (dist >= 0)[:, None])
        o = jnp.einsum("grqk,gqkd->qgrd", p.astype(v.dtype), v_sel)
        return o.reshape(B_QCHUNK, G * R, dh)

    out = lax.map(chunk, jnp.arange(B * n_q))
    return out.reshape(B, S, G * R, dh)


def nsa_attention(h, w_in, w_out, cmp_pos, cmp_w1, cmp_w2, rel_table):
    B, S, _ = h.shape
    G, R, dh = B_KV_GROUPS, N_HEADS // B_KV_GROUPS, HEAD_DIM
    kvw = G * dh
    splits = np.cumsum([N_HEADS * dh] + [kvw] * 6).tolist()
    q, kc, vc, ks, vs, kw, vw, gate = jnp.split(h @ w_in, splits, axis=-1)
    q = q.reshape(B, S, G, R, dh)
    to_kv = lambda a: a.reshape(B, S, G, dh)
    scale = dh ** -0.5
    t = jnp.arange(S)

    kc_b = compress_kv(to_kv(kc), cmp_pos[0], cmp_w1[0], cmp_w2[0])
    vc_b = compress_kv(to_kv(vc), cmp_pos[1], cmp_w1[1], cmp_w2[1])
    n_cmp = kc_b.shape[1]
    cmp_end = jnp.arange(n_cmp) * CMP_STRIDE + CMP_LEN - 1
    s_cmp = jnp.einsum("bsgrd,bngd->bgrsn", q, kc_b).astype(jnp.float32) * scale
    p_cmp = masked_softmax(s_cmp, cmp_end[None, :] <= t[:, None])
    o_cmp = jnp.einsum("bgrsn,bngd->bsgrd", p_cmp.astype(vc_b.dtype), vc_b)

    n_slc = S // SLC_BLK
    n_sel = min(SLC_TOPK, n_slc)
    c_start = np.arange(n_cmp)[:, None] * CMP_STRIDE
    s_start = np.arange(n_slc)[None, :] * SLC_BLK
    overlap = ((c_start < s_start + SLC_BLK) & (c_start + CMP_LEN > s_start)).astype(np.float32)
    imp = jnp.einsum("bgrsn,nj->bgsj", p_cmp, jnp.asarray(overlap))
    cur = (t // SLC_BLK)[:, None]
    j = jnp.arange(n_slc)[None, :]
    forced = (j == 0) | (j == cur) | (j == cur - 1)
    imp = jnp.where(j > cur, NEG, jnp.where(forced, -NEG, imp))
    _, sel_idx = lax.top_k(imp, n_sel)
    o_slc = selected_block_attention(q, to_kv(ks), to_kv(vs), sel_idx, rel_table)

    o_win = banded_attention(q.reshape(B, S, N_HEADS, dh), to_kv(kw), to_kv(vw),
                             rel_table, B_WINDOW)

    g = jax.nn.sigmoid(gate.astype(jnp.float32)).reshape(B, S, N_HEADS, 3).astype(o_win.dtype)
    o = (g[..., 0:1] * o_cmp.reshape(B, S, N_HEADS, dh)
         + g[..., 1:2] * o_slc + g[..., 2:3] * o_win)
    return o.reshape(B, S, -1) @ w_out


def moba_attention(h, w_in, w_out, rel_table):
    B, S, _ = h.shape
    H, dh = N_HEADS, HEAD_DIM
    q, k, v = jnp.split(h @ w_in, 3, axis=-1)
    q = q.reshape(B, S, H, dh)
    k = k.reshape(B, S, H, dh)
    v = v.reshape(B, S, H, dh)
    n_blk = -(-S // MOBA_BLK)
    s_pad = n_blk * MOBA_BLK
    kb = jnp.pad(k, ((0, 0), (0, s_pad - S), (0, 0), (0, 0))).reshape(B, n_blk, MOBA_BLK, H, dh)
    vb = jnp.pad(v, ((0, 0), (0, s_pad - S), (0, 0), (0, 0))).reshape(B, n_blk, MOBA_BLK, H, dh)
    n_top = min(MOBA_TOPK, n_blk - 1)
    table = rel_table.astype(jnp.float32)
    table_h = table.T
    scale = dh ** -0.5
    t = jnp.arange(S)
    if n_top > 0:
        k_mean = jnp.mean(kb.astype(jnp.float32), axis=2).astype(q.dtype)
        gate = jnp.einsum("bshd,bnhd->bhsn", q, k_mean).astype(jnp.float32)
        past = jnp.arange(n_blk)[None, :] < (t // MOBA_BLK)[:, None]
        _, sel = lax.top_k(jnp.where(past, gate, NEG), n_top)
        kbh = kb.transpose(0, 3, 1, 2, 4)
        vbh = vb.transpose(0, 3, 1, 2, 4)
    h_ix = jnp.arange(H)[:, None, None]
    n_q = S // MOBA_QCHUNK

    def chunk(i):
        b, c = i // n_q, i % n_q
        start = c * MOBA_QCHUNK
        q_c = lax.dynamic_slice_in_dim(q[b], start, MOBA_QCHUNK, axis=0)
        tq = start + jnp.arange(MOBA_QCHUNK)
        own = start // MOBA_BLK
        k_own = lax.dynamic_index_in_dim(kb[b], own, axis=0, keepdims=False)
        v_own = lax.dynamic_index_in_dim(vb[b], own, axis=0, keepdims=False)
        d_own = tq[:, None] - (own * MOBA_BLK + jnp.arange(MOBA_BLK))[None, :]
        s_own = (jnp.einsum("qhd,khd->hqk", q_c, k_own).astype(jnp.float32) * scale
                 + table[rel_bucket(d_own)].transpose(2, 0, 1))
        ok_own = jnp.broadcast_to(d_own >= 0, s_own.shape)
        if n_top == 0:
            p = masked_softmax(s_own, ok_own)
            return jnp.einsum("hqk,khd->qhd", p.astype(v.dtype), v_own)
        idx = lax.dynamic_slice_in_dim(sel[b], start, MOBA_QCHUNK, axis=1)
        n_keys = n_top * MOBA_BLK
        k_sel = kbh[b][h_ix, idx].reshape(H, MOBA_QCHUNK, n_keys, dh)
        v_sel = vbh[b][h_ix, idx].reshape(H, MOBA_QCHUNK, n_keys, dh)
        pos = (idx[..., None] * MOBA_BLK + jnp.arange(MOBA_BLK)).reshape(H, MOBA_QCHUNK, n_keys)
        d_sel = tq[None, :, None] - pos
        s_sel = (jnp.einsum("qhd,hqkd->hqk", q_c, k_sel).astype(jnp.float32) * scale
                 + table_h[h_ix, rel_bucket(d_sel)])
        ok_sel = jnp.broadcast_to((idx < own)[..., None],
                                  (H, MOBA_QCHUNK, n_top, MOBA_BLK)).reshape(H, MOBA_QCHUNK, n_keys)
        p = masked_softmax(jnp.concatenate([s_own, s_sel], axis=-1),
                           jnp.concatenate([ok_own, ok_sel], axis=-1)).astype(v.dtype)
        return (jnp.einsum("hqk,khd->qhd", p[..., :MOBA_BLK], v_own)
                + jnp.einsum("hqk,hqkd->qhd", p[..., MOBA_BLK:], v_sel))

    out = lax.map(chunk, jnp.arange(B * n_q))
    return out.reshape(B, S, H * dh) @ w_out


def setup_inputs(seed: int = 0) -> dict:
    key = jax.random.key(seed)
    ks = jax.random.split(key, 16)
    f32 = jnp.float32
    hd = N_HEADS * HEAD_DIM

    def nrm(k, shape, fan_in):
        return jax.random.normal(k, shape, f32) * fan_in ** -0.5

    return {
        "x": jax.random.normal(ks[0], (BATCH, SEQ, D_MODEL), f32),
        "rel_bias": 0.5 * jax.random.normal(ks[1], (REL_BUCKETS, N_HEADS), f32),
        "norm_g": 1.0 + 0.05 * jax.random.normal(ks[2], (DEPTH, 6, D_MODEL), f32),
        "ffn_w_in": nrm(ks[3], (DEPTH, 2, D_MODEL, 2 * D_FF), D_MODEL),
        "ffn_w_out": nrm(ks[4], (DEPTH, 2, D_FF, D_MODEL), D_FF),
        "a_w_in": nrm(ks[5], (N_A_LAYERS, D_MODEL, A_IN), D_MODEL),
        "a_w_out": nrm(ks[6], (N_A_LAYERS, hd, D_MODEL), hd),
        "a_sinks": jax.random.normal(ks[7], (N_A_LAYERS, N_HEADS), f32),
        "b_w_in": nrm(ks[8], (N_B_LAYERS, D_MODEL, B_IN), D_MODEL),
        "b_w_out": nrm(ks[9], (N_B_LAYERS, hd, D_MODEL), hd),
        "b_cmp_pos": 0.5 * jax.random.normal(ks[10], (N_B_LAYERS, 2, CMP_LEN, HEAD_DIM), f32),
        "b_cmp_w1": nrm(ks[11], (N_B_LAYERS, 2, CMP_LEN * HEAD_DIM, CMP_HIDDEN), CMP_LEN * HEAD_DIM),
        "b_cmp_w2": nrm(ks[12], (N_B_LAYERS, 2, CMP_HIDDEN, HEAD_DIM), CMP_HIDDEN),
        "c_w_in": nrm(ks[13], (N_C_LAYERS, D_MODEL, C_IN), D_MODEL),
        "c_w_out": nrm(ks[14], (N_C_LAYERS, hd, D_MODEL), hd),
    }


def reference(x, rel_bias, norm_g, ffn_w_in, ffn_w_out, a_w_in, a_w_out, a_sinks,
              b_w_in, b_w_out, b_cmp_pos, b_cmp_w1, b_cmp_w2, c_w_in, c_w_out):
    for i in range(DEPTH):
        g = norm_g[i]
        x = x + 0.5 * rms_norm(swiglu(rms_norm(x, g[0]), ffn_w_in[i, 0], ffn_w_out[i, 0]), g[1])
        hm = rms_norm(x, g[2])
        kind, j = i % N_MIXERS, i // N_MIXERS
        if kind == 0:
            m = sliding_sink_attention(hm, a_w_in[j], a_w_out[j], a_sinks[j], rel_bias)
        elif kind == 1:
            m = nsa_attention(hm, b_w_in[j], b_w_out[j], b_cmp_pos[j], b_cmp_w1[j],
                              b_cmp_w2[j], rel_bias)
        else:
            m = moba_attention(hm, c_w_in[j], c_w_out[j], rel_bias)
        x = x + rms_norm(m, g[3])
        x = x + 0.5 * rms_norm(swiglu(rms_norm(x, g[4]), ffn_w_in[i, 1], ffn_w_out[i, 1]), g[5])
    return x
```

```python
import dataclasses
import functools
import math

import numpy as np
import jax
import jax.numpy as jnp
from jax import lax
from jax.experimental import pallas as pl
from jax.experimental.pallas import tpu as pltpu

F32 = jnp.float32
BF16 = jnp.bfloat16

D_MODEL = 1024
HEAD_DIM = 64
N_HEADS = D_MODEL // HEAD_DIM
D_FF = 2816
NORM_EPS = 1e-6
N_MIXERS = 3
REL_BUCKETS = 32
REL_MAX_DIST = 1024
A_KV_HEADS = 2
A_WINDOW = 128
B_KV_GROUPS = 4
CMP_LEN = 32
CMP_STRIDE = 16
CMP_HIDDEN = 256
SLC_BLK = 64
SLC_TOPK = 16
B_WINDOW = 512
MOBA_BLK = 256
MOBA_TOPK = 3
NEG = -1e30
TINY = 1e-30

LANES = 128
MXU_DIM = 256
VMEM_LIMIT_BYTES = 56 * 1024 * 1024

SLOTS = 4
HB_W = SLOTS * HEAD_DIM
N_HB = N_HEADS // SLOTS
TQ = 256
TK = 256
TM = 512
FF_CHUNKS = ((0, 1536), (1536, D_FF))
SEL_BIG = float(2 ** 30)
SCALE = HEAD_DIM ** -0.5

_NT = (((1,), (1,)), ((), ()))


def _params(n_axes):
    return pltpu.CompilerParams(
        dimension_semantics=("arbitrary",) * n_axes,
        vmem_limit_bytes=VMEM_LIMIT_BYTES)


def _rms(x, g):
    return x * lax.rsqrt(jnp.mean(x * x, axis=-1, keepdims=True) + NORM_EPS) * g


def _sigmoid(x):
    return 1.0 / (1.0 + jnp.exp(-x))


def _div_pow2(x, d):
    assert d & (d - 1) == 0
    return jnp.right_shift(x, d.bit_length() - 1)


def _mod_pow2(x, d):
    assert d & (d - 1) == 0
    return jnp.bitwise_and(x, d - 1)


def _resident(shape):
    zeros = (0,) * len(shape)
    return pl.BlockSpec(shape, lambda *_: zeros, pipeline_mode=pl.Buffered(1))


def _ffn_kernel(x_ref, gpre_ref, win_ref, wout_ref, gpost_ref, o_ref):
    x = x_ref[...]
    hn = _rms(x, gpre_ref[...]).astype(BF16)
    y = None
    for lo, hi in FF_CHUNKS:
        gate = jnp.dot(hn, win_ref[:, lo:hi], preferred_element_type=F32)
        up = jnp.dot(hn, win_ref[:, D_FF + lo:D_FF + hi], preferred_element_type=F32)
        act = (gate * _sigmoid(gate) * up).astype(BF16)
        part = jnp.dot(act, wout_ref[lo:hi, :], preferred_element_type=F32)
        y = part if y is None else y + part
    o_ref[...] = x + 0.5 * _rms(y, gpost_ref[...])


def _ffn(xf, g_pre, w_in, w_out, g_post):
    n, d = xf.shape
    return pl.pallas_call(
        _ffn_kernel,
        grid=(n // TM,),
        in_specs=[
            pl.BlockSpec((TM, d), lambda i: (i, 0)),
            _resident((1, d)),
            _resident(w_in.shape),
            _resident(w_out.shape),
            _resident((1, d)),
        ],
        out_specs=pl.BlockSpec((TM, d), lambda i: (i, 0)),
        out_shape=jax.ShapeDtypeStruct((n, d), F32),
        compiler_params=_params(1),
        name="ffn",
    )(xf, g_pre.reshape(1, d), w_in.astype(BF16), w_out.astype(BF16), g_post.reshape(1, d))


def _proj_kernel(x_ref, g_ref, w_ref, o_ref):
    hn = _rms(x_ref[...], g_ref[...]).astype(BF16)
    o_ref[...] = jnp.dot(hn, w_ref[...], preferred_element_type=F32)


def _proj(xf, g, w):
    n, d = xf.shape
    c = w.shape[1]
    return pl.pallas_call(
        _proj_kernel,
        grid=(n // TM,),
        in_specs=[
            pl.BlockSpec((TM, d), lambda i: (i, 0)),
            _resident((1, d)),
            _resident(w.shape),
        ],
        out_specs=pl.BlockSpec((TM, c), lambda i: (i, 0)),
        out_shape=jax.ShapeDtypeStruct((n, c), F32),
        compiler_params=_params(1),
        name="proj",
    )(xf, g.reshape(1, d), w.astype(BF16))


def _outproj_kernel(*refs, n_in):
    o_refs = refs[:n_in]
    w_ref, g_ref, x_ref, out_ref = refs[n_in:]
    o = o_refs[0][...]
    for r in o_refs[1:]:
        o = o + r[...]
    m = jnp.dot(o.astype(BF16), w_ref[...], preferred_element_type=F32)
    out_ref[...] = x_ref[...] + _rms(m, g_ref[...])


def _outproj(o_list, w, g, xf):
    n, d = xf.shape
    tile = pl.BlockSpec((TM, d), lambda i: (i, 0))
    return pl.pallas_call(
        functools.partial(_outproj_kernel, n_in=len(o_list)),
        grid=(n // TM,),
        in_specs=[tile] * len(o_list) + [_resident(w.shape), _resident((1, d)), tile],
        out_specs=tile,
        out_shape=jax.ShapeDtypeStruct((n, d), F32),
        compiler_params=_params(1),
        name="outproj",
    )(*o_list, w.astype(BF16), g.reshape(1, d), xf)


def _rel_bucket(dist):
    dist = jnp.maximum(dist, 0)
    exact = REL_BUCKETS // 2
    d = jnp.maximum(dist, 1).astype(jnp.float32)
    log_b = exact + (jnp.log(d / exact) / math.log(REL_MAX_DIST / exact)
                     * (REL_BUCKETS - exact)).astype(jnp.int32)
    return jnp.where(dist < exact, dist, jnp.minimum(log_b, REL_BUCKETS - 1))


def _bias_tiles(rel_bias, n_delta):
    delta = jnp.arange(n_delta)[:, None, None]
    dist = delta * TK + jnp.arange(TQ)[None, :, None] - jnp.arange(TK)[None, None, :]
    bucket = _rel_bucket(dist).reshape(1, -1)
    onehot = (jnp.arange(REL_BUCKETS)[:, None] == bucket).astype(F32)
    tiles = jnp.dot(rel_bias.astype(F32).T, onehot, precision=lax.Precision.HIGHEST)
    return tiles.reshape(N_HEADS, n_delta, TQ, TK)


@dataclasses.dataclass(frozen=True)
class _AttnCfg:
    kv_width: int
    n_back: int | None = None
    window: int | None = None
    sel_blk: int | None = None
    sel_from_input: bool = False
    gate_branch: int | None = None
    sink: bool = False

    @property
    def selecting(self):
        return self.sel_blk is not None

    @property
    def moba(self):
        return self.selecting and not self.sel_from_input


def _rank_in_groups(val, pos, width, n_lanes):
    rank = jnp.zeros(val.shape, F32)
    for s in range(1, width):
        before = pltpu.roll(val, s, axis=1)
        rank = rank + jnp.where(pos >= s, jnp.where(before >= val, 1.0, 0.0), 0.0)
        after = pltpu.roll(val, n_lanes - s, axis=1)
        rank = rank + jnp.where(pos + s < width, jnp.where(after > val, 1.0, 0.0), 0.0)
    return rank


def _attn_kernel(*refs, cfg: _AttnCfg):
    it = iter(refs)
    q_ref, k_ref, v_ref, bias_ref = next(it), next(it), next(it), next(it)
    sel_ref = next(it) if cfg.sel_from_input else None
    gate_ref = next(it) if cfg.gate_branch is not None else None
    sink_ref = next(it) if cfg.sink else None
    o_ref = next(it)
    kb_ref, vb_ref, sc_ref = next(it), next(it), next(it)
    kmx_ref = next(it) if cfg.moba else None

    qt = pl.program_id(2)
    seq = k_ref.shape[1]
    n_tiles = seq // TK
    lane = lax.broadcasted_iota(jnp.int32, (1, HB_W), 1)
    slot_of_lane = _div_pow2(lane, HEAD_DIM)
    pos_in_slot = _mod_pow2(lane, HEAD_DIM)

    def keep_u_lanes(hh, n_blk, x):
        lo = ((hh + 1) % SLOTS) * HEAD_DIM
        return jnp.where(lane >= lo, jnp.where(lane < lo + n_blk, x, 0.0), 0.0)

    n_sel_blk = seq // cfg.sel_blk if cfg.selecting else 0

    @pl.when(qt == 0)
    def _prepare():
        if cfg.moba:
            kmx_ref[...] = jnp.zeros(kmx_ref.shape, F32)
        for c in range(n_tiles):
            rows = pl.ds(c * TK, TK)
            k = k_ref[0, rows, :]
            v = v_ref[0, rows, :]
            if cfg.kv_width == LANES:
                k = jnp.concatenate([k, k], axis=1)
                v = jnp.concatenate([v, v], axis=1)
            vb_ref[rows, :] = v.astype(BF16)
            if cfg.selecting:
                key_blk = _div_pow2(c * TK + lax.broadcasted_iota(jnp.int32, (TK, 1), 0),
                                    cfg.sel_blk)
                for hh in range(SLOTS):
                    u_lo = ((hh + 1) % SLOTS) * HEAD_DIM
                    onehot = (lane - u_lo) == key_blk
                    kb_ref[hh, rows, :] = jnp.where(
                        slot_of_lane == hh, k, jnp.where(onehot, 1.0, 0.0)).astype(BF16)
            else:
                kb_ref[0, rows, :] = k.astype(BF16)
            if cfg.moba:
                k_mean = jnp.mean(k, axis=0, keepdims=True)
                for hh in range(SLOTS):
                    r = ((hh + 1) % SLOTS) * HEAD_DIM + c
                    kmx_ref[pl.ds(r, 1), :] = jnp.where(slot_of_lane == hh, k_mean, 0.0)

    q = q_ref[0]
    ij = (lax.broadcasted_iota(jnp.int32, (TQ, TK), 0)
          - lax.broadcasted_iota(jnp.int32, (TQ, TK), 1))

    selw = None
    if cfg.sel_from_input:
        selw = sel_ref[0]
    elif cfg.moba:
        gate = lax.dot_general(q, kmx_ref[...], _NT, precision=lax.Precision.HIGHEST,
                               preferred_element_type=F32)
        val = jnp.where(pos_in_slot < qt, gate, NEG)
        rank = _rank_in_groups(val, pos_in_slot, n_sel_blk, HB_W)
        chosen = jnp.where(pos_in_slot < qt, jnp.where(rank < MOBA_TOPK, 1.0, 0.0), 0.0)
        selw = jnp.where(pos_in_slot == qt, 1.0, chosen)

    gsig = _sigmoid(gate_ref[0]) if cfg.gate_branch is not None else None
    lo = jnp.maximum(qt - cfg.n_back, 0) if cfg.n_back is not None else 0

    out = jnp.zeros((TQ, HB_W), F32)
    for hh in range(SLOTS):
        qm = jnp.where(slot_of_lane == hh, q, 0.0)
        if cfg.selecting:
            qm = qm + keep_u_lanes(hh, n_sel_blk, (selw - 1.0) * SEL_BIG)
        qm = qm.astype(BF16)
        k_idx = hh if cfg.selecting else 0

        def scores(kt, mrun, qm=qm, k_idx=k_idx, hh=hh):
            start = pl.multiple_of(kt * TK, TK)
            s = lax.dot_general(qm, kb_ref[k_idx, pl.ds(start, TK), :], _NT,
                                preferred_element_type=F32)
            s = s * SCALE + bias_ref[0, hh, qt - kt]
            base = (qt - kt) * TK
            s = jnp.where(ij >= -base, s, NEG)
            if cfg.window is not None:
                s = jnp.where(ij < cfg.window - base, s, NEG)
            sc_ref[kt] = s
            return jnp.maximum(mrun, jnp.maximum(s[:, :LANES], s[:, LANES:]))

        mrun = lax.fori_loop(lo, qt + 1, scores, jnp.full((TQ, LANES), NEG, F32))
        m = jnp.max(mrun, axis=1, keepdims=True)
        if cfg.sink:
            sink = sink_ref[0, :, hh:hh + 1]
            m = jnp.maximum(m, sink)

        def weighted(kt, carry, m=m):
            lrun, acc = carry
            start = pl.multiple_of(kt * TK, TK)
            p = jnp.exp(sc_ref[kt] - m)
            acc = acc + jnp.dot(p.astype(BF16), vb_ref[pl.ds(start, TK), :],
                                preferred_element_type=F32)
            return lrun + (p[:, :LANES] + p[:, LANES:]), acc

        lrun, acc = lax.fori_loop(
            lo, qt + 1, weighted,
            (jnp.zeros((TQ, LANES), F32), jnp.zeros((TQ, HB_W), F32)))
        denom = jnp.sum(lrun, axis=1, keepdims=True)
        if cfg.sink:
            denom = denom + jnp.exp(sink - m)
        o_h = acc / jnp.maximum(denom, TINY)
        if cfg.gate_branch is not None:
            col = cfg.gate_branch * SLOTS + hh
            o_h = o_h * gsig[:, col:col + 1]
        out = out + jnp.where(slot_of_lane == hh, o_h, 0.0)
    o_ref[0] = out


def _attention(cfg, proj, bias, *, q_blk0, k_blk, v_blk, sel=None, gate_blk0=None, sinks=None):
    bsz, seq, _ = proj.shape
    kw = cfg.kv_width
    n_delta = bias.shape[2]
    in_specs = [
        pl.BlockSpec((1, TQ, HB_W), lambda hb, b, qt: (b, qt, q_blk0 + hb)),
        pl.BlockSpec((1, seq, kw), lambda hb, b, qt: (b, 0, k_blk(hb))),
        pl.BlockSpec((1, seq, kw), lambda hb, b, qt: (b, 0, v_blk(hb))),
        pl.BlockSpec((1, SLOTS, n_delta, TQ, TK), lambda hb, b, qt: (hb, 0, 0, 0, 0)),
    ]
    args = [proj, proj, proj, bias]
    if cfg.sel_from_input:
        in_specs.append(pl.BlockSpec((1, TQ, HB_W), lambda hb, b, qt: (b, qt, 0)))
        args.append(sel)
    if cfg.gate_branch is not None:
        in_specs.append(pl.BlockSpec((1, TQ, LANES), lambda hb, b, qt: (b, qt, gate_blk0 + hb)))
        args.append(proj)
    if cfg.sink:
        in_specs.append(pl.BlockSpec((1, 1, LANES), lambda hb, b, qt: (hb, 0, 0)))
        args.append(sinks)
    scratch = [
        pltpu.VMEM((SLOTS if cfg.selecting else 1, seq, HB_W), BF16),
        pltpu.VMEM((seq, HB_W), BF16),
        pltpu.VMEM((seq // TK, TQ, TK), F32),
    ]
    if cfg.moba:
        scratch.append(pltpu.VMEM((HB_W, HB_W), F32))
    return pl.pallas_call(
        functools.partial(_attn_kernel, cfg=cfg),
        grid=(N_HB, bsz, seq // TQ),
        in_specs=in_specs,
        out_specs=pl.BlockSpec((1, TQ, HB_W), lambda hb, b, qt: (b, qt, hb)),
        out_shape=jax.ShapeDtypeStruct((bsz, seq, D_MODEL), F32),
        scratch_shapes=scratch,
        compiler_params=_params(3),
        name="attention",
    )(*args)


def _gelu_tanh(x):
    cdf = 0.5 * (1.0 + jnp.tanh(math.sqrt(2.0 / math.pi) * (x + 0.044715 * (x * x * x))))
    return x * cdf


def _compress_kernel(kc_lo_ref, kc_hi_ref, vc_lo_ref, vc_hi_ref, pos_ref, w1_ref, w2_ref,
                     kcb_ref, vcb_ref):
    half = CMP_LEN // 2
    n_chunk = kc_lo_ref.shape[1] // CMP_STRIDE
    flat_lane = lax.broadcasted_iota(jnp.int32, (1, half * HB_W), 1)
    group_of_lane = _div_pow2(_mod_pow2(flat_lane, HB_W), HEAD_DIM)
    sources = ((kc_lo_ref, kc_hi_ref, kcb_ref), (vc_lo_ref, vc_hi_ref, vcb_ref))
    for kind, (lo_ref, hi_ref, dst_ref) in enumerate(sources):
        chunks = [jnp.concatenate(
            [lo_ref[0, pl.ds(l, n_chunk, stride=CMP_STRIDE), :],
             hi_ref[0, pl.ds(l, n_chunk, stride=CMP_STRIDE), :]], axis=1) for l in range(half)]
        parts = []
        for h in range(2):
            pieces = [chunks[l] + pos_ref[kind, pl.ds(h * half + l, 1), :] for l in range(half)]
            flat = jnp.concatenate(pieces, axis=1)
            stacked = jnp.concatenate(
                [jnp.where(group_of_lane == g, flat, 0.0)
                 for g in range(B_KV_GROUPS)], axis=0).astype(BF16)
            parts.append(jnp.dot(stacked, w1_ref[kind, h], preferred_element_type=F32))
        rows = B_KV_GROUPS * n_chunk
        hidden = parts[0] + pltpu.roll(parts[1], rows - 1, axis=0)
        act = _gelu_tanh(hidden).astype(BF16)
        out = None
        for g in range(B_KV_GROUPS):
            o = jnp.dot(act[g * n_chunk:(g + 1) * n_chunk], w2_ref[kind, g],
                        preferred_element_type=F32)
            out = o if out is None else out + o
        dst_ref[0] = out


def _compress(proj, kc_blk, vc_blk, pos, w1, w2):
    bsz, seq, _ = proj.shape
    n_chunk = seq // CMP_STRIDE
    out = jax.ShapeDtypeStruct((bsz, n_chunk, HB_W), F32)
    blk = pl.BlockSpec((1, n_chunk, HB_W), lambda b: (b, 0, 0))
    return pl.pallas_call(
        _compress_kernel,
        grid=(bsz,),
        in_specs=[
            pl.BlockSpec((1, seq, LANES), lambda b: (b, 0, 2 * kc_blk)),
            pl.BlockSpec((1, seq, LANES), lambda b: (b, 0, 2 * kc_blk + 1)),
            pl.BlockSpec((1, seq, LANES), lambda b: (b, 0, 2 * vc_blk)),
            pl.BlockSpec((1, seq, LANES), lambda b: (b, 0, 2 * vc_blk + 1)),
            _resident(pos.shape),
            _resident(w1.shape),
            _resident(w2.shape),
        ],
        out_specs=[blk, blk],
        out_shape=[out, out],
        compiler_params=_params(1),
        name="nsa_compress",
    )(proj, proj, proj, proj, pos, w1, w2)


def _cmp_kernel(q_ref, kcb_ref, vcb_ref, gate_ref, ov_ref, perm_ref, o_ref, sel_ref):
    qt = pl.program_id(1)
    n_cmp_pad = kcb_ref.shape[1]
    n_slc = LANES // B_KV_GROUPS
    lane = lax.broadcasted_iota(jnp.int32, (1, HB_W), 1)
    slot_of_lane = _div_pow2(lane, HEAD_DIM)
    t = qt * TQ + lax.broadcasted_iota(jnp.int32, (TQ, 1), 0)
    cmp_end = (lax.broadcasted_iota(jnp.int32, (1, n_cmp_pad), 1) * CMP_STRIDE + CMP_LEN - 1)
    ok = cmp_end <= t
    kc = kcb_ref[0].astype(BF16)
    vc = vcb_ref[0].astype(BF16)
    gsig = _sigmoid(gate_ref[0])
    p_group = [jnp.zeros((TQ, n_cmp_pad), F32) for _ in range(B_KV_GROUPS)]
    for r in range(N_HB):
        q_r = q_ref[0, :, r * HB_W:(r + 1) * HB_W]
        out = jnp.zeros((TQ, HB_W), F32)
        for g in range(B_KV_GROUPS):
            qm = jnp.where(slot_of_lane == g, q_r, 0.0).astype(BF16)
            s = lax.dot_general(qm, kc, _NT, preferred_element_type=F32) * SCALE
            s = jnp.where(ok, s, NEG)
            m = jnp.max(s, axis=1, keepdims=True)
            e = jnp.where(ok, jnp.exp(s - m), 0.0)
            p = e / jnp.maximum(jnp.sum(e, axis=1, keepdims=True), TINY)
            p_group[g] = p_group[g] + p
            o = jnp.dot(p.astype(BF16), vc, preferred_element_type=F32)
            col = r * LANES + g
            out = out + jnp.where(slot_of_lane == g, o * gsig[:, col:col + 1], 0.0)
        o_ref[0, :, r * HB_W:(r + 1) * HB_W] = out

    imp = None
    for g in range(B_KV_GROUPS):
        part = jnp.dot(p_group[g], ov_ref[g], precision=lax.Precision.HIGHEST,
                       preferred_element_type=F32)
        imp = part if imp is None else imp + part
    lane_s = lax.broadcasted_iota(jnp.int32, (1, LANES), 1)
    j = _mod_pow2(lane_s, n_slc)
    cur = _div_pow2(t, SLC_BLK)
    val = jnp.where(j == cur - 1, -NEG, imp)
    val = jnp.where(j == cur, -NEG, val)
    val = jnp.where(j == 0, -NEG, val)
    val = jnp.where(j > cur, NEG, val)
    rank = _rank_in_groups(val, j, n_slc, LANES)
    chosen = jnp.where(rank < SLC_TOPK, 1.0, 0.0).astype(BF16)
    sel_ref[0] = jnp.dot(chosen, perm_ref[...], preferred_element_type=F32)


def _cmp_attention(proj, kcb, vcb, gate_blk, ov, perm):
    bsz, seq, _ = proj.shape
    n_cmp_pad = kcb.shape[1]
    kv_blk = pl.BlockSpec((1, n_cmp_pad, HB_W), lambda b, qt: (b, 0, 0))
    return pl.pallas_call(
        _cmp_kernel,
        grid=(bsz, seq // TQ),
        in_specs=[
            pl.BlockSpec((1, TQ, D_MODEL), lambda b, qt: (b, qt, 0)),
            kv_blk, kv_blk,
            pl.BlockSpec((1, TQ, N_HB * LANES), lambda b, qt: (b, qt, gate_blk)),
            _resident(ov.shape),
            _resident(perm.shape),
        ],
        out_specs=[
            pl.BlockSpec((1, TQ, D_MODEL), lambda b, qt: (b, qt, 0)),
            pl.BlockSpec((1, TQ, HB_W), lambda b, qt: (b, qt, 0)),
        ],
        out_shape=[
            jax.ShapeDtypeStruct((bsz, seq, D_MODEL), F32),
            jax.ShapeDtypeStruct((bsz, seq, HB_W), F32),
        ],
        compiler_params=_params(2),
        name="nsa_cmp_select",
    )(proj, kcb, vcb, proj, ov, perm)


def _interleave_heads(w, n_groups, axis):
    r = N_HEADS // n_groups
    shape = w.shape
    w = w.reshape(shape[:axis] + (n_groups, r) + shape[axis + 1:])
    w = jnp.swapaxes(w, axis, axis + 1)
    return w.reshape(shape)


def _mixer_a(xf, bsz, seq, g_pre, w_in, w_out, sinks, g_post, bias_nat):
    qw = N_HEADS * HEAD_DIM
    w_q = _interleave_heads(w_in[:, :qw].reshape(D_MODEL, N_HEADS, HEAD_DIM), A_KV_HEADS, 1)
    w = jnp.concatenate([w_q.reshape(D_MODEL, qw), w_in[:, qw:]], axis=1)
    proj = _proj(xf, g_pre, w).reshape(bsz, seq, -1)
    n_back = -(-(A_WINDOW - 1) // TK)
    bias = _interleave_heads(bias_nat[:, :n_back + 1], A_KV_HEADS, 0)
    bias = bias.reshape((N_HB, SLOTS) + bias.shape[1:])
    sink_rows = _interleave_heads(sinks.astype(F32), A_KV_HEADS, 0).reshape(N_HB, 1, SLOTS)
    sink_rows = jnp.pad(sink_rows, ((0, 0), (0, 0), (0, LANES - SLOTS)))
    cfg = _AttnCfg(kv_width=LANES, n_back=n_back, window=A_WINDOW, sink=True)
    k_blk0 = qw // LANES
    o = _attention(cfg, proj, bias, q_blk0=0, k_blk=lambda hb: k_blk0,
                   v_blk=lambda hb: k_blk0 + 1, sinks=sink_rows)
    w_o = _interleave_heads(w_out.reshape(N_HEADS, HEAD_DIM, D_MODEL), A_KV_HEADS, 0)
    return _outproj([o.reshape(bsz * seq, -1)], w_o.reshape(qw, D_MODEL), g_post, xf)


def _nsa_constants(seq):
    n_chunk = seq // CMP_STRIDE
    n_cmp = (seq - CMP_LEN) // CMP_STRIDE + 1
    n_slc = seq // SLC_BLK
    c_start = np.arange(n_cmp)[:, None] * CMP_STRIDE
    s_start = np.arange(n_slc)[None, :] * SLC_BLK
    overlap = ((c_start < s_start + SLC_BLK) & (c_start + CMP_LEN > s_start)).astype(np.float32)
    ov = np.zeros((B_KV_GROUPS, n_chunk, LANES), np.float32)
    perm = np.zeros((LANES, HB_W), np.float32)
    for g in range(B_KV_GROUPS):
        ov[g, :n_cmp, g * n_slc:(g + 1) * n_slc] = overlap
        for j in range(n_slc):
            perm[g * n_slc + j, ((g + 1) % SLOTS) * HEAD_DIM + j] = 1.0
    return jnp.asarray(ov), jnp.asarray(perm, dtype=BF16)


def _mixer_b(xf, bsz, seq, g_pre, w_in, w_out, cmp_pos, cmp_w1, cmp_w2, g_post, bias_nat):
    assert seq // SLC_BLK * B_KV_GROUPS == LANES and seq // CMP_STRIDE == LANES
    grp, dh = B_KV_GROUPS, HEAD_DIM
    qw, kvw = N_HEADS * dh, B_KV_GROUPS * dh
    w_q = _interleave_heads(w_in[:, :qw].reshape(D_MODEL, N_HEADS, dh), grp, 1)
    w_g = w_in[:, qw + 6 * kvw:].reshape(D_MODEL, grp, N_HB, 3)
    w_g = w_g.transpose(0, 2, 3, 1).reshape(D_MODEL, N_HB, 3 * grp)
    w_g = jnp.pad(w_g, ((0, 0), (0, 0), (0, LANES - 3 * grp))).reshape(D_MODEL, N_HB * LANES)
    w = jnp.concatenate([w_q.reshape(D_MODEL, qw), w_in[:, qw:qw + 6 * kvw], w_g], axis=1)
    proj = _proj(xf, g_pre, w).reshape(bsz, seq, -1)
    kv_blk0 = qw // HB_W
    gate_col0 = qw + 6 * kvw

    half = CMP_LEN // 2
    pos = jnp.tile(cmp_pos.astype(F32), (1, 1, grp))
    w1 = cmp_w1.reshape(2, 2, half, 1, dh, CMP_HIDDEN)
    w1 = jnp.broadcast_to(w1, (2, 2, half, grp, dh, CMP_HIDDEN))
    w1 = w1.reshape(2, 2, half * HB_W, CMP_HIDDEN).astype(BF16)
    w2 = jnp.zeros((2, grp, CMP_HIDDEN, grp, dh), F32)
    for g in range(grp):
        w2 = w2.at[:, g, :, g, :].set(cmp_w2)
    w2 = w2.reshape(2, grp, CMP_HIDDEN, HB_W).astype(BF16)
    kcb, vcb = _compress(proj, kv_blk0, kv_blk0 + 1, pos, w1, w2)

    ov, perm = _nsa_constants(seq)
    o_cmp, sel = _cmp_attention(proj, kcb, vcb, gate_col0 // (N_HB * LANES), ov, perm)

    bias = _interleave_heads(bias_nat, grp, 0)
    bias = bias.reshape((N_HB, SLOTS) + bias.shape[1:])
    gate_blk0 = gate_col0 // LANES
    cfg_slc = _AttnCfg(kv_width=HB_W, sel_blk=SLC_BLK, sel_from_input=True, gate_branch=1)
    o_slc = _attention(cfg_slc, proj, bias, q_blk0=0, k_blk=lambda hb: kv_blk0 + 2,
                       v_blk=lambda hb: kv_blk0 + 3, sel=sel, gate_blk0=gate_blk0)
    n_back = -(-(B_WINDOW - 1) // TK)
    cfg_win = _AttnCfg(kv_width=HB_W, n_back=n_back, window=B_WINDOW, gate_branch=2)
    o_win = _attention(cfg_win, proj, bias[:, :, :n_back + 1], q_blk0=0,
                       k_blk=lambda hb: kv_blk0 + 4, v_blk=lambda hb: kv_blk0 + 5,
                       gate_blk0=gate_blk0)
    w_o = _interleave_heads(w_out.reshape(N_HEADS, dh, D_MODEL), grp, 0).reshape(qw, D_MODEL)
    n = bsz * seq
    return _outproj([o_cmp.reshape(n, -1), o_slc.reshape(n, -1), o_win.reshape(n, -1)],
                    w_o, g_post, xf)


def _mixer_c(xf, bsz, seq, g_pre, w_in, w_out, g_post, bias_nat):
    assert seq % MOBA_BLK == 0 and MOBA_BLK == TK and seq // MOBA_BLK <= HEAD_DIM
    proj = _proj(xf, g_pre, w_in).reshape(bsz, seq, -1)
    bias = bias_nat.reshape((N_HB, SLOTS) + bias_nat.shape[1:])
    cfg = _AttnCfg(kv_width=HB_W, sel_blk=MOBA_BLK)
    o = _attention(cfg, proj, bias, q_blk0=0, k_blk=lambda hb: N_HB + hb,
                   v_blk=lambda hb: 2 * N_HB + hb)
    return _outproj([o.reshape(bsz * seq, -1)], w_out, g_post, xf)


def kernel(x, rel_bias, norm_g, ffn_w_in, ffn_w_out, a_w_in, a_w_out, a_sinks, b_w_in, b_w_out,
           b_cmp_pos, b_cmp_w1, b_cmp_w2, c_w_in, c_w_out):
    bsz, seq, d = x.shape
    assert d == D_MODEL and seq % TQ == 0 and (bsz * seq) % TM == 0
    depth = norm_g.shape[0]
    bias_nat = _bias_tiles(rel_bias, seq // TK)
    xf = x.reshape(bsz * seq, d)
    for i in range(depth):
        g = norm_g[i]
        xf = _ffn(xf, g[0], ffn_w_in[i, 0], ffn_w_out[i, 0], g[1])
        kind, j = i % N_MIXERS, i // N_MIXERS
        if kind == 0:
            xf = _mixer_a(xf, bsz, seq, g[2], a_w_in[j], a_w_out[j], a_sinks[j], g[3], bias_nat)
        elif kind == 1:
            xf = _mixer_b(xf, bsz, seq, g[2], b_w_in[j], b_w_out[j], b_cmp_pos[j], b_cmp_w1[j],
                          b_cmp_w2[j], g[3], bias_nat)
        else:
            xf = _mixer_c(xf, bsz, seq, g[2], c_w_in[j], c_w_out[j], g[3], bias_nat)
        xf = _ffn(xf, g[4], ffn_w_in[i, 1], ffn_w_out[i, 1], g[5])
    return xf.reshape(bsz, seq, d)
```

```python
import dataclasses
import functools
import math

import numpy as np
import jax
import jax.numpy as jnp
from jax import lax
from jax.experimental import pallas as pl
from jax.experimental.pallas import tpu as pltpu

F32 = jnp.float32
BF16 = jnp.bfloat16

D_MODEL = 1024
HEAD_DIM = 64
N_HEADS = D_MODEL // HEAD_DIM
D_FF = 2816
NORM_EPS = 1e-6
N_MIXERS = 3
REL_BUCKETS = 32
REL_MAX_DIST = 1024
A_KV_HEADS = 2
A_WINDOW = 128
B_KV_GROUPS = 4
CMP_LEN = 32
CMP_STRIDE = 16
CMP_HIDDEN = 256
SLC_BLK = 64
SLC_TOPK = 16
B_WINDOW = 512
MOBA_BLK = 256
MOBA_TOPK = 3
NEG = -1e30
TINY = 1e-30

LANES = 128
MXU_DIM = 256
VMEM_LIMIT_BYTES = 56 * 1024 * 1024

SLOTS = 4
HB_W = SLOTS * HEAD_DIM
N_HB = N_HEADS // SLOTS
TQ = 256
TK = 256
TM = 512
FF_CHUNKS = ((0, 1536), (1536, D_FF))
SEL_BIG = float(2 ** 30)
SCALE = HEAD_DIM ** -0.5

_NT = (((1,), (1,)), ((), ()))


def _params(n_axes):
    return pltpu.CompilerParams(
        dimension_semantics=("arbitrary",) * n_axes,
        vmem_limit_bytes=VMEM_LIMIT_BYTES)


def _rms(x, g):
    return x * lax.rsqrt(jnp.mean(x * x, axis=-1, keepdims=True) + NORM_EPS) * g


def _sigmoid(x):
    return 1.0 / (1.0 + jnp.exp(-x))


def _div_pow2(x, d):
    assert d & (d - 1) == 0
    return jnp.right_shift(x, d.bit_length() - 1)


def _mod_pow2(x, d):
    assert d & (d - 1) == 0
    return jnp.bitwise_and(x, d - 1)


def _resident(shape):
    zeros = (0,) * len(shape)
    return pl.BlockSpec(shape, lambda *_: zeros, pipeline_mode=pl.Buffered(1))


def _ffn_kernel(x_ref, gpre_ref, win_ref, wout_ref, gpost_ref, o_ref):
    x = x_ref[...]
    hn = _rms(x, gpre_ref[...]).astype(BF16)
    y = None
    for lo, hi in FF_CHUNKS:
        gate = jnp.dot(hn, win_ref[:, lo:hi], preferred_element_type=F32)
        up = jnp.dot(hn, win_ref[:, D_FF + lo:D_FF + hi], preferred_element_type=F32)
        act = (gate * _sigmoid(gate) * up).astype(BF16)
        part = jnp.dot(act, wout_ref[lo:hi, :], preferred_element_type=F32)
        y = part if y is None else y + part
    o_ref[...] = x + 0.5 * _rms(y, gpost_ref[...])


def _ffn(xf, g_pre, w_in, w_out, g_post):
    n, d = xf.shape
    return pl.pallas_call(
        _ffn_kernel,
        grid=(n // TM,),
        in_specs=[
            pl.BlockSpec((TM, d), lambda i: (i, 0)),
            _resident((1, d)),
            _resident(w_in.shape),
            _resident(w_out.shape),
            _resident((1, d)),
        ],
        out_specs=pl.BlockSpec((TM, d), lambda i: (i, 0)),
        out_shape=jax.ShapeDtypeStruct((n, d), F32),
        compiler_params=_params(1),
        name="ffn",
    )(xf, g_pre.reshape(1, d), w_in.astype(BF16), w_out.astype(BF16), g_post.reshape(1, d))


def _proj_kernel(x_ref, g_ref, w_ref, o_ref):
    hn = _rms(x_ref[...], g_ref[...]).astype(BF16)
    o_ref[...] = jnp.dot(hn, w_ref[...], preferred_element_type=F32)


def _proj(xf, g, w):
    n, d = xf.shape
    c = w.shape[1]
    return pl.pallas_call(
        _proj_kernel,
        grid=(n // TM,),
        in_specs=[
            pl.BlockSpec((TM, d), lambda i: (i, 0)),
            _resident((1, d)),
            _resident(w.shape),
        ],
        out_specs=pl.BlockSpec((TM, c), lambda i: (i, 0)),
        out_shape=jax.ShapeDtypeStruct((n, c), F32),
        compiler_params=_params(1),
        name="proj",
    )(xf, g.reshape(1, d), w.astype(BF16))


def _outproj_kernel(*refs, n_in):
    o_refs = refs[:n_in]
    w_ref, g_ref, x_ref, out_ref = refs[n_in:]
    o = o_refs[0][...]
    for r in o_refs[1:]:
        o = o + r[...]
    m = jnp.dot(o.astype(BF16), w_ref[...], preferred_element_type=F32)
    out_ref[...] = x_ref[...] + _rms(m, g_ref[...])


def _outproj(o_list, w, g, xf):
    n, d = xf.shape
    tile = pl.BlockSpec((TM, d), lambda i: (i, 0))
    return pl.pallas_call(
        functools.partial(_outproj_kernel, n_in=len(o_list)),
        grid=(n // TM,),
        in_specs=[tile] * len(o_list) + [_resident(w.shape), _resident((1, d)), tile],
        out_specs=tile,
        out_shape=jax.ShapeDtypeStruct((n, d), F32),
        compiler_params=_params(1),
        name="outproj",
    )(*o_list, w.astype(BF16), g.reshape(1, d), xf)


def _rel_bucket(dist):
    dist = jnp.maximum(dist, 0)
    exact = REL_BUCKETS // 2
    d = jnp.maximum(dist, 1).astype(jnp.float32)
    log_b = exact + (jnp.log(d / exact) / math.log(REL_MAX_DIST / exact)
                     * (REL_BUCKETS - exact)).astype(jnp.int32)
    return jnp.where(dist < exact, dist, jnp.minimum(log_b, REL_BUCKETS - 1))


def _bias_tiles(rel_bias, n_delta):
    delta = jnp.arange(n_delta)[:, None, None]
    dist = delta * TK + jnp.arange(TQ)[None, :, None] - jnp.arange(TK)[None, None, :]
    bucket = _rel_bucket(dist).reshape(1, -1)
    onehot = (jnp.arange(REL_BUCKETS)[:, None] == bucket).astype(F32)
    tiles = jnp.dot(rel_bias.astype(F32).T, onehot, precision=lax.Precision.HIGHEST)
    return tiles.reshape(N_HEADS, n_delta, TQ, TK)


@dataclasses.dataclass(frozen=True)
class _AttnCfg:
    kv_width: int
    n_back: int | None = None
    window: int | None = None
    sel_blk: int | None = None
    sel_from_input: bool = False
    gate_branch: int | None = None
    sink: bool = False

    @property
    def selecting(self):
        return self.sel_blk is not None

    @property
    def moba(self):
        return self.selecting and not self.sel_from_input


def _rank_rows(val):
    width = val.shape[1]
    row = lax.broadcasted_iota(jnp.int32, (1, width, 1), 1)
    rank = jnp.zeros(val.shape, F32)
    for m in range(width):
        vm = val[:, m:m + 1, :]
        rank = rank + jnp.where(row > m, jnp.where(vm >= val, 1.0, 0.0),
                                jnp.where(vm > val, 1.0, 0.0))
    return rank


def _u_lane0(hh, n_blk):
    return hh * n_blk + (LANES if hh < SLOTS // 2 else 0)


def _attn_kernel(*refs, cfg: _AttnCfg):
    it = iter(refs)
    q_ref, k_ref, v_ref, bias_ref = next(it), next(it), next(it), next(it)
    sel_ref = next(it) if cfg.sel_from_input else None
    gate_ref = next(it) if cfg.gate_branch is not None else None
    sink_ref = next(it) if cfg.sink else None
    o_ref = next(it)
    kb_ref, vb_ref = next(it), next(it)
    dyn_refs = (next(it), next(it), next(it), next(it)) if cfg.n_back is None else None
    kmx_ref = next(it) if cfg.moba else None

    qt = pl.program_id(2)
    seq = k_ref.shape[1]
    n_tiles = seq // TK
    lane = lax.broadcasted_iota(jnp.int32, (1, HB_W), 1)
    slot_of_lane = _div_pow2(lane, HEAD_DIM)
    n_sel_blk = seq // cfg.sel_blk if cfg.selecting else 0

    def keep_u_lanes(hh, x):
        lo = _u_lane0(hh, n_sel_blk)
        return jnp.where(lane >= lo, jnp.where(lane < lo + n_sel_blk, x, 0.0), 0.0)

    @pl.when(qt == 0)
    def _prepare():
        if cfg.moba:
            kmx_ref[...] = jnp.zeros(kmx_ref.shape, F32)
        for c in range(n_tiles):
            rows = pl.ds(c * TK, TK)
            k = k_ref[0, rows, :]
            v = v_ref[0, rows, :]
            if cfg.kv_width == LANES:
                k = jnp.concatenate([k, k], axis=1)
                v = jnp.concatenate([v, v], axis=1)
            vb_ref[rows, :] = v.astype(BF16)
            if cfg.selecting:
                key_blk = _div_pow2(c * TK + lax.broadcasted_iota(jnp.int32, (TK, 1), 0),
                                    cfg.sel_blk)
                for hh in range(SLOTS):
                    onehot = (lane - _u_lane0(hh, n_sel_blk)) == key_blk
                    kb_ref[hh, rows, :] = jnp.where(
                        slot_of_lane == hh, k, jnp.where(onehot, 1.0, 0.0)).astype(BF16)
            else:
                kb_ref[0, rows, :] = k.astype(BF16)
            if cfg.moba:
                k_mean = jnp.mean(k, axis=0, keepdims=True)
                for hh in range(SLOTS):
                    kmx_ref[pl.ds(hh * n_sel_blk + c, 1), :] = jnp.where(
                        slot_of_lane == hh, k_mean, 0.0)

    q = q_ref[0]
    ij = (lax.broadcasted_iota(jnp.int32, (TQ, TK), 0)
          - lax.broadcasted_iota(jnp.int32, (TQ, TK), 1))

    selw = None
    if cfg.sel_from_input:
        sel_half = sel_ref[0]
        selw = jnp.concatenate([sel_half, sel_half], axis=1)
    elif cfg.moba:
        gate = lax.dot_general(kmx_ref[...].astype(BF16), q.astype(BF16), _NT,
                               preferred_element_type=F32)
        gate = gate.reshape(SLOTS, n_sel_blk, TQ)
        blk = lax.broadcasted_iota(jnp.int32, (1, n_sel_blk, 1), 1)
        val = jnp.where(blk < qt, gate, NEG)
        rank = _rank_rows(val)
        chosen = jnp.where(blk < qt, jnp.where(rank < MOBA_TOPK, 1.0, 0.0), 0.0)
        chosen = jnp.where(blk == qt, 1.0, chosen)
        rows = jnp.concatenate(
            [chosen.reshape(SLOTS * n_sel_blk, TQ),
             jnp.zeros((LANES - SLOTS * n_sel_blk, TQ), F32)], axis=0)
        sel_half = rows.T
        selw = jnp.concatenate([sel_half, sel_half], axis=1)

    gsig = _sigmoid(gate_ref[0]) if cfg.gate_branch is not None else None

    q_scaled = q * SCALE
    qms = []
    for hh in range(SLOTS):
        qm = jnp.where(slot_of_lane == hh, q_scaled, 0.0)
        if cfg.selecting:
            qm = qm + keep_u_lanes(hh, (selw - 1.0) * SEL_BIG)
        qms.append(qm.astype(BF16))
    q_stack = None if cfg.selecting else jnp.concatenate(qms, axis=0)

    def biased_scores(start, delta):
        if cfg.selecting:
            s = [lax.dot_general(qms[hh], kb_ref[hh, pl.ds(start, TK), :], _NT,
                                 preferred_element_type=F32) for hh in range(SLOTS)]
        else:
            s_all = lax.dot_general(q_stack, kb_ref[0, pl.ds(start, TK), :], _NT,
                                    preferred_element_type=F32)
            s = [s_all[hh * TQ:(hh + 1) * TQ] for hh in range(SLOTS)]
        return [s[hh] + bias_ref[0, hh, delta] for hh in range(SLOTS)]

    def fold(x, op):
        return op(x[:, :LANES], x[:, LANES:])

    sinks = [sink_ref[0, :, hh:hh + 1] for hh in range(SLOTS)] if cfg.sink else None

    if cfg.n_back is not None:
        tiles = [[] for _ in range(SLOTS)]
        v_rows = []
        for d in range(cfg.n_back, -1, -1):
            start = pl.multiple_of(jnp.maximum(qt - d, 0) * TK, TK)
            s = biased_scores(start, d)
            base = d * TK
            for hh in range(SLOTS):
                sh = s[hh]
                if d == 0:
                    sh = jnp.where(ij >= 0, sh, NEG)
                if cfg.window is not None and base + TQ - 1 >= cfg.window:
                    sh = jnp.where(ij < cfg.window - base, sh, NEG)
                if d > 0:
                    sh = jnp.where(qt >= d, sh, NEG)
                tiles[hh].append(sh)
            v_rows.append(vb_ref[pl.ds(start, TK), :])
        ps, ms, denoms = [], [], []
        for hh in range(SLOTS):
            s_cat = jnp.concatenate(tiles[hh], axis=1)
            m = jnp.max(s_cat, axis=1, keepdims=True)
            if cfg.sink:
                m = jnp.maximum(m, sinks[hh])
            p = jnp.exp(s_cat - m)
            ps.append(p.astype(BF16))
            denoms.append(jnp.sum(p, axis=1, keepdims=True))
            ms.append(m)
        acc = jnp.dot(jnp.concatenate(ps, axis=0), jnp.concatenate(v_rows, axis=0),
                      preferred_element_type=F32)
    else:
        sc_ref, m_ref, l_ref, acc_ref = dyn_refs
        m_ref[...] = jnp.full(m_ref.shape, NEG, F32)

        def pass1(kt, carry):
            start = pl.multiple_of(kt * TK, TK)
            s = biased_scores(start, qt - kt)
            for hh in range(SLOTS):
                sc_ref[hh, kt] = s[hh]
                m_ref[hh] = jnp.maximum(m_ref[hh], fold(s[hh], jnp.maximum))
            return carry

        lax.fori_loop(0, qt, pass1, 0)
        s = biased_scores(pl.multiple_of(qt * TK, TK), 0)
        ms = []
        for hh in range(SLOTS):
            sh = jnp.where(ij >= 0, s[hh], NEG)
            sc_ref[hh, qt] = sh
            mrun = jnp.maximum(m_ref[hh], fold(sh, jnp.maximum))
            ms.append(jnp.max(mrun, axis=1, keepdims=True))
        l_ref[...] = jnp.zeros(l_ref.shape, F32)
        acc_ref[...] = jnp.zeros(acc_ref.shape, F32)

        def pass2(kt, carry):
            start = pl.multiple_of(kt * TK, TK)
            ps = []
            for hh in range(SLOTS):
                p = jnp.exp(sc_ref[hh, kt] - ms[hh])
                l_ref[hh] = l_ref[hh] + fold(p, jnp.add)
                ps.append(p.astype(BF16))
            acc_ref[...] = acc_ref[...] + jnp.dot(
                jnp.concatenate(ps, axis=0), vb_ref[pl.ds(start, TK), :],
                preferred_element_type=F32)
            return carry

        lax.fori_loop(0, qt + 1, pass2, 0)
        denoms = [jnp.sum(l_ref[hh], axis=1, keepdims=True) for hh in range(SLOTS)]
        acc = acc_ref[...]

    out = jnp.zeros((TQ, HB_W), F32)
    for hh in range(SLOTS):
        denom = denoms[hh]
        if cfg.sink:
            denom = denom + jnp.exp(sinks[hh] - ms[hh])
        o_h = acc[hh * TQ:(hh + 1) * TQ] / jnp.maximum(denom, TINY)
        if cfg.gate_branch is not None:
            col = cfg.gate_branch * SLOTS + hh
            o_h = o_h * gsig[:, col:col + 1]
        out = out + jnp.where(slot_of_lane == hh, o_h, 0.0)
    o_ref[0] = out


def _attention(cfg, proj, bias, *, q_blk0, k_blk, v_blk, sel=None, gate_blk0=None, sinks=None):
    bsz, seq, _ = proj.shape
    kw = cfg.kv_width
    n_delta = bias.shape[2]
    in_specs = [
        pl.BlockSpec((1, TQ, HB_W), lambda hb, b, qt: (b, qt, q_blk0 + hb)),
        pl.BlockSpec((1, seq, kw), lambda hb, b, qt: (b, 0, k_blk(hb))),
        pl.BlockSpec((1, seq, kw), lambda hb, b, qt: (b, 0, v_blk(hb))),
        pl.BlockSpec((1, SLOTS, n_delta, TQ, TK), lambda hb, b, qt: (hb, 0, 0, 0, 0)),
    ]
    args = [proj, proj, proj, bias]
    if cfg.sel_from_input:
        in_specs.append(pl.BlockSpec((1, TQ, LANES), lambda hb, b, qt: (b, qt, 0)))
        args.append(sel)
    if cfg.gate_branch is not None:
        in_specs.append(pl.BlockSpec((1, TQ, LANES), lambda hb, b, qt: (b, qt, gate_blk0 + hb)))
        args.append(proj)
    if cfg.sink:
        in_specs.append(pl.BlockSpec((1, 1, LANES), lambda hb, b, qt: (hb, 0, 0)))
        args.append(sinks)
    scratch = [
        pltpu.VMEM((SLOTS if cfg.selecting else 1, seq, HB_W), BF16),
        pltpu.VMEM((seq, HB_W), BF16),
    ]
    if cfg.n_back is None:
        scratch += [
            pltpu.VMEM((SLOTS, seq // TK, TQ, TK), F32),
            pltpu.VMEM((SLOTS, TQ, LANES), F32),
            pltpu.VMEM((SLOTS, TQ, LANES), F32),
            pltpu.VMEM((SLOTS * TQ, HB_W), F32),
        ]
    if cfg.moba:
        assert SLOTS * (seq // cfg.sel_blk) <= LANES
        scratch.append(pltpu.VMEM((SLOTS * (seq // cfg.sel_blk), HB_W), F32))
    return pl.pallas_call(
        functools.partial(_attn_kernel, cfg=cfg),
        grid=(N_HB, bsz, seq // TQ),
        in_specs=in_specs,
        out_specs=pl.BlockSpec((1, TQ, HB_W), lambda hb, b, qt: (b, qt, hb)),
        out_shape=jax.ShapeDtypeStruct((bsz, seq, D_MODEL), F32),
        scratch_shapes=scratch,
        compiler_params=_params(3),
        name="attention",
    )(*args)


def _gelu_tanh(x):
    cdf = 0.5 * (1.0 + jnp.tanh(math.sqrt(2.0 / math.pi) * (x + 0.044715 * (x * x * x))))
    return x * cdf


def _compress_kernel(kc_lo_ref, kc_hi_ref, vc_lo_ref, vc_hi_ref, pos_ref, w1_ref, w2_ref,
                     kcb_ref, vcb_ref):
    half = CMP_LEN // 2
    n_chunk = kc_lo_ref.shape[1] // CMP_STRIDE
    flat_lane = lax.broadcasted_iota(jnp.int32, (1, half * HB_W), 1)
    group_of_lane = _div_pow2(_mod_pow2(flat_lane, HB_W), HEAD_DIM)
    sources = ((kc_lo_ref, kc_hi_ref, kcb_ref), (vc_lo_ref, vc_hi_ref, vcb_ref))
    for kind, (lo_ref, hi_ref, dst_ref) in enumerate(sources):
        chunks = [jnp.concatenate(
            [lo_ref[0, pl.ds(l, n_chunk, stride=CMP_STRIDE), :],
             hi_ref[0, pl.ds(l, n_chunk, stride=CMP_STRIDE), :]], axis=1) for l in range(half)]
        parts = []
        for h in range(2):
            pieces = [chunks[l] + pos_ref[kind, pl.ds(h * half + l, 1), :] for l in range(half)]
            flat = jnp.concatenate(pieces, axis=1)
            stacked = jnp.concatenate(
                [jnp.where(group_of_lane == g, flat, 0.0)
                 for g in range(B_KV_GROUPS)], axis=0).astype(BF16)
            parts.append(jnp.dot(stacked, w1_ref[kind, h], preferred_element_type=F32))
        rows = B_KV_GROUPS * n_chunk
        hidden = parts[0] + pltpu.roll(parts[1], rows - 1, axis=0)
        act = _gelu_tanh(hidden).astype(BF16)
        out = None
        for g in range(B_KV_GROUPS):
            o = jnp.dot(act[g * n_chunk:(g + 1) * n_chunk], w2_ref[kind, g],
                        preferred_element_type=F32)
            out = o if out is None else out + o
        dst_ref[0] = out


def _compress(proj, kc_blk, vc_blk, pos, w1, w2):
    bsz, seq, _ = proj.shape
    n_chunk = seq // CMP_STRIDE
    out = jax.ShapeDtypeStruct((bsz, n_chunk, HB_W), F32)
    blk = pl.BlockSpec((1, n_chunk, HB_W), lambda b: (b, 0, 0))
    return pl.pallas_call(
        _compress_kernel,
        grid=(bsz,),
        in_specs=[
            pl.BlockSpec((1, seq, LANES), lambda b: (b, 0, 2 * kc_blk)),
            pl.BlockSpec((1, seq, LANES), lambda b: (b, 0, 2 * kc_blk + 1)),
            pl.BlockSpec((1, seq, LANES), lambda b: (b, 0, 2 * vc_blk)),
            pl.BlockSpec((1, seq, LANES), lambda b: (b, 0, 2 * vc_blk + 1)),
            _resident(pos.shape),
            _resident(w1.shape),
            _resident(w2.shape),
        ],
        out_specs=[blk, blk],
        out_shape=[out, out],
        compiler_params=_params(1),
        name="nsa_compress",
    )(proj, proj, proj, proj, pos, w1, w2)


def _cmp_kernel(q_ref, kcb_ref, vcb_ref, gate_ref, ov_ref, o_ref, sel_ref):
    qt = pl.program_id(1)
    n_cmp_pad = kcb_ref.shape[1]
    n_slc = LANES // B_KV_GROUPS
    lane = lax.broadcasted_iota(jnp.int32, (1, HB_W), 1)
    slot_of_lane = _div_pow2(lane, HEAD_DIM)
    t = qt * TQ + lax.broadcasted_iota(jnp.int32, (TQ, 1), 0)
    cmp_end = (lax.broadcasted_iota(jnp.int32, (1, n_cmp_pad), 1) * CMP_STRIDE + CMP_LEN - 1)
    ok = cmp_end <= t
    kc = kcb_ref[0].astype(BF16)
    vc = vcb_ref[0].astype(BF16)
    gsig = _sigmoid(gate_ref[0])
    p_group = [jnp.zeros((TQ, n_cmp_pad), F32) for _ in range(B_KV_GROUPS)]
    for r in range(N_HB):
        q_r = q_ref[0, :, r * HB_W:(r + 1) * HB_W]
        out = jnp.zeros((TQ, HB_W), F32)
        for g in range(B_KV_GROUPS):
            qm = jnp.where(slot_of_lane == g, q_r, 0.0).astype(BF16)
            s = lax.dot_general(qm, kc, _NT, preferred_element_type=F32) * SCALE
            s = jnp.where(ok, s, NEG)
            m = jnp.max(s, axis=1, keepdims=True)
            e = jnp.where(ok, jnp.exp(s - m), 0.0)
            p = e / jnp.maximum(jnp.sum(e, axis=1, keepdims=True), TINY)
            p_group[g] = p_group[g] + p
            o = jnp.dot(p.astype(BF16), vc, preferred_element_type=F32)
            col = r * LANES + g
            out = out + jnp.where(slot_of_lane == g, o * gsig[:, col:col + 1], 0.0)
        o_ref[0, :, r * HB_W:(r + 1) * HB_W] = out

    imp = None
    for g in range(B_KV_GROUPS):
        part = jnp.dot(p_group[g], ov_ref[g], precision=lax.Precision.HIGHEST,
                       preferred_element_type=F32)
        imp = part if imp is None else imp + part
    imp = imp.T.reshape(B_KV_GROUPS, n_slc, TQ)
    j = lax.broadcasted_iota(jnp.int32, (1, n_slc, 1), 1)
    cur = _div_pow2(qt * TQ + lax.broadcasted_iota(jnp.int32, (1, 1, TQ), 2), SLC_BLK)
    val = jnp.where(j == cur - 1, -NEG, imp)
    val = jnp.where(j == cur, -NEG, val)
    val = jnp.where(j == 0, -NEG, val)
    val = jnp.where(j > cur, NEG, val)
    chosen = jnp.where(_rank_rows(val) < SLC_TOPK, 1.0, 0.0)
    sel_ref[0] = chosen.reshape(LANES, TQ).T


def _cmp_attention(proj, kcb, vcb, gate_blk, ov):
    bsz, seq, _ = proj.shape
    n_cmp_pad = kcb.shape[1]
    kv_blk = pl.BlockSpec((1, n_cmp_pad, HB_W), lambda b, qt: (b, 0, 0))
    return pl.pallas_call(
        _cmp_kernel,
        grid=(bsz, seq // TQ),
        in_specs=[
            pl.BlockSpec((1, TQ, D_MODEL), lambda b, qt: (b, qt, 0)),
            kv_blk, kv_blk,
            pl.BlockSpec((1, TQ, N_HB * LANES), lambda b, qt: (b, qt, gate_blk)),
            _resident(ov.shape),
        ],
        out_specs=[
            pl.BlockSpec((1, TQ, D_MODEL), lambda b, qt: (b, qt, 0)),
            pl.BlockSpec((1, TQ, LANES), lambda b, qt: (b, qt, 0)),
        ],
        out_shape=[
            jax.ShapeDtypeStruct((bsz, seq, D_MODEL), F32),
            jax.ShapeDtypeStruct((bsz, seq, LANES), F32),
        ],
        compiler_params=_params(2),
        name="nsa_cmp_select",
    )(proj, kcb, vcb, proj, ov)


def _interleave_heads(w, n_groups, axis):
    r = N_HEADS // n_groups
    shape = w.shape
    w = w.reshape(shape[:axis] + (n_groups, r) + shape[axis + 1:])
    w = jnp.swapaxes(w, axis, axis + 1)
    return w.reshape(shape)


def _mixer_a(xf, bsz, seq, g_pre, w_in, w_out, sinks, g_post, bias_nat):
    qw = N_HEADS * HEAD_DIM
    w_q = _interleave_heads(w_in[:, :qw].reshape(D_MODEL, N_HEADS, HEAD_DIM), A_KV_HEADS, 1)
    w = jnp.concatenate([w_q.reshape(D_MODEL, qw), w_in[:, qw:]], axis=1)
    proj = _proj(xf, g_pre, w).reshape(bsz, seq, -1)
    n_back = -(-(A_WINDOW - 1) // TK)
    bias = _interleave_heads(bias_nat[:, :n_back + 1], A_KV_HEADS, 0)
    bias = bias.reshape((N_HB, SLOTS) + bias.shape[1:])
    sink_rows = _interleave_heads(sinks.astype(F32), A_KV_HEADS, 0).reshape(N_HB, 1, SLOTS)
    sink_rows = jnp.pad(sink_rows, ((0, 0), (0, 0), (0, LANES - SLOTS)))
    cfg = _AttnCfg(kv_width=LANES, n_back=n_back, window=A_WINDOW, sink=True)
    k_blk0 = qw // LANES
    o = _attention(cfg, proj, bias, q_blk0=0, k_blk=lambda hb: k_blk0,
                   v_blk=lambda hb: k_blk0 + 1, sinks=sink_rows)
    w_o = _interleave_heads(w_out.reshape(N_HEADS, HEAD_DIM, D_MODEL), A_KV_HEADS, 0)
    return _outproj([o.reshape(bsz * seq, -1)], w_o.reshape(qw, D_MODEL), g_post, xf)


def _nsa_constants(seq):
    n_chunk = seq // CMP_STRIDE
    n_cmp = (seq - CMP_LEN) // CMP_STRIDE + 1
    n_slc = seq // SLC_BLK
    c_start = np.arange(n_cmp)[:, None] * CMP_STRIDE
    s_start = np.arange(n_slc)[None, :] * SLC_BLK
    overlap = ((c_start < s_start + SLC_BLK) & (c_start + CMP_LEN > s_start)).astype(np.float32)
    ov = np.zeros((B_KV_GROUPS, n_chunk, LANES), np.float32)
    for g in range(B_KV_GROUPS):
        ov[g, :n_cmp, g * n_slc:(g + 1) * n_slc] = overlap
    return jnp.asarray(ov)


def _mixer_b(xf, bsz, seq, g_pre, w_in, w_out, cmp_pos, cmp_w1, cmp_w2, g_post, bias_nat):
    assert seq // SLC_BLK * B_KV_GROUPS == LANES and seq // CMP_STRIDE == LANES
    grp, dh = B_KV_GROUPS, HEAD_DIM
    qw, kvw = N_HEADS * dh, B_KV_GROUPS * dh
    w_q = _interleave_heads(w_in[:, :qw].reshape(D_MODEL, N_HEADS, dh), grp, 1)
    w_g = w_in[:, qw + 6 * kvw:].reshape(D_MODEL, grp, N_HB, 3)
    w_g = w_g.transpose(0, 2, 3, 1).reshape(D_MODEL, N_HB, 3 * grp)
    w_g = jnp.pad(w_g, ((0, 0), (0, 0), (0, LANES - 3 * grp))).reshape(D_MODEL, N_HB * LANES)
    w = jnp.concatenate([w_q.reshape(D_MODEL, qw), w_in[:, qw:qw + 6 * kvw], w_g], axis=1)
    proj = _proj(xf, g_pre, w).reshape(bsz, seq, -1)
    kv_blk0 = qw // HB_W
    gate_col0 = qw + 6 * kvw

    half = CMP_LEN // 2
    pos = jnp.tile(cmp_pos.astype(F32), (1, 1, grp))
    w1 = cmp_w1.reshape(2, 2, half, 1, dh, CMP_HIDDEN)
    w1 = jnp.broadcast_to(w1, (2, 2, half, grp, dh, CMP_HIDDEN))
    w1 = w1.reshape(2, 2, half * HB_W, CMP_HIDDEN).astype(BF16)
    w2 = jnp.zeros((2, grp, CMP_HIDDEN, grp, dh), F32)
    for g in range(grp):
        w2 = w2.at[:, g, :, g, :].set(cmp_w2)
    w2 = w2.reshape(2, grp, CMP_HIDDEN, HB_W).astype(BF16)
    kcb, vcb = _compress(proj, kv_blk0, kv_blk0 + 1, pos, w1, w2)

    ov = _nsa_constants(seq)
    o_cmp, sel = _cmp_attention(proj, kcb, vcb, gate_col0 // (N_HB * LANES), ov)

    bias = _interleave_heads(bias_nat, grp, 0)
    bias = bias.reshape((N_HB, SLOTS) + bias.shape[1:])
    gate_blk0 = gate_col0 // LANES
    cfg_slc = _AttnCfg(kv_width=HB_W, sel_blk=SLC_BLK, sel_from_input=True, gate_branch=1)
    o_slc = _attention(cfg_slc, proj, bias, q_blk0=0, k_blk=lambda hb: kv_blk0 + 2,
                       v_blk=lambda hb: kv_blk0 + 3, sel=sel, gate_blk0=gate_blk0)
    n_back = -(-(B_WINDOW - 1) // TK)
    cfg_win = _AttnCfg(kv_width=HB_W, n_back=n_back, window=B_WINDOW, gate_branch=2)
    o_win = _attention(cfg_win, proj, bias[:, :, :n_back + 1], q_blk0=0,
                       k_blk=lambda hb: kv_blk0 + 4, v_blk=lambda hb: kv_blk0 + 5,
                       gate_blk0=gate_blk0)
    w_o = _interleave_heads(w_out.reshape(N_HEADS, dh, D_MODEL), grp, 0).reshape(qw, D_MODEL)
    n = bsz * seq
    return _outproj([o_cmp.reshape(n, -1), o_slc.reshape(n, -1), o_win.reshape(n, -1)],
                    w_o, g_post, xf)


def _mixer_c(xf, bsz, seq, g_pre, w_in, w_out, g_post, bias_nat):
    assert seq % MOBA_BLK == 0 and MOBA_BLK == TK and seq // MOBA_BLK <= HEAD_DIM
    proj = _proj(xf, g_pre, w_in).reshape(bsz, seq, -1)
    bias = bias_nat.reshape((N_HB, SLOTS) + bias_nat.shape[1:])
    cfg = _AttnCfg(kv_width=HB_W, sel_blk=MOBA_BLK)
    o = _attention(cfg, proj, bias, q_blk0=0, k_blk=lambda hb: N_HB + hb,
                   v_blk=lambda hb: 2 * N_HB + hb)
    return _outproj([o.reshape(bsz * seq, -1)], w_out, g_post, xf)


def kernel(x, rel_bias, norm_g, ffn_w_in, ffn_w_out, a_w_in, a_w_out, a_sinks, b_w_in, b_w_out,
           b_cmp_pos, b_cmp_w1, b_cmp_w2, c_w_in, c_w_out):
    bsz, seq, d = x.shape
    assert d == D_MODEL and seq % TQ == 0 and (bsz * seq) % TM == 0
    depth = norm_g.shape[0]
    bias_nat = _bias_tiles(rel_bias, seq // TK)
    xf = x.reshape(bsz * seq, d)
    for i in range(depth):
        g = norm_g[i]
        xf = _ffn(xf, g[0], ffn_w_in[i, 0], ffn_w_out[i, 0], g[1])
        kind, j = i % N_MIXERS, i // N_MIXERS
        if kind == 0:
            xf = _mixer_a(xf, bsz, seq, g[2], a_w_in[j], a_w_out[j], a_sinks[j], g[3], bias_nat)
        elif kind == 1:
            xf = _mixer_b(xf, bsz, seq, g[2], b_w_in[j], b_w_out[j], b_cmp_pos[j], b_cmp_w1[j],
                          b_cmp_w2[j], g[3], bias_nat)
        else:
            xf = _mixer_c(xf, bsz, seq, g[2], c_w_in[j], c_w_out[j], g[3], bias_nat)
        xf = _ffn(xf, g[4], ffn_w_in[i, 1], ffn_w_out[i, 1], g[5])
    return xf.reshape(bsz, seq, d)
```

```python
import dataclasses
import functools
import math

import numpy as np
import jax
import jax.numpy as jnp
from jax import lax
from jax.experimental import pallas as pl
from jax.experimental.pallas import tpu as pltpu

F32 = jnp.float32
BF16 = jnp.bfloat16

D_MODEL = 1024
HEAD_DIM = 64
N_HEADS = D_MODEL // HEAD_DIM
D_FF = 2816
NORM_EPS = 1e-6
N_MIXERS = 3
REL_BUCKETS = 32
REL_MAX_DIST = 1024
A_KV_HEADS = 2
A_WINDOW = 128
B_KV_GROUPS = 4
CMP_LEN = 32
CMP_STRIDE = 16
CMP_HIDDEN = 256
SLC_BLK = 64
SLC_TOPK = 16
B_WINDOW = 512
MOBA_BLK = 256
MOBA_TOPK = 3
NEG = -1e30
TINY = 1e-30

LANES = 128
MXU_DIM = 256
VMEM_LIMIT_BYTES = 56 * 1024 * 1024

SLOTS = 4
HB_W = SLOTS * HEAD_DIM
N_HB = N_HEADS // SLOTS
TQ = 256
TK = 256
TM = 512
FF_CHUNKS = ((0, 1536), (1536, D_FF))
SEL_BIG = float(2 ** 30)
SCALE = HEAD_DIM ** -0.5

_NT = (((1,), (1,)), ((), ()))


def _params(n_axes, flags=None):
    return pltpu.CompilerParams(
        dimension_semantics=("arbitrary",) * n_axes,
        vmem_limit_bytes=VMEM_LIMIT_BYTES,
        flags=flags)


def _rms(x, g):
    return x * lax.rsqrt(jnp.mean(x * x, axis=-1, keepdims=True) + NORM_EPS) * g


def _sigmoid(x):
    return 1.0 / (1.0 + jnp.exp(-x))


def _div_pow2(x, d):
    assert d & (d - 1) == 0
    return jnp.right_shift(x, d.bit_length() - 1)


def _mod_pow2(x, d):
    assert d & (d - 1) == 0
    return jnp.bitwise_and(x, d - 1)


def _resident(shape):
    zeros = (0,) * len(shape)
    return pl.BlockSpec(shape, lambda *_: zeros, pipeline_mode=pl.Buffered(1))


def _ffn_kernel(x_ref, gpre_ref, win_ref, wout_ref, gpost_ref, o_ref):
    x = x_ref[...]
    hn = _rms(x, gpre_ref[...]).astype(BF16)
    y = None
    for lo, hi in FF_CHUNKS:
        gate = jnp.dot(hn, win_ref[:, lo:hi], preferred_element_type=F32)
        up = jnp.dot(hn, win_ref[:, D_FF + lo:D_FF + hi], preferred_element_type=F32)
        act = (gate * _sigmoid(gate) * up).astype(BF16)
        part = jnp.dot(act, wout_ref[lo:hi, :], preferred_element_type=F32)
        y = part if y is None else y + part
    o_ref[...] = x + 0.5 * _rms(y, gpost_ref[...])


def _ffn(xf, g_pre, w_in, w_out, g_post):
    n, d = xf.shape
    return pl.pallas_call(
        _ffn_kernel,
        grid=(n // TM,),
        in_specs=[
            pl.BlockSpec((TM, d), lambda i: (i, 0)),
            _resident((1, d)),
            _resident(w_in.shape),
            _resident(w_out.shape),
            _resident((1, d)),
        ],
        out_specs=pl.BlockSpec((TM, d), lambda i: (i, 0)),
        out_shape=jax.ShapeDtypeStruct((n, d), F32),
        compiler_params=_params(1),
        name="ffn",
    )(xf, g_pre.reshape(1, d), w_in.astype(BF16), w_out.astype(BF16), g_post.reshape(1, d))


def _proj_kernel(x_ref, g_ref, w_ref, o_ref):
    hn = _rms(x_ref[...], g_ref[...]).astype(BF16)
    o_ref[...] = jnp.dot(hn, w_ref[...], preferred_element_type=F32)


def _proj(xf, g, w):
    n, d = xf.shape
    c = w.shape[1]
    return pl.pallas_call(
        _proj_kernel,
        grid=(n // TM,),
        in_specs=[
            pl.BlockSpec((TM, d), lambda i: (i, 0)),
            _resident((1, d)),
            _resident(w.shape),
        ],
        out_specs=pl.BlockSpec((TM, c), lambda i: (i, 0)),
        out_shape=jax.ShapeDtypeStruct((n, c), F32),
        compiler_params=_params(1),
        name="proj",
    )(xf, g.reshape(1, d), w.astype(BF16))


def _outproj_kernel(*refs, n_in):
    o_refs = refs[:n_in]
    w_ref, g_ref, x_ref, out_ref = refs[n_in:]
    o = o_refs[0][...]
    for r in o_refs[1:]:
        o = o + r[...]
    m = jnp.dot(o.astype(BF16), w_ref[...], preferred_element_type=F32)
    out_ref[...] = x_ref[...] + _rms(m, g_ref[...])


def _outproj(o_list, w, g, xf):
    n, d = xf.shape
    tile = pl.BlockSpec((TM, d), lambda i: (i, 0))
    return pl.pallas_call(
        functools.partial(_outproj_kernel, n_in=len(o_list)),
        grid=(n // TM,),
        in_specs=[tile] * len(o_list) + [_resident(w.shape), _resident((1, d)), tile],
        out_specs=tile,
        out_shape=jax.ShapeDtypeStruct((n, d), F32),
        compiler_params=_params(1),
        name="outproj",
    )(*o_list, w.astype(BF16), g.reshape(1, d), xf)


def _rel_bucket(dist):
    dist = jnp.maximum(dist, 0)
    exact = REL_BUCKETS // 2
    d = jnp.maximum(dist, 1).astype(jnp.float32)
    log_b = exact + (jnp.log(d / exact) / math.log(REL_MAX_DIST / exact)
                     * (REL_BUCKETS - exact)).astype(jnp.int32)
    return jnp.where(dist < exact, dist, jnp.minimum(log_b, REL_BUCKETS - 1))


def _bias_tiles(rel_bias, n_delta):
    delta = jnp.arange(n_delta)[:, None, None]
    dist = delta * TK + jnp.arange(TQ)[None, :, None] - jnp.arange(TK)[None, None, :]
    bucket = _rel_bucket(dist).reshape(1, -1)
    onehot = (jnp.arange(REL_BUCKETS)[:, None] == bucket).astype(F32)
    tiles = jnp.dot(rel_bias.astype(F32).T, onehot, precision=lax.Precision.HIGHEST)
    return tiles.reshape(N_HEADS, n_delta, TQ, TK)


def _block_bias(bias_nat, n_groups, n_delta, window=None):
    delta = np.arange(n_delta)[:, None, None]
    dist = delta * TK + np.arange(TQ)[None, :, None] - np.arange(TK)[None, None, :]
    visible = dist >= 0
    if window is not None:
        visible &= dist < window
    bias = bias_nat[:, :n_delta]
    if n_groups > 1:
        bias = _interleave_heads(bias, n_groups, 0)
    bias = jnp.where(visible, bias, NEG)
    return bias.reshape((N_HB, SLOTS) + bias.shape[1:])


@dataclasses.dataclass(frozen=True)
class _AttnCfg:
    kv_width: int
    n_back: int | None = None
    sel_blk: int | None = None
    sel_from_input: bool = False
    gate_branch: int | None = None
    sink: bool = False

    @property
    def selecting(self):
        return self.sel_blk is not None

    @property
    def moba(self):
        return self.selecting and not self.sel_from_input


def _rank_rows(val):
    width = val.shape[1]
    row = lax.broadcasted_iota(jnp.int32, (1, width, 1), 1)
    rank = jnp.zeros(val.shape, F32)
    for m in range(width):
        vm = val[:, m:m + 1, :]
        rank = rank + jnp.where(row > m, jnp.where(vm >= val, 1.0, 0.0),
                                jnp.where(vm > val, 1.0, 0.0))
    return rank


def _u_lane0(hh, n_blk):
    return hh * n_blk + (LANES if hh < SLOTS // 2 else 0)


def _attn_kernel(*refs, cfg: _AttnCfg):
    it = iter(refs)
    q_ref, k_ref, v_ref, bias_ref = next(it), next(it), next(it), next(it)
    sel_ref = next(it) if cfg.sel_from_input else None
    gate_ref = next(it) if cfg.gate_branch is not None else None
    sink_ref = next(it) if cfg.sink else None
    o_ref = next(it)
    kb_ref, vb_ref = next(it), next(it)
    sc_ref, m_ref, l_ref, acc_ref = next(it), next(it), next(it), next(it)
    kmx_ref = next(it) if cfg.moba else None

    qt = pl.program_id(2)
    seq = k_ref.shape[1]
    n_tiles = seq // TK
    lane = lax.broadcasted_iota(jnp.int32, (1, HB_W), 1)
    slot_of_lane = _div_pow2(lane, HEAD_DIM)
    n_sel_blk = seq // cfg.sel_blk if cfg.selecting else 0

    def keep_u_lanes(hh, x):
        lo = _u_lane0(hh, n_sel_blk)
        return jnp.where(lane >= lo, jnp.where(lane < lo + n_sel_blk, x, 0.0), 0.0)

    @pl.when(qt == 0)
    def _prepare():
        if cfg.moba:
            kmx_ref[...] = jnp.zeros(kmx_ref.shape, F32)
        for c in range(n_tiles):
            rows = pl.ds(c * TK, TK)
            k = k_ref[0, rows, :]
            v = v_ref[0, rows, :]
            if cfg.kv_width == LANES:
                k = jnp.concatenate([k, k], axis=1)
                v = jnp.concatenate([v, v], axis=1)
            vb_ref[rows, :] = v.astype(BF16)
            if cfg.selecting:
                key_blk = _div_pow2(c * TK + lax.broadcasted_iota(jnp.int32, (TK, 1), 0),
                                    cfg.sel_blk)
                for hh in range(SLOTS):
                    onehot = (lane - _u_lane0(hh, n_sel_blk)) == key_blk
                    kb_ref[hh, rows, :] = jnp.where(
                        slot_of_lane == hh, k, jnp.where(onehot, 1.0, 0.0)).astype(BF16)
            else:
                kb_ref[0, rows, :] = k.astype(BF16)
            if cfg.moba:
                k_mean = jnp.mean(k, axis=0, keepdims=True)
                for hh in range(SLOTS):
                    kmx_ref[pl.ds(hh * n_sel_blk + c, 1), :] = jnp.where(
                        slot_of_lane == hh, k_mean, 0.0)

    q = q_ref[0]

    selw = None
    if cfg.sel_from_input:
        sel_half = sel_ref[0]
        selw = jnp.concatenate([sel_half, sel_half], axis=1)
    elif cfg.moba:
        gate = lax.dot_general(kmx_ref[...].astype(BF16), q.astype(BF16), _NT,
                               preferred_element_type=F32)
        gate = gate.reshape(SLOTS, n_sel_blk, TQ)
        blk = lax.broadcasted_iota(jnp.int32, (1, n_sel_blk, 1), 1)
        val = jnp.where(blk < qt, gate, NEG)
        rank = _rank_rows(val)
        chosen = jnp.where(blk < qt, jnp.where(rank < MOBA_TOPK, 1.0, 0.0), 0.0)
        chosen = jnp.where(blk == qt, 1.0, chosen)
        rows = jnp.concatenate(
            [chosen.reshape(SLOTS * n_sel_blk, TQ),
             jnp.zeros((LANES - SLOTS * n_sel_blk, TQ), F32)], axis=0)
        sel_half = rows.T
        selw = jnp.concatenate([sel_half, sel_half], axis=1)

    gsig = _sigmoid(gate_ref[0]) if cfg.gate_branch is not None else None

    q_scaled = q * SCALE
    qms = []
    for hh in range(SLOTS):
        qm = jnp.where(slot_of_lane == hh, q_scaled, 0.0)
        if cfg.selecting:
            qm = qm + keep_u_lanes(hh, (selw - 1.0) * SEL_BIG)
        qms.append(qm.astype(BF16))
    q_stack = None if cfg.selecting else jnp.concatenate(qms, axis=0)

    def fold(x, op):
        out = x[:, :LANES]
        for c in range(1, x.shape[1] // LANES):
            out = op(out, x[:, c * LANES:(c + 1) * LANES])
        return out

    sinks = [sink_ref[0, :, hh:hh + 1] for hh in range(SLOTS)] if cfg.sink else None

    lo = jnp.maximum(qt - cfg.n_back, 0) if cfg.n_back is not None else 0
    n_odd = jnp.bitwise_and(qt + 1 - lo, 1)
    n_pairs = jnp.right_shift(qt + 1 - lo, 1)
    m_ref[...] = jnp.full(m_ref.shape, NEG, F32)
    l_ref[...] = jnp.zeros(l_ref.shape, F32)
    acc_ref[...] = jnp.zeros(acc_ref.shape, F32)

    def pass1(kt, width):
        rows = pl.ds(pl.multiple_of(kt * TK, TK), width * TK)
        if cfg.selecting:
            s = [lax.dot_general(qms[hh], kb_ref[hh, rows, :], _NT,
                                 preferred_element_type=F32) for hh in range(SLOTS)]
        else:
            s_all = lax.dot_general(q_stack, kb_ref[0, rows, :], _NT,
                                    preferred_element_type=F32)
            s = [s_all[hh * TQ:(hh + 1) * TQ] for hh in range(SLOTS)]
        for hh in range(SLOTS):
            tiles = [s[hh][:, w * TK:(w + 1) * TK] + bias_ref[0, hh, qt - kt - w]
                     for w in range(width)]
            for w in range(width):
                sc_ref[hh, kt - lo + w] = tiles[w]
            m_ref[hh] = jnp.maximum(
                m_ref[hh], fold(jnp.concatenate(tiles, axis=1), jnp.maximum))

    def pass2(kt, width, ms):
        rows = pl.ds(pl.multiple_of(kt * TK, TK), width * TK)
        ps = []
        for hh in range(SLOTS):
            sc = jnp.concatenate([sc_ref[hh, kt - lo + w] for w in range(width)], axis=1)
            p = jnp.exp(sc - ms[hh])
            l_ref[hh] = l_ref[hh] + fold(p, jnp.add)
            ps.append(p.astype(BF16))
        acc_ref[...] = acc_ref[...] + jnp.dot(
            jnp.concatenate(ps, axis=0), vb_ref[rows, :], preferred_element_type=F32)

    def over_tiles(body):
        @pl.when(n_odd == 1)
        def _single():
            body(lo, 1)

        def pair(i, carry):
            body(lo + n_odd + 2 * i, 2)
            return carry

        lax.fori_loop(0, n_pairs, pair, 0)

    over_tiles(pass1)
    ms = []
    for hh in range(SLOTS):
        m = jnp.max(m_ref[hh], axis=1, keepdims=True)
        ms.append(jnp.maximum(m, sinks[hh]) if cfg.sink else m)
    over_tiles(functools.partial(pass2, ms=ms))
    denoms = [jnp.sum(l_ref[hh], axis=1, keepdims=True) for hh in range(SLOTS)]
    acc = acc_ref[...]

    out = jnp.zeros((TQ, HB_W), F32)
    for hh in range(SLOTS):
        denom = denoms[hh]
        if cfg.sink:
            denom = denom + jnp.exp(sinks[hh] - ms[hh])
        o_h = acc[hh * TQ:(hh + 1) * TQ] / jnp.maximum(denom, TINY)
        if cfg.gate_branch is not None:
            col = cfg.gate_branch * SLOTS + hh
            o_h = o_h * gsig[:, col:col + 1]
        out = out + jnp.where(slot_of_lane == hh, o_h, 0.0)
    o_ref[0] = out


def _attention(cfg, proj, bias, *, q_blk0, k_blk, v_blk, sel=None, gate_blk0=None, sinks=None):
    bsz, seq, _ = proj.shape
    kw = cfg.kv_width
    n_delta = bias.shape[2]
    in_specs = [
        pl.BlockSpec((1, TQ, HB_W), lambda hb, b, qt: (b, qt, q_blk0 + hb)),
        pl.BlockSpec((1, seq, kw), lambda hb, b, qt: (b, 0, k_blk(hb))),
        pl.BlockSpec((1, seq, kw), lambda hb, b, qt: (b, 0, v_blk(hb))),
        pl.BlockSpec((1, SLOTS, n_delta, TQ, TK), lambda hb, b, qt: (hb, 0, 0, 0, 0)),
    ]
    args = [proj, proj, proj, bias]
    if cfg.sel_from_input:
        in_specs.append(pl.BlockSpec((1, TQ, LANES), lambda hb, b, qt: (b, qt, 0)))
        args.append(sel)
    if cfg.gate_branch is not None:
        in_specs.append(pl.BlockSpec((1, TQ, LANES), lambda hb, b, qt: (b, qt, gate_blk0 + hb)))
        args.append(proj)
    if cfg.sink:
        in_specs.append(pl.BlockSpec((1, 1, LANES), lambda hb, b, qt: (hb, 0, 0)))
        args.append(sinks)
    scratch = [
        pltpu.VMEM((SLOTS if cfg.selecting else 1, seq, HB_W), BF16),
        pltpu.VMEM((seq, HB_W), BF16),
        pltpu.VMEM((SLOTS, n_delta, TQ, TK), F32),
        pltpu.VMEM((SLOTS, TQ, LANES), F32),
        pltpu.VMEM((SLOTS, TQ, LANES), F32),
        pltpu.VMEM((SLOTS * TQ, HB_W), F32),
    ]
    if cfg.moba:
        assert SLOTS * (seq // cfg.sel_blk) <= LANES
        scratch.append(pltpu.VMEM((SLOTS * (seq // cfg.sel_blk), HB_W), F32))
    return pl.pallas_call(
        functools.partial(_attn_kernel, cfg=cfg),
        grid=(N_HB, bsz, seq // TQ),
        in_specs=in_specs,
        out_specs=pl.BlockSpec((1, TQ, HB_W), lambda hb, b, qt: (b, qt, hb)),
        out_shape=jax.ShapeDtypeStruct((bsz, seq, D_MODEL), F32),
        scratch_shapes=scratch,
        compiler_params=_params(3),
        name="attention",
    )(*args)


def _gelu_tanh(x):
    cdf = 0.5 * (1.0 + jnp.tanh(math.sqrt(2.0 / math.pi) * (x + 0.044715 * (x * x * x))))
    return x * cdf


def _compress_kernel(kc_lo_ref, kc_hi_ref, vc_lo_ref, vc_hi_ref, pos_ref, w1_ref, w2_ref,
                     kcb_ref, vcb_ref):
    half = CMP_LEN // 2
    n_chunk = kc_lo_ref.shape[1] // CMP_STRIDE
    flat_lane = lax.broadcasted_iota(jnp.int32, (1, half * HB_W), 1)
    group_of_lane = _div_pow2(_mod_pow2(flat_lane, HB_W), HEAD_DIM)
    sources = ((kc_lo_ref, kc_hi_ref, kcb_ref), (vc_lo_ref, vc_hi_ref, vcb_ref))
    for kind, (lo_ref, hi_ref, dst_ref) in enumerate(sources):
        chunks = [jnp.concatenate(
            [lo_ref[0, pl.ds(l, n_chunk, stride=CMP_STRIDE), :],
             hi_ref[0, pl.ds(l, n_chunk, stride=CMP_STRIDE), :]], axis=1) for l in range(half)]
        parts = []
        for h in range(2):
            pieces = [chunks[l] + pos_ref[kind, pl.ds(h * half + l, 1), :] for l in range(half)]
            flat = jnp.concatenate(pieces, axis=1)
            stacked = jnp.concatenate(
                [jnp.where(group_of_lane == g, flat, 0.0)
                 for g in range(B_KV_GROUPS)], axis=0).astype(BF16)
            parts.append(jnp.dot(stacked, w1_ref[kind, h], preferred_element_type=F32))
        rows = B_KV_GROUPS * n_chunk
        hidden = parts[0] + pltpu.roll(parts[1], rows - 1, axis=0)
        act = _gelu_tanh(hidden).astype(BF16)
        out = None
        for g in range(B_KV_GROUPS):
            o = jnp.dot(act[g * n_chunk:(g + 1) * n_chunk], w2_ref[kind, g],
                        preferred_element_type=F32)
            out = o if out is None else out + o
        dst_ref[0] = out


def _compress(proj, kc_blk, vc_blk, pos, w1, w2):
    bsz, seq, _ = proj.shape
    n_chunk = seq // CMP_STRIDE
    out = jax.ShapeDtypeStruct((bsz, n_chunk, HB_W), F32)
    blk = pl.BlockSpec((1, n_chunk, HB_W), lambda b: (b, 0, 0))
    return pl.pallas_call(
        _compress_kernel,
        grid=(bsz,),
        in_specs=[
            pl.BlockSpec((1, seq, LANES), lambda b: (b, 0, 2 * kc_blk)),
            pl.BlockSpec((1, seq, LANES), lambda b: (b, 0, 2 * kc_blk + 1)),
            pl.BlockSpec((1, seq, LANES), lambda b: (b, 0, 2 * vc_blk)),
            pl.BlockSpec((1, seq, LANES), lambda b: (b, 0, 2 * vc_blk + 1)),
            _resident(pos.shape),
            _resident(w1.shape),
            _resident(w2.shape),
        ],
        out_specs=[blk, blk],
        out_shape=[out, out],
        compiler_params=_params(1),
        name="nsa_compress",
    )(proj, proj, proj, proj, pos, w1, w2)


def _cmp_kernel(q_ref, kcb_ref, vcb_ref, gate_ref, ov_ref, o_ref, sel_ref):
    qt = pl.program_id(1)
    n_cmp_pad = kcb_ref.shape[1]
    n_slc = LANES // B_KV_GROUPS
    lane = lax.broadcasted_iota(jnp.int32, (1, HB_W), 1)
    slot_of_lane = _div_pow2(lane, HEAD_DIM)
    t = qt * TQ + lax.broadcasted_iota(jnp.int32, (TQ, 1), 0)
    cmp_end = (lax.broadcasted_iota(jnp.int32, (1, n_cmp_pad), 1) * CMP_STRIDE + CMP_LEN - 1)
    ok = cmp_end <= t
    kc = kcb_ref[0].astype(BF16)
    vc = vcb_ref[0].astype(BF16)
    gsig = _sigmoid(gate_ref[0])
    p_group = [jnp.zeros((TQ, n_cmp_pad), F32) for _ in range(B_KV_GROUPS)]
    for r in range(N_HB):
        q_r = q_ref[0, :, r * HB_W:(r + 1) * HB_W]
        out = jnp.zeros((TQ, HB_W), F32)
        for g in range(B_KV_GROUPS):
            qm = jnp.where(slot_of_lane == g, q_r, 0.0).astype(BF16)
            s = lax.dot_general(qm, kc, _NT, preferred_element_type=F32) * SCALE
            s = jnp.where(ok, s, NEG)
            m = jnp.max(s, axis=1, keepdims=True)
            e = jnp.where(ok, jnp.exp(s - m), 0.0)
            p = e / jnp.maximum(jnp.sum(e, axis=1, keepdims=True), TINY)
            p_group[g] = p_group[g] + p
            o = jnp.dot(p.astype(BF16), vc, preferred_element_type=F32)
            col = r * LANES + g
            out = out + jnp.where(slot_of_lane == g, o * gsig[:, col:col + 1], 0.0)
        o_ref[0, :, r * HB_W:(r + 1) * HB_W] = out

    imp = None
    for g in range(B_KV_GROUPS):
        part = jnp.dot(p_group[g], ov_ref[g], precision=lax.Precision.HIGHEST,
                       preferred_element_type=F32)
        imp = part if imp is None else imp + part
    imp = imp.T.reshape(B_KV_GROUPS, n_slc, TQ)
    j = lax.broadcasted_iota(jnp.int32, (1, n_slc, 1), 1)
    cur = _div_pow2(qt * TQ + lax.broadcasted_iota(jnp.int32, (1, 1, TQ), 2), SLC_BLK)
    val = jnp.where(j == cur - 1, -NEG, imp)
    val = jnp.where(j == cur, -NEG, val)
    val = jnp.where(j == 0, -NEG, val)
    val = jnp.where(j > cur, NEG, val)
    chosen = jnp.where(_rank_rows(val) < SLC_TOPK, 1.0, 0.0)
    sel_ref[0] = chosen.reshape(LANES, TQ).T


def _cmp_attention(proj, kcb, vcb, gate_blk, ov):
    bsz, seq, _ = proj.shape
    n_cmp_pad = kcb.shape[1]
    kv_blk = pl.BlockSpec((1, n_cmp_pad, HB_W), lambda b, qt: (b, 0, 0))
    return pl.pallas_call(
        _cmp_kernel,
        grid=(bsz, seq // TQ),
        in_specs=[
            pl.BlockSpec((1, TQ, D_MODEL), lambda b, qt: (b, qt, 0)),
            kv_blk, kv_blk,
            pl.BlockSpec((1, TQ, N_HB * LANES), lambda b, qt: (b, qt, gate_blk)),
            _resident(ov.shape),
        ],
        out_specs=[
            pl.BlockSpec((1, TQ, D_MODEL), lambda b, qt: (b, qt, 0)),
            pl.BlockSpec((1, TQ, LANES), lambda b, qt: (b, qt, 0)),
        ],
        out_shape=[
            jax.ShapeDtypeStruct((bsz, seq, D_MODEL), F32),
            jax.ShapeDtypeStruct((bsz, seq, LANES), F32),
        ],
        compiler_params=_params(2),
        name="nsa_cmp_select",
    )(proj, kcb, vcb, proj, ov)


def _interleave_heads(w, n_groups, axis):
    r = N_HEADS // n_groups
    shape = w.shape
    w = w.reshape(shape[:axis] + (n_groups, r) + shape[axis + 1:])
    w = jnp.swapaxes(w, axis, axis + 1)
    return w.reshape(shape)


def _mixer_a(xf, bsz, seq, g_pre, w_in, w_out, sinks, g_post, bias_nat):
    qw = N_HEADS * HEAD_DIM
    w_q = _interleave_heads(w_in[:, :qw].reshape(D_MODEL, N_HEADS, HEAD_DIM), A_KV_HEADS, 1)
    w = jnp.concatenate([w_q.reshape(D_MODEL, qw), w_in[:, qw:]], axis=1)
    proj = _proj(xf, g_pre, w).reshape(bsz, seq, -1)
    n_back = -(-(A_WINDOW - 1) // TK)
    bias = _block_bias(bias_nat, A_KV_HEADS, n_back + 1, A_WINDOW)
    sink_rows = _interleave_heads(sinks.astype(F32), A_KV_HEADS, 0).reshape(N_HB, 1, SLOTS)
    sink_rows = jnp.pad(sink_rows, ((0, 0), (0, 0), (0, LANES - SLOTS)))
    cfg = _AttnCfg(kv_width=LANES, n_back=n_back, sink=True)
    k_blk0 = qw // LANES
    o = _attention(cfg, proj, bias, q_blk0=0, k_blk=lambda hb: k_blk0,
                   v_blk=lambda hb: k_blk0 + 1, sinks=sink_rows)
    w_o = _interleave_heads(w_out.reshape(N_HEADS, HEAD_DIM, D_MODEL), A_KV_HEADS, 0)
    return _outproj([o.reshape(bsz * seq, -1)], w_o.reshape(qw, D_MODEL), g_post, xf)


def _nsa_constants(seq):
    n_chunk = seq // CMP_STRIDE
    n_cmp = (seq - CMP_LEN) // CMP_STRIDE + 1
    n_slc = seq // SLC_BLK
    c_start = np.arange(n_cmp)[:, None] * CMP_STRIDE
    s_start = np.arange(n_slc)[None, :] * SLC_BLK
    overlap = ((c_start < s_start + SLC_BLK) & (c_start + CMP_LEN > s_start)).astype(np.float32)
    ov = np.zeros((B_KV_GROUPS, n_chunk, LANES), np.float32)
    for g in range(B_KV_GROUPS):
        ov[g, :n_cmp, g * n_slc:(g + 1) * n_slc] = overlap
    return jnp.asarray(ov)


def _mixer_b(xf, bsz, seq, g_pre, w_in, w_out, cmp_pos, cmp_w1, cmp_w2, g_post, bias_nat):
    assert seq // SLC_BLK * B_KV_GROUPS == LANES and seq // CMP_STRIDE == LANES
    grp, dh = B_KV_GROUPS, HEAD_DIM
    qw, kvw = N_HEADS * dh, B_KV_GROUPS * dh
    w_q = _interleave_heads(w_in[:, :qw].reshape(D_MODEL, N_HEADS, dh), grp, 1)
    w_g = w_in[:, qw + 6 * kvw:].reshape(D_MODEL, grp, N_HB, 3)
    w_g = w_g.transpose(0, 2, 3, 1).reshape(D_MODEL, N_HB, 3 * grp)
    w_g = jnp.pad(w_g, ((0, 0), (0, 0), (0, LANES - 3 * grp))).reshape(D_MODEL, N_HB * LANES)
    w = jnp.concatenate([w_q.reshape(D_MODEL, qw), w_in[:, qw:qw + 6 * kvw], w_g], axis=1)
    proj = _proj(xf, g_pre, w).reshape(bsz, seq, -1)
    kv_blk0 = qw // HB_W
    gate_col0 = qw + 6 * kvw

    half = CMP_LEN // 2
    pos = jnp.tile(cmp_pos.astype(F32), (1, 1, grp))
    w1 = cmp_w1.reshape(2, 2, half, 1, dh, CMP_HIDDEN)
    w1 = jnp.broadcast_to(w1, (2, 2, half, grp, dh, CMP_HIDDEN))
    w1 = w1.reshape(2, 2, half * HB_W, CMP_HIDDEN).astype(BF16)
    w2 = jnp.zeros((2, grp, CMP_HIDDEN, grp, dh), F32)
    for g in range(grp):
        w2 = w2.at[:, g, :, g, :].set(cmp_w2)
    w2 = w2.reshape(2, grp, CMP_HIDDEN, HB_W).astype(BF16)
    kcb, vcb = _compress(proj, kv_blk0, kv_blk0 + 1, pos, w1, w2)

    ov = _nsa_constants(seq)
    o_cmp, sel = _cmp_attention(proj, kcb, vcb, gate_col0 // (N_HB * LANES), ov)

    bias = _block_bias(bias_nat, grp, seq // TK)
    gate_blk0 = gate_col0 // LANES
    cfg_slc = _AttnCfg(kv_width=HB_W, sel_blk=SLC_BLK, sel_from_input=True, gate_branch=1)
    o_slc = _attention(cfg_slc, proj, bias, q_blk0=0, k_blk=lambda hb: kv_blk0 + 2,
                       v_blk=lambda hb: kv_blk0 + 3, sel=sel, gate_blk0=gate_blk0)
    n_back = -(-(B_WINDOW - 1) // TK)
    cfg_win = _AttnCfg(kv_width=HB_W, n_back=n_back, gate_branch=2)
    bias_win = _block_bias(bias_nat, grp, n_back + 1, B_WINDOW)
    o_win = _attention(cfg_win, proj, bias_win, q_blk0=0,
                       k_blk=lambda hb: kv_blk0 + 4, v_blk=lambda hb: kv_blk0 + 5,
                       gate_blk0=gate_blk0)
    w_o = _interleave_heads(w_out.reshape(N_HEADS, dh, D_MODEL), grp, 0).reshape(qw, D_MODEL)
    n = bsz * seq
    return _outproj([o_cmp.reshape(n, -1), o_slc.reshape(n, -1), o_win.reshape(n, -1)],
                    w_o, g_post, xf)


def _mixer_c(xf, bsz, seq, g_pre, w_in, w_out, g_post, bias_nat):
    assert seq % MOBA_BLK == 0 and MOBA_BLK == TK and seq // MOBA_BLK <= HEAD_DIM
    proj = _proj(xf, g_pre, w_in).reshape(bsz, seq, -1)
    bias = _block_bias(bias_nat, 1, seq // TK)
    cfg = _AttnCfg(kv_width=HB_W, sel_blk=MOBA_BLK)
    o = _attention(cfg, proj, bias, q_blk0=0, k_blk=lambda hb: N_HB + hb,
                   v_blk=lambda hb: 2 * N_HB + hb)
    return _outproj([o.reshape(bsz * seq, -1)], w_out, g_post, xf)


def kernel(x, rel_bias, norm_g, ffn_w_in, ffn_w_out, a_w_in, a_w_out, a_sinks, b_w_in, b_w_out,
           b_cmp_pos, b_cmp_w1, b_cmp_w2, c_w_in, c_w_out):
    bsz, seq, d = x.shape
    assert d == D_MODEL and seq % TQ == 0 and (bsz * seq) % TM == 0
    depth = norm_g.shape[0]
    bias_nat = _bias_tiles(rel_bias, seq // TK)
    xf = x.reshape(bsz * seq, d)
    for i in range(depth):
        g = norm_g[i]
        xf = _ffn(xf, g[0], ffn_w_in[i, 0], ffn_w_out[i, 0], g[1])
        kind, j = i % N_MIXERS, i // N_MIXERS
        if kind == 0:
            xf = _mixer_a(xf, bsz, seq, g[2], a_w_in[j], a_w_out[j], a_sinks[j], g[3], bias_nat)
        elif kind == 1:
            xf = _mixer_b(xf, bsz, seq, g[2], b_w_in[j], b_w_out[j], b_cmp_pos[j], b_cmp_w1[j],
                          b_cmp_w2[j], g[3], bias_nat)
        else:
            xf = _mixer_c(xf, bsz, seq, g[2], c_w_in[j], c_w_out[j], g[3], bias_nat)
        xf = _ffn(xf, g[4], ffn_w_in[i, 1], ffn_w_out[i, 1], g[5])
    return xf.reshape(bsz, seq, d)
```

```python
import dataclasses
import functools
import math

import numpy as np
import jax
import jax.numpy as jnp
from jax import lax
from jax.experimental import pallas as pl
from jax.experimental.pallas import tpu as pltpu

F32 = jnp.float32
BF16 = jnp.bfloat16

D_MODEL = 1024
HEAD_DIM = 64
N_HEADS = D_MODEL // HEAD_DIM
D_FF = 2816
NORM_EPS = 1e-6
N_MIXERS = 3
REL_BUCKETS = 32
REL_MAX_DIST = 1024
A_KV_HEADS = 2
A_WINDOW = 128
B_KV_GROUPS = 4
CMP_LEN = 32
CMP_STRIDE = 16
CMP_HIDDEN = 256
SLC_BLK = 64
SLC_TOPK = 16
B_WINDOW = 512
MOBA_BLK = 256
MOBA_TOPK = 3
NEG = -1e30
TINY = 1e-30

LANES = 128
SUBLANES = 8
MXU_DIM = 256
VMEM_LIMIT_BYTES = 56 * 1024 * 1024

SLOTS = 4
HB_W = SLOTS * HEAD_DIM
N_HB = N_HEADS // SLOTS
TQ = 256
TK = 256
TM = 512
FF_CHUNKS = ((0, 1536), (1536, D_FF))
GATE_ROWS = 16
SEL_BIG = float(2 ** 30)
SCALE = HEAD_DIM ** -0.5

_NT = (((1,), (1,)), ((), ()))


def _params(n_axes):
    return pltpu.CompilerParams(
        dimension_semantics=("arbitrary",) * n_axes,
        vmem_limit_bytes=VMEM_LIMIT_BYTES)


def _rms(x, g):
    return x * lax.rsqrt(jnp.mean(x * x, axis=-1, keepdims=True) + NORM_EPS) * g


def _sigmoid(x):
    return 1.0 / (1.0 + jnp.exp(-x))


def _div_pow2(x, d):
    assert d & (d - 1) == 0
    return jnp.right_shift(x, d.bit_length() - 1)


def _resident(shape):
    zeros = (0,) * len(shape)
    return pl.BlockSpec(shape, lambda *_: zeros, pipeline_mode=pl.Buffered(1))


def _ffn_kernel(x_ref, gpre_ref, win_ref, wout_ref, gpost_ref, o_ref):
    x = x_ref[...]
    hn = _rms(x, gpre_ref[...]).astype(BF16)
    y = None
    for lo, hi in FF_CHUNKS:
        gate = jnp.dot(hn, win_ref[:, lo:hi], preferred_element_type=F32)
        up = jnp.dot(hn, win_ref[:, D_FF + lo:D_FF + hi], preferred_element_type=F32)
        act = (gate * _sigmoid(gate) * up).astype(BF16)
        part = jnp.dot(act, wout_ref[lo:hi, :], preferred_element_type=F32)
        y = part if y is None else y + part
    o_ref[...] = x + 0.5 * _rms(y, gpost_ref[...])


def _ffn(xf, g_pre, w_in, w_out, g_post):
    n, d = xf.shape
    return pl.pallas_call(
        _ffn_kernel,
        grid=(n // TM,),
        in_specs=[
            pl.BlockSpec((TM, d), lambda i: (i, 0)),
            _resident((1, d)),
            _resident(w_in.shape),
            _resident(w_out.shape),
            _resident((1, d)),
        ],
        out_specs=pl.BlockSpec((TM, d), lambda i: (i, 0)),
        out_shape=jax.ShapeDtypeStruct((n, d), F32),
        compiler_params=_params(1),
        name="ffn",
    )(xf, g_pre.reshape(1, d), w_in.astype(BF16), w_out.astype(BF16), g_post.reshape(1, d))


def _proj_kernel(x_ref, g_ref, wr_ref, wct_ref, rows_ref, cols_ref):
    hn = _rms(x_ref[...], g_ref[...]).astype(BF16)
    rows_ref[...] = jnp.dot(hn, wr_ref[...], preferred_element_type=F32)
    cols_ref[...] = lax.dot_general(wct_ref[...], hn, _NT, preferred_element_type=F32)


def _proj(xf, g, w_rows, w_cols):
    n, d = xf.shape
    cr, cc = w_rows.shape[1], w_cols.shape[1]
    return pl.pallas_call(
        _proj_kernel,
        grid=(n // TM,),
        in_specs=[
            pl.BlockSpec((TM, d), lambda i: (i, 0)),
            _resident((1, d)),
            _resident((d, cr)),
            _resident((cc, d)),
        ],
        out_specs=[pl.BlockSpec((TM, cr), lambda i: (i, 0)),
                   pl.BlockSpec((cc, TM), lambda i: (0, i))],
        out_shape=[jax.ShapeDtypeStruct((n, cr), F32), jax.ShapeDtypeStruct((cc, n), F32)],
        compiler_params=_params(1),
        name="proj",
    )(xf, g.reshape(1, d), w_rows.astype(BF16), w_cols.T.astype(BF16))


def _outproj_kernel(*refs, n_in):
    o_refs = refs[:n_in]
    w_ref, g_ref, x_ref, out_ref = refs[n_in:]
    o = o_refs[0][...]
    for r in o_refs[1:]:
        o = o + r[...]
    m = jnp.dot(o.astype(BF16), w_ref[...], preferred_element_type=F32)
    out_ref[...] = x_ref[...] + _rms(m, g_ref[...])


def _outproj(o_list, w, g, xf):
    n, d = xf.shape
    tile = pl.BlockSpec((TM, d), lambda i: (i, 0))
    return pl.pallas_call(
        functools.partial(_outproj_kernel, n_in=len(o_list)),
        grid=(n // TM,),
        in_specs=[tile] * len(o_list) + [_resident(w.shape), _resident((1, d)), tile],
        out_specs=tile,
        out_shape=jax.ShapeDtypeStruct((n, d), F32),
        compiler_params=_params(1),
        name="outproj",
    )(*o_list, w.astype(BF16), g.reshape(1, d), xf)


def _rel_bucket(dist):
    dist = jnp.maximum(dist, 0)
    exact = REL_BUCKETS // 2
    d = jnp.maximum(dist, 1).astype(jnp.float32)
    log_b = exact + (jnp.log(d / exact) / math.log(REL_MAX_DIST / exact)
                     * (REL_BUCKETS - exact)).astype(jnp.int32)
    return jnp.where(dist < exact, dist, jnp.minimum(log_b, REL_BUCKETS - 1))


def _tile_distance(n_delta, xp):
    delta = xp.arange(n_delta)[:, None, None]
    return delta * TK + xp.arange(TQ)[None, None, :] - xp.arange(TK)[None, :, None]


def _bias_tiles(rel_bias, n_delta):
    bucket = _rel_bucket(_tile_distance(n_delta, jnp)).reshape(1, -1)
    onehot = (jnp.arange(REL_BUCKETS)[:, None] == bucket).astype(F32)
    tiles = jnp.dot(rel_bias.astype(F32).T, onehot, precision=lax.Precision.HIGHEST)
    return tiles.reshape(N_HEADS, n_delta, TK, TQ)


def _interleave_heads(w, n_groups, axis):
    r = N_HEADS // n_groups
    shape = w.shape
    w = w.reshape(shape[:axis] + (n_groups, r) + shape[axis + 1:])
    w = jnp.swapaxes(w, axis, axis + 1)
    return w.reshape(shape)


def _block_bias(bias_nat, n_groups, n_delta, window=None):
    dist = _tile_distance(n_delta, np)
    visible = dist >= 0
    if window is not None:
        visible &= dist < window
    bias = bias_nat[:, :n_delta]
    if n_groups > 1:
        bias = _interleave_heads(bias, n_groups, 0)
    bias = jnp.where(visible, bias, NEG)
    return bias.reshape((N_HB, SLOTS) + bias.shape[1:])


@dataclasses.dataclass(frozen=True)
class _AttnCfg:
    kv_width: int
    n_back: int | None = None
    sel_blk: int | None = None
    sel_from_input: bool = False
    gate_branch: int | None = None
    sink: bool = False

    @property
    def selecting(self):
        return self.sel_blk is not None

    @property
    def moba(self):
        return self.selecting and not self.sel_from_input


def _rank_rows(val):
    width = val.shape[1]
    row = lax.broadcasted_iota(jnp.int32, (1, width, 1), 1)
    rank = jnp.zeros(val.shape, F32)
    for m in range(width):
        vm = val[:, m:m + 1, :]
        rank = rank + jnp.where(row > m, jnp.where(vm >= val, 1.0, 0.0),
                                jnp.where(vm > val, 1.0, 0.0))
    return rank


def _u_chan0(hh, n_blk):
    return hh * n_blk + (LANES if hh < SLOTS // 2 else 0)


def _fold_rows(x, op):
    return op(x.reshape(x.shape[0] // SUBLANES, SUBLANES, x.shape[1]), axis=0)


def _attn_kernel(*refs, cfg: _AttnCfg):
    it = iter(refs)
    qt_ref, k_ref, vt_ref, bias_ref = next(it), next(it), next(it), next(it)
    sel_ref = next(it) if cfg.sel_from_input else None
    gate_ref = next(it) if cfg.gate_branch is not None else None
    sink_ref = next(it) if cfg.sink else None
    o_ref = next(it)
    kb_ref, vtb_ref = next(it), next(it)
    sc_ref, m_ref, l_ref, acc_ref = next(it), next(it), next(it), next(it)
    kmx_ref = next(it) if cfg.moba else None

    qt = pl.program_id(2)
    seq = k_ref.shape[1]
    n_tiles = seq // TK
    lane = lax.broadcasted_iota(jnp.int32, (1, HB_W), 1)
    slot_of_lane = _div_pow2(lane, HEAD_DIM)
    chan = lax.broadcasted_iota(jnp.int32, (HB_W, 1), 0)
    slot_of_chan = _div_pow2(chan, HEAD_DIM)
    n_sel_blk = seq // cfg.sel_blk if cfg.selecting else 0

    @pl.when(qt == 0)
    def _prepare():
        if cfg.moba:
            kmx_ref[...] = jnp.zeros(kmx_ref.shape, F32)
        for c in range(n_tiles):
            rows = pl.ds(c * TK, TK)
            k = k_ref[0, rows, :]
            vt = vt_ref[:, c * TK:(c + 1) * TK]
            if cfg.kv_width == LANES:
                k = jnp.concatenate([k, k], axis=1)
                vt = jnp.concatenate([vt, vt], axis=0)
            vtb_ref[c] = vt.astype(BF16)
            if cfg.selecting:
                key_blk = _div_pow2(c * TK + lax.broadcasted_iota(jnp.int32, (TK, 1), 0),
                                    cfg.sel_blk)
                for hh in range(SLOTS):
                    onehot = (lane - _u_chan0(hh, n_sel_blk)) == key_blk
                    kb_ref[hh, rows, :] = jnp.where(
                        slot_of_lane == hh, k, jnp.where(onehot, 1.0, 0.0)).astype(BF16)
            else:
                kb_ref[0, rows, :] = k.astype(BF16)
            if cfg.moba:
                k_mean = jnp.mean(k, axis=0, keepdims=True)
                for hh in range(SLOTS):
                    kmx_ref[pl.ds(hh * n_sel_blk + c, 1), :] = jnp.where(
                        slot_of_lane == hh, k_mean, 0.0)

    q_t = qt_ref[...]

    sel_rows = None
    if cfg.sel_from_input:
        sel_rows = sel_ref[0]
    elif cfg.moba:
        gate = jnp.dot(kmx_ref[...].astype(BF16), q_t.astype(BF16),
                       preferred_element_type=F32)
        gate = gate.reshape(SLOTS, n_sel_blk, TQ)
        blk = lax.broadcasted_iota(jnp.int32, (1, n_sel_blk, 1), 1)
        val = jnp.where(blk < qt, gate, NEG)
        rank = _rank_rows(val)
        chosen = jnp.where(blk < qt, jnp.where(rank < MOBA_TOPK, 1.0, 0.0), 0.0)
        chosen = jnp.where(blk == qt, 1.0, chosen)
        sel_rows = chosen.reshape(SLOTS * n_sel_blk, TQ)

    q_scaled = q_t * SCALE
    qms = []
    for hh in range(SLOTS):
        qm = jnp.where(slot_of_chan == hh, q_scaled, 0.0)
        if cfg.selecting:
            u0 = _u_chan0(hh, n_sel_blk)
            penalty = (sel_rows[hh * n_sel_blk:(hh + 1) * n_sel_blk] - 1.0) * SEL_BIG
            qm = qm + jnp.concatenate(
                [jnp.zeros((u0, TQ), F32), penalty,
                 jnp.zeros((HB_W - u0 - n_sel_blk, TQ), F32)], axis=0)
        qms.append(qm.astype(BF16))
    q_stack = None if cfg.selecting else jnp.concatenate(qms, axis=1)

    lo = jnp.maximum(qt - cfg.n_back, 0) if cfg.n_back is not None else 0
    n_odd = jnp.bitwise_and(qt + 1 - lo, 1)
    n_pairs = jnp.right_shift(qt + 1 - lo, 1)
    m_ref[...] = jnp.full(m_ref.shape, NEG, F32)
    l_ref[...] = jnp.zeros(l_ref.shape, F32)
    acc_ref[...] = jnp.zeros(acc_ref.shape, F32)

    def pass1(kt, width):
        rows = pl.ds(pl.multiple_of(kt * TK, TK), width * TK)
        if cfg.selecting:
            s = [jnp.dot(kb_ref[hh, rows, :], qms[hh], preferred_element_type=F32)
                 for hh in range(SLOTS)]
        else:
            s_all = jnp.dot(kb_ref[0, rows, :], q_stack, preferred_element_type=F32)
            s = [s_all[:, hh * TQ:(hh + 1) * TQ] for hh in range(SLOTS)]
        for hh in range(SLOTS):
            m_new = m_ref[hh]
            for w in range(width):
                tile = s[hh][w * TK:(w + 1) * TK] + bias_ref[0, hh, qt - kt - w]
                sc_ref[hh, kt - lo + w] = tile
                m_new = jnp.maximum(m_new, _fold_rows(tile, jnp.max))
            m_ref[hh] = m_new

    def pass2(kt, width, ms):
        for hh in range(SLOTS):
            l_new = l_ref[hh]
            o_new = acc_ref[hh * HEAD_DIM:(hh + 1) * HEAD_DIM, :]
            for w in range(width):
                p = jnp.exp(sc_ref[hh, kt - lo + w] - ms[hh])
                l_new = l_new + _fold_rows(p, jnp.sum)
                o_new = o_new + jnp.dot(
                    vtb_ref[kt + w, hh * HEAD_DIM:(hh + 1) * HEAD_DIM, :], p.astype(BF16),
                    preferred_element_type=F32)
            l_ref[hh] = l_new
            acc_ref[hh * HEAD_DIM:(hh + 1) * HEAD_DIM, :] = o_new

    def over_tiles(body):
        @pl.when(n_odd == 1)
        def _single():
            body(lo, 1)

        def pair(i, carry):
            body(lo + n_odd + 2 * i, 2)
            return carry

        lax.fori_loop(0, n_pairs, pair, 0)

    over_tiles(pass1)
    sinks = [sink_ref[0, :, hh:hh + 1] for hh in range(SLOTS)] if cfg.sink else None
    ms = []
    for hh in range(SLOTS):
        m = jnp.max(m_ref[hh], axis=0, keepdims=True)
        ms.append(jnp.maximum(m, sinks[hh]) if cfg.sink else m)
    over_tiles(functools.partial(pass2, ms=ms))

    gsig = _sigmoid(gate_ref[...]) if cfg.gate_branch is not None else None
    outs = []
    for hh in range(SLOTS):
        denom = jnp.sum(l_ref[hh], axis=0, keepdims=True)
        if cfg.sink:
            denom = denom + jnp.exp(sinks[hh] - ms[hh])
        o_h = acc_ref[hh * HEAD_DIM:(hh + 1) * HEAD_DIM, :] / jnp.maximum(denom, TINY)
        if cfg.gate_branch is not None:
            row = cfg.gate_branch * SLOTS + hh
            o_h = o_h * gsig[row:row + 1, :]
        outs.append(o_h)
    o_ref[0] = jnp.concatenate(outs, axis=0).T


def _attention(cfg, rows, cols, bias, *, k_blk, v_rblk, sel=None, gate_rblk0=None, sinks=None):
    bsz, seq, _ = rows.shape
    kw = cfg.kv_width
    n_delta = bias.shape[2]
    n_q = seq // TQ
    in_specs = [
        pl.BlockSpec((HB_W, TQ), lambda hb, b, qt: (hb, b * n_q + qt)),
        pl.BlockSpec((1, seq, kw), lambda hb, b, qt: (b, 0, k_blk(hb))),
        pl.BlockSpec((kw, seq), lambda hb, b, qt: (v_rblk(hb), b)),
        pl.BlockSpec((1, SLOTS, n_delta, TK, TQ), lambda hb, b, qt: (hb, 0, 0, 0, 0)),
    ]
    args = [cols, rows, cols, bias]
    if cfg.sel_from_input:
        in_specs.append(pl.BlockSpec((1, LANES, TQ), lambda hb, b, qt: (b, 0, qt)))
        args.append(sel)
    if cfg.gate_branch is not None:
        in_specs.append(
            pl.BlockSpec((GATE_ROWS, TQ), lambda hb, b, qt: (gate_rblk0 + hb, b * n_q + qt)))
        args.append(cols)
    if cfg.sink:
        in_specs.append(pl.BlockSpec((1, 1, LANES), lambda hb, b, qt: (hb, 0, 0)))
        args.append(sinks)
    scratch = [
        pltpu.VMEM((SLOTS if cfg.selecting else 1, seq, HB_W), BF16),
        pltpu.VMEM((seq // TK, HB_W, TK), BF16),
        pltpu.VMEM((SLOTS, n_delta, TK, TQ), F32),
        pltpu.VMEM((SLOTS, SUBLANES, TQ), F32),
        pltpu.VMEM((SLOTS, SUBLANES, TQ), F32),
        pltpu.VMEM((HB_W, TQ), F32),
    ]
    if cfg.moba:
        assert SLOTS * (seq // cfg.sel_blk) <= LANES
        scratch.append(pltpu.VMEM((SLOTS * (seq // cfg.sel_blk), HB_W), F32))
    return pl.pallas_call(
        functools.partial(_attn_kernel, cfg=cfg),
        grid=(N_HB, bsz, n_q),
        in_specs=in_specs,
        out_specs=pl.BlockSpec((1, TQ, HB_W), lambda hb, b, qt: (b, qt, hb)),
        out_shape=jax.ShapeDtypeStruct((bsz, seq, D_MODEL), F32),
        scratch_shapes=scratch,
        compiler_params=_params(3),
        name="attention",
    )(*args)


def _gelu_tanh(x):
    cdf = 0.5 * (1.0 + jnp.tanh(math.sqrt(2.0 / math.pi) * (x + 0.044715 * (x * x * x))))
    return x * cdf


def _compress_kernel(kc_lo_ref, kc_hi_ref, vc_lo_ref, vc_hi_ref, pos_ref, w1_ref, w2k_ref,
                     w2vt_ref, kcb_ref, vcbt_ref):
    half = CMP_LEN // 2
    n_chunk = kc_lo_ref.shape[1] // CMP_STRIDE
    flat_lane = lax.broadcasted_iota(jnp.int32, (1, half * HB_W), 1)
    group_of_lane = _div_pow2(jnp.bitwise_and(flat_lane, HB_W - 1), HEAD_DIM)
    for kind, (lo_ref, hi_ref) in enumerate(((kc_lo_ref, kc_hi_ref), (vc_lo_ref, vc_hi_ref))):
        chunks = [jnp.concatenate(
            [lo_ref[0, pl.ds(l, n_chunk, stride=CMP_STRIDE), :],
             hi_ref[0, pl.ds(l, n_chunk, stride=CMP_STRIDE), :]], axis=1) for l in range(half)]
        parts = []
        for h in range(2):
            pieces = [chunks[l] + pos_ref[kind, pl.ds(h * half + l, 1), :] for l in range(half)]
            flat = jnp.concatenate(pieces, axis=1)
            stacked = jnp.concatenate(
                [jnp.where(group_of_lane == g, flat, 0.0)
                 for g in range(B_KV_GROUPS)], axis=0).astype(BF16)
            parts.append(jnp.dot(stacked, w1_ref[kind, h], preferred_element_type=F32))
        rows = B_KV_GROUPS * n_chunk
        hidden = parts[0] + pltpu.roll(parts[1], rows - 1, axis=0)
        act = _gelu_tanh(hidden).astype(BF16)
        out = None
        for g in range(B_KV_GROUPS):
            act_g = act[g * n_chunk:(g + 1) * n_chunk]
            if kind == 0:
                o = jnp.dot(act_g, w2k_ref[g], preferred_element_type=F32)
            else:
                o = lax.dot_general(w2vt_ref[g], act_g, _NT, preferred_element_type=F32)
            out = o if out is None else out + o
        if kind == 0:
            kcb_ref[0] = out
        else:
            vcbt_ref[0] = out


def _compress(rows, kc_blk, vc_blk, pos, w1, w2k, w2vt):
    bsz, seq, _ = rows.shape
    n_chunk = seq // CMP_STRIDE
    return pl.pallas_call(
        _compress_kernel,
        grid=(bsz,),
        in_specs=[
            pl.BlockSpec((1, seq, LANES), lambda b: (b, 0, 2 * kc_blk)),
            pl.BlockSpec((1, seq, LANES), lambda b: (b, 0, 2 * kc_blk + 1)),
            pl.BlockSpec((1, seq, LANES), lambda b: (b, 0, 2 * vc_blk)),
            pl.BlockSpec((1, seq, LANES), lambda b: (b, 0, 2 * vc_blk + 1)),
            _resident(pos.shape),
            _resident(w1.shape),
            _resident(w2k.shape),
            _resident(w2vt.shape),
        ],
        out_specs=[pl.BlockSpec((1, n_chunk, HB_W), lambda b: (b, 0, 0)),
                   pl.BlockSpec((1, HB_W, n_chunk), lambda b: (b, 0, 0))],
        out_shape=[jax.ShapeDtypeStruct((bsz, n_chunk, HB_W), F32),
                   jax.ShapeDtypeStruct((bsz, HB_W, n_chunk), F32)],
        compiler_params=_params(1),
        name="nsa_compress",
    )(rows, rows, rows, rows, pos, w1, w2k, w2vt)


def _cmp_kernel(qt_ref, kcb_ref, vcbt_ref, gate_ref, ovt_ref, o_ref, sel_ref):
    qt = pl.program_id(1)
    n_cmp_pad = kcb_ref.shape[1]
    n_slc = LANES // B_KV_GROUPS
    chan = lax.broadcasted_iota(jnp.int32, (HB_W, 1), 0)
    slot_of_chan = _div_pow2(chan, HEAD_DIM)
    t = qt * TQ + lax.broadcasted_iota(jnp.int32, (1, TQ), 1)
    cmp_end = (lax.broadcasted_iota(jnp.int32, (n_cmp_pad, 1), 0) * CMP_STRIDE + CMP_LEN - 1)
    ok = cmp_end <= t
    kc = kcb_ref[0].astype(BF16)
    vct = vcbt_ref[0].astype(BF16)
    gsig = _sigmoid(gate_ref[...])
    p_group = [jnp.zeros((n_cmp_pad, TQ), F32) for _ in range(B_KV_GROUPS)]
    for r in range(N_HB):
        q_r = qt_ref[r * HB_W:(r + 1) * HB_W, :] * SCALE
        outs = []
        for g in range(B_KV_GROUPS):
            qm = jnp.where(slot_of_chan == g, q_r, 0.0).astype(BF16)
            s = jnp.dot(kc, qm, preferred_element_type=F32)
            s = jnp.where(ok, s, NEG)
            m = jnp.max(s, axis=0, keepdims=True)
            e = jnp.where(ok, jnp.exp(s - m), 0.0)
            p = e / jnp.maximum(jnp.sum(e, axis=0, keepdims=True), TINY)
            p_group[g] = p_group[g] + p
            o = jnp.dot(vct[g * HEAD_DIM:(g + 1) * HEAD_DIM], p.astype(BF16),
                        preferred_element_type=F32)
            row = r * GATE_ROWS + g
            outs.append(o * gsig[row:row + 1, :])
        o_ref[0, :, r * HB_W:(r + 1) * HB_W] = jnp.concatenate(outs, axis=0).T

    imp = None
    for g in range(B_KV_GROUPS):
        part = jnp.dot(ovt_ref[g], p_group[g], precision=lax.Precision.HIGHEST,
                       preferred_element_type=F32)
        imp = part if imp is None else imp + part
    imp = imp.reshape(B_KV_GROUPS, n_slc, TQ)
    j = lax.broadcasted_iota(jnp.int32, (1, n_slc, 1), 1)
    cur = _div_pow2(t, SLC_BLK).reshape(1, 1, TQ)
    val = jnp.where(j == cur - 1, -NEG, imp)
    val = jnp.where(j == cur, -NEG, val)
    val = jnp.where(j == 0, -NEG, val)
    val = jnp.where(j > cur, NEG, val)
    chosen = jnp.where(_rank_rows(val) < SLC_TOPK, 1.0, 0.0)
    sel_ref[0] = chosen.reshape(LANES, TQ)


def _cmp_attention(cols, kcb, vcbt, gate_rblk, ovt, bsz, seq):
    n_cmp_pad = kcb.shape[1]
    n_q = seq // TQ
    return pl.pallas_call(
        _cmp_kernel,
        grid=(bsz, n_q),
        in_specs=[
            pl.BlockSpec((D_MODEL, TQ), lambda b, qt: (0, b * n_q + qt)),
            pl.BlockSpec((1, n_cmp_pad, HB_W), lambda b, qt: (b, 0, 0)),
            pl.BlockSpec((1, HB_W, n_cmp_pad), lambda b, qt: (b, 0, 0)),
            pl.BlockSpec((N_HB * GATE_ROWS, TQ), lambda b, qt: (gate_rblk, b * n_q + qt)),
            _resident(ovt.shape),
        ],
        out_specs=[
            pl.BlockSpec((1, TQ, D_MODEL), lambda b, qt: (b, qt, 0)),
            pl.BlockSpec((1, LANES, TQ), lambda b, qt: (b, 0, qt)),
        ],
        out_shape=[
            jax.ShapeDtypeStruct((bsz, seq, D_MODEL), F32),
            jax.ShapeDtypeStruct((bsz, LANES, seq), F32),
        ],
        compiler_params=_params(2),
        name="nsa_cmp_select",
    )(cols, kcb, vcbt, cols, ovt)


def _mixer_a(xf, bsz, seq, g_pre, w_in, w_out, sinks, g_post, bias_nat):
    qw, kvw = N_HEADS * HEAD_DIM, A_KV_HEADS * HEAD_DIM
    w_q = _interleave_heads(w_in[:, :qw].reshape(D_MODEL, N_HEADS, HEAD_DIM), A_KV_HEADS, 1)
    w_cols = jnp.concatenate([w_q.reshape(D_MODEL, qw), w_in[:, qw + kvw:]], axis=1)
    rows, cols = _proj(xf, g_pre, w_in[:, qw:qw + kvw], w_cols)
    rows = rows.reshape(bsz, seq, kvw)
    n_back = -(-(A_WINDOW - 1) // TK)
    bias = _block_bias(bias_nat, A_KV_HEADS, n_back + 1, A_WINDOW)
    sink_rows = _interleave_heads(sinks.astype(F32), A_KV_HEADS, 0).reshape(N_HB, 1, SLOTS)
    sink_rows = jnp.pad(sink_rows, ((0, 0), (0, 0), (0, LANES - SLOTS)))
    cfg = _AttnCfg(kv_width=kvw, n_back=n_back, sink=True)
    o = _attention(cfg, rows, cols, bias, k_blk=lambda hb: 0, v_rblk=lambda hb: qw // kvw,
                   sinks=sink_rows)
    w_o = _interleave_heads(w_out.reshape(N_HEADS, HEAD_DIM, D_MODEL), A_KV_HEADS, 0)
    return _outproj([o.reshape(bsz * seq, -1)], w_o.reshape(qw, D_MODEL), g_post, xf)


def _nsa_overlap_t(seq):
    n_chunk = seq // CMP_STRIDE
    n_cmp = (seq - CMP_LEN) // CMP_STRIDE + 1
    n_slc = seq // SLC_BLK
    c_start = np.arange(n_cmp)[:, None] * CMP_STRIDE
    s_start = np.arange(n_slc)[None, :] * SLC_BLK
    overlap = ((c_start < s_start + SLC_BLK) & (c_start + CMP_LEN > s_start)).astype(np.float32)
    ovt = np.zeros((B_KV_GROUPS, LANES, n_chunk), np.float32)
    for g in range(B_KV_GROUPS):
        ovt[g, g * n_slc:(g + 1) * n_slc, :n_cmp] = overlap.T
    return jnp.asarray(ovt)


def _mixer_b(xf, bsz, seq, g_pre, w_in, w_out, cmp_pos, cmp_w1, cmp_w2, g_post, bias_nat):
    assert seq // SLC_BLK * B_KV_GROUPS == LANES and seq // CMP_STRIDE == LANES
    grp, dh = B_KV_GROUPS, HEAD_DIM
    qw, kvw = N_HEADS * dh, B_KV_GROUPS * dh
    w_q = _interleave_heads(w_in[:, :qw].reshape(D_MODEL, N_HEADS, dh), grp, 1)
    kv = [w_in[:, qw + i * kvw:qw + (i + 1) * kvw] for i in range(6)]
    w_g = w_in[:, qw + 6 * kvw:].reshape(D_MODEL, grp, N_HB, 3)
    w_g = w_g.transpose(0, 2, 3, 1).reshape(D_MODEL, N_HB, 3 * grp)
    w_g = jnp.pad(w_g, ((0, 0), (0, 0), (0, GATE_ROWS - 3 * grp)))
    w_rows = jnp.concatenate([kv[0], kv[1], kv[2], kv[4]], axis=1)
    w_cols = jnp.concatenate([w_q.reshape(D_MODEL, qw), kv[3], kv[5],
                              w_g.reshape(D_MODEL, N_HB * GATE_ROWS)], axis=1)
    rows, cols = _proj(xf, g_pre, w_rows, w_cols)
    rows = rows.reshape(bsz, seq, 4 * kvw)
    gate_row0 = qw + 2 * kvw

    half = CMP_LEN // 2
    pos = jnp.tile(cmp_pos.astype(F32), (1, 1, grp))
    w1 = cmp_w1.reshape(2, 2, half, 1, dh, CMP_HIDDEN)
    w1 = jnp.broadcast_to(w1, (2, 2, half, grp, dh, CMP_HIDDEN))
    w1 = w1.reshape(2, 2, half * HB_W, CMP_HIDDEN).astype(BF16)
    w2 = jnp.zeros((2, grp, CMP_HIDDEN, grp, dh), F32)
    for g in range(grp):
        w2 = w2.at[:, g, :, g, :].set(cmp_w2)
    w2 = w2.reshape(2, grp, CMP_HIDDEN, HB_W).astype(BF16)
    kcb, vcbt = _compress(rows, 0, 1, pos, w1, w2[0], jnp.swapaxes(w2[1], 1, 2))

    o_cmp, sel = _cmp_attention(cols, kcb, vcbt, gate_row0 // (N_HB * GATE_ROWS),
                                _nsa_overlap_t(seq), bsz, seq)

    gate_rblk0 = gate_row0 // GATE_ROWS
    bias = _block_bias(bias_nat, grp, seq // TK)
    cfg_slc = _AttnCfg(kv_width=kvw, sel_blk=SLC_BLK, sel_from_input=True, gate_branch=1)
    o_slc = _attention(cfg_slc, rows, cols, bias, k_blk=lambda hb: 2,
                       v_rblk=lambda hb: qw // kvw, sel=sel, gate_rblk0=gate_rblk0)
    n_back = -(-(B_WINDOW - 1) // TK)
    cfg_win = _AttnCfg(kv_width=kvw, n_back=n_back, gate_branch=2)
    bias_win = _block_bias(bias_nat, grp, n_back + 1, B_WINDOW)
    o_win = _attention(cfg_win, rows, cols, bias_win, k_blk=lambda hb: 3,
                       v_rblk=lambda hb: qw // kvw + 1, gate_rblk0=gate_rblk0)
    w_o = _interleave_heads(w_out.reshape(N_HEADS, dh, D_MODEL), grp, 0).reshape(qw, D_MODEL)
    n = bsz * seq
    return _outproj([o_cmp.reshape(n, -1), o_slc.reshape(n, -1), o_win.reshape(n, -1)],
                    w_o, g_post, xf)


def _mixer_c(xf, bsz, seq, g_pre, w_in, w_out, g_post, bias_nat):
    assert seq % MOBA_BLK == 0 and MOBA_BLK == TK
    qw = N_HEADS * HEAD_DIM
    w_cols = jnp.concatenate([w_in[:, :qw], w_in[:, 2 * qw:]], axis=1)
    rows, cols = _proj(xf, g_pre, w_in[:, qw:2 * qw], w_cols)
    rows = rows.reshape(bsz, seq, qw)
    bias = _block_bias(bias_nat, 1, seq // TK)
    cfg = _AttnCfg(kv_width=HB_W, sel_blk=MOBA_BLK)
    o = _attention(cfg, rows, cols, bias, k_blk=lambda hb: hb, v_rblk=lambda hb: N_HB + hb)
    return _outproj([o.reshape(bsz * seq, -1)], w_out, g_post, xf)


def kernel(x, rel_bias, norm_g, ffn_w_in, ffn_w_out, a_w_in, a_w_out, a_sinks, b_w_in, b_w_out,
           b_cmp_pos, b_cmp_w1, b_cmp_w2, c_w_in, c_w_out):
    bsz, seq, d = x.shape
    assert d == D_MODEL and seq % TQ == 0 and (bsz * seq) % TM == 0
    depth = norm_g.shape[0]
    bias_nat = _bias_tiles(rel_bias, seq // TK)
    xf = x.reshape(bsz * seq, d)
    for i in range(depth):
        g = norm_g[i]
        xf = _ffn(xf, g[0], ffn_w_in[i, 0], ffn_w_out[i, 0], g[1])
        kind, j = i % N_MIXERS, i // N_MIXERS
        if kind == 0:
            xf = _mixer_a(xf, bsz, seq, g[2], a_w_in[j], a_w_out[j], a_sinks[j], g[3], bias_nat)
        elif kind == 1:
            xf = _mixer_b(xf, bsz, seq, g[2], b_w_in[j], b_w_out[j], b_cmp_pos[j], b_cmp_w1[j],
                          b_cmp_w2[j], g[3], bias_nat)
        else:
            xf = _mixer_c(xf, bsz, seq, g[2], c_w_in[j], c_w_out[j], g[3], bias_nat)
        xf = _ffn(xf, g[4], ffn_w_in[i, 1], ffn_w_out[i, 1], g[5])
    return xf.reshape(bsz, seq, d)
```

```python
import dataclasses
import functools
import math

import numpy as np
import jax
import jax.numpy as jnp
from jax import lax
from jax.experimental import pallas as pl
from jax.experimental.pallas import tpu as pltpu

F32 = jnp.float32
BF16 = jnp.bfloat16

D_MODEL = 1024
HEAD_DIM = 64
N_HEADS = D_MODEL // HEAD_DIM
D_FF = 2816
NORM_EPS = 1e-6
N_MIXERS = 3
REL_BUCKETS = 32
REL_MAX_DIST = 1024
A_KV_HEADS = 2
A_WINDOW = 128
B_KV_GROUPS = 4
CMP_LEN = 32
CMP_STRIDE = 16
CMP_HIDDEN = 256
SLC_BLK = 64
SLC_TOPK = 16
B_WINDOW = 512
MOBA_BLK = 256
MOBA_TOPK = 3
NEG = -1e30
TINY = 1e-30

LANES = 128
SUBLANES = 8
MXU_DIM = 256
VMEM_LIMIT_BYTES = 56 * 1024 * 1024

SLOTS = 4
HB_W = SLOTS * HEAD_DIM
N_HB = N_HEADS // SLOTS
TQ = 256
TK = 256
TM = 512
FF_CHUNKS = ((0, 1536), (1536, D_FF))
GATE_ROWS = 16
SEL_BIG = float(2 ** 30)
SCALE = HEAD_DIM ** -0.5
LOG2E = math.log2(math.e)

_NT = (((1,), (1,)), ((), ()))


def _params(n_axes):
    return pltpu.CompilerParams(
        dimension_semantics=("arbitrary",) * n_axes,
        vmem_limit_bytes=VMEM_LIMIT_BYTES)


def _rms(x, g):
    return x * lax.rsqrt(jnp.mean(x * x, axis=-1, keepdims=True) + NORM_EPS) * g


def _sigmoid(x):
    return 1.0 / (1.0 + jnp.exp(-x))


def _div_pow2(x, d):
    assert d & (d - 1) == 0
    return jnp.right_shift(x, d.bit_length() - 1)


def _resident(shape):
    zeros = (0,) * len(shape)
    return pl.BlockSpec(shape, lambda *_: zeros, pipeline_mode=pl.Buffered(1))


def _ffn_kernel(*refs, n_mix):
    mix_refs, refs = refs[:n_mix], refs[n_mix:]
    if n_mix:
        wmix_ref, gmix_ref = refs[:2]
        refs = refs[2:]
    x_ref, gpre_ref, win_ref, wout_ref, gpost_ref, o_ref = refs
    x = x_ref[...]
    if n_mix:
        o = mix_refs[0][...]
        for r in mix_refs[1:]:
            o = o + r[...]
        mixed = jnp.dot(o.astype(BF16), wmix_ref[...], preferred_element_type=F32)
        x = x + _rms(mixed, gmix_ref[...])
    hn = _rms(x, gpre_ref[...]).astype(BF16)
    y = None
    for lo, hi in FF_CHUNKS:
        gate = jnp.dot(hn, win_ref[:, lo:hi], preferred_element_type=F32)
        up = jnp.dot(hn, win_ref[:, D_FF + lo:D_FF + hi], preferred_element_type=F32)
        act = (gate * _sigmoid(gate) * up).astype(BF16)
        part = jnp.dot(act, wout_ref[lo:hi, :], preferred_element_type=F32)
        y = part if y is None else y + part
    o_ref[...] = x + 0.5 * _rms(y, gpost_ref[...])


def _ffn(xf, g_pre, w_in, w_out, g_post, mix=None):
    n, d = xf.shape
    tile = pl.BlockSpec((TM, d), lambda i: (i, 0))
    mix_specs, mix_args = [], []
    if mix is not None:
        o_list, w_mix, g_mix = mix
        mix_specs = [tile] * len(o_list) + [_resident(w_mix.shape), _resident((1, d))]
        mix_args = list(o_list) + [w_mix.astype(BF16), g_mix.reshape(1, d)]
    return pl.pallas_call(
        functools.partial(_ffn_kernel, n_mix=len(mix_args) - 2 if mix_args else 0),
        grid=(n // TM,),
        in_specs=mix_specs + [
            tile,
            _resident((1, d)),
            _resident(w_in.shape),
            _resident(w_out.shape),
            _resident((1, d)),
        ],
        out_specs=tile,
        out_shape=jax.ShapeDtypeStruct((n, d), F32),
        compiler_params=_params(1),
        name="ffn",
    )(*mix_args, xf, g_pre.reshape(1, d), w_in.astype(BF16), w_out.astype(BF16),
      g_post.reshape(1, d))


def _proj_kernel(x_ref, g_ref, wr_ref, wct_ref, rows_ref, cols_ref):
    hn = _rms(x_ref[...], g_ref[...]).astype(BF16)
    rows_ref[...] = jnp.dot(hn, wr_ref[...], preferred_element_type=F32)
    cols_ref[...] = lax.dot_general(wct_ref[...], hn, _NT, preferred_element_type=F32)


def _proj(xf, g, w_rows, w_cols):
    n, d = xf.shape
    cr, cc = w_rows.shape[1], w_cols.shape[1]
    return pl.pallas_call(
        _proj_kernel,
        grid=(n // TM,),
        in_specs=[
            pl.BlockSpec((TM, d), lambda i: (i, 0)),
            _resident((1, d)),
            _resident((d, cr)),
            _resident((cc, d)),
        ],
        out_specs=[pl.BlockSpec((TM, cr), lambda i: (i, 0)),
                   pl.BlockSpec((cc, TM), lambda i: (0, i))],
        out_shape=[jax.ShapeDtypeStruct((n, cr), F32), jax.ShapeDtypeStruct((cc, n), F32)],
        compiler_params=_params(1),
        name="proj",
    )(xf, g.reshape(1, d), w_rows.astype(BF16), w_cols.T.astype(BF16))


def _rel_bucket(dist):
    dist = jnp.maximum(dist, 0)
    exact = REL_BUCKETS // 2
    d = jnp.maximum(dist, 1).astype(jnp.float32)
    log_b = exact + (jnp.log(d / exact) / math.log(REL_MAX_DIST / exact)
                     * (REL_BUCKETS - exact)).astype(jnp.int32)
    return jnp.where(dist < exact, dist, jnp.minimum(log_b, REL_BUCKETS - 1))


def _tile_distance(n_delta, xp):
    delta = xp.arange(n_delta)[:, None, None]
    return delta * TK + xp.arange(TQ)[None, None, :] - xp.arange(TK)[None, :, None]


def _bias_tiles(rel_bias, n_delta):
    bucket = _rel_bucket(_tile_distance(n_delta, jnp)).reshape(1, -1)
    onehot = (jnp.arange(REL_BUCKETS)[:, None] == bucket).astype(F32)
    tiles = jnp.dot(rel_bias.astype(F32).T, onehot, precision=lax.Precision.HIGHEST)
    return tiles.reshape(N_HEADS, n_delta, TK, TQ)


def _interleave_heads(w, n_groups, axis):
    r = N_HEADS // n_groups
    shape = w.shape
    w = w.reshape(shape[:axis] + (n_groups, r) + shape[axis + 1:])
    w = jnp.swapaxes(w, axis, axis + 1)
    return w.reshape(shape)


def _block_bias(bias_nat, n_groups, n_delta, window=None):
    dist = _tile_distance(n_delta, np)
    visible = dist >= 0
    if window is not None:
        visible &= dist < window
    bias = bias_nat[:, :n_delta]
    if n_groups > 1:
        bias = _interleave_heads(bias, n_groups, 0)
    bias = jnp.where(visible, bias * LOG2E, NEG)
    return bias.reshape((N_HB, SLOTS) + bias.shape[1:])


@dataclasses.dataclass(frozen=True)
class _AttnCfg:
    kv_width: int
    n_back: int | None = None
    sel_blk: int | None = None
    sel_from_input: bool = False
    gate_branch: int | None = None
    sink: bool = False

    @property
    def selecting(self):
        return self.sel_blk is not None

    @property
    def moba(self):
        return self.selecting and not self.sel_from_input


def _rank_rows(val):
    width = val.shape[1]
    row = lax.broadcasted_iota(jnp.int32, (1, width, 1), 1)
    rank = jnp.zeros(val.shape, F32)
    for m in range(width):
        vm = val[:, m:m + 1, :]
        rank = rank + jnp.where(row > m, jnp.where(vm >= val, 1.0, 0.0),
                                jnp.where(vm > val, 1.0, 0.0))
    return rank


def _u_chan0(hh, n_blk):
    return hh * n_blk + (LANES if hh < SLOTS // 2 else 0)


def _fold_rows(x, op):
    return op(x.reshape(x.shape[0] // SUBLANES, SUBLANES, x.shape[1]), axis=0)


def _attn_kernel(*refs, cfg: _AttnCfg):
    it = iter(refs)
    qt_ref, k_ref, vt_ref, bias_ref = next(it), next(it), next(it), next(it)
    sel_ref = next(it) if cfg.sel_from_input else None
    gate_ref = next(it) if cfg.gate_branch is not None else None
    sink_ref = next(it) if cfg.sink else None
    o_ref = next(it)
    kb_ref, vtb_ref = next(it), next(it)
    sc_ref, m_ref, l_ref, acc_ref = next(it), next(it), next(it), next(it)
    kmx_ref = next(it) if cfg.moba else None

    qt = pl.program_id(2)
    seq = k_ref.shape[1]
    n_tiles = seq // TK
    lane = lax.broadcasted_iota(jnp.int32, (1, HB_W), 1)
    slot_of_lane = _div_pow2(lane, HEAD_DIM)
    chan = lax.broadcasted_iota(jnp.int32, (HB_W, 1), 0)
    slot_of_chan = _div_pow2(chan, HEAD_DIM)
    n_sel_blk = seq // cfg.sel_blk if cfg.selecting else 0

    @pl.when(qt == 0)
    def _prepare():
        if cfg.moba:
            kmx_ref[...] = jnp.zeros(kmx_ref.shape, F32)
        for c in range(n_tiles):
            rows = pl.ds(c * TK, TK)
            k = k_ref[0, rows, :]
            vt = vt_ref[:, c * TK:(c + 1) * TK]
            if cfg.kv_width == LANES:
                k = jnp.concatenate([k, k], axis=1)
                vt = jnp.concatenate([vt, vt], axis=0)
            vtb_ref[c] = vt.astype(BF16)
            if cfg.selecting:
                key_blk = _div_pow2(c * TK + lax.broadcasted_iota(jnp.int32, (TK, 1), 0),
                                    cfg.sel_blk)
                for hh in range(SLOTS):
                    onehot = (lane - _u_chan0(hh, n_sel_blk)) == key_blk
                    kb_ref[hh, rows, :] = jnp.where(
                        slot_of_lane == hh, k, jnp.where(onehot, 1.0, 0.0)).astype(BF16)
            else:
                kb_ref[0, rows, :] = k.astype(BF16)
            if cfg.moba:
                k_mean = jnp.mean(k, axis=0, keepdims=True)
                for hh in range(SLOTS):
                    kmx_ref[pl.ds(hh * n_sel_blk + c, 1), :] = jnp.where(
                        slot_of_lane == hh, k_mean, 0.0)

    q_t = qt_ref[...]

    sel_rows = None
    if cfg.sel_from_input:
        sel_rows = sel_ref[0]
    elif cfg.moba:
        gate = jnp.dot(kmx_ref[...].astype(BF16), q_t.astype(BF16),
                       preferred_element_type=F32)
        gate = gate.reshape(SLOTS, n_sel_blk, TQ)
        blk = lax.broadcasted_iota(jnp.int32, (1, n_sel_blk, 1), 1)
        val = jnp.where(blk < qt, gate, NEG)
        rank = _rank_rows(val)
        chosen = jnp.where(blk < qt, jnp.where(rank < MOBA_TOPK, 1.0, 0.0), 0.0)
        chosen = jnp.where(blk == qt, 1.0, chosen)
        sel_rows = chosen.reshape(SLOTS * n_sel_blk, TQ)

    q_scaled = q_t * (SCALE * LOG2E)
    qms = []
    for hh in range(SLOTS):
        qm = jnp.where(slot_of_chan == hh, q_scaled, 0.0)
        if cfg.selecting:
            u0 = _u_chan0(hh, n_sel_blk)
            penalty = (sel_rows[hh * n_sel_blk:(hh + 1) * n_sel_blk] - 1.0) * SEL_BIG
            qm = qm + jnp.concatenate(
                [jnp.zeros((u0, TQ), F32), penalty,
                 jnp.zeros((HB_W - u0 - n_sel_blk, TQ), F32)], axis=0)
        qms.append(qm.astype(BF16))
    q_stack = None if cfg.selecting else jnp.concatenate(qms, axis=1)

    lo = jnp.maximum(qt - cfg.n_back, 0) if cfg.n_back is not None else 0
    n_odd = jnp.bitwise_and(qt + 1 - lo, 1)
    n_pairs = jnp.right_shift(qt + 1 - lo, 1)
    m_ref[...] = jnp.full(m_ref.shape, NEG, F32)
    l_ref[...] = jnp.zeros(l_ref.shape, F32)
    acc_ref[...] = jnp.zeros(acc_ref.shape, F32)

    def pass1(kt, width):
        rows = pl.ds(pl.multiple_of(kt * TK, TK), width * TK)
        if cfg.selecting:
            s = [jnp.dot(kb_ref[hh, rows, :], qms[hh], preferred_element_type=F32)
                 for hh in range(SLOTS)]
        else:
            s_all = jnp.dot(kb_ref[0, rows, :], q_stack, preferred_element_type=F32)
            s = [s_all[:, hh * TQ:(hh + 1) * TQ] for hh in range(SLOTS)]
        for hh in range(SLOTS):
            m_new = m_ref[hh]
            for w in range(width):
                tile = s[hh][w * TK:(w + 1) * TK] + bias_ref[0, hh, qt - kt - w]
                sc_ref[hh, kt - lo + w] = tile
                m_new = jnp.maximum(m_new, _fold_rows(tile, jnp.max))
            m_ref[hh] = m_new

    def pass2(kt, width, ms):
        for hh in range(SLOTS):
            head = pl.ds(hh * HEAD_DIM, HEAD_DIM)
            l_new = l_ref[hh]
            o_new = acc_ref[head, :]
            for w in range(width):
                p = jnp.exp2(sc_ref[hh, kt - lo + w] - ms[hh])
                l_new = l_new + _fold_rows(p, jnp.sum)
                o_new = o_new + jnp.dot(vtb_ref[kt + w, head, :], p.astype(BF16),
                                        preferred_element_type=F32)
            l_ref[hh] = l_new
            acc_ref[head, :] = o_new

    def over_tiles(body):
        @pl.when(n_odd == 1)
        def _single():
            body(lo, 1)

        def pair(i, carry):
            body(lo + n_odd + 2 * i, 2)
            return carry

        lax.fori_loop(0, n_pairs, pair, 0)

    over_tiles(pass1)
    sinks = ([sink_ref[0, :, hh:hh + 1] * LOG2E for hh in range(SLOTS)]
             if cfg.sink else None)
    ms = []
    for hh in range(SLOTS):
        m = jnp.max(m_ref[hh], axis=0, keepdims=True)
        ms.append(jnp.maximum(m, sinks[hh]) if cfg.sink else m)
    over_tiles(functools.partial(pass2, ms=ms))

    gsig = _sigmoid(gate_ref[...]) if cfg.gate_branch is not None else None
    outs = []
    for hh in range(SLOTS):
        denom = jnp.sum(l_ref[hh], axis=0, keepdims=True)
        if cfg.sink:
            denom = denom + jnp.exp2(sinks[hh] - ms[hh])
        o_h = acc_ref[hh * HEAD_DIM:(hh + 1) * HEAD_DIM, :] / jnp.maximum(denom, TINY)
        if cfg.gate_branch is not None:
            row = cfg.gate_branch * SLOTS + hh
            o_h = o_h * gsig[row:row + 1, :]
        outs.append(o_h)
    o_ref[0] = jnp.concatenate(outs, axis=0).T


def _attention(cfg, rows, cols, bias, *, k_blk, v_rblk, sel=None, gate_rblk0=None, sinks=None):
    bsz, seq, _ = rows.shape
    kw = cfg.kv_width
    n_delta = bias.shape[2]
    n_q = seq // TQ
    in_specs = [
        pl.BlockSpec((HB_W, TQ), lambda hb, b, qt: (hb, b * n_q + qt)),
        pl.BlockSpec((1, seq, kw), lambda hb, b, qt: (b, 0, k_blk(hb))),
        pl.BlockSpec((kw, seq), lambda hb, b, qt: (v_rblk(hb), b)),
        pl.BlockSpec((1, SLOTS, n_delta, TK, TQ), lambda hb, b, qt: (hb, 0, 0, 0, 0)),
    ]
    args = [cols, rows, cols, bias]
    if cfg.sel_from_input:
        in_specs.append(pl.BlockSpec((1, LANES, TQ), lambda hb, b, qt: (b, 0, qt)))
        args.append(sel)
    if cfg.gate_branch is not None:
        in_specs.append(
            pl.BlockSpec((GATE_ROWS, TQ), lambda hb, b, qt: (gate_rblk0 + hb, b * n_q + qt)))
        args.append(cols)
    if cfg.sink:
        in_specs.append(pl.BlockSpec((1, 1, LANES), lambda hb, b, qt: (hb, 0, 0)))
        args.append(sinks)
    scratch = [
        pltpu.VMEM((SLOTS if cfg.selecting else 1, seq, HB_W), BF16),
        pltpu.VMEM((seq // TK, HB_W, TK), BF16),
        pltpu.VMEM((SLOTS, n_delta, TK, TQ), F32),
        pltpu.VMEM((SLOTS, SUBLANES, TQ), F32),
        pltpu.VMEM((SLOTS, SUBLANES, TQ), F32),
        pltpu.VMEM((HB_W, TQ), F32),
    ]
    if cfg.moba:
        assert SLOTS * (seq // cfg.sel_blk) <= LANES
        scratch.append(pltpu.VMEM((SLOTS * (seq // cfg.sel_blk), HB_W), F32))
    return pl.pallas_call(
        functools.partial(_attn_kernel, cfg=cfg),
        grid=(N_HB, bsz, n_q),
        in_specs=in_specs,
        out_specs=pl.BlockSpec((1, TQ, HB_W), lambda hb, b, qt: (b, qt, hb)),
        out_shape=jax.ShapeDtypeStruct((bsz, seq, D_MODEL), F32),
        scratch_shapes=scratch,
        compiler_params=_params(3),
        name="attention",
    )(*args)


def _gelu_tanh(x):
    cdf = 0.5 * (1.0 + jnp.tanh(math.sqrt(2.0 / math.pi) * (x + 0.044715 * (x * x * x))))
    return x * cdf


def _compress_kernel(kc_lo_ref, kc_hi_ref, vc_lo_ref, vc_hi_ref, pos_ref, w1_ref, w2k_ref,
                     w2vt_ref, kcb_ref, vcbt_ref):
    half = CMP_LEN // 2
    n_chunk = kc_lo_ref.shape[1] // CMP_STRIDE
    flat_lane = lax.broadcasted_iota(jnp.int32, (1, half * HB_W), 1)
    group_of_lane = _div_pow2(jnp.bitwise_and(flat_lane, HB_W - 1), HEAD_DIM)
    for kind, (lo_ref, hi_ref) in enumerate(((kc_lo_ref, kc_hi_ref), (vc_lo_ref, vc_hi_ref))):
        chunks = [jnp.concatenate(
            [lo_ref[0, pl.ds(l, n_chunk, stride=CMP_STRIDE), :],
             hi_ref[0, pl.ds(l, n_chunk, stride=CMP_STRIDE), :]], axis=1) for l in range(half)]
        parts = []
        for h in range(2):
            pieces = [chunks[l] + pos_ref[kind, pl.ds(h * half + l, 1), :] for l in range(half)]
            flat = jnp.concatenate(pieces, axis=1)
            stacked = jnp.concatenate(
                [jnp.where(group_of_lane == g, flat, 0.0)
                 for g in range(B_KV_GROUPS)], axis=0).astype(BF16)
            parts.append(jnp.dot(stacked, w1_ref[kind, h], preferred_element_type=F32))
        rows = B_KV_GROUPS * n_chunk
        hidden = parts[0] + pltpu.roll(parts[1], rows - 1, axis=0)
        act = _gelu_tanh(hidden).astype(BF16)
        out = None
        for g in range(B_KV_GROUPS):
            act_g = act[g * n_chunk:(g + 1) * n_chunk]
            if kind == 0:
                o = jnp.dot(act_g, w2k_ref[g], preferred_element_type=F32)
            else:
                o = lax.dot_general(w2vt_ref[g], act_g, _NT, preferred_element_type=F32)
            out = o if out is None else out + o
        if kind == 0:
            kcb_ref[0] = out
        else:
            vcbt_ref[0] = out


def _compress(rows, kc_blk, vc_blk, pos, w1, w2k, w2vt):
    bsz, seq, _ = rows.shape
    n_chunk = seq // CMP_STRIDE
    return pl.pallas_call(
        _compress_kernel,
        grid=(bsz,),
        in_specs=[
            pl.BlockSpec((1, seq, LANES), lambda b: (b, 0, 2 * kc_blk)),
            pl.BlockSpec((1, seq, LANES), lambda b: (b, 0, 2 * kc_blk + 1)),
            pl.BlockSpec((1, seq, LANES), lambda b: (b, 0, 2 * vc_blk)),
            pl.BlockSpec((1, seq, LANES), lambda b: (b, 0, 2 * vc_blk + 1)),
            _resident(pos.shape),
            _resident(w1.shape),
            _resident(w2k.shape),
            _resident(w2vt.shape),
        ],
        out_specs=[pl.BlockSpec((1, n_chunk, HB_W), lambda b: (b, 0, 0)),
                   pl.BlockSpec((1, HB_W, n_chunk), lambda b: (b, 0, 0))],
        out_shape=[jax.ShapeDtypeStruct((bsz, n_chunk, HB_W), F32),
                   jax.ShapeDtypeStruct((bsz, HB_W, n_chunk), F32)],
        compiler_params=_params(1),
        name="nsa_compress",
    )(rows, rows, rows, rows, pos, w1, w2k, w2vt)


def _cmp_kernel(qt_ref, kcb_ref, vcbt_ref, gate_ref, ovt_ref, o_ref, sel_ref):
    qt = pl.program_id(1)
    n_cmp_pad = kcb_ref.shape[1]
    n_slc = LANES // B_KV_GROUPS
    chan = lax.broadcasted_iota(jnp.int32, (HB_W, 1), 0)
    slot_of_chan = _div_pow2(chan, HEAD_DIM)
    t = qt * TQ + lax.broadcasted_iota(jnp.int32, (1, TQ), 1)
    cmp_end = (lax.broadcasted_iota(jnp.int32, (n_cmp_pad, 1), 0) * CMP_STRIDE + CMP_LEN - 1)
    ok = cmp_end <= jnp.concatenate([t] * N_HB, axis=1)
    kc = kcb_ref[0].astype(BF16)
    vct = vcbt_ref[0].astype(BF16)
    gsig = _sigmoid(gate_ref[...])
    outs = [[None] * B_KV_GROUPS for _ in range(N_HB)]
    imp = None
    for g in range(B_KV_GROUPS):
        qm = jnp.concatenate(
            [jnp.where(slot_of_chan == g, qt_ref[r * HB_W:(r + 1) * HB_W, :], 0.0)
             for r in range(N_HB)], axis=1) * SCALE
        s = jnp.dot(kc, qm.astype(BF16), preferred_element_type=F32)
        s = jnp.where(ok, s, NEG)
        m = jnp.max(s, axis=0, keepdims=True)
        e = jnp.where(ok, jnp.exp(s - m), 0.0)
        p = e / jnp.maximum(jnp.sum(e, axis=0, keepdims=True), TINY)
        o = jnp.dot(vct[g * HEAD_DIM:(g + 1) * HEAD_DIM], p.astype(BF16),
                    preferred_element_type=F32)
        p_group = None
        for r in range(N_HB):
            p_r = p[:, r * TQ:(r + 1) * TQ]
            p_group = p_r if p_group is None else p_group + p_r
            row = r * GATE_ROWS + g
            outs[r][g] = o[:, r * TQ:(r + 1) * TQ] * gsig[row:row + 1, :]
        part = jnp.dot(ovt_ref[g], p_group, precision=lax.Precision.HIGHEST,
                       preferred_element_type=F32)
        imp = part if imp is None else imp + part
    for r in range(N_HB):
        o_ref[0, :, r * HB_W:(r + 1) * HB_W] = jnp.concatenate(outs[r], axis=0).T
    imp = imp.reshape(B_KV_GROUPS, n_slc, TQ)
    j = lax.broadcasted_iota(jnp.int32, (1, n_slc, 1), 1)
    cur = _div_pow2(t, SLC_BLK).reshape(1, 1, TQ)
    val = jnp.where(j == cur - 1, -NEG, imp)
    val = jnp.where(j == cur, -NEG, val)
    val = jnp.where(j == 0, -NEG, val)
    val = jnp.where(j > cur, NEG, val)
    chosen = jnp.where(_rank_rows(val) < SLC_TOPK, 1.0, 0.0)
    sel_ref[0] = chosen.reshape(LANES, TQ)


def _cmp_attention(cols, kcb, vcbt, gate_rblk, ovt, bsz, seq):
    n_cmp_pad = kcb.shape[1]
    n_q = seq // TQ
    return pl.pallas_call(
        _cmp_kernel,
        grid=(bsz, n_q),
        in_specs=[
            pl.BlockSpec((D_MODEL, TQ), lambda b, qt: (0, b * n_q + qt)),
            pl.BlockSpec((1, n_cmp_pad, HB_W), lambda b, qt: (b, 0, 0)),
            pl.BlockSpec((1, HB_W, n_cmp_pad), lambda b, qt: (b, 0, 0)),
            pl.BlockSpec((N_HB * GATE_ROWS, TQ), lambda b, qt: (gate_rblk, b * n_q + qt)),
            _resident(ovt.shape),
        ],
        out_specs=[
            pl.BlockSpec((1, TQ, D_MODEL), lambda b, qt: (b, qt, 0)),
            pl.BlockSpec((1, LANES, TQ), lambda b, qt: (b, 0, qt)),
        ],
        out_shape=[
            jax.ShapeDtypeStruct((bsz, seq, D_MODEL), F32),
            jax.ShapeDtypeStruct((bsz, LANES, seq), F32),
        ],
        compiler_params=_params(2),
        name="nsa_cmp_select",
    )(cols, kcb, vcbt, cols, ovt)


def _mixer_a(xf, bsz, seq, g_pre, w_in, w_out, sinks, bias_nat):
    qw, kvw = N_HEADS * HEAD_DIM, A_KV_HEADS * HEAD_DIM
    w_q = _interleave_heads(w_in[:, :qw].reshape(D_MODEL, N_HEADS, HEAD_DIM), A_KV_HEADS, 1)
    w_cols = jnp.concatenate([w_q.reshape(D_MODEL, qw), w_in[:, qw + kvw:]], axis=1)
    rows, cols = _proj(xf, g_pre, w_in[:, qw:qw + kvw], w_cols)
    rows = rows.reshape(bsz, seq, kvw)
    n_back = -(-(A_WINDOW - 1) // TK)
    bias = _block_bias(bias_nat, A_KV_HEADS, n_back + 1, A_WINDOW)
    sink_rows = _interleave_heads(sinks.astype(F32), A_KV_HEADS, 0).reshape(N_HB, 1, SLOTS)
    sink_rows = jnp.pad(sink_rows, ((0, 0), (0, 0), (0, LANES - SLOTS)))
    cfg = _AttnCfg(kv_width=kvw, n_back=n_back, sink=True)
    o = _attention(cfg, rows, cols, bias, k_blk=lambda hb: 0, v_rblk=lambda hb: qw // kvw,
                   sinks=sink_rows)
    w_o = _interleave_heads(w_out.reshape(N_HEADS, HEAD_DIM, D_MODEL), A_KV_HEADS, 0)
    return [o.reshape(bsz * seq, -1)], w_o.reshape(qw, D_MODEL)


def _nsa_overlap_t(seq):
    n_chunk = seq // CMP_STRIDE
    n_cmp = (seq - CMP_LEN) // CMP_STRIDE + 1
    n_slc = seq // SLC_BLK
    c_start = np.arange(n_cmp)[:, None] * CMP_STRIDE
    s_start = np.arange(n_slc)[None, :] * SLC_BLK
    overlap = ((c_start < s_start + SLC_BLK) & (c_start + CMP_LEN > s_start)).astype(np.float32)
    ovt = np.zeros((B_KV_GROUPS, LANES, n_chunk), np.float32)
    for g in range(B_KV_GROUPS):
        ovt[g, g * n_slc:(g + 1) * n_slc, :n_cmp] = overlap.T
    return jnp.asarray(ovt)


def _mixer_b(xf, bsz, seq, g_pre, w_in, w_out, cmp_pos, cmp_w1, cmp_w2, bias_nat):
    assert seq // SLC_BLK * B_KV_GROUPS == LANES and seq // CMP_STRIDE == LANES
    grp, dh = B_KV_GROUPS, HEAD_DIM
    qw, kvw = N_HEADS * dh, B_KV_GROUPS * dh
    w_q = _interleave_heads(w_in[:, :qw].reshape(D_MODEL, N_HEADS, dh), grp, 1)
    kv = [w_in[:, qw + i * kvw:qw + (i + 1) * kvw] for i in range(6)]
    w_g = w_in[:, qw + 6 * kvw:].reshape(D_MODEL, grp, N_HB, 3)
    w_g = w_g.transpose(0, 2, 3, 1).reshape(D_MODEL, N_HB, 3 * grp)
    w_g = jnp.pad(w_g, ((0, 0), (0, 0), (0, GATE_ROWS - 3 * grp)))
    w_rows = jnp.concatenate([kv[0], kv[1], kv[2], kv[4]], axis=1)
    w_cols = jnp.concatenate([w_q.reshape(D_MODEL, qw), kv[3], kv[5],
                              w_g.reshape(D_MODEL, N_HB * GATE_ROWS)], axis=1)
    rows, cols = _proj(xf, g_pre, w_rows, w_cols)
    rows = rows.reshape(bsz, seq, 4 * kvw)
    gate_row0 = qw + 2 * kvw

    half = CMP_LEN // 2
    pos = jnp.tile(cmp_pos.astype(F32), (1, 1, grp))
    w1 = cmp_w1.reshape(2, 2, half, 1, dh, CMP_HIDDEN)
    w1 = jnp.broadcast_to(w1, (2, 2, half, grp, dh, CMP_HIDDEN))
    w1 = w1.reshape(2, 2, half * HB_W, CMP_HIDDEN).astype(BF16)
    w2 = jnp.zeros((2, grp, CMP_HIDDEN, grp, dh), F32)
    for g in range(grp):
        w2 = w2.at[:, g, :, g, :].set(cmp_w2)
    w2 = w2.reshape(2, grp, CMP_HIDDEN, HB_W).astype(BF16)
    kcb, vcbt = _compress(rows, 0, 1, pos, w1, w2[0], jnp.swapaxes(w2[1], 1, 2))

    o_cmp, sel = _cmp_attention(cols, kcb, vcbt, gate_row0 // (N_HB * GATE_ROWS),
                                _nsa_overlap_t(seq), bsz, seq)

    gate_rblk0 = gate_row0 // GATE_ROWS
    bias = _block_bias(bias_nat, grp, seq // TK)
    cfg_slc = _AttnCfg(kv_width=kvw, sel_blk=SLC_BLK, sel_from_input=True, gate_branch=1)
    o_slc = _attention(cfg_slc, rows, cols, bias, k_blk=lambda hb: 2,
                       v_rblk=lambda hb: qw // kvw, sel=sel, gate_rblk0=gate_rblk0)
    n_back = -(-(B_WINDOW - 1) // TK)
    cfg_win = _AttnCfg(kv_width=kvw, n_back=n_back, gate_branch=2)
    bias_win = _block_bias(bias_nat, grp, n_back + 1, B_WINDOW)
    o_win = _attention(cfg_win, rows, cols, bias_win, k_blk=lambda hb: 3,
                       v_rblk=lambda hb: qw // kvw + 1, gate_rblk0=gate_rblk0)
    w_o = _interleave_heads(w_out.reshape(N_HEADS, dh, D_MODEL), grp, 0).reshape(qw, D_MODEL)
    n = bsz * seq
    return [o_cmp.reshape(n, -1), o_slc.reshape(n, -1), o_win.reshape(n, -1)], w_o


def _mixer_c(xf, bsz, seq, g_pre, w_in, w_out, bias_nat):
    assert seq % MOBA_BLK == 0 and MOBA_BLK == TK
    qw = N_HEADS * HEAD_DIM
    w_cols = jnp.concatenate([w_in[:, :qw], w_in[:, 2 * qw:]], axis=1)
    rows, cols = _proj(xf, g_pre, w_in[:, qw:2 * qw], w_cols)
    rows = rows.reshape(bsz, seq, qw)
    bias = _block_bias(bias_nat, 1, seq // TK)
    cfg = _AttnCfg(kv_width=HB_W, sel_blk=MOBA_BLK)
    o = _attention(cfg, rows, cols, bias, k_blk=lambda hb: hb, v_rblk=lambda hb: N_HB + hb)
    return [o.reshape(bsz * seq, -1)], w_out


def kernel(x, rel_bias, norm_g, ffn_w_in, ffn_w_out, a_w_in, a_w_out, a_sinks, b_w_in, b_w_out,
           b_cmp_pos, b_cmp_w1, b_cmp_w2, c_w_in, c_w_out):
    bsz, seq, d = x.shape
    assert d == D_MODEL and seq % TQ == 0 and (bsz * seq) % TM == 0
    depth = norm_g.shape[0]
    bias_nat = _bias_tiles(rel_bias, seq // TK)
    xf = x.reshape(bsz * seq, d)
    for i in range(depth):
        g = norm_g[i]
        xf = _ffn(xf, g[0], ffn_w_in[i, 0], ffn_w_out[i, 0], g[1])
        kind, j = i % N_MIXERS, i // N_MIXERS
        if kind == 0:
            o_list, w_mix = _mixer_a(xf, bsz, seq, g[2], a_w_in[j], a_w_out[j], a_sinks[j],
                                     bias_nat)
        elif kind == 1:
            o_list, w_mix = _mixer_b(xf, bsz, seq, g[2], b_w_in[j], b_w_out[j], b_cmp_pos[j],
                                     b_cmp_w1[j], b_cmp_w2[j], bias_nat)
        else:
            o_list, w_mix = _mixer_c(xf, bsz, seq, g[2], c_w_in[j], c_w_out[j], bias_nat)
        xf = _ffn(xf, g[4], ffn_w_in[i, 1], ffn_w_out[i, 1], g[5], mix=(o_list, w_mix, g[3]))
    return xf.reshape(bsz, seq, d)
```

```python
import dataclasses
import functools
import math

import numpy as np
import jax
import jax.numpy as jnp
from jax import lax
from jax.experimental import pallas as pl
from jax.experimental.pallas import tpu as pltpu

F32 = jnp.float32
BF16 = jnp.bfloat16

D_MODEL = 1024
HEAD_DIM = 64
N_HEADS = D_MODEL // HEAD_DIM
D_FF = 2816
NORM_EPS = 1e-6
N_MIXERS = 3
REL_BUCKETS = 32
REL_MAX_DIST = 1024
A_KV_HEADS = 2
A_WINDOW = 128
B_KV_GROUPS = 4
CMP_LEN = 32
CMP_STRIDE = 16
CMP_HIDDEN = 256
SLC_BLK = 64
SLC_TOPK = 16
B_WINDOW = 512
MOBA_BLK = 256
MOBA_TOPK = 3
NEG = -1e30
TINY = 1e-30

LANES = 128
SUBLANES = 8
MXU_DIM = 256
VMEM_LIMIT_BYTES = 56 * 1024 * 1024

SLOTS = 4
HB_W = SLOTS * HEAD_DIM
N_HB = N_HEADS // SLOTS
TQ = 256
TK = 256
TM = 512
FF_CHUNKS = ((0, 1536), (1536, D_FF))
GATE_ROWS = 16
SEL_BIG = float(2 ** 30)
SCALE = HEAD_DIM ** -0.5
LOG2E = math.log2(math.e)

_NT = (((1,), (1,)), ((), ()))


def _params(n_axes):
    return pltpu.CompilerParams(
        dimension_semantics=("arbitrary",) * n_axes,
        vmem_limit_bytes=VMEM_LIMIT_BYTES)


def _rms(x, g):
    return x * lax.rsqrt(jnp.mean(x * x, axis=-1, keepdims=True) + NORM_EPS) * g


def _sigmoid(x):
    return 1.0 / (1.0 + jnp.exp(-x))


def _div_pow2(x, d):
    assert d & (d - 1) == 0
    return jnp.right_shift(x, d.bit_length() - 1)


def _resident(shape):
    zeros = (0,) * len(shape)
    return pl.BlockSpec(shape, lambda *_: zeros, pipeline_mode=pl.Buffered(1))


def _ffn_kernel(*refs, n_mix):
    mix_refs, refs = refs[:n_mix], refs[n_mix:]
    if n_mix:
        wmix_ref, gmix_ref = refs[:2]
        refs = refs[2:]
    x_ref, gpre_ref, win_ref, wout_ref, gpost_ref, o_ref = refs
    x = x_ref[...]
    if n_mix:
        o = mix_refs[0][...]
        for r in mix_refs[1:]:
            o = o + r[...]
        mixed = jnp.dot(o.astype(BF16), wmix_ref[...], preferred_element_type=F32)
        x = x + _rms(mixed, gmix_ref[...])
    hn = _rms(x, gpre_ref[...]).astype(BF16)
    y = None
    for lo, hi in FF_CHUNKS:
        gate = jnp.dot(hn, win_ref[:, lo:hi], preferred_element_type=F32)
        up = jnp.dot(hn, win_ref[:, D_FF + lo:D_FF + hi], preferred_element_type=F32)
        act = (gate * _sigmoid(gate) * up).astype(BF16)
        part = jnp.dot(act, wout_ref[lo:hi, :], preferred_element_type=F32)
        y = part if y is None else y + part
    o_ref[...] = x + 0.5 * _rms(y, gpost_ref[...])


def _ffn(xf, g_pre, w_in, w_out, g_post, mix=None):
    n, d = xf.shape
    tile = pl.BlockSpec((TM, d), lambda i: (i, 0))
    mix_specs, mix_args = [], []
    if mix is not None:
        o_list, w_mix, g_mix = mix
        mix_specs = [tile] * len(o_list) + [_resident(w_mix.shape), _resident((1, d))]
        mix_args = list(o_list) + [w_mix.astype(BF16), g_mix.reshape(1, d)]
    return pl.pallas_call(
        functools.partial(_ffn_kernel, n_mix=len(mix_args) - 2 if mix_args else 0),
        grid=(n // TM,),
        in_specs=mix_specs + [
            tile,
            _resident((1, d)),
            _resident(w_in.shape),
            _resident(w_out.shape),
            _resident((1, d)),
        ],
        out_specs=tile,
        out_shape=jax.ShapeDtypeStruct((n, d), F32),
        compiler_params=_params(1),
        name="ffn",
    )(*mix_args, xf, g_pre.reshape(1, d), w_in.astype(BF16), w_out.astype(BF16),
      g_post.reshape(1, d))


def _proj_kernel(x_ref, g_ref, wr_ref, wct_ref, rows_ref, cols_ref):
    hn = _rms(x_ref[...], g_ref[...]).astype(BF16)
    rows_ref[...] = jnp.dot(hn, wr_ref[...], preferred_element_type=F32)
    cols_ref[...] = lax.dot_general(wct_ref[...], hn, _NT, preferred_element_type=F32)


def _proj(xf, g, w_rows, w_cols):
    n, d = xf.shape
    cr, cc = w_rows.shape[1], w_cols.shape[1]
    return pl.pallas_call(
        _proj_kernel,
        grid=(n // TM,),
        in_specs=[
            pl.BlockSpec((TM, d), lambda i: (i, 0)),
            _resident((1, d)),
            _resident((d, cr)),
            _resident((cc, d)),
        ],
        out_specs=[pl.BlockSpec((TM, cr), lambda i: (i, 0)),
                   pl.BlockSpec((cc, TM), lambda i: (0, i))],
        out_shape=[jax.ShapeDtypeStruct((n, cr), F32), jax.ShapeDtypeStruct((cc, n), F32)],
        compiler_params=_params(1),
        name="proj",
    )(xf, g.reshape(1, d), w_rows.astype(BF16), w_cols.T.astype(BF16))


def _rel_bucket(dist):
    dist = jnp.maximum(dist, 0)
    exact = REL_BUCKETS // 2
    d = jnp.maximum(dist, 1).astype(jnp.float32)
    log_b = exact + (jnp.log(d / exact) / math.log(REL_MAX_DIST / exact)
                     * (REL_BUCKETS - exact)).astype(jnp.int32)
    return jnp.where(dist < exact, dist, jnp.minimum(log_b, REL_BUCKETS - 1))


def _tile_distance(n_delta, xp):
    delta = xp.arange(n_delta)[:, None, None]
    return delta * TK + xp.arange(TQ)[None, None, :] - xp.arange(TK)[None, :, None]


def _bias_of_distance(rel_bias, dist):
    bucket = _rel_bucket(dist).reshape(1, -1)
    onehot = (jnp.arange(REL_BUCKETS)[:, None] == bucket).astype(F32)
    tiles = jnp.dot(rel_bias.astype(F32).T, onehot, precision=lax.Precision.HIGHEST)
    return tiles.reshape((N_HEADS,) + dist.shape)


def _bias_tiles(rel_bias, n_delta):
    return _bias_of_distance(rel_bias, _tile_distance(n_delta, jnp))


def _band_distance(band_rows, xp):
    first = xp.arange(TQ)[None, :] - xp.arange(band_rows)[:, None]
    return xp.stack([first, first + (band_rows - TQ)])


def _band_bias(rel_bias, n_groups, band_rows, window):
    bias = _bias_of_distance(rel_bias, _band_distance(band_rows, jnp))
    bias = _interleave_heads(bias, n_groups, 0)
    dist = _band_distance(band_rows, np)
    bias = jnp.where((dist >= 0) & (dist < window), bias * LOG2E, NEG)
    return bias.reshape((N_HB, SLOTS) + bias.shape[1:])


def _interleave_heads(w, n_groups, axis):
    r = N_HEADS // n_groups
    shape = w.shape
    w = w.reshape(shape[:axis] + (n_groups, r) + shape[axis + 1:])
    w = jnp.swapaxes(w, axis, axis + 1)
    return w.reshape(shape)


def _block_bias(bias_nat, n_groups, n_delta, window=None):
    dist = _tile_distance(n_delta, np)
    visible = dist >= 0
    if window is not None:
        visible &= dist < window
    bias = bias_nat[:, :n_delta]
    if n_groups > 1:
        bias = _interleave_heads(bias, n_groups, 0)
    bias = jnp.where(visible, bias * LOG2E, NEG)
    return bias.reshape((N_HB, SLOTS) + bias.shape[1:])


@dataclasses.dataclass(frozen=True)
class _AttnCfg:
    kv_width: int
    n_back: int | None = None
    band_rows: int | None = None
    sel_blk: int | None = None
    sel_from_input: bool = False
    gate_branch: int | None = None
    sink: bool = False

    @property
    def selecting(self):
        return self.sel_blk is not None

    @property
    def moba(self):
        return self.selecting and not self.sel_from_input


def _rank_rows(val):
    width = val.shape[1]
    row = lax.broadcasted_iota(jnp.int32, (1, width, 1), 1)
    rank = jnp.zeros(val.shape, F32)
    for m in range(width):
        vm = val[:, m:m + 1, :]
        rank = rank + jnp.where(row > m, jnp.where(vm >= val, 1.0, 0.0),
                                jnp.where(vm > val, 1.0, 0.0))
    return rank


def _u_chan0(hh, n_blk):
    return hh * n_blk + (LANES if hh < SLOTS // 2 else 0)


def _fold_rows(x, op):
    return op(x.reshape(x.shape[0] // SUBLANES, SUBLANES, x.shape[1]), axis=0)


def _attn_kernel(*refs, cfg: _AttnCfg):
    it = iter(refs)
    qt_ref, k_ref, vt_ref, bias_ref = next(it), next(it), next(it), next(it)
    sel_ref = next(it) if cfg.sel_from_input else None
    gate_ref = next(it) if cfg.gate_branch is not None else None
    sink_ref = next(it) if cfg.sink else None
    o_ref = next(it)
    kb_ref, vtb_ref = next(it), next(it)
    if cfg.band_rows is None:
        sc_ref, m_ref, l_ref, acc_ref = next(it), next(it), next(it), next(it)
    kmx_ref = next(it) if cfg.moba else None
    vt_tile = LANES if cfg.band_rows is not None else TK

    qt = pl.program_id(2)
    seq = k_ref.shape[1]
    n_tiles = seq // TK
    lane = lax.broadcasted_iota(jnp.int32, (1, HB_W), 1)
    slot_of_lane = _div_pow2(lane, HEAD_DIM)
    chan = lax.broadcasted_iota(jnp.int32, (HB_W, 1), 0)
    slot_of_chan = _div_pow2(chan, HEAD_DIM)
    n_sel_blk = seq // cfg.sel_blk if cfg.selecting else 0

    @pl.when(qt == 0)
    def _prepare():
        if cfg.moba:
            kmx_ref[...] = jnp.zeros(kmx_ref.shape, F32)
        for c in range(n_tiles):
            rows = pl.ds(c * TK, TK)
            k = k_ref[0, rows, :]
            vt = vt_ref[:, c * TK:(c + 1) * TK]
            if cfg.kv_width == LANES:
                k = jnp.concatenate([k, k], axis=1)
                vt = jnp.concatenate([vt, vt], axis=0)
            for i in range(TK // vt_tile):
                vtb_ref[c * (TK // vt_tile) + i] = vt[:, i * vt_tile:(i + 1) * vt_tile].astype(BF16)
            if cfg.selecting:
                key_blk = _div_pow2(c * TK + lax.broadcasted_iota(jnp.int32, (TK, 1), 0),
                                    cfg.sel_blk)
                for hh in range(SLOTS):
                    onehot = (lane - _u_chan0(hh, n_sel_blk)) == key_blk
                    kb_ref[hh, rows, :] = jnp.where(
                        slot_of_lane == hh, k, jnp.where(onehot, 1.0, 0.0)).astype(BF16)
            else:
                kb_ref[0, rows, :] = k.astype(BF16)
            if cfg.moba:
                k_mean = jnp.mean(k, axis=0, keepdims=True)
                for hh in range(SLOTS):
                    kmx_ref[pl.ds(hh * n_sel_blk + c, 1), :] = jnp.where(
                        slot_of_lane == hh, k_mean, 0.0)

    q_t = qt_ref[...]

    sel_rows = None
    if cfg.sel_from_input:
        sel_rows = sel_ref[0]
    elif cfg.moba:
        gate = jnp.dot(kmx_ref[...].astype(BF16), q_t.astype(BF16),
                       preferred_element_type=F32)
        gate = gate.reshape(SLOTS, n_sel_blk, TQ)
        blk = lax.broadcasted_iota(jnp.int32, (1, n_sel_blk, 1), 1)
        val = jnp.where(blk < qt, gate, NEG)
        rank = _rank_rows(val)
        chosen = jnp.where(blk < qt, jnp.where(rank < MOBA_TOPK, 1.0, 0.0), 0.0)
        chosen = jnp.where(blk == qt, 1.0, chosen)
        sel_rows = chosen.reshape(SLOTS * n_sel_blk, TQ)

    q_scaled = q_t * (SCALE * LOG2E)
    qms = []
    for hh in range(SLOTS):
        qm = jnp.where(slot_of_chan == hh, q_scaled, 0.0)
        if cfg.selecting:
            u0 = _u_chan0(hh, n_sel_blk)
            penalty = (sel_rows[hh * n_sel_blk:(hh + 1) * n_sel_blk] - 1.0) * SEL_BIG
            qm = qm + jnp.concatenate(
                [jnp.zeros((u0, TQ), F32), penalty,
                 jnp.zeros((HB_W - u0 - n_sel_blk, TQ), F32)], axis=0)
        qms.append(qm.astype(BF16))
    q_stack = None if cfg.selecting else jnp.concatenate(qms, axis=1)

    sinks = ([sink_ref[0, :, hh:hh + 1] * LOG2E for hh in range(SLOTS)]
             if cfg.sink else None)
    if cfg.band_rows is not None:
        ms, sums, accs = _band_softmax(cfg, qt, q_stack, kb_ref, vtb_ref, bias_ref, sinks)
    else:
        ms, sums, accs = _tiled_softmax(cfg, qt, qms, q_stack, kb_ref, vtb_ref, bias_ref,
                                        sinks, sc_ref, m_ref, l_ref, acc_ref)

    gsig = _sigmoid(gate_ref[...]) if cfg.gate_branch is not None else None
    outs = []
    for hh in range(SLOTS):
        denom = jnp.sum(sums[hh], axis=0, keepdims=True)
        if cfg.sink:
            denom = denom + jnp.exp2(sinks[hh] - ms[hh])
        o_h = accs[hh] / jnp.maximum(denom, TINY)
        if cfg.gate_branch is not None:
            row = cfg.gate_branch * SLOTS + hh
            o_h = o_h * gsig[row:row + 1, :]
        outs.append(o_h)
    o_ref[0] = jnp.concatenate(outs, axis=0).T


def _band_softmax(cfg, qt, q_stack, kb_ref, vtb_ref, bias_ref, sinks):
    band = cfg.band_rows
    start = pl.multiple_of(jnp.maximum(qt * TQ - (band - TQ), 0), LANES)
    s_all = jnp.dot(kb_ref[0, pl.ds(start, band), :], q_stack, preferred_element_type=F32)
    which = jnp.minimum(qt, 1)
    tile0 = _div_pow2(start, LANES)
    ms, sums, accs = [], [], []
    for hh in range(SLOTS):
        head = pl.ds(hh * HEAD_DIM, HEAD_DIM)
        sc = s_all[:, hh * TQ:(hh + 1) * TQ] + bias_ref[0, hh, which]
        m = jnp.max(_fold_rows(sc, jnp.max), axis=0, keepdims=True)
        if cfg.sink:
            m = jnp.maximum(m, sinks[hh])
        p = jnp.exp2(sc - m)
        vt = jnp.concatenate([vtb_ref[tile0 + c, head, :] for c in range(band // LANES)],
                             axis=1)
        ms.append(m)
        sums.append(_fold_rows(p, jnp.sum))
        accs.append(jnp.dot(vt, p.astype(BF16), preferred_element_type=F32))
    return ms, sums, accs


def _tiled_softmax(cfg, qt, qms, q_stack, kb_ref, vtb_ref, bias_ref, sinks,
                   sc_ref, m_ref, l_ref, acc_ref):
    lo = jnp.maximum(qt - cfg.n_back, 0) if cfg.n_back is not None else 0
    m_ref[...] = jnp.full(m_ref.shape, NEG, F32)
    l_ref[...] = jnp.zeros(l_ref.shape, F32)
    acc_ref[...] = jnp.zeros(acc_ref.shape, F32)

    def pass1(kt, width):
        rows = pl.ds(pl.multiple_of(kt * TK, TK), width * TK)
        if cfg.selecting:
            s = [jnp.dot(kb_ref[hh, rows, :], qms[hh], preferred_element_type=F32)
                 for hh in range(SLOTS)]
        else:
            s_all = jnp.dot(kb_ref[0, rows, :], q_stack, preferred_element_type=F32)
            s = [s_all[:, hh * TQ:(hh + 1) * TQ] for hh in range(SLOTS)]
        for hh in range(SLOTS):
            m_new = m_ref[hh]
            for w in range(width):
                tile = s[hh][w * TK:(w + 1) * TK] + bias_ref[0, hh, qt - kt - w]
                sc_ref[hh, kt - lo + w] = tile
                m_new = jnp.maximum(m_new, _fold_rows(tile, jnp.max))
            m_ref[hh] = m_new

    def pass2(kt, width, ms):
        for hh in range(SLOTS):
            head = pl.ds(hh * HEAD_DIM, HEAD_DIM)
            l_new = l_ref[hh]
            o_new = acc_ref[head, :]
            for w in range(width):
                p = jnp.exp2(sc_ref[hh, kt - lo + w] - ms[hh])
                l_new = l_new + _fold_rows(p, jnp.sum)
                o_new = o_new + jnp.dot(vtb_ref[kt + w, head, :], p.astype(BF16),
                                        preferred_element_type=F32)
            l_ref[hh] = l_new
            acc_ref[head, :] = o_new

    def over_tiles(body):
        n = qt + 1 - lo
        start = lo
        for width in (1, 2):
            has = jnp.bitwise_and(n, width)

            @pl.when(has != 0)
            def _part(start=start, width=width):
                body(start, width)

            start = start + has

        def group(i, carry):
            body(start + 4 * i, 4)
            return carry

        lax.fori_loop(0, jnp.right_shift(n, 2), group, 0)

    over_tiles(pass1)
    ms = []
    for hh in range(SLOTS):
        m = jnp.max(m_ref[hh], axis=0, keepdims=True)
        ms.append(jnp.maximum(m, sinks[hh]) if cfg.sink else m)
    over_tiles(functools.partial(pass2, ms=ms))
    return (ms, [l_ref[hh] for hh in range(SLOTS)],
            [acc_ref[hh * HEAD_DIM:(hh + 1) * HEAD_DIM, :] for hh in range(SLOTS)])


def _attention(cfg, rows, cols, bias, *, k_blk, v_rblk, sel=None, gate_rblk0=None, sinks=None):
    bsz, seq, _ = rows.shape
    kw = cfg.kv_width
    n_delta = bias.shape[2]
    n_q = seq // TQ
    in_specs = [
        pl.BlockSpec((HB_W, TQ), lambda hb, b, qt: (hb, b * n_q + qt)),
        pl.BlockSpec((1, seq, kw), lambda hb, b, qt: (b, 0, k_blk(hb))),
        pl.BlockSpec((kw, seq), lambda hb, b, qt: (v_rblk(hb), b)),
        pl.BlockSpec((1,) + bias.shape[1:], lambda hb, b, qt: (hb, 0, 0, 0, 0)),
    ]
    args = [cols, rows, cols, bias]
    if cfg.sel_from_input:
        in_specs.append(pl.BlockSpec((1, LANES, TQ), lambda hb, b, qt: (b, 0, qt)))
        args.append(sel)
    if cfg.gate_branch is not None:
        in_specs.append(
            pl.BlockSpec((GATE_ROWS, TQ), lambda hb, b, qt: (gate_rblk0 + hb, b * n_q + qt)))
        args.append(cols)
    if cfg.sink:
        in_specs.append(pl.BlockSpec((1, 1, LANES), lambda hb, b, qt: (hb, 0, 0)))
        args.append(sinks)
    vt_tile = LANES if cfg.band_rows is not None else TK
    scratch = [
        pltpu.VMEM((SLOTS if cfg.selecting else 1, seq, HB_W), BF16),
        pltpu.VMEM((seq // vt_tile, HB_W, vt_tile), BF16),
    ]
    if cfg.band_rows is None:
        scratch += [
            pltpu.VMEM((SLOTS, n_delta, TK, TQ), F32),
            pltpu.VMEM((SLOTS, SUBLANES, TQ), F32),
            pltpu.VMEM((SLOTS, SUBLANES, TQ), F32),
            pltpu.VMEM((HB_W, TQ), F32),
        ]
    if cfg.moba:
        assert SLOTS * (seq // cfg.sel_blk) <= LANES
        scratch.append(pltpu.VMEM((SLOTS * (seq // cfg.sel_blk), HB_W), F32))
    return pl.pallas_call(
        functools.partial(_attn_kernel, cfg=cfg),
        grid=(N_HB, bsz, n_q),
        in_specs=in_specs,
        out_specs=pl.BlockSpec((1, TQ, HB_W), lambda hb, b, qt: (b, qt, hb)),
        out_shape=jax.ShapeDtypeStruct((bsz, seq, D_MODEL), F32),
        scratch_shapes=scratch,
        compiler_params=_params(3),
        name="attention",
    )(*args)


def _gelu_tanh(x):
    cdf = 0.5 * (1.0 + jnp.tanh(math.sqrt(2.0 / math.pi) * (x + 0.044715 * (x * x * x))))
    return x * cdf


def _compress_kernel(kc_lo_ref, kc_hi_ref, vc_lo_ref, vc_hi_ref, pos_ref, w1_ref, w2k_ref,
                     w2vt_ref, kcb_ref, vcbt_ref):
    half = CMP_LEN // 2
    n_chunk = kc_lo_ref.shape[1] // CMP_STRIDE
    flat_lane = lax.broadcasted_iota(jnp.int32, (1, half * HB_W), 1)
    group_of_lane = _div_pow2(jnp.bitwise_and(flat_lane, HB_W - 1), HEAD_DIM)
    for kind, (lo_ref, hi_ref) in enumerate(((kc_lo_ref, kc_hi_ref), (vc_lo_ref, vc_hi_ref))):
        chunks = [jnp.concatenate(
            [lo_ref[0, pl.ds(l, n_chunk, stride=CMP_STRIDE), :],
             hi_ref[0, pl.ds(l, n_chunk, stride=CMP_STRIDE), :]], axis=1) for l in range(half)]
        parts = []
        for h in range(2):
            pieces = [chunks[l] + pos_ref[kind, pl.ds(h * half + l, 1), :] for l in range(half)]
            flat = jnp.concatenate(pieces, axis=1)
            stacked = jnp.concatenate(
                [jnp.where(group_of_lane == g, flat, 0.0)
                 for g in range(B_KV_GROUPS)], axis=0).astype(BF16)
            parts.append(jnp.dot(stacked, w1_ref[kind, h], preferred_element_type=F32))
        rows = B_KV_GROUPS * n_chunk
        hidden = parts[0] + pltpu.roll(parts[1], rows - 1, axis=0)
        act = _gelu_tanh(hidden).astype(BF16)
        out = None
        for g in range(B_KV_GROUPS):
            act_g = act[g * n_chunk:(g + 1) * n_chunk]
            if kind == 0:
                o = jnp.dot(act_g, w2k_ref[g], preferred_element_type=F32)
            else:
                o = lax.dot_general(w2vt_ref[g], act_g, _NT, preferred_element_type=F32)
            out = o if out is None else out + o
        if kind == 0:
            kcb_ref[0] = out
        else:
            vcbt_ref[0] = out


def _compress(rows, kc_blk, vc_blk, pos, w1, w2k, w2vt):
    bsz, seq, _ = rows.shape
    n_chunk = seq // CMP_STRIDE
    return pl.pallas_call(
        _compress_kernel,
        grid=(bsz,),
        in_specs=[
            pl.BlockSpec((1, seq, LANES), lambda b: (b, 0, 2 * kc_blk)),
            pl.BlockSpec((1, seq, LANES), lambda b: (b, 0, 2 * kc_blk + 1)),
            pl.BlockSpec((1, seq, LANES), lambda b: (b, 0, 2 * vc_blk)),
            pl.BlockSpec((1, seq, LANES), lambda b: (b, 0, 2 * vc_blk + 1)),
            _resident(pos.shape),
            _resident(w1.shape),
            _resident(w2k.shape),
            _resident(w2vt.shape),
        ],
        out_specs=[pl.BlockSpec((1, n_chunk, HB_W), lambda b: (b, 0, 0)),
                   pl.BlockSpec((1, HB_W, n_chunk), lambda b: (b, 0, 0))],
        out_shape=[jax.ShapeDtypeStruct((bsz, n_chunk, HB_W), F32),
                   jax.ShapeDtypeStruct((bsz, HB_W, n_chunk), F32)],
        compiler_params=_params(1),
        name="nsa_compress",
    )(rows, rows, rows, rows, pos, w1, w2k, w2vt)


def _cmp_kernel(qt_ref, kcb_ref, vcbt_ref, gate_ref, ovt_ref, o_ref, sel_ref):
    qt = pl.program_id(1)
    n_cmp_pad = kcb_ref.shape[1]
    n_slc = LANES // B_KV_GROUPS
    chan = lax.broadcasted_iota(jnp.int32, (HB_W, 1), 0)
    slot_of_chan = _div_pow2(chan, HEAD_DIM)
    t = qt * TQ + lax.broadcasted_iota(jnp.int32, (1, TQ), 1)
    cmp_end = (lax.broadcasted_iota(jnp.int32, (n_cmp_pad, 1), 0) * CMP_STRIDE + CMP_LEN - 1)
    ok = cmp_end <= jnp.concatenate([t] * N_HB, axis=1)
    kc = kcb_ref[0].astype(BF16)
    vct = vcbt_ref[0].astype(BF16)
    gsig = _sigmoid(gate_ref[...])
    outs = [[None] * B_KV_GROUPS for _ in range(N_HB)]
    imp = None
    for g in range(B_KV_GROUPS):
        qm = jnp.concatenate(
            [jnp.where(slot_of_chan == g, qt_ref[r * HB_W:(r + 1) * HB_W, :], 0.0)
             for r in range(N_HB)], axis=1) * SCALE
        s = jnp.dot(kc, qm.astype(BF16), preferred_element_type=F32)
        s = jnp.where(ok, s, NEG)
        m = jnp.max(s, axis=0, keepdims=True)
        e = jnp.where(ok, jnp.exp(s - m), 0.0)
        p = e / jnp.maximum(jnp.sum(e, axis=0, keepdims=True), TINY)
        o = jnp.dot(vct[g * HEAD_DIM:(g + 1) * HEAD_DIM], p.astype(BF16),
                    preferred_element_type=F32)
        p_group = None
        for r in range(N_HB):
            p_r = p[:, r * TQ:(r + 1) * TQ]
            p_group = p_r if p_group is None else p_group + p_r
            row = r * GATE_ROWS + g
            outs[r][g] = o[:, r * TQ:(r + 1) * TQ] * gsig[row:row + 1, :]
        part = jnp.dot(ovt_ref[g], p_group, precision=lax.Precision.HIGHEST,
                       preferred_element_type=F32)
        imp = part if imp is None else imp + part
    for r in range(N_HB):
        o_ref[0, :, r * HB_W:(r + 1) * HB_W] = jnp.concatenate(outs[r], axis=0).T
    imp = imp.reshape(B_KV_GROUPS, n_slc, TQ)
    j = lax.broadcasted_iota(jnp.int32, (1, n_slc, 1), 1)
    cur = _div_pow2(t, SLC_BLK).reshape(1, 1, TQ)
    val = jnp.where(j == cur - 1, -NEG, imp)
    val = jnp.where(j == cur, -NEG, val)
    val = jnp.where(j == 0, -NEG, val)
    val = jnp.where(j > cur, NEG, val)
    chosen = jnp.where(_rank_rows(val) < SLC_TOPK, 1.0, 0.0)
    sel_ref[0] = chosen.reshape(LANES, TQ)


def _cmp_attention(cols, kcb, vcbt, gate_rblk, ovt, bsz, seq):
    n_cmp_pad = kcb.shape[1]
    n_q = seq // TQ
    return pl.pallas_call(
        _cmp_kernel,
        grid=(bsz, n_q),
        in_specs=[
            pl.BlockSpec((D_MODEL, TQ), lambda b, qt: (0, b * n_q + qt)),
            pl.BlockSpec((1, n_cmp_pad, HB_W), lambda b, qt: (b, 0, 0)),
            pl.BlockSpec((1, HB_W, n_cmp_pad), lambda b, qt: (b, 0, 0)),
            pl.BlockSpec((N_HB * GATE_ROWS, TQ), lambda b, qt: (gate_rblk, b * n_q + qt)),
            _resident(ovt.shape),
        ],
        out_specs=[
            pl.BlockSpec((1, TQ, D_MODEL), lambda b, qt: (b, qt, 0)),
            pl.BlockSpec((1, LANES, TQ), lambda b, qt: (b, 0, qt)),
        ],
        out_shape=[
            jax.ShapeDtypeStruct((bsz, seq, D_MODEL), F32),
            jax.ShapeDtypeStruct((bsz, LANES, seq), F32),
        ],
        compiler_params=_params(2),
        name="nsa_cmp_select",
    )(cols, kcb, vcbt, cols, ovt)


def _mixer_a(xf, bsz, seq, g_pre, w_in, w_out, sinks, rel_bias):
    qw, kvw = N_HEADS * HEAD_DIM, A_KV_HEADS * HEAD_DIM
    w_q = _interleave_heads(w_in[:, :qw].reshape(D_MODEL, N_HEADS, HEAD_DIM), A_KV_HEADS, 1)
    w_cols = jnp.concatenate([w_q.reshape(D_MODEL, qw), w_in[:, qw + kvw:]], axis=1)
    rows, cols = _proj(xf, g_pre, w_in[:, qw:qw + kvw], w_cols)
    rows = rows.reshape(bsz, seq, kvw)
    band_rows = TQ + -(-(A_WINDOW - 1) // LANES) * LANES
    assert band_rows <= seq
    bias = _band_bias(rel_bias, A_KV_HEADS, band_rows, A_WINDOW)
    sink_rows = _interleave_heads(sinks.astype(F32), A_KV_HEADS, 0).reshape(N_HB, 1, SLOTS)
    sink_rows = jnp.pad(sink_rows, ((0, 0), (0, 0), (0, LANES - SLOTS)))
    cfg = _AttnCfg(kv_width=kvw, band_rows=band_rows, sink=True)
    o = _attention(cfg, rows, cols, bias, k_blk=lambda hb: 0, v_rblk=lambda hb: qw // kvw,
                   sinks=sink_rows)
    w_o = _interleave_heads(w_out.reshape(N_HEADS, HEAD_DIM, D_MODEL), A_KV_HEADS, 0)
    return [o.reshape(bsz * seq, -1)], w_o.reshape(qw, D_MODEL)


def _nsa_overlap_t(seq):
    n_chunk = seq // CMP_STRIDE
    n_cmp = (seq - CMP_LEN) // CMP_STRIDE + 1
    n_slc = seq // SLC_BLK
    c_start = np.arange(n_cmp)[:, None] * CMP_STRIDE
    s_start = np.arange(n_slc)[None, :] * SLC_BLK
    overlap = ((c_start < s_start + SLC_BLK) & (c_start + CMP_LEN > s_start)).astype(np.float32)
    ovt = np.zeros((B_KV_GROUPS, LANES, n_chunk), np.float32)
    for g in range(B_KV_GROUPS):
        ovt[g, g * n_slc:(g + 1) * n_slc, :n_cmp] = overlap.T
    return jnp.asarray(ovt)


def _mixer_b(xf, bsz, seq, g_pre, w_in, w_out, cmp_pos, cmp_w1, cmp_w2, bias_nat):
    assert seq // SLC_BLK * B_KV_GROUPS == LANES and seq // CMP_STRIDE == LANES
    grp, dh = B_KV_GROUPS, HEAD_DIM
    qw, kvw = N_HEADS * dh, B_KV_GROUPS * dh
    w_q = _interleave_heads(w_in[:, :qw].reshape(D_MODEL, N_HEADS, dh), grp, 1)
    kv = [w_in[:, qw + i * kvw:qw + (i + 1) * kvw] for i in range(6)]
    w_g = w_in[:, qw + 6 * kvw:].reshape(D_MODEL, grp, N_HB, 3)
    w_g = w_g.transpose(0, 2, 3, 1).reshape(D_MODEL, N_HB, 3 * grp)
    w_g = jnp.pad(w_g, ((0, 0), (0, 0), (0, GATE_ROWS - 3 * grp)))
    w_rows = jnp.concatenate([kv[0], kv[1], kv[2], kv[4]], axis=1)
    w_cols = jnp.concatenate([w_q.reshape(D_MODEL, qw), kv[3], kv[5],
                              w_g.reshape(D_MODEL, N_HB * GATE_ROWS)], axis=1)
    rows, cols = _proj(xf, g_pre, w_rows, w_cols)
    rows = rows.reshape(bsz, seq, 4 * kvw)
    gate_row0 = qw + 2 * kvw

    half = CMP_LEN // 2
    pos = jnp.tile(cmp_pos.astype(F32), (1, 1, grp))
    w1 = cmp_w1.reshape(2, 2, half, 1, dh, CMP_HIDDEN)
    w1 = jnp.broadcast_to(w1, (2, 2, half, grp, dh, CMP_HIDDEN))
    w1 = w1.reshape(2, 2, half * HB_W, CMP_HIDDEN).astype(BF16)
    w2 = jnp.zeros((2, grp, CMP_HIDDEN, grp, dh), F32)
    for g in range(grp):
        w2 = w2.at[:, g, :, g, :].set(cmp_w2)
    w2 = w2.reshape(2, grp, CMP_HIDDEN, HB_W).astype(BF16)
    kcb, vcbt = _compress(rows, 0, 1, pos, w1, w2[0], jnp.swapaxes(w2[1], 1, 2))

    o_cmp, sel = _cmp_attention(cols, kcb, vcbt, gate_row0 // (N_HB * GATE_ROWS),
                                _nsa_overlap_t(seq), bsz, seq)

    gate_rblk0 = gate_row0 // GATE_ROWS
    bias = _block_bias(bias_nat, grp, seq // TK)
    cfg_slc = _AttnCfg(kv_width=kvw, sel_blk=SLC_BLK, sel_from_input=True, gate_branch=1)
    o_slc = _attention(cfg_slc, rows, cols, bias, k_blk=lambda hb: 2,
                       v_rblk=lambda hb: qw // kvw, sel=sel, gate_rblk0=gate_rblk0)
    n_back = -(-(B_WINDOW - 1) // TK)
    cfg_win = _AttnCfg(kv_width=kvw, n_back=n_back, gate_branch=2)
    bias_win = _block_bias(bias_nat, grp, n_back + 1, B_WINDOW)
    o_win = _attention(cfg_win, rows, cols, bias_win, k_blk=lambda hb: 3,
                       v_rblk=lambda hb: qw // kvw + 1, gate_rblk0=gate_rblk0)
    w_o = _interleave_heads(w_out.reshape(N_HEADS, dh, D_MODEL), grp, 0).reshape(qw, D_MODEL)
    n = bsz * seq
    return [o_cmp.reshape(n, -1), o_slc.reshape(n, -1), o_win.reshape(n, -1)], w_o


def _mixer_c(xf, bsz, seq, g_pre, w_in, w_out, bias_nat):
    assert seq % MOBA_BLK == 0 and MOBA_BLK == TK
    qw = N_HEADS * HEAD_DIM
    w_cols = jnp.concatenate([w_in[:, :qw], w_in[:, 2 * qw:]], axis=1)
    rows, cols = _proj(xf, g_pre, w_in[:, qw:2 * qw], w_cols)
    rows = rows.reshape(bsz, seq, qw)
    bias = _block_bias(bias_nat, 1, seq // TK)
    cfg = _AttnCfg(kv_width=HB_W, sel_blk=MOBA_BLK)
    o = _attention(cfg, rows, cols, bias, k_blk=lambda hb: hb, v_rblk=lambda hb: N_HB + hb)
    return [o.reshape(bsz * seq, -1)], w_out


def kernel(x, rel_bias, norm_g, ffn_w_in, ffn_w_out, a_w_in, a_w_out, a_sinks, b_w_in, b_w_out,
           b_cmp_pos, b_cmp_w1, b_cmp_w2, c_w_in, c_w_out):
    bsz, seq, d = x.shape
    assert d == D_MODEL and seq % TQ == 0 and (bsz * seq) % TM == 0
    depth = norm_g.shape[0]
    bias_nat = _bias_tiles(rel_bias, seq // TK)
    xf = x.reshape(bsz * seq, d)
    for i in range(depth):
        g = norm_g[i]
        xf = _ffn(xf, g[0], ffn_w_in[i, 0], ffn_w_out[i, 0], g[1])
        kind, j = i % N_MIXERS, i // N_MIXERS
        if kind == 0:
            o_list, w_mix = _mixer_a(xf, bsz, seq, g[2], a_w_in[j], a_w_out[j], a_sinks[j],
                                     rel_bias)
        elif kind == 1:
            o_list, w_mix = _mixer_b(xf, bsz, seq, g[2], b_w_in[j], b_w_out[j], b_cmp_pos[j],
                                     b_cmp_w1[j], b_cmp_w2[j], bias_nat)
        else:
            o_list, w_mix = _mixer_c(xf, bsz, seq, g[2], c_w_in[j], c_w_out[j], bias_nat)
        xf = _ffn(xf, g[4], ffn_w_in[i, 1], ffn_w_out[i, 1], g[5], mix=(o_list, w_mix, g[3]))
    return xf.reshape(bsz, seq, d)
```

```python
import dataclasses
import functools
import math

import numpy as np
import jax
import jax.numpy as jnp
from jax import lax
from jax.experimental import pallas as pl
from jax.experimental.pallas import tpu as pltpu

F32 = jnp.float32
BF16 = jnp.bfloat16

D_MODEL = 1024
HEAD_DIM = 64
N_HEADS = D_MODEL // HEAD_DIM
D_FF = 2816
NORM_EPS = 1e-6
N_MIXERS = 3
REL_BUCKETS = 32
REL_MAX_DIST = 1024
A_KV_HEADS = 2
A_WINDOW = 128
B_KV_GROUPS = 4
CMP_LEN = 32
CMP_STRIDE = 16
CMP_HIDDEN = 256
SLC_BLK = 64
SLC_TOPK = 16
B_WINDOW = 512
MOBA_BLK = 256
MOBA_TOPK = 3
NEG = -1e30
TINY = 1e-30

LANES = 128
SUBLANES = 8
MXU_DIM = 256
VMEM_LIMIT_BYTES = 56 * 1024 * 1024

SLOTS = 4
HB_W = SLOTS * HEAD_DIM
N_HB = N_HEADS // SLOTS
TQ = 256
TK = 256
TM = 512
FF_CHUNKS = ((0, 1536), (1536, D_FF))
GATE_ROWS = 16
SEL_BIG = float(2 ** 30)
SCALE = HEAD_DIM ** -0.5
LOG2E = math.log2(math.e)

_NT = (((1,), (1,)), ((), ()))


def _params(n_axes):
    return pltpu.CompilerParams(
        dimension_semantics=("arbitrary",) * n_axes,
        vmem_limit_bytes=VMEM_LIMIT_BYTES)


def _rms(x, g):
    return x * lax.rsqrt(jnp.mean(x * x, axis=-1, keepdims=True) + NORM_EPS) * g


def _sigmoid(x):
    return 1.0 / (1.0 + jnp.exp(-x))


def _div_pow2(x, d):
    assert d & (d - 1) == 0
    return jnp.right_shift(x, d.bit_length() - 1)


def _resident(shape):
    zeros = (0,) * len(shape)
    return pl.BlockSpec(shape, lambda *_: zeros, pipeline_mode=pl.Buffered(1))


def _ffn_kernel(*refs, n_mix):
    mix_refs, refs = refs[:n_mix], refs[n_mix:]
    if n_mix:
        wmix_ref, gmix_ref = refs[:2]
        refs = refs[2:]
    x_ref, gpre_ref, win_ref, wout_ref, gpost_ref, o_ref = refs
    x = x_ref[...]
    if n_mix:
        o = mix_refs[0][...]
        for r in mix_refs[1:]:
            o = o + r[...]
        mixed = jnp.dot(o.astype(BF16), wmix_ref[...], preferred_element_type=F32)
        x = x + _rms(mixed, gmix_ref[...])
    hn = _rms(x, gpre_ref[...]).astype(BF16)
    y = None
    for lo, hi in FF_CHUNKS:
        gate = jnp.dot(hn, win_ref[:, lo:hi], preferred_element_type=F32)
        up = jnp.dot(hn, win_ref[:, D_FF + lo:D_FF + hi], preferred_element_type=F32)
        act = (gate * _sigmoid(gate) * up).astype(BF16)
        part = jnp.dot(act, wout_ref[lo:hi, :], preferred_element_type=F32)
        y = part if y is None else y + part
    o_ref[...] = x + 0.5 * _rms(y, gpost_ref[...])


def _ffn(xf, g_pre, w_in, w_out, g_post, mix=None):
    n, d = xf.shape
    tile = pl.BlockSpec((TM, d), lambda i: (i, 0))
    mix_specs, mix_args = [], []
    if mix is not None:
        o_list, w_mix, g_mix = mix
        mix_specs = [tile] * len(o_list) + [_resident(w_mix.shape), _resident((1, d))]
        mix_args = list(o_list) + [w_mix.astype(BF16), g_mix.reshape(1, d)]
    return pl.pallas_call(
        functools.partial(_ffn_kernel, n_mix=len(mix_args) - 2 if mix_args else 0),
        grid=(n // TM,),
        in_specs=mix_specs + [
            tile,
            _resident((1, d)),
            _resident(w_in.shape),
            _resident(w_out.shape),
            _resident((1, d)),
        ],
        out_specs=tile,
        out_shape=jax.ShapeDtypeStruct((n, d), F32),
        compiler_params=_params(1),
        name="ffn",
    )(*mix_args, xf, g_pre.reshape(1, d), w_in.astype(BF16), w_out.astype(BF16),
      g_post.reshape(1, d))


def _proj_kernel(x_ref, g_ref, wr_ref, wct_ref, rows_ref, cols_ref):
    hn = _rms(x_ref[...], g_ref[...]).astype(BF16)
    rows_ref[...] = jnp.dot(hn, wr_ref[...], preferred_element_type=F32)
    cols_ref[...] = lax.dot_general(wct_ref[...], hn, _NT, preferred_element_type=F32)


def _proj(xf, g, w_rows, w_cols):
    n, d = xf.shape
    cr, cc = w_rows.shape[1], w_cols.shape[1]
    return pl.pallas_call(
        _proj_kernel,
        grid=(n // TM,),
        in_specs=[
            pl.BlockSpec((TM, d), lambda i: (i, 0)),
            _resident((1, d)),
            _resident((d, cr)),
            _resident((cc, d)),
        ],
        out_specs=[pl.BlockSpec((TM, cr), lambda i: (i, 0)),
                   pl.BlockSpec((cc, TM), lambda i: (0, i))],
        out_shape=[jax.ShapeDtypeStruct((n, cr), F32), jax.ShapeDtypeStruct((cc, n), F32)],
        compiler_params=_params(1),
        name="proj",
    )(xf, g.reshape(1, d), w_rows.astype(BF16), w_cols.T.astype(BF16))


def _rel_bucket(dist):
    dist = jnp.maximum(dist, 0)
    exact = REL_BUCKETS // 2
    d = jnp.maximum(dist, 1).astype(jnp.float32)
    log_b = exact + (jnp.log(d / exact) / math.log(REL_MAX_DIST / exact)
                     * (REL_BUCKETS - exact)).astype(jnp.int32)
    return jnp.where(dist < exact, dist, jnp.minimum(log_b, REL_BUCKETS - 1))


def _tile_distance(n_delta, xp):
    delta = xp.arange(n_delta)[:, None, None]
    return delta * TK + xp.arange(TQ)[None, None, :] - xp.arange(TK)[None, :, None]


def _bias_of_distance(rel_bias, dist):
    bucket = _rel_bucket(dist).reshape(1, -1)
    onehot = (jnp.arange(REL_BUCKETS)[:, None] == bucket).astype(F32)
    tiles = jnp.dot(rel_bias.astype(F32).T, onehot, precision=lax.Precision.HIGHEST)
    return tiles.reshape((N_HEADS,) + dist.shape)


def _bias_tiles(rel_bias, n_delta):
    return _bias_of_distance(rel_bias, _tile_distance(n_delta, jnp))


def _band_distance(band_rows, xp):
    first = xp.arange(TQ)[None, :] - xp.arange(band_rows)[:, None]
    return xp.stack([first, first + (band_rows - TQ)])


def _band_bias(rel_bias, n_groups, band_rows, window):
    bias = _bias_of_distance(rel_bias, _band_distance(band_rows, jnp))
    bias = _interleave_heads(bias, n_groups, 0)
    dist = _band_distance(band_rows, np)
    bias = jnp.where((dist >= 0) & (dist < window), bias * LOG2E, NEG)
    return bias.reshape((N_HB, SLOTS) + bias.shape[1:])


def _interleave_heads(w, n_groups, axis):
    r = N_HEADS // n_groups
    shape = w.shape
    w = w.reshape(shape[:axis] + (n_groups, r) + shape[axis + 1:])
    w = jnp.swapaxes(w, axis, axis + 1)
    return w.reshape(shape)


def _block_bias(bias_nat, n_groups, n_delta, window=None):
    dist = _tile_distance(n_delta, np)
    visible = dist >= 0
    if window is not None:
        visible &= dist < window
    bias = bias_nat[:, :n_delta]
    if n_groups > 1:
        bias = _interleave_heads(bias, n_groups, 0)
    bias = jnp.where(visible, bias * LOG2E, NEG)
    return bias.reshape((N_HB, SLOTS) + bias.shape[1:])


@dataclasses.dataclass(frozen=True)
class _AttnCfg:
    kv_width: int
    n_back: int | None = None
    band_rows: int | None = None
    shared_kv: bool = False
    sel_blk: int | None = None
    sel_from_input: bool = False
    gate_branch: int | None = None
    sink: bool = False

    @property
    def selecting(self):
        return self.sel_blk is not None

    @property
    def moba(self):
        return self.selecting and not self.sel_from_input


def _rank_rows(val):
    width = val.shape[1]
    row = lax.broadcasted_iota(jnp.int32, (1, width, 1), 1)
    rank = jnp.zeros(val.shape, F32)
    for m in range(width):
        vm = val[:, m:m + 1, :]
        rank = rank + jnp.where(row > m, jnp.where(vm >= val, 1.0, 0.0),
                                jnp.where(vm > val, 1.0, 0.0))
    return rank


def _u_chan0(hh, n_blk):
    return hh * n_blk + (LANES if hh < SLOTS // 2 else 0)


def _fold_rows(x, op):
    return op(x.reshape(x.shape[0] // SUBLANES, SUBLANES, x.shape[1]), axis=0)


def _attn_kernel(*refs, cfg: _AttnCfg):
    it = iter(refs)
    qt_ref, k_ref, vt_ref, bias_ref = next(it), next(it), next(it), next(it)
    sel_ref = next(it) if cfg.sel_from_input else None
    gate_ref = next(it) if cfg.gate_branch is not None else None
    sink_ref = next(it) if cfg.sink else None
    o_ref = next(it)
    kb_ref, vtb_ref = next(it), next(it)
    scratch_refs = ((next(it), next(it), next(it), next(it))
                    if cfg.band_rows is None else ())
    kmx_ref = next(it) if cfg.moba else None
    vt_tile = LANES if cfg.band_rows is not None else TK

    qt = pl.program_id(2)
    seq = k_ref.shape[1]
    n_tiles = seq // TK
    lane = lax.broadcasted_iota(jnp.int32, (1, HB_W), 1)
    slot_of_lane = _div_pow2(lane, HEAD_DIM)
    chan = lax.broadcasted_iota(jnp.int32, (HB_W, 1), 0)
    slot_of_chan = _div_pow2(chan, HEAD_DIM)
    n_sel_blk = seq // cfg.sel_blk if cfg.selecting else 0

    @pl.when(qt == 0)
    def _prepare():
        if cfg.moba:
            kmx_ref[...] = jnp.zeros(kmx_ref.shape, F32)
        for c in range(n_tiles):
            rows = pl.ds(c * TK, TK)
            k = k_ref[0, rows, :]
            vt = vt_ref[:, c * TK:(c + 1) * TK]
            if cfg.kv_width == LANES:
                k = jnp.concatenate([k, k], axis=1)
                vt = jnp.concatenate([vt, vt], axis=0)
            for i in range(TK // vt_tile):
                vtb_ref[c * (TK // vt_tile) + i] = vt[:, i * vt_tile:(i + 1) * vt_tile].astype(BF16)
            if cfg.selecting:
                key_blk = _div_pow2(c * TK + lax.broadcasted_iota(jnp.int32, (TK, 1), 0),
                                    cfg.sel_blk)
                for hh in range(SLOTS):
                    onehot = (lane - _u_chan0(hh, n_sel_blk)) == key_blk
                    kb_ref[hh, rows, :] = jnp.where(
                        slot_of_lane == hh, k, jnp.where(onehot, 1.0, 0.0)).astype(BF16)
            else:
                kb_ref[0, rows, :] = k.astype(BF16)
            if cfg.moba:
                k_mean = jnp.mean(k, axis=0, keepdims=True)
                for hh in range(SLOTS):
                    kmx_ref[pl.ds(hh * n_sel_blk + c, 1), :] = jnp.where(
                        slot_of_lane == hh, k_mean, 0.0)

    refs_of_step = (qt_ref, bias_ref, sel_ref, gate_ref, sink_ref, o_ref, kb_ref, vtb_ref,
                    scratch_refs, kmx_ref)
    for blk in range(qt_ref.shape[0] // HB_W):
        _attend_block(cfg, blk, qt, n_sel_blk, slot_of_chan, refs_of_step)


def _attend_block(cfg, blk, qt, n_sel_blk, slot_of_chan, refs):
    (qt_ref, bias_ref, sel_ref, gate_ref, sink_ref, o_ref, kb_ref, vtb_ref, scratch_refs,
     kmx_ref) = refs
    bias_blk = bias_ref.at[blk]
    q_t = qt_ref[blk * HB_W:(blk + 1) * HB_W, :]

    sel_rows = None
    if cfg.sel_from_input:
        sel_rows = sel_ref[0]
    elif cfg.moba:
        gate = jnp.dot(kmx_ref[...].astype(BF16), q_t.astype(BF16),
                       preferred_element_type=F32)
        gate = gate.reshape(SLOTS, n_sel_blk, TQ)
        key_blk = lax.broadcasted_iota(jnp.int32, (1, n_sel_blk, 1), 1)
        val = jnp.where(key_blk < qt, gate, NEG)
        rank = _rank_rows(val)
        chosen = jnp.where(key_blk < qt, jnp.where(rank < MOBA_TOPK, 1.0, 0.0), 0.0)
        chosen = jnp.where(key_blk == qt, 1.0, chosen)
        sel_rows = chosen.reshape(SLOTS * n_sel_blk, TQ)

    q_scaled = q_t * (SCALE * LOG2E)
    qms = []
    for hh in range(SLOTS):
        qm = jnp.where(slot_of_chan == hh, q_scaled, 0.0)
        if cfg.selecting:
            u0 = _u_chan0(hh, n_sel_blk)
            penalty = (sel_rows[hh * n_sel_blk:(hh + 1) * n_sel_blk] - 1.0) * SEL_BIG
            qm = qm + jnp.concatenate(
                [jnp.zeros((u0, TQ), F32), penalty,
                 jnp.zeros((HB_W - u0 - n_sel_blk, TQ), F32)], axis=0)
        qms.append(qm.astype(BF16))
    q_stack = None if cfg.selecting else jnp.concatenate(qms, axis=1)

    sinks = ([sink_ref[blk, :, hh:hh + 1] * LOG2E for hh in range(SLOTS)]
             if cfg.sink else None)
    if cfg.band_rows is not None:
        ms, sums, accs = _band_softmax(cfg, qt, q_stack, kb_ref, vtb_ref, bias_blk, sinks)
    else:
        ms, sums, accs = _tiled_softmax(cfg, qt, qms, q_stack, kb_ref, vtb_ref, bias_blk,
                                        sinks, *scratch_refs)

    gsig = None
    if cfg.gate_branch is not None:
        gsig = _sigmoid(gate_ref[blk * GATE_ROWS:(blk + 1) * GATE_ROWS, :])
    outs = []
    for hh in range(SLOTS):
        denom = jnp.sum(sums[hh], axis=0, keepdims=True)
        if cfg.sink:
            denom = denom + jnp.exp2(sinks[hh] - ms[hh])
        o_h = accs[hh] / jnp.maximum(denom, TINY)
        if cfg.gate_branch is not None:
            row = cfg.gate_branch * SLOTS + hh
            o_h = o_h * gsig[row:row + 1, :]
        outs.append(o_h)
    o_ref[0, :, blk * HB_W:(blk + 1) * HB_W] = jnp.concatenate(outs, axis=0).T


def _band_softmax(cfg, qt, q_stack, kb_ref, vtb_ref, bias_ref, sinks):
    band = cfg.band_rows
    start = pl.multiple_of(jnp.maximum(qt * TQ - (band - TQ), 0), LANES)
    s_all = jnp.dot(kb_ref[0, pl.ds(start, band), :], q_stack, preferred_element_type=F32)
    which = jnp.minimum(qt, 1)
    tile0 = _div_pow2(start, LANES)
    ms, sums, accs = [], [], []
    for hh in range(SLOTS):
        head = pl.ds(hh * HEAD_DIM, HEAD_DIM)
        sc = s_all[:, hh * TQ:(hh + 1) * TQ] + bias_ref[hh, which]
        m = jnp.max(_fold_rows(sc, jnp.max), axis=0, keepdims=True)
        if cfg.sink:
            m = jnp.maximum(m, sinks[hh])
        p = jnp.exp2(sc - m)
        vt = jnp.concatenate([vtb_ref[tile0 + c, head, :] for c in range(band // LANES)],
                             axis=1)
        ms.append(m)
        sums.append(_fold_rows(p, jnp.sum))
        accs.append(jnp.dot(vt, p.astype(BF16), preferred_element_type=F32))
    return ms, sums, accs


def _tiled_softmax(cfg, qt, qms, q_stack, kb_ref, vtb_ref, bias_ref, sinks,
                   sc_ref, m_ref, l_ref, acc_ref):
    lo = jnp.maximum(qt - cfg.n_back, 0) if cfg.n_back is not None else 0
    m_ref[...] = jnp.full(m_ref.shape, NEG, F32)
    l_ref[...] = jnp.zeros(l_ref.shape, F32)
    acc_ref[...] = jnp.zeros(acc_ref.shape, F32)

    def pass1(kt, width):
        rows = pl.ds(pl.multiple_of(kt * TK, TK), width * TK)
        if cfg.selecting:
            s = [jnp.dot(kb_ref[hh, rows, :], qms[hh], preferred_element_type=F32)
                 for hh in range(SLOTS)]
        else:
            s_all = jnp.dot(kb_ref[0, rows, :], q_stack, preferred_element_type=F32)
            s = [s_all[:, hh * TQ:(hh + 1) * TQ] for hh in range(SLOTS)]
        for hh in range(SLOTS):
            m_new = m_ref[hh]
            for w in range(width):
                tile = s[hh][w * TK:(w + 1) * TK] + bias_ref[hh, qt - kt - w]
                sc_ref[hh, kt - lo + w] = tile
                m_new = jnp.maximum(m_new, _fold_rows(tile, jnp.max))
            m_ref[hh] = m_new

    def pass2(kt, width, ms):
        for hh in range(SLOTS):
            head = pl.ds(hh * HEAD_DIM, HEAD_DIM)
            l_new = l_ref[hh]
            o_new = acc_ref[head, :]
            for w in range(width):
                p = jnp.exp2(sc_ref[hh, kt - lo + w] - ms[hh])
                l_new = l_new + _fold_rows(p, jnp.sum)
                o_new = o_new + jnp.dot(vtb_ref[kt + w, head, :], p.astype(BF16),
                                        preferred_element_type=F32)
            l_ref[hh] = l_new
            acc_ref[head, :] = o_new

    def over_tiles(body):
        n = qt + 1 - lo
        start = lo
        for width in (1, 2):
            has = jnp.bitwise_and(n, width)

            @pl.when(has != 0)
            def _part(start=start, width=width):
                body(start, width)

            start = start + has

        def group(i, carry):
            body(start + 4 * i, 4)
            return carry

        lax.fori_loop(0, jnp.right_shift(n, 2), group, 0)

    over_tiles(pass1)
    ms = []
    for hh in range(SLOTS):
        m = jnp.max(m_ref[hh], axis=0, keepdims=True)
        ms.append(jnp.maximum(m, sinks[hh]) if cfg.sink else m)
    over_tiles(functools.partial(pass2, ms=ms))
    return (ms, [l_ref[hh] for hh in range(SLOTS)],
            [acc_ref[hh * HEAD_DIM:(hh + 1) * HEAD_DIM, :] for hh in range(SLOTS)])


def _attention(cfg, rows, cols, bias, *, k_blk, v_rblk, sel=None, gate_rblk0=None, sinks=None):
    bsz, seq, _ = rows.shape
    kw = cfg.kv_width
    n_delta = bias.shape[2]
    n_q = seq // TQ
    nb = N_HB if cfg.shared_kv else 1
    in_specs = [
        pl.BlockSpec((nb * HB_W, TQ), lambda hb, b, qt: (hb, b * n_q + qt)),
        pl.BlockSpec((1, seq, kw), lambda hb, b, qt: (b, 0, k_blk(hb))),
        pl.BlockSpec((kw, seq), lambda hb, b, qt: (v_rblk(hb), b)),
        pl.BlockSpec((nb,) + bias.shape[1:], lambda hb, b, qt: (hb, 0, 0, 0, 0)),
    ]
    args = [cols, rows, cols, bias]
    if cfg.sel_from_input:
        in_specs.append(pl.BlockSpec((1, LANES, TQ), lambda hb, b, qt: (b, 0, qt)))
        args.append(sel)
    if cfg.gate_branch is not None:
        assert gate_rblk0 % nb == 0
        in_specs.append(pl.BlockSpec(
            (nb * GATE_ROWS, TQ), lambda hb, b, qt: (gate_rblk0 // nb + hb, b * n_q + qt)))
        args.append(cols)
    if cfg.sink:
        in_specs.append(pl.BlockSpec((nb, 1, LANES), lambda hb, b, qt: (hb, 0, 0)))
        args.append(sinks)
    vt_tile = LANES if cfg.band_rows is not None else TK
    scratch = [
        pltpu.VMEM((SLOTS if cfg.selecting else 1, seq, HB_W), BF16),
        pltpu.VMEM((seq // vt_tile, HB_W, vt_tile), BF16),
    ]
    if cfg.band_rows is None:
        scratch += [
            pltpu.VMEM((SLOTS, n_delta, TK, TQ), F32),
            pltpu.VMEM((SLOTS, SUBLANES, TQ), F32),
            pltpu.VMEM((SLOTS, SUBLANES, TQ), F32),
            pltpu.VMEM((HB_W, TQ), F32),
        ]
    if cfg.moba:
        assert SLOTS * (seq // cfg.sel_blk) <= LANES
        scratch.append(pltpu.VMEM((SLOTS * (seq // cfg.sel_blk), HB_W), F32))
    return pl.pallas_call(
        functools.partial(_attn_kernel, cfg=cfg),
        grid=(N_HB // nb, bsz, n_q),
        in_specs=in_specs,
        out_specs=pl.BlockSpec((1, TQ, nb * HB_W), lambda hb, b, qt: (b, qt, hb)),
        out_shape=jax.ShapeDtypeStruct((bsz, seq, D_MODEL), F32),
        scratch_shapes=scratch,
        compiler_params=_params(3),
        name="attention",
    )(*args)


def _gelu_tanh(x):
    cdf = 0.5 * (1.0 + jnp.tanh(math.sqrt(2.0 / math.pi) * (x + 0.044715 * (x * x * x))))
    return x * cdf


def _compress_kernel(kc_lo_ref, kc_hi_ref, vc_lo_ref, vc_hi_ref, pos_ref, w1_ref, w2k_ref,
                     w2vt_ref, kcb_ref, vcbt_ref):
    half = CMP_LEN // 2
    n_chunk = kc_lo_ref.shape[1] // CMP_STRIDE
    flat_lane = lax.broadcasted_iota(jnp.int32, (1, half * HB_W), 1)
    group_of_lane = _div_pow2(jnp.bitwise_and(flat_lane, HB_W - 1), HEAD_DIM)
    for kind, (lo_ref, hi_ref) in enumerate(((kc_lo_ref, kc_hi_ref), (vc_lo_ref, vc_hi_ref))):
        chunks = [jnp.concatenate(
            [lo_ref[0, pl.ds(l, n_chunk, stride=CMP_STRIDE), :],
             hi_ref[0, pl.ds(l, n_chunk, stride=CMP_STRIDE), :]], axis=1) for l in range(half)]
        parts = []
        for h in range(2):
            pieces = [chunks[l] + pos_ref[kind, pl.ds(h * half + l, 1), :] for l in range(half)]
            flat = jnp.concatenate(pieces, axis=1)
            stacked = jnp.concatenate(
                [jnp.where(group_of_lane == g, flat, 0.0)
                 for g in range(B_KV_GROUPS)], axis=0).astype(BF16)
            parts.append(jnp.dot(stacked, w1_ref[kind, h], preferred_element_type=F32))
        rows = B_KV_GROUPS * n_chunk
        hidden = parts[0] + pltpu.roll(parts[1], rows - 1, axis=0)
        act = _gelu_tanh(hidden).astype(BF16)
        out = None
        for g in range(B_KV_GROUPS):
            act_g = act[g * n_chunk:(g + 1) * n_chunk]
            if kind == 0:
                o = jnp.dot(act_g, w2k_ref[g], preferred_element_type=F32)
            else:
                o = lax.dot_general(w2vt_ref[g], act_g, _NT, preferred_element_type=F32)
            out = o if out is None else out + o
        if kind == 0:
            kcb_ref[0] = out
        else:
            vcbt_ref[0] = out


def _compress(rows, kc_blk, vc_blk, pos, w1, w2k, w2vt):
    bsz, seq, _ = rows.shape
    n_chunk = seq // CMP_STRIDE
    return pl.pallas_call(
        _compress_kernel,
        grid=(bsz,),
        in_specs=[
            pl.BlockSpec((1, seq, LANES), lambda b: (b, 0, 2 * kc_blk)),
            pl.BlockSpec((1, seq, LANES), lambda b: (b, 0, 2 * kc_blk + 1)),
            pl.BlockSpec((1, seq, LANES), lambda b: (b, 0, 2 * vc_blk)),
            pl.BlockSpec((1, seq, LANES), lambda b: (b, 0, 2 * vc_blk + 1)),
            _resident(pos.shape),
            _resident(w1.shape),
            _resident(w2k.shape),
            _resident(w2vt.shape),
        ],
        out_specs=[pl.BlockSpec((1, n_chunk, HB_W), lambda b: (b, 0, 0)),
                   pl.BlockSpec((1, HB_W, n_chunk), lambda b: (b, 0, 0))],
        out_shape=[jax.ShapeDtypeStruct((bsz, n_chunk, HB_W), F32),
                   jax.ShapeDtypeStruct((bsz, HB_W, n_chunk), F32)],
        compiler_params=_params(1),
        name="nsa_compress",
    )(rows, rows, rows, rows, pos, w1, w2k, w2vt)


def _cmp_kernel(qt_ref, kcb_ref, vcbt_ref, gate_ref, ovt_ref, o_ref, sel_ref):
    qt = pl.program_id(1)
    n_cmp_pad = kcb_ref.shape[1]
    n_slc = LANES // B_KV_GROUPS
    chan = lax.broadcasted_iota(jnp.int32, (HB_W, 1), 0)
    slot_of_chan = _div_pow2(chan, HEAD_DIM)
    t = qt * TQ + lax.broadcasted_iota(jnp.int32, (1, TQ), 1)
    cmp_end = (lax.broadcasted_iota(jnp.int32, (n_cmp_pad, 1), 0) * CMP_STRIDE + CMP_LEN - 1)
    ok = cmp_end <= jnp.concatenate([t] * N_HB, axis=1)
    kc = kcb_ref[0].astype(BF16)
    vct = vcbt_ref[0].astype(BF16)
    gsig = _sigmoid(gate_ref[...])
    outs = [[None] * B_KV_GROUPS for _ in range(N_HB)]
    imp = None
    for g in range(B_KV_GROUPS):
        qm = jnp.concatenate(
            [jnp.where(slot_of_chan == g, qt_ref[r * HB_W:(r + 1) * HB_W, :], 0.0)
             for r in range(N_HB)], axis=1) * SCALE
        s = jnp.dot(kc, qm.astype(BF16), preferred_element_type=F32)
        s = jnp.where(ok, s, NEG)
        m = jnp.max(s, axis=0, keepdims=True)
        e = jnp.where(ok, jnp.exp(s - m), 0.0)
        p = e / jnp.maximum(jnp.sum(e, axis=0, keepdims=True), TINY)
        o = jnp.dot(vct[g * HEAD_DIM:(g + 1) * HEAD_DIM], p.astype(BF16),
                    preferred_element_type=F32)
        p_group = None
        for r in range(N_HB):
            p_r = p[:, r * TQ:(r + 1) * TQ]
            p_group = p_r if p_group is None else p_group + p_r
            row = r * GATE_ROWS + g
            outs[r][g] = o[:, r * TQ:(r + 1) * TQ] * gsig[row:row + 1, :]
        part = jnp.dot(ovt_ref[g], p_group, precision=lax.Precision.HIGHEST,
                       preferred_element_type=F32)
        imp = part if imp is None else imp + part
    for r in range(N_HB):
        o_ref[0, :, r * HB_W:(r + 1) * HB_W] = jnp.concatenate(outs[r], axis=0).T
    imp = imp.reshape(B_KV_GROUPS, n_slc, TQ)
    j = lax.broadcasted_iota(jnp.int32, (1, n_slc, 1), 1)
    cur = _div_pow2(t, SLC_BLK).reshape(1, 1, TQ)
    val = jnp.where(j == cur - 1, -NEG, imp)
    val = jnp.where(j == cur, -NEG, val)
    val = jnp.where(j == 0, -NEG, val)
    val = jnp.where(j > cur, NEG, val)
    chosen = jnp.where(_rank_rows(val) < SLC_TOPK, 1.0, 0.0)
    sel_ref[0] = chosen.reshape(LANES, TQ)


def _cmp_attention(cols, kcb, vcbt, gate_rblk, ovt, bsz, seq):
    n_cmp_pad = kcb.shape[1]
    n_q = seq // TQ
    return pl.pallas_call(
        _cmp_kernel,
        grid=(bsz, n_q),
        in_specs=[
            pl.BlockSpec((D_MODEL, TQ), lambda b, qt: (0, b * n_q + qt)),
            pl.BlockSpec((1, n_cmp_pad, HB_W), lambda b, qt: (b, 0, 0)),
            pl.BlockSpec((1, HB_W, n_cmp_pad), lambda b, qt: (b, 0, 0)),
            pl.BlockSpec((N_HB * GATE_ROWS, TQ), lambda b, qt: (gate_rblk, b * n_q + qt)),
            _resident(ovt.shape),
        ],
        out_specs=[
            pl.BlockSpec((1, TQ, D_MODEL), lambda b, qt: (b, qt, 0)),
            pl.BlockSpec((1, LANES, TQ), lambda b, qt: (b, 0, qt)),
        ],
        out_shape=[
            jax.ShapeDtypeStruct((bsz, seq, D_MODEL), F32),
            jax.ShapeDtypeStruct((bsz, LANES, seq), F32),
        ],
        compiler_params=_params(2),
        name="nsa_cmp_select",
    )(cols, kcb, vcbt, cols, ovt)


def _mixer_a(xf, bsz, seq, g_pre, w_in, w_out, sinks, rel_bias):
    qw, kvw = N_HEADS * HEAD_DIM, A_KV_HEADS * HEAD_DIM
    w_q = _interleave_heads(w_in[:, :qw].reshape(D_MODEL, N_HEADS, HEAD_DIM), A_KV_HEADS, 1)
    w_cols = jnp.concatenate([w_q.reshape(D_MODEL, qw), w_in[:, qw + kvw:]], axis=1)
    rows, cols = _proj(xf, g_pre, w_in[:, qw:qw + kvw], w_cols)
    rows = rows.reshape(bsz, seq, kvw)
    band_rows = TQ + -(-(A_WINDOW - 1) // LANES) * LANES
    assert band_rows <= seq
    bias = _band_bias(rel_bias, A_KV_HEADS, band_rows, A_WINDOW)
    sink_rows = _interleave_heads(sinks.astype(F32), A_KV_HEADS, 0).reshape(N_HB, 1, SLOTS)
    sink_rows = jnp.pad(sink_rows, ((0, 0), (0, 0), (0, LANES - SLOTS)))
    cfg = _AttnCfg(kv_width=kvw, band_rows=band_rows, sink=True, shared_kv=True)
    o = _attention(cfg, rows, cols, bias, k_blk=lambda hb: 0, v_rblk=lambda hb: qw // kvw,
                   sinks=sink_rows)
    w_o = _interleave_heads(w_out.reshape(N_HEADS, HEAD_DIM, D_MODEL), A_KV_HEADS, 0)
    return [o.reshape(bsz * seq, -1)], w_o.reshape(qw, D_MODEL)


def _nsa_overlap_t(seq):
    n_chunk = seq // CMP_STRIDE
    n_cmp = (seq - CMP_LEN) // CMP_STRIDE + 1
    n_slc = seq // SLC_BLK
    c_start = np.arange(n_cmp)[:, None] * CMP_STRIDE
    s_start = np.arange(n_slc)[None, :] * SLC_BLK
    overlap = ((c_start < s_start + SLC_BLK) & (c_start + CMP_LEN > s_start)).astype(np.float32)
    ovt = np.zeros((B_KV_GROUPS, LANES, n_chunk), np.float32)
    for g in range(B_KV_GROUPS):
        ovt[g, g * n_slc:(g + 1) * n_slc, :n_cmp] = overlap.T
    return jnp.asarray(ovt)


def _mixer_b(xf, bsz, seq, g_pre, w_in, w_out, cmp_pos, cmp_w1, cmp_w2, bias_nat):
    assert seq // SLC_BLK * B_KV_GROUPS == LANES and seq // CMP_STRIDE == LANES
    grp, dh = B_KV_GROUPS, HEAD_DIM
    qw, kvw = N_HEADS * dh, B_KV_GROUPS * dh
    w_q = _interleave_heads(w_in[:, :qw].reshape(D_MODEL, N_HEADS, dh), grp, 1)
    kv = [w_in[:, qw + i * kvw:qw + (i + 1) * kvw] for i in range(6)]
    w_g = w_in[:, qw + 6 * kvw:].reshape(D_MODEL, grp, N_HB, 3)
    w_g = w_g.transpose(0, 2, 3, 1).reshape(D_MODEL, N_HB, 3 * grp)
    w_g = jnp.pad(w_g, ((0, 0), (0, 0), (0, GATE_ROWS - 3 * grp)))
    w_rows = jnp.concatenate([kv[0], kv[1], kv[2], kv[4]], axis=1)
    w_cols = jnp.concatenate([w_q.reshape(D_MODEL, qw), kv[3], kv[5],
                              w_g.reshape(D_MODEL, N_HB * GATE_ROWS)], axis=1)
    rows, cols = _proj(xf, g_pre, w_rows, w_cols)
    rows = rows.reshape(bsz, seq, 4 * kvw)
    gate_row0 = qw + 2 * kvw

    half = CMP_LEN // 2
    pos = jnp.tile(cmp_pos.astype(F32), (1, 1, grp))
    w1 = cmp_w1.reshape(2, 2, half, 1, dh, CMP_HIDDEN)
    w1 = jnp.broadcast_to(w1, (2, 2, half, grp, dh, CMP_HIDDEN))
    w1 = w1.reshape(2, 2, half * HB_W, CMP_HIDDEN).astype(BF16)
    w2 = jnp.zeros((2, grp, CMP_HIDDEN, grp, dh), F32)
    for g in range(grp):
        w2 = w2.at[:, g, :, g, :].set(cmp_w2)
    w2 = w2.reshape(2, grp, CMP_HIDDEN, HB_W).astype(BF16)
    kcb, vcbt = _compress(rows, 0, 1, pos, w1, w2[0], jnp.swapaxes(w2[1], 1, 2))

    o_cmp, sel = _cmp_attention(cols, kcb, vcbt, gate_row0 // (N_HB * GATE_ROWS),
                                _nsa_overlap_t(seq), bsz, seq)

    gate_rblk0 = gate_row0 // GATE_ROWS
    bias = _block_bias(bias_nat, grp, seq // TK)
    cfg_slc = _AttnCfg(kv_width=kvw, sel_blk=SLC_BLK, sel_from_input=True, gate_branch=1)
    o_slc = _attention(cfg_slc, rows, cols, bias, k_blk=lambda hb: 2,
                       v_rblk=lambda hb: qw // kvw, sel=sel, gate_rblk0=gate_rblk0)
    n_back = -(-(B_WINDOW - 1) // TK)
    cfg_win = _AttnCfg(kv_width=kvw, n_back=n_back, gate_branch=2, shared_kv=True)
    bias_win = _block_bias(bias_nat, grp, n_back + 1, B_WINDOW)
    o_win = _attention(cfg_win, rows, cols, bias_win, k_blk=lambda hb: 3,
                       v_rblk=lambda hb: qw // kvw + 1, gate_rblk0=gate_rblk0)
    w_o = _interleave_heads(w_out.reshape(N_HEADS, dh, D_MODEL), grp, 0).reshape(qw, D_MODEL)
    n = bsz * seq
    return [o_cmp.reshape(n, -1), o_slc.reshape(n, -1), o_win.reshape(n, -1)], w_o


def _mixer_c(xf, bsz, seq, g_pre, w_in, w_out, bias_nat):
    assert seq % MOBA_BLK == 0 and MOBA_BLK == TK
    qw = N_HEADS * HEAD_DIM
    w_cols = jnp.concatenate([w_in[:, :qw], w_in[:, 2 * qw:]], axis=1)
    rows, cols = _proj(xf, g_pre, w_in[:, qw:2 * qw], w_cols)
    rows = rows.reshape(bsz, seq, qw)
    bias = _block_bias(bias_nat, 1, seq // TK)
    cfg = _AttnCfg(kv_width=HB_W, sel_blk=MOBA_BLK)
    o = _attention(cfg, rows, cols, bias, k_blk=lambda hb: hb, v_rblk=lambda hb: N_HB + hb)
    return [o.reshape(bsz * seq, -1)], w_out


def kernel(x, rel_bias, norm_g, ffn_w_in, ffn_w_out, a_w_in, a_w_out, a_sinks, b_w_in, b_w_out,
           b_cmp_pos, b_cmp_w1, b_cmp_w2, c_w_in, c_w_out):
    bsz, seq, d = x.shape
    assert d == D_MODEL and seq % TQ == 0 and (bsz * seq) % TM == 0
    depth = norm_g.shape[0]
    bias_nat = _bias_tiles(rel_bias, seq // TK)
    xf = x.reshape(bsz * seq, d)
    for i in range(depth):
        g = norm_g[i]
        xf = _ffn(xf, g[0], ffn_w_in[i, 0], ffn_w_out[i, 0], g[1])
        kind, j = i % N_MIXERS, i // N_MIXERS
        if kind == 0:
            o_list, w_mix = _mixer_a(xf, bsz, seq, g[2], a_w_in[j], a_w_out[j], a_sinks[j],
                                     rel_bias)
        elif kind == 1:
            o_list, w_mix = _mixer_b(xf, bsz, seq, g[2], b_w_in[j], b_w_out[j], b_cmp_pos[j],
                                     b_cmp_w1[j], b_cmp_w2[j], bias_nat)
        else:
            o_list, w_mix = _mixer_c(xf, bsz, seq, g[2], c_w_in[j], c_w_out[j], bias_nat)
        xf = _ffn(xf, g[4], ffn_w_in[i, 1], ffn_w_out[i, 1], g[5], mix=(o_list, w_mix, g[3]))
    return xf.reshape(bsz, seq, d)
```

```python
import dataclasses
import functools
import math

import numpy as np
import jax
import jax.numpy as jnp
from jax import lax
from jax.experimental import pallas as pl
from jax.experimental.pallas import tpu as pltpu

F32 = jnp.float32
BF16 = jnp.bfloat16

D_MODEL = 1024
HEAD_DIM = 64
N_HEADS = D_MODEL // HEAD_DIM
D_FF = 2816
NORM_EPS = 1e-6
N_MIXERS = 3
REL_BUCKETS = 32
REL_MAX_DIST = 1024
A_KV_HEADS = 2
A_WINDOW = 128
B_KV_GROUPS = 4
CMP_LEN = 32
CMP_STRIDE = 16
CMP_HIDDEN = 256
SLC_BLK = 64
SLC_TOPK = 16
B_WINDOW = 512
MOBA_BLK = 256
MOBA_TOPK = 3
NEG = -1e30
TINY = 1e-30

LANES = 128
SUBLANES = 8
MXU_DIM = 256
VMEM_LIMIT_BYTES = 56 * 1024 * 1024

SLOTS = 4
HB_W = SLOTS * HEAD_DIM
N_HB = N_HEADS // SLOTS
TQ = 256
TK = 256
TM = 512
FF_CHUNKS = ((0, 1536), (1536, D_FF))
GATE_ROWS = 16
SEL_BIG = float(2 ** 30)
SCALE = HEAD_DIM ** -0.5
LOG2E = math.log2(math.e)

_NT = (((1,), (1,)), ((), ()))


def _params(n_axes):
    return pltpu.CompilerParams(
        dimension_semantics=("arbitrary",) * n_axes,
        vmem_limit_bytes=VMEM_LIMIT_BYTES)


def _rms(x, g):
    return x * lax.rsqrt(jnp.mean(x * x, axis=-1, keepdims=True) + NORM_EPS) * g


def _sigmoid(x):
    return 1.0 / (1.0 + jnp.exp(-x))


def _div_pow2(x, d):
    assert d & (d - 1) == 0
    return jnp.right_shift(x, d.bit_length() - 1)


def _resident(shape):
    zeros = (0,) * len(shape)
    return pl.BlockSpec(shape, lambda *_: zeros, pipeline_mode=pl.Buffered(1))


def _ffn_kernel(*refs, n_mix):
    mix_refs, refs = refs[:n_mix], refs[n_mix:]
    if n_mix:
        wmix_ref, gmix_ref = refs[:2]
        refs = refs[2:]
    x_ref, gpre_ref, win_ref, wout_ref, gpost_ref, o_ref = refs
    x = x_ref[...]
    if n_mix:
        o = mix_refs[0][...]
        for r in mix_refs[1:]:
            o = o + r[...]
        mixed = jnp.dot(o.astype(BF16), wmix_ref[...], preferred_element_type=F32)
        x = x + _rms(mixed, gmix_ref[...])
    hn = _rms(x, gpre_ref[...]).astype(BF16)
    y = None
    for lo, hi in FF_CHUNKS:
        gate = jnp.dot(hn, win_ref[:, lo:hi], preferred_element_type=F32)
        up = jnp.dot(hn, win_ref[:, D_FF + lo:D_FF + hi], preferred_element_type=F32)
        act = (gate * _sigmoid(gate) * up).astype(BF16)
        part = jnp.dot(act, wout_ref[lo:hi, :], preferred_element_type=F32)
        y = part if y is None else y + part
    o_ref[...] = x + 0.5 * _rms(y, gpost_ref[...])


def _ffn(xf, g_pre, w_in, w_out, g_post, mix=None):
    n, d = xf.shape
    tile = pl.BlockSpec((TM, d), lambda i: (i, 0))
    mix_specs, mix_args = [], []
    if mix is not None:
        o_list, w_mix, g_mix = mix
        mix_specs = [tile] * len(o_list) + [_resident(w_mix.shape), _resident((1, d))]
        mix_args = list(o_list) + [w_mix.astype(BF16), g_mix.reshape(1, d)]
    return pl.pallas_call(
        functools.partial(_ffn_kernel, n_mix=len(mix_args) - 2 if mix_args else 0),
        grid=(n // TM,),
        in_specs=mix_specs + [
            tile,
            _resident((1, d)),
            _resident(w_in.shape),
            _resident(w_out.shape),
            _resident((1, d)),
        ],
        out_specs=tile,
        out_shape=jax.ShapeDtypeStruct((n, d), F32),
        compiler_params=_params(1),
        name="ffn",
    )(*mix_args, xf, g_pre.reshape(1, d), w_in.astype(BF16), w_out.astype(BF16),
      g_post.reshape(1, d))


def _proj_kernel(x_ref, g_ref, wr_ref, wct_ref, rows_ref, cols_ref):
    hn = _rms(x_ref[...], g_ref[...]).astype(BF16)
    rows_ref[...] = jnp.dot(hn, wr_ref[...], preferred_element_type=F32)
    cols_ref[...] = lax.dot_general(wct_ref[...], hn, _NT, preferred_element_type=F32)


def _proj(xf, g, w_rows, w_cols):
    n, d = xf.shape
    cr, cc = w_rows.shape[1], w_cols.shape[1]
    return pl.pallas_call(
        _proj_kernel,
        grid=(n // TM,),
        in_specs=[
            pl.BlockSpec((TM, d), lambda i: (i, 0)),
            _resident((1, d)),
            _resident((d, cr)),
            _resident((cc, d)),
        ],
        out_specs=[pl.BlockSpec((TM, cr), lambda i: (i, 0)),
                   pl.BlockSpec((cc, TM), lambda i: (0, i))],
        out_shape=[jax.ShapeDtypeStruct((n, cr), F32), jax.ShapeDtypeStruct((cc, n), F32)],
        compiler_params=_params(1),
        name="proj",
    )(xf, g.reshape(1, d), w_rows.astype(BF16), w_cols.T.astype(BF16))


def _rel_bucket(dist):
    dist = jnp.maximum(dist, 0)
    exact = REL_BUCKETS // 2
    d = jnp.maximum(dist, 1).astype(jnp.float32)
    log_b = exact + (jnp.log(d / exact) / math.log(REL_MAX_DIST / exact)
                     * (REL_BUCKETS - exact)).astype(jnp.int32)
    return jnp.where(dist < exact, dist, jnp.minimum(log_b, REL_BUCKETS - 1))


def _tile_distance(n_delta, xp):
    delta = xp.arange(n_delta)[:, None, None]
    return delta * TK + xp.arange(TQ)[None, None, :] - xp.arange(TK)[None, :, None]


def _bias_of_distance(rel_bias, dist):
    bucket = _rel_bucket(dist).reshape(1, -1)
    onehot = (jnp.arange(REL_BUCKETS)[:, None] == bucket).astype(F32)
    tiles = jnp.dot(rel_bias.astype(F32).T, onehot, precision=lax.Precision.HIGHEST)
    return tiles.reshape((N_HEADS,) + dist.shape)


def _bias_tiles(rel_bias, n_delta):
    return _bias_of_distance(rel_bias, _tile_distance(n_delta, jnp))


def _band_distance(band_rows, xp):
    back = band_rows - TQ
    return back + xp.arange(TQ)[None, :] - xp.arange(band_rows + back)[:, None]


def _band_bias(rel_bias, n_groups, band_rows, window):
    bias = _bias_of_distance(rel_bias, _band_distance(band_rows, jnp))
    bias = _interleave_heads(bias, n_groups, 0)
    dist = _band_distance(band_rows, np)
    bias = jnp.where((dist >= 0) & (dist < window), bias * LOG2E, NEG)
    return bias.reshape((N_HB, SLOTS) + bias.shape[1:])


def _interleave_heads(w, n_groups, axis):
    r = N_HEADS // n_groups
    shape = w.shape
    w = w.reshape(shape[:axis] + (n_groups, r) + shape[axis + 1:])
    w = jnp.swapaxes(w, axis, axis + 1)
    return w.reshape(shape)


def _block_bias(bias_nat, n_groups, n_delta, window=None):
    dist = _tile_distance(n_delta, np)
    visible = dist >= 0
    if window is not None:
        visible &= dist < window
    bias = bias_nat[:, :n_delta]
    if n_groups > 1:
        bias = _interleave_heads(bias, n_groups, 0)
    bias = jnp.where(visible, bias * LOG2E, NEG)
    return bias.reshape((N_HB, SLOTS) + bias.shape[1:])


@dataclasses.dataclass(frozen=True)
class _AttnCfg:
    kv_width: int
    n_back: int | None = None
    band_rows: int | None = None
    blocks_per_step: int = 1
    sel_blk: int | None = None
    sel_from_input: bool = False
    gate_branch: int | None = None
    sink: bool = False

    @property
    def selecting(self):
        return self.sel_blk is not None

    @property
    def moba(self):
        return self.selecting and not self.sel_from_input


def _rank_rows(val):
    width = val.shape[1]
    row = lax.broadcasted_iota(jnp.int32, (1, width, 1), 1)
    rank = jnp.zeros(val.shape, F32)
    for m in range(width):
        vm = val[:, m:m + 1, :]
        rank = rank + jnp.where(row > m, jnp.where(vm >= val, 1.0, 0.0),
                                jnp.where(vm > val, 1.0, 0.0))
    return rank


def _u_chan0(hh, n_blk):
    return hh * n_blk + (LANES if hh < SLOTS // 2 else 0)


def _fold_rows(x, op):
    return op(x.reshape(x.shape[0] // SUBLANES, SUBLANES, x.shape[1]), axis=0)


def _attn_kernel(*refs, cfg: _AttnCfg):
    it = iter(refs)
    qt_ref, k_ref, vt_ref, bias_ref = next(it), next(it), next(it), next(it)
    sel_ref = next(it) if cfg.sel_from_input else None
    gate_ref = next(it) if cfg.gate_branch is not None else None
    sink_ref = next(it) if cfg.sink else None
    o_ref = next(it)
    kb_ref, vtb_ref = next(it), next(it)
    scratch_refs = ((next(it), next(it), next(it), next(it))
                    if cfg.band_rows is None else ())
    kmx_ref = next(it) if cfg.moba else None
    vt_tile = LANES if cfg.band_rows is not None else TK

    qt = pl.program_id(2)
    seq = k_ref.shape[1]
    n_tiles = seq // TK
    lane = lax.broadcasted_iota(jnp.int32, (1, HB_W), 1)
    slot_of_lane = _div_pow2(lane, HEAD_DIM)
    chan = lax.broadcasted_iota(jnp.int32, (HB_W, 1), 0)
    slot_of_chan = _div_pow2(chan, HEAD_DIM)
    n_sel_blk = seq // cfg.sel_blk if cfg.selecting else 0

    @pl.when(qt == 0)
    def _prepare():
        if cfg.moba:
            kmx_ref[...] = jnp.zeros(kmx_ref.shape, F32)
        for c in range(n_tiles):
            rows = pl.ds(c * TK, TK)
            k = k_ref[0, rows, :]
            vt = vt_ref[:, c * TK:(c + 1) * TK]
            if cfg.kv_width == LANES:
                k = jnp.concatenate([k, k], axis=1)
                vt = jnp.concatenate([vt, vt], axis=0)
            for i in range(TK // vt_tile):
                vtb_ref[c * (TK // vt_tile) + i] = vt[:, i * vt_tile:(i + 1) * vt_tile].astype(BF16)
            if cfg.selecting:
                key_blk = _div_pow2(c * TK + lax.broadcasted_iota(jnp.int32, (TK, 1), 0),
                                    cfg.sel_blk)
                for hh in range(SLOTS):
                    onehot = (lane - _u_chan0(hh, n_sel_blk)) == key_blk
                    kb_ref[hh, rows, :] = jnp.where(
                        slot_of_lane == hh, k, jnp.where(onehot, 1.0, 0.0)).astype(BF16)
            else:
                kb_ref[0, rows, :] = k.astype(BF16)
            if cfg.moba:
                k_mean = jnp.mean(k, axis=0, keepdims=True)
                for hh in range(SLOTS):
                    kmx_ref[pl.ds(hh * n_sel_blk + c, 1), :] = jnp.where(
                        slot_of_lane == hh, k_mean, 0.0)

    refs_of_step = (qt_ref, bias_ref, sel_ref, gate_ref, sink_ref, o_ref, kb_ref, vtb_ref,
                    scratch_refs, kmx_ref)
    for blk in range(qt_ref.shape[0] // HB_W):
        _attend_block(cfg, blk, qt, n_sel_blk, slot_of_chan, refs_of_step)


def _attend_block(cfg, blk, qt, n_sel_blk, slot_of_chan, refs):
    (qt_ref, bias_ref, sel_ref, gate_ref, sink_ref, o_ref, kb_ref, vtb_ref, scratch_refs,
     kmx_ref) = refs
    bias_blk = bias_ref.at[blk]
    q_t = qt_ref[blk * HB_W:(blk + 1) * HB_W, :]

    sel_rows = None
    if cfg.sel_from_input:
        sel_rows = sel_ref[0]
    elif cfg.moba:
        gate = jnp.dot(kmx_ref[...].astype(BF16), q_t.astype(BF16),
                       preferred_element_type=F32)
        gate = gate.reshape(SLOTS, n_sel_blk, TQ)
        key_blk = lax.broadcasted_iota(jnp.int32, (1, n_sel_blk, 1), 1)
        val = jnp.where(key_blk < qt, gate, NEG)
        rank = _rank_rows(val)
        chosen = jnp.where(key_blk < qt, jnp.where(rank < MOBA_TOPK, 1.0, 0.0), 0.0)
        chosen = jnp.where(key_blk == qt, 1.0, chosen)
        sel_rows = chosen.reshape(SLOTS * n_sel_blk, TQ)

    q_scaled = q_t * (SCALE * LOG2E)
    qms = []
    for hh in range(SLOTS):
        qm = jnp.where(slot_of_chan == hh, q_scaled, 0.0)
        if cfg.selecting:
            u0 = _u_chan0(hh, n_sel_blk)
            penalty = (sel_rows[hh * n_sel_blk:(hh + 1) * n_sel_blk] - 1.0) * SEL_BIG
            qm = qm + jnp.concatenate(
                [jnp.zeros((u0, TQ), F32), penalty,
                 jnp.zeros((HB_W - u0 - n_sel_blk, TQ), F32)], axis=0)
        qms.append(qm.astype(BF16))
    q_stack = None if cfg.selecting else jnp.concatenate(qms, axis=1)

    sinks = ([sink_ref[blk, :, hh:hh + 1] * LOG2E for hh in range(SLOTS)]
             if cfg.sink else None)
    if cfg.band_rows is not None:
        ms, sums, accs = _band_softmax(cfg, qt, q_stack, kb_ref, vtb_ref, bias_blk, sinks)
    else:
        ms, sums, accs = _tiled_softmax(cfg, qt, qms, q_stack, kb_ref, vtb_ref, bias_blk,
                                        sinks, *scratch_refs)

    gsig = None
    if cfg.gate_branch is not None:
        gsig = _sigmoid(gate_ref[blk * GATE_ROWS:(blk + 1) * GATE_ROWS, :])
    outs = []
    for hh in range(SLOTS):
        denom = jnp.sum(sums[hh], axis=0, keepdims=True)
        if cfg.sink:
            denom = denom + jnp.exp2(sinks[hh] - ms[hh])
        o_h = accs[hh] / jnp.maximum(denom, TINY)
        if cfg.gate_branch is not None:
            row = cfg.gate_branch * SLOTS + hh
            o_h = o_h * gsig[row:row + 1, :]
        outs.append(o_h)
    o_ref[0, :, blk * HB_W:(blk + 1) * HB_W] = jnp.concatenate(outs, axis=0).T


def _band_softmax(cfg, qt, q_stack, kb_ref, vtb_ref, bias_ref, sinks):
    band = cfg.band_rows
    back = band - TQ
    start = pl.multiple_of(jnp.maximum(qt * TQ - back, 0), LANES)
    s_all = jnp.dot(kb_ref[0, pl.ds(start, band), :], q_stack, preferred_element_type=F32)
    bias_rows = pl.ds(pl.multiple_of(back - (qt * TQ - start), LANES), band)
    tile0 = _div_pow2(start, LANES)
    ms, sums, accs = [], [], []
    for hh in range(SLOTS):
        head = pl.ds(hh * HEAD_DIM, HEAD_DIM)
        sc = s_all[:, hh * TQ:(hh + 1) * TQ] + bias_ref[hh, bias_rows, :]
        m = jnp.max(_fold_rows(sc, jnp.max), axis=0, keepdims=True)
        if cfg.sink:
            m = jnp.maximum(m, sinks[hh])
        p = jnp.exp2(sc - m)
        vt = jnp.concatenate([vtb_ref[tile0 + c, head, :] for c in range(band // LANES)],
                             axis=1)
        ms.append(m)
        sums.append(_fold_rows(p, jnp.sum))
        accs.append(jnp.dot(vt, p.astype(BF16), preferred_element_type=F32))
    return ms, sums, accs


def _tiled_softmax(cfg, qt, qms, q_stack, kb_ref, vtb_ref, bias_ref, sinks,
                   sc_ref, m_ref, l_ref, acc_ref):
    lo = jnp.maximum(qt - cfg.n_back, 0) if cfg.n_back is not None else 0
    m_ref[...] = jnp.full(m_ref.shape, NEG, F32)
    l_ref[...] = jnp.zeros(l_ref.shape, F32)
    acc_ref[...] = jnp.zeros(acc_ref.shape, F32)

    def pass1(kt, width):
        rows = pl.ds(pl.multiple_of(kt * TK, TK), width * TK)
        if cfg.selecting:
            s = [jnp.dot(kb_ref[hh, rows, :], qms[hh], preferred_element_type=F32)
                 for hh in range(SLOTS)]
        else:
            s_all = jnp.dot(kb_ref[0, rows, :], q_stack, preferred_element_type=F32)
            s = [s_all[:, hh * TQ:(hh + 1) * TQ] for hh in range(SLOTS)]
        for hh in range(SLOTS):
            m_new = m_ref[hh]
            for w in range(width):
                tile = s[hh][w * TK:(w + 1) * TK] + bias_ref[hh, qt - kt - w]
                sc_ref[hh, kt - lo + w] = tile
                m_new = jnp.maximum(m_new, _fold_rows(tile, jnp.max))
            m_ref[hh] = m_new

    def pass2(kt, width, ms):
        for hh in range(SLOTS):
            head = pl.ds(hh * HEAD_DIM, HEAD_DIM)
            l_new = l_ref[hh]
            o_new = acc_ref[head, :]
            for w in range(width):
                p = jnp.exp2(sc_ref[hh, kt - lo + w] - ms[hh])
                l_new = l_new + _fold_rows(p, jnp.sum)
                o_new = o_new + jnp.dot(vtb_ref[kt + w, head, :], p.astype(BF16),
                                        preferred_element_type=F32)
            l_ref[hh] = l_new
            acc_ref[head, :] = o_new

    def over_tiles(body):
        n = qt + 1 - lo
        start = lo
        for width in (1, 2):
            has = jnp.bitwise_and(n, width)

            @pl.when(has != 0)
            def _part(start=start, width=width):
                body(start, width)

            start = start + has

        def group(i, carry):
            body(start + 4 * i, 4)
            return carry

        lax.fori_loop(0, jnp.right_shift(n, 2), group, 0)

    over_tiles(pass1)
    ms = []
    for hh in range(SLOTS):
        m = jnp.max(m_ref[hh], axis=0, keepdims=True)
        ms.append(jnp.maximum(m, sinks[hh]) if cfg.sink else m)
    over_tiles(functools.partial(pass2, ms=ms))
    return (ms, [l_ref[hh] for hh in range(SLOTS)],
            [acc_ref[hh * HEAD_DIM:(hh + 1) * HEAD_DIM, :] for hh in range(SLOTS)])


def _attention(cfg, rows, cols, bias, *, k_blk, v_rblk, sel=None, gate_rblk0=None, sinks=None):
    bsz, seq, _ = rows.shape
    kw = cfg.kv_width
    n_delta = bias.shape[2]
    n_q = seq // TQ
    nb = cfg.blocks_per_step
    bias_tail = (0,) * (bias.ndim - 1)
    in_specs = [
        pl.BlockSpec((nb * HB_W, TQ), lambda hb, b, qt: (hb, b * n_q + qt)),
        pl.BlockSpec((1, seq, kw), lambda hb, b, qt: (b, 0, k_blk(hb))),
        pl.BlockSpec((kw, seq), lambda hb, b, qt: (v_rblk(hb), b)),
        pl.BlockSpec((nb,) + bias.shape[1:], lambda hb, b, qt: (hb,) + bias_tail,
                     pipeline_mode=pl.Buffered(1)),
    ]
    args = [cols, rows, cols, bias]
    if cfg.sel_from_input:
        in_specs.append(pl.BlockSpec((1, LANES, TQ), lambda hb, b, qt: (b, 0, qt)))
        args.append(sel)
    if cfg.gate_branch is not None:
        assert gate_rblk0 % nb == 0
        in_specs.append(pl.BlockSpec(
            (nb * GATE_ROWS, TQ), lambda hb, b, qt: (gate_rblk0 // nb + hb, b * n_q + qt)))
        args.append(cols)
    if cfg.sink:
        in_specs.append(pl.BlockSpec((nb, 1, LANES), lambda hb, b, qt: (hb, 0, 0)))
        args.append(sinks)
    vt_tile = LANES if cfg.band_rows is not None else TK
    scratch = [
        pltpu.VMEM((SLOTS if cfg.selecting else 1, seq, HB_W), BF16),
        pltpu.VMEM((seq // vt_tile, HB_W, vt_tile), BF16),
    ]
    if cfg.band_rows is None:
        scratch += [
            pltpu.VMEM((SLOTS, n_delta, TK, TQ), F32),
            pltpu.VMEM((SLOTS, SUBLANES, TQ), F32),
            pltpu.VMEM((SLOTS, SUBLANES, TQ), F32),
            pltpu.VMEM((HB_W, TQ), F32),
        ]
    if cfg.moba:
        assert SLOTS * (seq // cfg.sel_blk) <= LANES
        scratch.append(pltpu.VMEM((SLOTS * (seq // cfg.sel_blk), HB_W), F32))
    return pl.pallas_call(
        functools.partial(_attn_kernel, cfg=cfg),
        grid=(N_HB // nb, bsz, n_q),
        in_specs=in_specs,
        out_specs=pl.BlockSpec((1, TQ, nb * HB_W), lambda hb, b, qt: (b, qt, hb)),
        out_shape=jax.ShapeDtypeStruct((bsz, seq, D_MODEL), F32),
        scratch_shapes=scratch,
        compiler_params=_params(3),
        name="attention",
    )(*args)


def _gelu_tanh(x):
    cdf = 0.5 * (1.0 + jnp.tanh(math.sqrt(2.0 / math.pi) * (x + 0.044715 * (x * x * x))))
    return x * cdf


def _compress_kernel(kc_lo_ref, kc_hi_ref, vc_lo_ref, vc_hi_ref, pos_ref, w1_ref, w2k_ref,
                     w2vt_ref, kcb_ref, vcbt_ref):
    half = CMP_LEN // 2
    n_chunk = kc_lo_ref.shape[1] // CMP_STRIDE
    flat_lane = lax.broadcasted_iota(jnp.int32, (1, half * HB_W), 1)
    group_of_lane = _div_pow2(jnp.bitwise_and(flat_lane, HB_W - 1), HEAD_DIM)
    for kind, (lo_ref, hi_ref) in enumerate(((kc_lo_ref, kc_hi_ref), (vc_lo_ref, vc_hi_ref))):
        chunks = [jnp.concatenate(
            [lo_ref[0, pl.ds(l, n_chunk, stride=CMP_STRIDE), :],
             hi_ref[0, pl.ds(l, n_chunk, stride=CMP_STRIDE), :]], axis=1) for l in range(half)]
        parts = []
        for h in range(2):
            pieces = [chunks[l] + pos_ref[kind, pl.ds(h * half + l, 1), :] for l in range(half)]
            flat = jnp.concatenate(pieces, axis=1)
            stacked = jnp.concatenate(
                [jnp.where(group_of_lane == g, flat, 0.0)
                 for g in range(B_KV_GROUPS)], axis=0).astype(BF16)
            parts.append(jnp.dot(stacked, w1_ref[kind, h], preferred_element_type=F32))
        rows = B_KV_GROUPS * n_chunk
        hidden = parts[0] + pltpu.roll(parts[1], rows - 1, axis=0)
        act = _gelu_tanh(hidden).astype(BF16)
        out = None
        for g in range(B_KV_GROUPS):
            act_g = act[g * n_chunk:(g + 1) * n_chunk]
            if kind == 0:
                o = jnp.dot(act_g, w2k_ref[g], preferred_element_type=F32)
            else:
                o = lax.dot_general(w2vt_ref[g], act_g, _NT, preferred_element_type=F32)
            out = o if out is None else out + o
        if kind == 0:
            kcb_ref[0] = out
        else:
            vcbt_ref[0] = out


def _compress(rows, kc_blk, vc_blk, pos, w1, w2k, w2vt):
    bsz, seq, _ = rows.shape
    n_chunk = seq // CMP_STRIDE
    return pl.pallas_call(
        _compress_kernel,
        grid=(bsz,),
        in_specs=[
            pl.BlockSpec((1, seq, LANES), lambda b: (b, 0, 2 * kc_blk)),
            pl.BlockSpec((1, seq, LANES), lambda b: (b, 0, 2 * kc_blk + 1)),
            pl.BlockSpec((1, seq, LANES), lambda b: (b, 0, 2 * vc_blk)),
            pl.BlockSpec((1, seq, LANES), lambda b: (b, 0, 2 * vc_blk + 1)),
            _resident(pos.shape),
            _resident(w1.shape),
            _resident(w2k.shape),
            _resident(w2vt.shape),
        ],
        out_specs=[pl.BlockSpec((1, n_chunk, HB_W), lambda b: (b, 0, 0)),
                   pl.BlockSpec((1, HB_W, n_chunk), lambda b: (b, 0, 0))],
        out_shape=[jax.ShapeDtypeStruct((bsz, n_chunk, HB_W), F32),
                   jax.ShapeDtypeStruct((bsz, HB_W, n_chunk), F32)],
        compiler_params=_params(1),
        name="nsa_compress",
    )(rows, rows, rows, rows, pos, w1, w2k, w2vt)


def _cmp_kernel(qt_ref, kcb_ref, vcbt_ref, gate_ref, ovt_ref, o_ref, sel_ref):
    qt = pl.program_id(1)
    n_cmp_pad = kcb_ref.shape[1]
    n_slc = LANES // B_KV_GROUPS
    chan = lax.broadcasted_iota(jnp.int32, (HB_W, 1), 0)
    slot_of_chan = _div_pow2(chan, HEAD_DIM)
    t = qt * TQ + lax.broadcasted_iota(jnp.int32, (1, TQ), 1)
    cmp_end = (lax.broadcasted_iota(jnp.int32, (n_cmp_pad, 1), 0) * CMP_STRIDE + CMP_LEN - 1)
    ok = cmp_end <= jnp.concatenate([t] * N_HB, axis=1)
    kc = kcb_ref[0].astype(BF16)
    vct = vcbt_ref[0].astype(BF16)
    gsig = _sigmoid(gate_ref[...])
    outs = [[None] * B_KV_GROUPS for _ in range(N_HB)]
    imp = None
    for g in range(B_KV_GROUPS):
        qm = jnp.concatenate(
            [jnp.where(slot_of_chan == g, qt_ref[r * HB_W:(r + 1) * HB_W, :], 0.0)
             for r in range(N_HB)], axis=1) * SCALE
        s = jnp.dot(kc, qm.astype(BF16), preferred_element_type=F32)
        s = jnp.where(ok, s, NEG)
        m = jnp.max(s, axis=0, keepdims=True)
        e = jnp.where(ok, jnp.exp(s - m), 0.0)
        p = e / jnp.maximum(jnp.sum(e, axis=0, keepdims=True), TINY)
        o = jnp.dot(vct[g * HEAD_DIM:(g + 1) * HEAD_DIM], p.astype(BF16),
                    preferred_element_type=F32)
        p_group = None
        for r in range(N_HB):
            p_r = p[:, r * TQ:(r + 1) * TQ]
            p_group = p_r if p_group is None else p_group + p_r
            row = r * GATE_ROWS + g
            outs[r][g] = o[:, r * TQ:(r + 1) * TQ] * gsig[row:row + 1, :]
        part = jnp.dot(ovt_ref[g], p_group, precision=lax.Precision.HIGHEST,
                       preferred_element_type=F32)
        imp = part if imp is None else imp + part
    for r in range(N_HB):
        o_ref[0, :, r * HB_W:(r + 1) * HB_W] = jnp.concatenate(outs[r], axis=0).T
    imp = imp.reshape(B_KV_GROUPS, n_slc, TQ)
    j = lax.broadcasted_iota(jnp.int32, (1, n_slc, 1), 1)
    cur = _div_pow2(t, SLC_BLK).reshape(1, 1, TQ)
    val = jnp.where(j == cur - 1, -NEG, imp)
    val = jnp.where(j == cur, -NEG, val)
    val = jnp.where(j == 0, -NEG, val)
    val = jnp.where(j > cur, NEG, val)
    chosen = jnp.where(_rank_rows(val) < SLC_TOPK, 1.0, 0.0)
    sel_ref[0] = chosen.reshape(LANES, TQ)


def _cmp_attention(cols, kcb, vcbt, gate_rblk, ovt, bsz, seq):
    n_cmp_pad = kcb.shape[1]
    n_q = seq // TQ
    return pl.pallas_call(
        _cmp_kernel,
        grid=(bsz, n_q),
        in_specs=[
            pl.BlockSpec((D_MODEL, TQ), lambda b, qt: (0, b * n_q + qt)),
            pl.BlockSpec((1, n_cmp_pad, HB_W), lambda b, qt: (b, 0, 0)),
            pl.BlockSpec((1, HB_W, n_cmp_pad), lambda b, qt: (b, 0, 0)),
            pl.BlockSpec((N_HB * GATE_ROWS, TQ), lambda b, qt: (gate_rblk, b * n_q + qt)),
            _resident(ovt.shape),
        ],
        out_specs=[
            pl.BlockSpec((1, TQ, D_MODEL), lambda b, qt: (b, qt, 0)),
            pl.BlockSpec((1, LANES, TQ), lambda b, qt: (b, 0, qt)),
        ],
        out_shape=[
            jax.ShapeDtypeStruct((bsz, seq, D_MODEL), F32),
            jax.ShapeDtypeStruct((bsz, LANES, seq), F32),
        ],
        compiler_params=_params(2),
        name="nsa_cmp_select",
    )(cols, kcb, vcbt, cols, ovt)


def _mixer_a(xf, bsz, seq, g_pre, w_in, w_out, sinks, rel_bias):
    qw, kvw = N_HEADS * HEAD_DIM, A_KV_HEADS * HEAD_DIM
    w_q = _interleave_heads(w_in[:, :qw].reshape(D_MODEL, N_HEADS, HEAD_DIM), A_KV_HEADS, 1)
    w_cols = jnp.concatenate([w_q.reshape(D_MODEL, qw), w_in[:, qw + kvw:]], axis=1)
    rows, cols = _proj(xf, g_pre, w_in[:, qw:qw + kvw], w_cols)
    rows = rows.reshape(bsz, seq, kvw)
    band_rows = TQ + -(-(A_WINDOW - 1) // LANES) * LANES
    assert band_rows <= seq
    bias = _band_bias(rel_bias, A_KV_HEADS, band_rows, A_WINDOW)
    sink_rows = _interleave_heads(sinks.astype(F32), A_KV_HEADS, 0).reshape(N_HB, 1, SLOTS)
    sink_rows = jnp.pad(sink_rows, ((0, 0), (0, 0), (0, LANES - SLOTS)))
    cfg = _AttnCfg(kv_width=kvw, band_rows=band_rows, sink=True, blocks_per_step=N_HB)
    o = _attention(cfg, rows, cols, bias, k_blk=lambda hb: 0, v_rblk=lambda hb: qw // kvw,
                   sinks=sink_rows)
    w_o = _interleave_heads(w_out.reshape(N_HEADS, HEAD_DIM, D_MODEL), A_KV_HEADS, 0)
    return [o.reshape(bsz * seq, -1)], w_o.reshape(qw, D_MODEL)


def _nsa_overlap_t(seq):
    n_chunk = seq // CMP_STRIDE
    n_cmp = (seq - CMP_LEN) // CMP_STRIDE + 1
    n_slc = seq // SLC_BLK
    c_start = np.arange(n_cmp)[:, None] * CMP_STRIDE
    s_start = np.arange(n_slc)[None, :] * SLC_BLK
    overlap = ((c_start < s_start + SLC_BLK) & (c_start + CMP_LEN > s_start)).astype(np.float32)
    ovt = np.zeros((B_KV_GROUPS, LANES, n_chunk), np.float32)
    for g in range(B_KV_GROUPS):
        ovt[g, g * n_slc:(g + 1) * n_slc, :n_cmp] = overlap.T
    return jnp.asarray(ovt)


def _mixer_b(xf, bsz, seq, g_pre, w_in, w_out, cmp_pos, cmp_w1, cmp_w2, bias_nat, rel_bias):
    assert seq // SLC_BLK * B_KV_GROUPS == LANES and seq // CMP_STRIDE == LANES
    grp, dh = B_KV_GROUPS, HEAD_DIM
    qw, kvw = N_HEADS * dh, B_KV_GROUPS * dh
    w_q = _interleave_heads(w_in[:, :qw].reshape(D_MODEL, N_HEADS, dh), grp, 1)
    kv = [w_in[:, qw + i * kvw:qw + (i + 1) * kvw] for i in range(6)]
    w_g = w_in[:, qw + 6 * kvw:].reshape(D_MODEL, grp, N_HB, 3)
    w_g = w_g.transpose(0, 2, 3, 1).reshape(D_MODEL, N_HB, 3 * grp)
    w_g = jnp.pad(w_g, ((0, 0), (0, 0), (0, GATE_ROWS - 3 * grp)))
    w_rows = jnp.concatenate([kv[0], kv[1], kv[2], kv[4]], axis=1)
    w_cols = jnp.concatenate([w_q.reshape(D_MODEL, qw), kv[3], kv[5],
                              w_g.reshape(D_MODEL, N_HB * GATE_ROWS)], axis=1)
    rows, cols = _proj(xf, g_pre, w_rows, w_cols)
    rows = rows.reshape(bsz, seq, 4 * kvw)
    gate_row0 = qw + 2 * kvw

    half = CMP_LEN // 2
    pos = jnp.tile(cmp_pos.astype(F32), (1, 1, grp))
    w1 = cmp_w1.reshape(2, 2, half, 1, dh, CMP_HIDDEN)
    w1 = jnp.broadcast_to(w1, (2, 2, half, grp, dh, CMP_HIDDEN))
    w1 = w1.reshape(2, 2, half * HB_W, CMP_HIDDEN).astype(BF16)
    w2 = jnp.zeros((2, grp, CMP_HIDDEN, grp, dh), F32)
    for g in range(grp):
        w2 = w2.at[:, g, :, g, :].set(cmp_w2)
    w2 = w2.reshape(2, grp, CMP_HIDDEN, HB_W).astype(BF16)
    kcb, vcbt = _compress(rows, 0, 1, pos, w1, w2[0], jnp.swapaxes(w2[1], 1, 2))

    o_cmp, sel = _cmp_attention(cols, kcb, vcbt, gate_row0 // (N_HB * GATE_ROWS),
                                _nsa_overlap_t(seq), bsz, seq)

    gate_rblk0 = gate_row0 // GATE_ROWS
    bias = _block_bias(bias_nat, grp, seq // TK)
    cfg_slc = _AttnCfg(kv_width=kvw, sel_blk=SLC_BLK, sel_from_input=True, gate_branch=1,
                       blocks_per_step=2)
    o_slc = _attention(cfg_slc, rows, cols, bias, k_blk=lambda hb: 2,
                       v_rblk=lambda hb: qw // kvw, sel=sel, gate_rblk0=gate_rblk0)
    band_rows = TQ + -(-(B_WINDOW - 1) // LANES) * LANES
    assert band_rows <= seq
    cfg_win = _AttnCfg(kv_width=kvw, band_rows=band_rows, gate_branch=2, blocks_per_step=2)
    bias_win = _band_bias(rel_bias, grp, band_rows, B_WINDOW)
    o_win = _attention(cfg_win, rows, cols, bias_win, k_blk=lambda hb: 3,
                       v_rblk=lambda hb: qw // kvw + 1, gate_rblk0=gate_rblk0)
    w_o = _interleave_heads(w_out.reshape(N_HEADS, dh, D_MODEL), grp, 0).reshape(qw, D_MODEL)
    n = bsz * seq
    return [o_cmp.reshape(n, -1), o_slc.reshape(n, -1), o_win.reshape(n, -1)], w_o


def _mixer_c(xf, bsz, seq, g_pre, w_in, w_out, bias_nat):
    assert seq % MOBA_BLK == 0 and MOBA_BLK == TK
    qw = N_HEADS * HEAD_DIM
    w_cols = jnp.concatenate([w_in[:, :qw], w_in[:, 2 * qw:]], axis=1)
    rows, cols = _proj(xf, g_pre, w_in[:, qw:2 * qw], w_cols)
    rows = rows.reshape(bsz, seq, qw)
    bias = _block_bias(bias_nat, 1, seq // TK)
    cfg = _AttnCfg(kv_width=HB_W, sel_blk=MOBA_BLK)
    o = _attention(cfg, rows, cols, bias, k_blk=lambda hb: hb, v_rblk=lambda hb: N_HB + hb)
    return [o.reshape(bsz * seq, -1)], w_out


def kernel(x, rel_bias, norm_g, ffn_w_in, ffn_w_out, a_w_in, a_w_out, a_sinks, b_w_in, b_w_out,
           b_cmp_pos, b_cmp_w1, b_cmp_w2, c_w_in, c_w_out):
    bsz, seq, d = x.shape
    assert d == D_MODEL and seq % TQ == 0 and (bsz * seq) % TM == 0
    depth = norm_g.shape[0]
    bias_nat = _bias_tiles(rel_bias, seq // TK)
    xf = x.reshape(bsz * seq, d)
    for i in range(depth):
        g = norm_g[i]
        xf = _ffn(xf, g[0], ffn_w_in[i, 0], ffn_w_out[i, 0], g[1])
        kind, j = i % N_MIXERS, i // N_MIXERS
        if kind == 0:
            o_list, w_mix = _mixer_a(xf, bsz, seq, g[2], a_w_in[j], a_w_out[j], a_sinks[j],
                                     rel_bias)
        elif kind == 1:
            o_list, w_mix = _mixer_b(xf, bsz, seq, g[2], b_w_in[j], b_w_out[j], b_cmp_pos[j],
                                     b_cmp_w1[j], b_cmp_w2[j], bias_nat, rel_bias)
        else:
            o_list, w_mix = _mixer_c(xf, bsz, seq, g[2], c_w_in[j], c_w_out[j], bias_nat)
        xf = _ffn(xf, g[4], ffn_w_in[i, 1], ffn_w_out[i, 1], g[5], mix=(o_list, w_mix, g[3]))
    return xf.reshape(bsz, seq, d)
```

```python
import dataclasses
import functools
import math

import numpy as np
import jax
import jax.numpy as jnp
from jax import lax
from jax.experimental import pallas as pl
from jax.experimental.pallas import tpu as pltpu

F32 = jnp.float32
BF16 = jnp.bfloat16

D_MODEL = 1024
HEAD_DIM = 64
N_HEADS = D_MODEL // HEAD_DIM
D_FF = 2816
NORM_EPS = 1e-6
N_MIXERS = 3
REL_BUCKETS = 32
REL_MAX_DIST = 1024
A_KV_HEADS = 2
A_WINDOW = 128
B_KV_GROUPS = 4
CMP_LEN = 32
CMP_STRIDE = 16
CMP_HIDDEN = 256
SLC_BLK = 64
SLC_TOPK = 16
B_WINDOW = 512
MOBA_BLK = 256
MOBA_TOPK = 3
NEG = -1e30
TINY = 1e-30

LANES = 128
SUBLANES = 8
MXU_DIM = 256
VMEM_LIMIT_BYTES = 56 * 1024 * 1024

SLOTS = 4
HB_W = SLOTS * HEAD_DIM
N_HB = N_HEADS // SLOTS
TQ = 256
TK = 256
TM = 512
FF_CHUNKS = ((0, 1536), (1536, D_FF))
GATE_ROWS = 16
SEL_BIG = float(2 ** 30)
SCALE = HEAD_DIM ** -0.5
LOG2E = math.log2(math.e)

_NT = (((1,), (1,)), ((), ()))


def _params(n_axes):
    return pltpu.CompilerParams(
        dimension_semantics=("arbitrary",) * n_axes,
        vmem_limit_bytes=VMEM_LIMIT_BYTES)


def _rms(x, g):
    return x * lax.rsqrt(jnp.mean(x * x, axis=-1, keepdims=True) + NORM_EPS) * g


def _sigmoid(x):
    return 1.0 / (1.0 + jnp.exp(-x))


def _div_pow2(x, d):
    assert d & (d - 1) == 0
    return jnp.right_shift(x, d.bit_length() - 1)


def _resident(shape):
    zeros = (0,) * len(shape)
    return pl.BlockSpec(shape, lambda *_: zeros, pipeline_mode=pl.Buffered(1))


def _ffn_kernel(*refs, n_mix):
    mix_refs, refs = refs[:n_mix], refs[n_mix:]
    if n_mix:
        wmix_ref, gmix_ref = refs[:2]
        refs = refs[2:]
    x_ref, gpre_ref, win_ref, wout_ref, gpost_ref, o_ref = refs
    x = x_ref[...]
    if n_mix:
        o = mix_refs[0][...]
        for r in mix_refs[1:]:
            o = o.astype(F32) + r[...].astype(F32)
        mixed = jnp.dot(o.astype(BF16), wmix_ref[...], preferred_element_type=F32)
        x = x + _rms(mixed, gmix_ref[...])
    hn = _rms(x, gpre_ref[...]).astype(BF16)
    y = None
    for lo, hi in FF_CHUNKS:
        gate = jnp.dot(hn, win_ref[:, lo:hi], preferred_element_type=F32)
        up = jnp.dot(hn, win_ref[:, D_FF + lo:D_FF + hi], preferred_element_type=F32)
        act = (gate * _sigmoid(gate) * up).astype(BF16)
        part = jnp.dot(act, wout_ref[lo:hi, :], preferred_element_type=F32)
        y = part if y is None else y + part
    o_ref[...] = x + 0.5 * _rms(y, gpost_ref[...])


def _ffn(xf, g_pre, w_in, w_out, g_post, mix=None):
    n, d = xf.shape
    tile = pl.BlockSpec((TM, d), lambda i: (i, 0))
    mix_specs, mix_args = [], []
    if mix is not None:
        o_list, w_mix, g_mix = mix
        mix_specs = [tile] * len(o_list) + [_resident(w_mix.shape), _resident((1, d))]
        mix_args = list(o_list) + [w_mix.astype(BF16), g_mix.reshape(1, d)]
    return pl.pallas_call(
        functools.partial(_ffn_kernel, n_mix=len(mix_args) - 2 if mix_args else 0),
        grid=(n // TM,),
        in_specs=mix_specs + [
            tile,
            _resident((1, d)),
            _resident(w_in.shape),
            _resident(w_out.shape),
            _resident((1, d)),
        ],
        out_specs=tile,
        out_shape=jax.ShapeDtypeStruct((n, d), F32),
        compiler_params=_params(1),
        name="ffn",
    )(*mix_args, xf, g_pre.reshape(1, d), w_in.astype(BF16), w_out.astype(BF16),
      g_post.reshape(1, d))


def _proj_kernel(x_ref, g_ref, wr_ref, wct_ref, rows_ref, cols_ref):
    hn = _rms(x_ref[...], g_ref[...]).astype(BF16)
    rows_ref[...] = jnp.dot(hn, wr_ref[...], preferred_element_type=F32)
    cols_ref[...] = lax.dot_general(wct_ref[...], hn, _NT,
                                    preferred_element_type=F32).astype(cols_ref.dtype)


def _proj(xf, g, w_rows, w_cols):
    n, d = xf.shape
    cr, cc = w_rows.shape[1], w_cols.shape[1]
    return pl.pallas_call(
        _proj_kernel,
        grid=(n // TM,),
        in_specs=[
            pl.BlockSpec((TM, d), lambda i: (i, 0)),
            _resident((1, d)),
            _resident((d, cr)),
            _resident((cc, d)),
        ],
        out_specs=[pl.BlockSpec((TM, cr), lambda i: (i, 0)),
                   pl.BlockSpec((cc, TM), lambda i: (0, i))],
        out_shape=[jax.ShapeDtypeStruct((n, cr), F32), jax.ShapeDtypeStruct((cc, n), BF16)],
        compiler_params=_params(1),
        name="proj",
    )(xf, g.reshape(1, d), w_rows.astype(BF16), w_cols.T.astype(BF16))


def _rel_bucket(dist):
    dist = jnp.maximum(dist, 0)
    exact = REL_BUCKETS // 2
    d = jnp.maximum(dist, 1).astype(jnp.float32)
    log_b = exact + (jnp.log(d / exact) / math.log(REL_MAX_DIST / exact)
                     * (REL_BUCKETS - exact)).astype(jnp.int32)
    return jnp.where(dist < exact, dist, jnp.minimum(log_b, REL_BUCKETS - 1))


def _tile_distance(n_delta, xp):
    delta = xp.arange(n_delta)[:, None, None]
    return delta * TK + xp.arange(TQ)[None, None, :] - xp.arange(TK)[None, :, None]


def _bias_of_distance(rel_bias, dist):
    bucket = _rel_bucket(dist).reshape(1, -1)
    onehot = (jnp.arange(REL_BUCKETS)[:, None] == bucket).astype(F32)
    tiles = jnp.dot(rel_bias.astype(F32).T, onehot, precision=lax.Precision.HIGHEST)
    return tiles.reshape((N_HEADS,) + dist.shape)


def _bias_tiles(rel_bias, n_delta):
    return _bias_of_distance(rel_bias, _tile_distance(n_delta, jnp))


def _band_distance(band_rows, xp):
    back = band_rows - TQ
    return back + xp.arange(TQ)[None, :] - xp.arange(band_rows + back)[:, None]


def _band_bias(rel_bias, n_groups, band_rows, window):
    bias = _bias_of_distance(rel_bias, _band_distance(band_rows, jnp))
    bias = _interleave_heads(bias, n_groups, 0)
    dist = _band_distance(band_rows, np)
    bias = jnp.where((dist >= 0) & (dist < window), bias * LOG2E, NEG)
    return bias.reshape((N_HB, SLOTS) + bias.shape[1:])


def _interleave_heads(w, n_groups, axis):
    r = N_HEADS // n_groups
    shape = w.shape
    w = w.reshape(shape[:axis] + (n_groups, r) + shape[axis + 1:])
    w = jnp.swapaxes(w, axis, axis + 1)
    return w.reshape(shape)


def _block_bias(bias_nat, n_groups, n_delta, window=None):
    dist = _tile_distance(n_delta, np)
    visible = dist >= 0
    if window is not None:
        visible &= dist < window
    bias = bias_nat[:, :n_delta]
    if n_groups > 1:
        bias = _interleave_heads(bias, n_groups, 0)
    bias = jnp.where(visible, bias * LOG2E, NEG)
    return bias.reshape((N_HB, SLOTS) + bias.shape[1:])


@dataclasses.dataclass(frozen=True)
class _AttnCfg:
    kv_width: int
    n_back: int | None = None
    band_rows: int | None = None
    blocks_per_step: int = 1
    sel_blk: int | None = None
    sel_from_input: bool = False
    gate_branch: int | None = None
    sink: bool = False

    @property
    def selecting(self):
        return self.sel_blk is not None

    @property
    def moba(self):
        return self.selecting and not self.sel_from_input


def _rank_rows(val):
    width = val.shape[1]
    row = lax.broadcasted_iota(jnp.int32, (1, width, 1), 1)
    rank = jnp.zeros(val.shape, F32)
    for m in range(width):
        vm = val[:, m:m + 1, :]
        rank = rank + jnp.where(row > m, jnp.where(vm >= val, 1.0, 0.0),
                                jnp.where(vm > val, 1.0, 0.0))
    return rank


def _u_chan0(hh, n_blk):
    return hh * n_blk + (LANES if hh < SLOTS // 2 else 0)


def _fold_rows(x, op):
    return op(x.reshape(x.shape[0] // SUBLANES, SUBLANES, x.shape[1]), axis=0)


def _attn_kernel(*refs, cfg: _AttnCfg):
    it = iter(refs)
    qt_ref, k_ref, vt_ref, bias_ref = next(it), next(it), next(it), next(it)
    sel_ref = next(it) if cfg.sel_from_input else None
    gate_ref = next(it) if cfg.gate_branch is not None else None
    sink_ref = next(it) if cfg.sink else None
    o_ref = next(it)
    kb_ref, vtb_ref = next(it), next(it)
    scratch_refs = ((next(it), next(it), next(it), next(it))
                    if cfg.band_rows is None else ())
    kmx_ref = next(it) if cfg.moba else None
    vt_tile = LANES if cfg.band_rows is not None else TK

    qt = pl.program_id(2)
    seq = k_ref.shape[1]
    n_tiles = seq // TK
    lane = lax.broadcasted_iota(jnp.int32, (1, HB_W), 1)
    slot_of_lane = _div_pow2(lane, HEAD_DIM)
    chan = lax.broadcasted_iota(jnp.int32, (HB_W, 1), 0)
    slot_of_chan = _div_pow2(chan, HEAD_DIM)
    n_sel_blk = seq // cfg.sel_blk if cfg.selecting else 0

    @pl.when(qt == 0)
    def _prepare():
        if cfg.moba:
            kmx_ref[...] = jnp.zeros(kmx_ref.shape, F32)
        for c in range(n_tiles):
            rows = pl.ds(c * TK, TK)
            k = k_ref[0, rows, :]
            vt = vt_ref[:, c * TK:(c + 1) * TK]
            if cfg.kv_width == LANES:
                k = jnp.concatenate([k, k], axis=1)
                vt = jnp.concatenate([vt, vt], axis=0)
            for i in range(TK // vt_tile):
                vtb_ref[c * (TK // vt_tile) + i] = vt[:, i * vt_tile:(i + 1) * vt_tile].astype(BF16)
            if cfg.selecting:
                key_blk = _div_pow2(c * TK + lax.broadcasted_iota(jnp.int32, (TK, 1), 0),
                                    cfg.sel_blk)
                for hh in range(SLOTS):
                    onehot = (lane - _u_chan0(hh, n_sel_blk)) == key_blk
                    kb_ref[hh, rows, :] = jnp.where(
                        slot_of_lane == hh, k, jnp.where(onehot, 1.0, 0.0)).astype(BF16)
            else:
                kb_ref[0, rows, :] = k.astype(BF16)
            if cfg.moba:
                k_mean = jnp.mean(k, axis=0, keepdims=True)
                for hh in range(SLOTS):
                    kmx_ref[pl.ds(hh * n_sel_blk + c, 1), :] = jnp.where(
                        slot_of_lane == hh, k_mean, 0.0)

    refs_of_step = (qt_ref, bias_ref, sel_ref, gate_ref, sink_ref, o_ref, kb_ref, vtb_ref,
                    scratch_refs, kmx_ref)
    for blk in range(qt_ref.shape[0] // HB_W):
        _attend_block(cfg, blk, qt, n_sel_blk, slot_of_chan, refs_of_step)


def _attend_block(cfg, blk, qt, n_sel_blk, slot_of_chan, refs):
    (qt_ref, bias_ref, sel_ref, gate_ref, sink_ref, o_ref, kb_ref, vtb_ref, scratch_refs,
     kmx_ref) = refs
    bias_blk = bias_ref.at[blk]
    q_t = qt_ref[blk * HB_W:(blk + 1) * HB_W, :].astype(F32)

    sel_rows = None
    if cfg.sel_from_input:
        sel_rows = sel_ref[0]
    elif cfg.moba:
        gate = jnp.dot(kmx_ref[...].astype(BF16), q_t.astype(BF16),
                       preferred_element_type=F32)
        gate = gate.reshape(SLOTS, n_sel_blk, TQ)
        key_blk = lax.broadcasted_iota(jnp.int32, (1, n_sel_blk, 1), 1)
        val = jnp.where(key_blk < qt, gate, NEG)
        rank = _rank_rows(val)
        chosen = jnp.where(key_blk < qt, jnp.where(rank < MOBA_TOPK, 1.0, 0.0), 0.0)
        chosen = jnp.where(key_blk == qt, 1.0, chosen)
        sel_rows = chosen.reshape(SLOTS * n_sel_blk, TQ)

    q_scaled = q_t * (SCALE * LOG2E)
    qms = []
    for hh in range(SLOTS):
        pieces = {hh * HEAD_DIM: q_scaled[hh * HEAD_DIM:(hh + 1) * HEAD_DIM]}
        if cfg.selecting:
            pieces[_u_chan0(hh, n_sel_blk)] = (
                sel_rows[hh * n_sel_blk:(hh + 1) * n_sel_blk] - 1.0) * SEL_BIG
        segs, row = [], 0
        for start in sorted(pieces):
            if start > row:
                segs.append(jnp.zeros((start - row, TQ), F32))
            segs.append(pieces[start])
            row = start + pieces[start].shape[0]
        if row < HB_W:
            segs.append(jnp.zeros((HB_W - row, TQ), F32))
        qms.append(jnp.concatenate(segs, axis=0).astype(BF16))
    q_stack = None if cfg.selecting else jnp.concatenate(qms, axis=1)

    sinks = ([sink_ref[blk, :, hh:hh + 1] * LOG2E for hh in range(SLOTS)]
             if cfg.sink else None)
    if cfg.band_rows is not None:
        ms, sums, accs = _band_softmax(cfg, qt, q_stack, kb_ref, vtb_ref, bias_blk, sinks)
    else:
        ms, sums, accs = _tiled_softmax(cfg, qt, qms, q_stack, kb_ref, vtb_ref, bias_blk,
                                        sinks, *scratch_refs)

    gsig = None
    if cfg.gate_branch is not None:
        gsig = _sigmoid(
            gate_ref[blk * GATE_ROWS:(blk + 1) * GATE_ROWS, :].astype(F32))
    outs = []
    for hh in range(SLOTS):
        denom = jnp.sum(sums[hh], axis=0, keepdims=True)
        if cfg.sink:
            denom = denom + jnp.exp2(sinks[hh] - ms[hh])
        o_h = accs[hh] / jnp.maximum(denom, TINY)
        if cfg.gate_branch is not None:
            row = cfg.gate_branch * SLOTS + hh
            o_h = o_h * gsig[row:row + 1, :]
        outs.append(o_h)
    o_ref[0, :, blk * HB_W:(blk + 1) * HB_W] = (
        jnp.concatenate(outs, axis=0).T.astype(o_ref.dtype))


def _band_softmax(cfg, qt, q_stack, kb_ref, vtb_ref, bias_ref, sinks):
    band = cfg.band_rows
    back = band - TQ
    start = pl.multiple_of(jnp.maximum(qt * TQ - back, 0), LANES)
    s_all = jnp.dot(kb_ref[0, pl.ds(start, band), :], q_stack, preferred_element_type=F32)
    bias_rows = pl.ds(pl.multiple_of(back - (qt * TQ - start), LANES), band)
    tile0 = _div_pow2(start, LANES)
    ms, sums, accs = [], [], []
    for hh in range(SLOTS):
        head = pl.ds(hh * HEAD_DIM, HEAD_DIM)
        sc = s_all[:, hh * TQ:(hh + 1) * TQ] + bias_ref[hh, bias_rows, :]
        m = jnp.max(_fold_rows(sc, jnp.max), axis=0, keepdims=True)
        if cfg.sink:
            m = jnp.maximum(m, sinks[hh])
        p = jnp.exp2(sc - m)
        vt = jnp.concatenate([vtb_ref[tile0 + c, head, :] for c in range(band // LANES)],
                             axis=1)
        ms.append(m)
        sums.append(_fold_rows(p, jnp.sum))
        accs.append(jnp.dot(vt, p.astype(BF16), preferred_element_type=F32))
    return ms, sums, accs


def _tiled_softmax(cfg, qt, qms, q_stack, kb_ref, vtb_ref, bias_ref, sinks,
                   sc_ref, m_ref, l_ref, acc_ref):
    lo = jnp.maximum(qt - cfg.n_back, 0) if cfg.n_back is not None else 0
    m_ref[...] = jnp.full(m_ref.shape, NEG, F32)
    l_ref[...] = jnp.zeros(l_ref.shape, F32)
    acc_ref[...] = jnp.zeros(acc_ref.shape, F32)

    def pass1(kt, width):
        rows = pl.ds(pl.multiple_of(kt * TK, TK), width * TK)
        if cfg.selecting:
            s = [jnp.dot(kb_ref[hh, rows, :], qms[hh], preferred_element_type=F32)
                 for hh in range(SLOTS)]
        else:
            s_all = jnp.dot(kb_ref[0, rows, :], q_stack, preferred_element_type=F32)
            s = [s_all[:, hh * TQ:(hh + 1) * TQ] for hh in range(SLOTS)]
        for hh in range(SLOTS):
            m_new = m_ref[hh]
            for w in range(width):
                tile = s[hh][w * TK:(w + 1) * TK] + bias_ref[hh, qt - kt - w]
                sc_ref[hh, kt - lo + w] = tile
                m_new = jnp.maximum(m_new, _fold_rows(tile, jnp.max))
            m_ref[hh] = m_new

    def pass2(kt, width, ms):
        for hh in range(SLOTS):
            head = pl.ds(hh * HEAD_DIM, HEAD_DIM)
            l_new = l_ref[hh]
            o_new = acc_ref[head, :]
            for w in range(width):
                p = jnp.exp2(sc_ref[hh, kt - lo + w] - ms[hh])
                l_new = l_new + _fold_rows(p, jnp.sum)
                o_new = o_new + jnp.dot(vtb_ref[kt + w, head, :], p.astype(BF16),
                                        preferred_element_type=F32)
            l_ref[hh] = l_new
            acc_ref[head, :] = o_new

    def over_tiles(body):
        n = qt + 1 - lo
        start = lo
        for width in (1, 2):
            has = jnp.bitwise_and(n, width)

            @pl.when(has != 0)
            def _part(start=start, width=width):
                body(start, width)

            start = start + has

        def group(i, carry):
            body(start + 4 * i, 4)
            return carry

        lax.fori_loop(0, jnp.right_shift(n, 2), group, 0)

    over_tiles(pass1)
    ms = []
    for hh in range(SLOTS):
        m = jnp.max(m_ref[hh], axis=0, keepdims=True)
        ms.append(jnp.maximum(m, sinks[hh]) if cfg.sink else m)
    over_tiles(functools.partial(pass2, ms=ms))
    return (ms, [l_ref[hh] for hh in range(SLOTS)],
            [acc_ref[hh * HEAD_DIM:(hh + 1) * HEAD_DIM, :] for hh in range(SLOTS)])


def _attention(cfg, rows, cols, bias, *, k_blk, v_rblk, sel=None, gate_rblk0=None, sinks=None):
    bsz, seq, _ = rows.shape
    kw = cfg.kv_width
    n_delta = bias.shape[2]
    n_q = seq // TQ
    nb = cfg.blocks_per_step
    bias_tail = (0,) * (bias.ndim - 1)
    in_specs = [
        pl.BlockSpec((nb * HB_W, TQ), lambda hb, b, qt: (hb, b * n_q + qt)),
        pl.BlockSpec((1, seq, kw), lambda hb, b, qt: (b, 0, k_blk(hb))),
        pl.BlockSpec((kw, seq), lambda hb, b, qt: (v_rblk(hb), b)),
        pl.BlockSpec((nb,) + bias.shape[1:], lambda hb, b, qt: (hb,) + bias_tail,
                     pipeline_mode=pl.Buffered(1)),
    ]
    args = [cols, rows, cols, bias]
    if cfg.sel_from_input:
        in_specs.append(pl.BlockSpec((1, LANES, TQ), lambda hb, b, qt: (b, 0, qt)))
        args.append(sel)
    if cfg.gate_branch is not None:
        assert gate_rblk0 % nb == 0
        in_specs.append(pl.BlockSpec(
            (nb * GATE_ROWS, TQ), lambda hb, b, qt: (gate_rblk0 // nb + hb, b * n_q + qt)))
        args.append(cols)
    if cfg.sink:
        in_specs.append(pl.BlockSpec((nb, 1, LANES), lambda hb, b, qt: (hb, 0, 0)))
        args.append(sinks)
    vt_tile = LANES if cfg.band_rows is not None else TK
    scratch = [
        pltpu.VMEM((SLOTS if cfg.selecting else 1, seq, HB_W), BF16),
        pltpu.VMEM((seq // vt_tile, HB_W, vt_tile), BF16),
    ]
    if cfg.band_rows is None:
        scratch += [
            pltpu.VMEM((SLOTS, n_delta, TK, TQ), F32),
            pltpu.VMEM((SLOTS, SUBLANES, TQ), F32),
            pltpu.VMEM((SLOTS, SUBLANES, TQ), F32),
            pltpu.VMEM((HB_W, TQ), F32),
        ]
    if cfg.moba:
        assert SLOTS * (seq // cfg.sel_blk) <= LANES
        scratch.append(pltpu.VMEM((SLOTS * (seq // cfg.sel_blk), HB_W), F32))
    return pl.pallas_call(
        functools.partial(_attn_kernel, cfg=cfg),
        grid=(N_HB // nb, bsz, n_q),
        in_specs=in_specs,
        out_specs=pl.BlockSpec((1, TQ, nb * HB_W), lambda hb, b, qt: (b, qt, hb)),
        out_shape=jax.ShapeDtypeStruct((bsz, seq, D_MODEL), BF16),
        scratch_shapes=scratch,
        compiler_params=_params(3),
        name="attention",
    )(*args)


def _gelu_tanh(x):
    cdf = 0.5 * (1.0 + jnp.tanh(math.sqrt(2.0 / math.pi) * (x + 0.044715 * (x * x * x))))
    return x * cdf


def _compress_kernel(kc_lo_ref, kc_hi_ref, vc_lo_ref, vc_hi_ref, pos_ref, w1_ref, w2k_ref,
                     w2vt_ref, kcb_ref, vcbt_ref):
    half = CMP_LEN // 2
    n_chunk = kc_lo_ref.shape[1] // CMP_STRIDE
    flat_lane = lax.broadcasted_iota(jnp.int32, (1, half * HB_W), 1)
    group_of_lane = _div_pow2(jnp.bitwise_and(flat_lane, HB_W - 1), HEAD_DIM)
    for kind, (lo_ref, hi_ref) in enumerate(((kc_lo_ref, kc_hi_ref), (vc_lo_ref, vc_hi_ref))):
        chunks = [jnp.concatenate(
            [lo_ref[0, pl.ds(l, n_chunk, stride=CMP_STRIDE), :],
             hi_ref[0, pl.ds(l, n_chunk, stride=CMP_STRIDE), :]], axis=1) for l in range(half)]
        parts = []
        for h in range(2):
            pieces = [chunks[l] + pos_ref[kind, pl.ds(h * half + l, 1), :] for l in range(half)]
            flat = jnp.concatenate(pieces, axis=1)
            stacked = jnp.concatenate(
                [jnp.where(group_of_lane == g, flat, 0.0)
                 for g in range(B_KV_GROUPS)], axis=0).astype(BF16)
            parts.append(jnp.dot(stacked, w1_ref[kind, h], preferred_element_type=F32))
        rows = B_KV_GROUPS * n_chunk
        hidden = parts[0] + pltpu.roll(parts[1], rows - 1, axis=0)
        act = _gelu_tanh(hidden).astype(BF16)
        out = None
        for g in range(B_KV_GROUPS):
            act_g = act[g * n_chunk:(g + 1) * n_chunk]
            if kind == 0:
                o = jnp.dot(act_g, w2k_ref[g], preferred_element_type=F32)
            else:
                o = lax.dot_general(w2vt_ref[g], act_g, _NT, preferred_element_type=F32)
            out = o if out is None else out + o
        if kind == 0:
            kcb_ref[0] = out
        else:
            vcbt_ref[0] = out


def _compress(rows, kc_blk, vc_blk, pos, w1, w2k, w2vt):
    bsz, seq, _ = rows.shape
    n_chunk = seq // CMP_STRIDE
    return pl.pallas_call(
        _compress_kernel,
        grid=(bsz,),
        in_specs=[
            pl.BlockSpec((1, seq, LANES), lambda b: (b, 0, 2 * kc_blk)),
            pl.BlockSpec((1, seq, LANES), lambda b: (b, 0, 2 * kc_blk + 1)),
            pl.BlockSpec((1, seq, LANES), lambda b: (b, 0, 2 * vc_blk)),
            pl.BlockSpec((1, seq, LANES), lambda b: (b, 0, 2 * vc_blk + 1)),
            _resident(pos.shape),
            _resident(w1.shape),
            _resident(w2k.shape),
            _resident(w2vt.shape),
        ],
        out_specs=[pl.BlockSpec((1, n_chunk, HB_W), lambda b: (b, 0, 0)),
                   pl.BlockSpec((1, HB_W, n_chunk), lambda b: (b, 0, 0))],
        out_shape=[jax.ShapeDtypeStruct((bsz, n_chunk, HB_W), F32),
                   jax.ShapeDtypeStruct((bsz, HB_W, n_chunk), F32)],
        compiler_params=_params(1),
        name="nsa_compress",
    )(rows, rows, rows, rows, pos, w1, w2k, w2vt)


def _cmp_kernel(qt_ref, kcb_ref, vcbt_ref, gate_ref, ovt_ref, o_ref, sel_ref):
    qt = pl.program_id(1)
    n_cmp_pad = kcb_ref.shape[1]
    n_slc = LANES // B_KV_GROUPS
    chan = lax.broadcasted_iota(jnp.int32, (HB_W, 1), 0)
    slot_of_chan = _div_pow2(chan, HEAD_DIM)
    t = qt * TQ + lax.broadcasted_iota(jnp.int32, (1, TQ), 1)
    cmp_end = (lax.broadcasted_iota(jnp.int32, (n_cmp_pad, 1), 0) * CMP_STRIDE + CMP_LEN - 1)
    ok = cmp_end <= jnp.concatenate([t] * N_HB, axis=1)
    kc = kcb_ref[0].astype(BF16)
    vct = vcbt_ref[0].astype(BF16)
    gsig = _sigmoid(gate_ref[...].astype(F32))
    outs = [[None] * B_KV_GROUPS for _ in range(N_HB)]
    imp = None
    for g in range(B_KV_GROUPS):
        own = jnp.concatenate(
            [qt_ref[r * HB_W + g * HEAD_DIM:r * HB_W + (g + 1) * HEAD_DIM, :].astype(F32)
             for r in range(N_HB)], axis=1) * SCALE
        qm = jnp.concatenate(
            [jnp.zeros((g * HEAD_DIM, N_HB * TQ), F32), own,
             jnp.zeros(((B_KV_GROUPS - 1 - g) * HEAD_DIM, N_HB * TQ), F32)][
                 (1 if g == 0 else 0):(2 if g == B_KV_GROUPS - 1 else 3)], axis=0)
        s = jnp.dot(kc, qm.astype(BF16), preferred_element_type=F32)
        s = jnp.where(ok, s, NEG)
        m = jnp.max(s, axis=0, keepdims=True)
        e = jnp.where(ok, jnp.exp(s - m), 0.0)
        p = e / jnp.maximum(jnp.sum(e, axis=0, keepdims=True), TINY)
        o = jnp.dot(vct[g * HEAD_DIM:(g + 1) * HEAD_DIM], p.astype(BF16),
                    preferred_element_type=F32)
        p_group = None
        for r in range(N_HB):
            p_r = p[:, r * TQ:(r + 1) * TQ]
            p_group = p_r if p_group is None else p_group + p_r
            row = r * GATE_ROWS + g
            outs[r][g] = o[:, r * TQ:(r + 1) * TQ] * gsig[row:row + 1, :]
        part = jnp.dot(ovt_ref[g], p_group, precision=lax.Precision.HIGHEST,
                       preferred_element_type=F32)
        imp = part if imp is None else imp + part
    for r in range(N_HB):
        o_ref[0, :, r * HB_W:(r + 1) * HB_W] = (
            jnp.concatenate(outs[r], axis=0).T.astype(o_ref.dtype))
    imp = imp.reshape(B_KV_GROUPS, n_slc, TQ)
    j = lax.broadcasted_iota(jnp.int32, (1, n_slc, 1), 1)
    cur = _div_pow2(t, SLC_BLK).reshape(1, 1, TQ)
    val = jnp.where(j == cur - 1, -NEG, imp)
    val = jnp.where(j == cur, -NEG, val)
    val = jnp.where(j == 0, -NEG, val)
    val = jnp.where(j > cur, NEG, val)
    chosen = jnp.where(_rank_rows(val) < SLC_TOPK, 1.0, 0.0)
    sel_ref[0] = chosen.reshape(LANES, TQ)


def _cmp_attention(cols, kcb, vcbt, gate_rblk, ovt, bsz, seq):
    n_cmp_pad = kcb.shape[1]
    n_q = seq // TQ
    return pl.pallas_call(
        _cmp_kernel,
        grid=(bsz, n_q),
        in_specs=[
            pl.BlockSpec((D_MODEL, TQ), lambda b, qt: (0, b * n_q + qt)),
            pl.BlockSpec((1, n_cmp_pad, HB_W), lambda b, qt: (b, 0, 0)),
            pl.BlockSpec((1, HB_W, n_cmp_pad), lambda b, qt: (b, 0, 0)),
            pl.BlockSpec((N_HB * GATE_ROWS, TQ), lambda b, qt: (gate_rblk, b * n_q + qt)),
            _resident(ovt.shape),
        ],
        out_specs=[
            pl.BlockSpec((1, TQ, D_MODEL), lambda b, qt: (b, qt, 0)),
            pl.BlockSpec((1, LANES, TQ), lambda b, qt: (b, 0, qt)),
        ],
        out_shape=[
            jax.ShapeDtypeStruct((bsz, seq, D_MODEL), BF16),
            jax.ShapeDtypeStruct((bsz, LANES, seq), F32),
        ],
        compiler_params=_params(2),
        name="nsa_cmp_select",
    )(cols, kcb, vcbt, cols, ovt)


def _mixer_a(xf, bsz, seq, g_pre, w_in, w_out, sinks, rel_bias):
    qw, kvw = N_HEADS * HEAD_DIM, A_KV_HEADS * HEAD_DIM
    w_q = _interleave_heads(w_in[:, :qw].reshape(D_MODEL, N_HEADS, HEAD_DIM), A_KV_HEADS, 1)
    w_cols = jnp.concatenate([w_q.reshape(D_MODEL, qw), w_in[:, qw + kvw:]], axis=1)
    rows, cols = _proj(xf, g_pre, w_in[:, qw:qw + kvw], w_cols)
    rows = rows.reshape(bsz, seq, kvw)
    band_rows = TQ + -(-(A_WINDOW - 1) // LANES) * LANES
    assert band_rows <= seq
    bias = _band_bias(rel_bias, A_KV_HEADS, band_rows, A_WINDOW)
    sink_rows = _interleave_heads(sinks.astype(F32), A_KV_HEADS, 0).reshape(N_HB, 1, SLOTS)
    sink_rows = jnp.pad(sink_rows, ((0, 0), (0, 0), (0, LANES - SLOTS)))
    cfg = _AttnCfg(kv_width=kvw, band_rows=band_rows, sink=True, blocks_per_step=N_HB)
    o = _attention(cfg, rows, cols, bias, k_blk=lambda hb: 0, v_rblk=lambda hb: qw // kvw,
                   sinks=sink_rows)
    w_o = _interleave_heads(w_out.reshape(N_HEADS, HEAD_DIM, D_MODEL), A_KV_HEADS, 0)
    return [o.reshape(bsz * seq, -1)], w_o.reshape(qw, D_MODEL)


def _nsa_overlap_t(seq):
    n_chunk = seq // CMP_STRIDE
    n_cmp = (seq - CMP_LEN) // CMP_STRIDE + 1
    n_slc = seq // SLC_BLK
    c_start = np.arange(n_cmp)[:, None] * CMP_STRIDE
    s_start = np.arange(n_slc)[None, :] * SLC_BLK
    overlap = ((c_start < s_start + SLC_BLK) & (c_start + CMP_LEN > s_start)).astype(np.float32)
    ovt = np.zeros((B_KV_GROUPS, LANES, n_chunk), np.float32)
    for g in range(B_KV_GROUPS):
        ovt[g, g * n_slc:(g + 1) * n_slc, :n_cmp] = overlap.T
    return jnp.asarray(ovt)


def _mixer_b(xf, bsz, seq, g_pre, w_in, w_out, cmp_pos, cmp_w1, cmp_w2, bias_nat, rel_bias):
    assert seq // SLC_BLK * B_KV_GROUPS == LANES and seq // CMP_STRIDE == LANES
    grp, dh = B_KV_GROUPS, HEAD_DIM
    qw, kvw = N_HEADS * dh, B_KV_GROUPS * dh
    w_q = _interleave_heads(w_in[:, :qw].reshape(D_MODEL, N_HEADS, dh), grp, 1)
    kv = [w_in[:, qw + i * kvw:qw + (i + 1) * kvw] for i in range(6)]
    w_g = w_in[:, qw + 6 * kvw:].reshape(D_MODEL, grp, N_HB, 3)
    w_g = w_g.transpose(0, 2, 3, 1).reshape(D_MODEL, N_HB, 3 * grp)
    w_g = jnp.pad(w_g, ((0, 0), (0, 0), (0, GATE_ROWS - 3 * grp)))
    w_rows = jnp.concatenate([kv[0], kv[1], kv[2], kv[4]], axis=1)
    w_cols = jnp.concatenate([w_q.reshape(D_MODEL, qw), kv[3], kv[5],
                              w_g.reshape(D_MODEL, N_HB * GATE_ROWS)], axis=1)
    rows, cols = _proj(xf, g_pre, w_rows, w_cols)
    rows = rows.reshape(bsz, seq, 4 * kvw)
    gate_row0 = qw + 2 * kvw

    half = CMP_LEN // 2
    pos = jnp.tile(cmp_pos.astype(F32), (1, 1, grp))
    w1 = cmp_w1.reshape(2, 2, half, 1, dh, CMP_HIDDEN)
    w1 = jnp.broadcast_to(w1, (2, 2, half, grp, dh, CMP_HIDDEN))
    w1 = w1.reshape(2, 2, half * HB_W, CMP_HIDDEN).astype(BF16)
    w2 = jnp.zeros((2, grp, CMP_HIDDEN, grp, dh), F32)
    for g in range(grp):
        w2 = w2.at[:, g, :, g, :].set(cmp_w2)
    w2 = w2.reshape(2, grp, CMP_HIDDEN, HB_W).astype(BF16)
    kcb, vcbt = _compress(rows, 0, 1, pos, w1, w2[0], jnp.swapaxes(w2[1], 1, 2))

    o_cmp, sel = _cmp_attention(cols, kcb, vcbt, gate_row0 // (N_HB * GATE_ROWS),
                                _nsa_overlap_t(seq), bsz, seq)

    gate_rblk0 = gate_row0 // GATE_ROWS
    bias = _block_bias(bias_nat, grp, seq // TK)
    cfg_slc = _AttnCfg(kv_width=kvw, sel_blk=SLC_BLK, sel_from_input=True, gate_branch=1,
                       blocks_per_step=2)
    o_slc = _attention(cfg_slc, rows, cols, bias, k_blk=lambda hb: 2,
                       v_rblk=lambda hb: qw // kvw, sel=sel, gate_rblk0=gate_rblk0)
    band_rows = TQ + -(-(B_WINDOW - 1) // LANES) * LANES
    assert band_rows <= seq
    cfg_win = _AttnCfg(kv_width=kvw, band_rows=band_rows, gate_branch=2, blocks_per_step=2)
    bias_win = _band_bias(rel_bias, grp, band_rows, B_WINDOW)
    o_win = _attention(cfg_win, rows, cols, bias_win, k_blk=lambda hb: 3,
                       v_rblk=lambda hb: qw // kvw + 1, gate_rblk0=gate_rblk0)
    w_o = _interleave_heads(w_out.reshape(N_HEADS, dh, D_MODEL), grp, 0).reshape(qw, D_MODEL)
    n = bsz * seq
    return [o_cmp.reshape(n, -1), o_slc.reshape(n, -1), o_win.reshape(n, -1)], w_o


def _mixer_c(xf, bsz, seq, g_pre, w_in, w_out, bias_nat):
    assert seq % MOBA_BLK == 0 and MOBA_BLK == TK
    qw = N_HEADS * HEAD_DIM
    w_cols = jnp.concatenate([w_in[:, :qw], w_in[:, 2 * qw:]], axis=1)
    rows, cols = _proj(xf, g_pre, w_in[:, qw:2 * qw], w_cols)
    rows = rows.reshape(bsz, seq, qw)
    bias = _block_bias(bias_nat, 1, seq // TK)
    cfg = _AttnCfg(kv_width=HB_W, sel_blk=MOBA_BLK)
    o = _attention(cfg, rows, cols, bias, k_blk=lambda hb: hb, v_rblk=lambda hb: N_HB + hb)
    return [o.reshape(bsz * seq, -1)], w_out


def kernel(x, rel_bias, norm_g, ffn_w_in, ffn_w_out, a_w_in, a_w_out, a_sinks, b_w_in, b_w_out,
           b_cmp_pos, b_cmp_w1, b_cmp_w2, c_w_in, c_w_out):
    bsz, seq, d = x.shape
    assert d == D_MODEL and seq % TQ == 0 and (bsz * seq) % TM == 0
    depth = norm_g.shape[0]
    bias_nat = _bias_tiles(rel_bias, seq // TK)
    xf = x.reshape(bsz * seq, d)
    for i in range(depth):
        g = norm_g[i]
        xf = _ffn(xf, g[0], ffn_w_in[i, 0], ffn_w_out[i, 0], g[1])
        kind, j = i % N_MIXERS, i // N_MIXERS
        if kind == 0:
            o_list, w_mix = _mixer_a(xf, bsz, seq, g[2], a_w_in[j], a_w_out[j], a_sinks[j],
                                     rel_bias)
        elif kind == 1:
            o_list, w_mix = _mixer_b(xf, bsz, seq, g[2], b_w_in[j], b_w_out[j], b_cmp_pos[j],
                                     b_cmp_w1[j], b_cmp_w2[j], bias_nat, rel_bias)
        else:
            o_list, w_mix = _mixer_c(xf, bsz, seq, g[2], c_w_in[j], c_w_out[j], bias_nat)
        xf = _ffn(xf, g[4], ffn_w_in[i, 1], ffn_w_out[i, 1], g[5], mix=(o_list, w_mix, g[3]))
    return xf.reshape(bsz, seq, d)
```

```python
import dataclasses
import functools
import math

import numpy as np
import jax
import jax.numpy as jnp
from jax import lax
from jax.experimental import pallas as pl
from jax.experimental.pallas import tpu as pltpu

F32 = jnp.float32
BF16 = jnp.bfloat16

D_MODEL = 1024
HEAD_DIM = 64
N_HEADS = D_MODEL // HEAD_DIM
D_FF = 2816
NORM_EPS = 1e-6
N_MIXERS = 3
REL_BUCKETS = 32
REL_MAX_DIST = 1024
A_KV_HEADS = 2
A_WINDOW = 128
B_KV_GROUPS = 4
CMP_LEN = 32
CMP_STRIDE = 16
CMP_HIDDEN = 256
SLC_BLK = 64
SLC_TOPK = 16
B_WINDOW = 512
MOBA_BLK = 256
MOBA_TOPK = 3
NEG = -1e30
TINY = 1e-30

LANES = 128
SUBLANES = 8
MXU_DIM = 256
VMEM_LIMIT_BYTES = 56 * 1024 * 1024

SLOTS = 4
HB_W = SLOTS * HEAD_DIM
N_HB = N_HEADS // SLOTS
TQ = 256
TK = 256
TM = 512
FF_CHUNKS = ((0, 1536), (1536, D_FF))
GATE_ROWS = 16
SEL_BIG = float(2 ** 30)
SCALE = HEAD_DIM ** -0.5
LOG2E = math.log2(math.e)

_NT = (((1,), (1,)), ((), ()))


def _params(n_axes):
    return pltpu.CompilerParams(
        dimension_semantics=("arbitrary",) * n_axes,
        vmem_limit_bytes=VMEM_LIMIT_BYTES)


def _rms(x, g):
    return x * lax.rsqrt(jnp.mean(x * x, axis=-1, keepdims=True) + NORM_EPS) * g


def _sigmoid(x):
    return 1.0 / (1.0 + jnp.exp(-x))


def _div_pow2(x, d):
    assert d & (d - 1) == 0
    return jnp.right_shift(x, d.bit_length() - 1)


def _resident(shape):
    zeros = (0,) * len(shape)
    return pl.BlockSpec(shape, lambda *_: zeros, pipeline_mode=pl.Buffered(1))


def _ffn_kernel(*refs, n_mix):
    mix_refs, refs = refs[:n_mix], refs[n_mix:]
    if n_mix:
        wmix_ref, gmix_ref = refs[:2]
        refs = refs[2:]
    x_ref, gpre_ref, win_ref, wout_ref, gpost_ref, o_ref = refs
    x = x_ref[...]
    if n_mix:
        o = mix_refs[0][...]
        for r in mix_refs[1:]:
            o = o.astype(F32) + r[...].astype(F32)
        mixed = jnp.dot(o.astype(BF16), wmix_ref[...], preferred_element_type=F32)
        x = x + _rms(mixed, gmix_ref[...])
    hn = _rms(x, gpre_ref[...]).astype(BF16)
    y = None
    for lo, hi in FF_CHUNKS:
        gate = jnp.dot(hn, win_ref[:, lo:hi], preferred_element_type=F32)
        up = jnp.dot(hn, win_ref[:, D_FF + lo:D_FF + hi], preferred_element_type=F32)
        act = (gate * _sigmoid(gate) * up).astype(BF16)
        part = jnp.dot(act, wout_ref[lo:hi, :], preferred_element_type=F32)
        y = part if y is None else y + part
    o_ref[...] = x + 0.5 * _rms(y, gpost_ref[...])


def _ffn(xf, g_pre, w_in, w_out, g_post, mix=None):
    n, d = xf.shape
    tile = pl.BlockSpec((TM, d), lambda i: (i, 0))
    mix_specs, mix_args = [], []
    if mix is not None:
        o_list, w_mix, g_mix = mix
        mix_specs = [tile] * len(o_list) + [_resident(w_mix.shape), _resident((1, d))]
        mix_args = list(o_list) + [w_mix.astype(BF16), g_mix.reshape(1, d)]
    return pl.pallas_call(
        functools.partial(_ffn_kernel, n_mix=len(mix_args) - 2 if mix_args else 0),
        grid=(n // TM,),
        in_specs=mix_specs + [
            tile,
            _resident((1, d)),
            _resident(w_in.shape),
            _resident(w_out.shape),
            _resident((1, d)),
        ],
        out_specs=tile,
        out_shape=jax.ShapeDtypeStruct((n, d), F32),
        compiler_params=_params(1),
        name="ffn",
    )(*mix_args, xf, g_pre.reshape(1, d), w_in.astype(BF16), w_out.astype(BF16),
      g_post.reshape(1, d))


def _proj_kernel(x_ref, g_ref, wr_ref, wct_ref, rows_ref, cols_ref):
    hn = _rms(x_ref[...], g_ref[...]).astype(BF16)
    rows_ref[...] = jnp.dot(hn, wr_ref[...], preferred_element_type=F32)
    cols_ref[...] = lax.dot_general(wct_ref[...], hn, _NT,
                                    preferred_element_type=F32).astype(cols_ref.dtype)


def _proj(xf, g, w_rows, w_cols):
    n, d = xf.shape
    cr, cc = w_rows.shape[1], w_cols.shape[1]
    return pl.pallas_call(
        _proj_kernel,
        grid=(n // TM,),
        in_specs=[
            pl.BlockSpec((TM, d), lambda i: (i, 0)),
            _resident((1, d)),
            _resident((d, cr)),
            _resident((cc, d)),
        ],
        out_specs=[pl.BlockSpec((TM, cr), lambda i: (i, 0)),
                   pl.BlockSpec((cc, TM), lambda i: (0, i))],
        out_shape=[jax.ShapeDtypeStruct((n, cr), F32), jax.ShapeDtypeStruct((cc, n), BF16)],
        compiler_params=_params(1),
        name="proj",
    )(xf, g.reshape(1, d), w_rows.astype(BF16), w_cols.T.astype(BF16))


def _rel_bucket(dist):
    dist = jnp.maximum(dist, 0)
    exact = REL_BUCKETS // 2
    d = jnp.maximum(dist, 1).astype(jnp.float32)
    log_b = exact + (jnp.log(d / exact) / math.log(REL_MAX_DIST / exact)
                     * (REL_BUCKETS - exact)).astype(jnp.int32)
    return jnp.where(dist < exact, dist, jnp.minimum(log_b, REL_BUCKETS - 1))


def _tile_distance(n_delta, xp):
    delta = xp.arange(n_delta)[:, None, None]
    return delta * TK + xp.arange(TQ)[None, None, :] - xp.arange(TK)[None, :, None]


def _bias_of_distance(rel_bias, dist):
    bucket = _rel_bucket(dist).reshape(1, -1)
    onehot = (jnp.arange(REL_BUCKETS)[:, None] == bucket).astype(F32)
    tiles = jnp.dot(rel_bias.astype(F32).T, onehot, precision=lax.Precision.HIGHEST)
    return tiles.reshape((N_HEADS,) + dist.shape)


def _bias_tiles(rel_bias, n_delta):
    return _bias_of_distance(rel_bias, _tile_distance(n_delta, jnp))


def _band_distance(band_rows, xp):
    back = band_rows - TQ
    return back + xp.arange(TQ)[None, :] - xp.arange(band_rows + back)[:, None]


def _band_bias(rel_bias, n_groups, band_rows, window):
    bias = _bias_of_distance(rel_bias, _band_distance(band_rows, jnp))
    bias = _interleave_heads(bias, n_groups, 0)
    dist = _band_distance(band_rows, np)
    bias = jnp.where((dist >= 0) & (dist < window), bias * LOG2E, NEG)
    return bias.reshape((N_HB, SLOTS) + bias.shape[1:])


def _interleave_heads(w, n_groups, axis):
    r = N_HEADS // n_groups
    shape = w.shape
    w = w.reshape(shape[:axis] + (n_groups, r) + shape[axis + 1:])
    w = jnp.swapaxes(w, axis, axis + 1)
    return w.reshape(shape)


def _block_bias(bias_nat, n_groups, n_delta, window=None):
    dist = _tile_distance(n_delta, np)
    visible = dist >= 0
    if window is not None:
        visible &= dist < window
    bias = bias_nat[:, :n_delta]
    if n_groups > 1:
        bias = _interleave_heads(bias, n_groups, 0)
    bias = jnp.where(visible, bias * LOG2E, NEG)
    return bias.reshape((N_HB, SLOTS) + bias.shape[1:])


@dataclasses.dataclass(frozen=True)
class _AttnCfg:
    kv_width: int
    n_back: int | None = None
    band_rows: int | None = None
    blocks_per_step: int = 1
    kv_per_block: bool = False
    sel_blk: int | None = None
    sel_from_input: bool = False
    gate_branch: int | None = None
    sink: bool = False

    @property
    def selecting(self):
        return self.sel_blk is not None

    @property
    def moba(self):
        return self.selecting and not self.sel_from_input


def _rank_rows(val):
    width = val.shape[1]
    row = lax.broadcasted_iota(jnp.int32, (1, width, 1), 1)
    rank = jnp.zeros(val.shape, F32)
    for m in range(width):
        vm = val[:, m:m + 1, :]
        rank = rank + jnp.where(row > m, jnp.where(vm >= val, 1.0, 0.0),
                                jnp.where(vm > val, 1.0, 0.0))
    return rank


def _u_chan0(hh, n_blk):
    return hh * n_blk + (LANES if hh < SLOTS // 2 else 0)


def _fold_rows(x, op):
    return op(x.reshape(x.shape[0] // SUBLANES, SUBLANES, x.shape[1]), axis=0)


def _attn_kernel(*refs, cfg: _AttnCfg):
    it = iter(refs)
    qt_ref, k_ref, vt_ref, bias_ref = next(it), next(it), next(it), next(it)
    sel_ref = next(it) if cfg.sel_from_input else None
    gate_ref = next(it) if cfg.gate_branch is not None else None
    sink_ref = next(it) if cfg.sink else None
    o_ref = next(it)
    kb_ref, vtb_ref = next(it), next(it)
    scratch_refs = ((next(it), next(it), next(it), next(it))
                    if cfg.band_rows is None else ())
    kmx_ref = next(it) if cfg.moba else None
    vt_tile = LANES if cfg.band_rows is not None else TK

    qt = pl.program_id(2)
    seq = k_ref.shape[1]
    n_tiles = seq // TK
    lane = lax.broadcasted_iota(jnp.int32, (1, HB_W), 1)
    slot_of_lane = _div_pow2(lane, HEAD_DIM)
    n_sel_blk = seq // cfg.sel_blk if cfg.selecting else 0

    n_kv = kb_ref.shape[0]
    kw = k_ref.shape[2] // n_kv

    @pl.when(qt == 0)
    def _prepare():
        if cfg.moba:
            kmx_ref[...] = jnp.zeros(kmx_ref.shape, F32)
        for kv, c in [(kv, c) for kv in range(n_kv) for c in range(n_tiles)]:
            rows = pl.ds(c * TK, TK)
            k = k_ref[0, rows, kv * kw:(kv + 1) * kw]
            vt = vt_ref[kv * kw:(kv + 1) * kw, c * TK:(c + 1) * TK]
            if kw == LANES:
                k = jnp.concatenate([k, k], axis=1)
                vt = jnp.concatenate([vt, vt], axis=0)
            for i in range(TK // vt_tile):
                vtb_ref[kv, c * (TK // vt_tile) + i] = (
                    vt[:, i * vt_tile:(i + 1) * vt_tile].astype(BF16))
            if cfg.selecting:
                key_blk = _div_pow2(c * TK + lax.broadcasted_iota(jnp.int32, (TK, 1), 0),
                                    cfg.sel_blk)
                for hh in range(SLOTS):
                    onehot = (lane - _u_chan0(hh, n_sel_blk)) == key_blk
                    kb_ref[kv, hh, rows, :] = jnp.where(
                        slot_of_lane == hh, k, jnp.where(onehot, 1.0, 0.0)).astype(BF16)
            else:
                kb_ref[kv, 0, rows, :] = k.astype(BF16)
            if cfg.moba:
                k_mean = jnp.mean(k, axis=0, keepdims=True)
                for hh in range(SLOTS):
                    kmx_ref[kv, pl.ds(hh * n_sel_blk + c, 1), :] = jnp.where(
                        slot_of_lane == hh, k_mean, 0.0)

    for blk in range(qt_ref.shape[0] // HB_W):
        kv = blk if n_kv > 1 else 0
        refs_of_block = (qt_ref, bias_ref, sel_ref, gate_ref, sink_ref, o_ref, kb_ref.at[kv],
                         vtb_ref.at[kv], scratch_refs,
                         kmx_ref.at[kv] if cfg.moba else None)
        _attend_block(cfg, blk, qt, n_sel_blk, refs_of_block)


def _attend_block(cfg, blk, qt, n_sel_blk, refs):
    (qt_ref, bias_ref, sel_ref, gate_ref, sink_ref, o_ref, kb_ref, vtb_ref, scratch_refs,
     kmx_ref) = refs
    bias_blk = bias_ref.at[blk]
    q_t = qt_ref[blk * HB_W:(blk + 1) * HB_W, :].astype(F32)

    sel_rows = None
    if cfg.sel_from_input:
        sel_rows = sel_ref[0]
    elif cfg.moba:
        gate = jnp.dot(kmx_ref[...].astype(BF16), q_t.astype(BF16),
                       preferred_element_type=F32)
        gate = gate.reshape(SLOTS, n_sel_blk, TQ)
        key_blk = lax.broadcasted_iota(jnp.int32, (1, n_sel_blk, 1), 1)
        val = jnp.where(key_blk < qt, gate, NEG)
        rank = _rank_rows(val)
        chosen = jnp.where(key_blk < qt, jnp.where(rank < MOBA_TOPK, 1.0, 0.0), 0.0)
        chosen = jnp.where(key_blk == qt, 1.0, chosen)
        sel_rows = chosen.reshape(SLOTS * n_sel_blk, TQ)

    q_scaled = q_t * (SCALE * LOG2E)
    qms = []
    for hh in range(SLOTS):
        pieces = {hh * HEAD_DIM: q_scaled[hh * HEAD_DIM:(hh + 1) * HEAD_DIM]}
        if cfg.selecting:
            pieces[_u_chan0(hh, n_sel_blk)] = (
                sel_rows[hh * n_sel_blk:(hh + 1) * n_sel_blk] - 1.0) * SEL_BIG
        segs, row = [], 0
        for start in sorted(pieces):
            if start > row:
                segs.append(jnp.zeros((start - row, TQ), F32))
            segs.append(pieces[start])
            row = start + pieces[start].shape[0]
        if row < HB_W:
            segs.append(jnp.zeros((HB_W - row, TQ), F32))
        qms.append(jnp.concatenate(segs, axis=0).astype(BF16))
    q_stack = None if cfg.selecting else jnp.concatenate(qms, axis=1)

    sinks = ([sink_ref[blk, :, hh:hh + 1] * LOG2E for hh in range(SLOTS)]
             if cfg.sink else None)
    if cfg.band_rows is not None:
        ms, sums, accs = _band_softmax(cfg, qt, q_stack, kb_ref, vtb_ref, bias_blk, sinks)
    else:
        ms, sums, accs = _tiled_softmax(cfg, qt, qms, q_stack, kb_ref, vtb_ref, bias_blk,
                                        sinks, *scratch_refs)

    gsig = None
    if cfg.gate_branch is not None:
        gsig = _sigmoid(
            gate_ref[blk * GATE_ROWS:(blk + 1) * GATE_ROWS, :].astype(F32))
    outs = []
    for hh in range(SLOTS):
        denom = jnp.sum(sums[hh], axis=0, keepdims=True)
        if cfg.sink:
            denom = denom + jnp.exp2(sinks[hh] - ms[hh])
        o_h = accs[hh] / jnp.maximum(denom, TINY)
        if cfg.gate_branch is not None:
            row = cfg.gate_branch * SLOTS + hh
            o_h = o_h * gsig[row:row + 1, :]
        outs.append(o_h)
    o_ref[0, :, blk * HB_W:(blk + 1) * HB_W] = (
        jnp.concatenate(outs, axis=0).T.astype(o_ref.dtype))


def _band_softmax(cfg, qt, q_stack, kb_ref, vtb_ref, bias_ref, sinks):
    band = cfg.band_rows
    back = band - TQ
    start = pl.multiple_of(jnp.maximum(qt * TQ - back, 0), LANES)
    s_all = jnp.dot(kb_ref[0, pl.ds(start, band), :], q_stack, preferred_element_type=F32)
    bias_rows = pl.ds(pl.multiple_of(back - (qt * TQ - start), LANES), band)
    tile0 = _div_pow2(start, LANES)
    ms, sums, accs = [], [], []
    for hh in range(SLOTS):
        head = pl.ds(hh * HEAD_DIM, HEAD_DIM)
        sc = s_all[:, hh * TQ:(hh + 1) * TQ] + bias_ref[hh, bias_rows, :]
        m = jnp.max(_fold_rows(sc, jnp.max), axis=0, keepdims=True)
        if cfg.sink:
            m = jnp.maximum(m, sinks[hh])
        p = jnp.exp2(sc - m)
        vt = jnp.concatenate([vtb_ref[tile0 + c, head, :] for c in range(band // LANES)],
                             axis=1)
        ms.append(m)
        sums.append(_fold_rows(p, jnp.sum))
        accs.append(jnp.dot(vt, p.astype(BF16), preferred_element_type=F32))
    return ms, sums, accs


def _tiled_softmax(cfg, qt, qms, q_stack, kb_ref, vtb_ref, bias_ref, sinks,
                   sc_ref, m_ref, l_ref, acc_ref):
    lo = jnp.maximum(qt - cfg.n_back, 0) if cfg.n_back is not None else 0
    m_ref[...] = jnp.full(m_ref.shape, NEG, F32)
    l_ref[...] = jnp.zeros(l_ref.shape, F32)
    acc_ref[...] = jnp.zeros(acc_ref.shape, F32)

    def pass1(kt, width):
        rows = pl.ds(pl.multiple_of(kt * TK, TK), width * TK)
        if cfg.selecting:
            s = [jnp.dot(kb_ref[hh, rows, :], qms[hh], preferred_element_type=F32)
                 for hh in range(SLOTS)]
        else:
            s_all = jnp.dot(kb_ref[0, rows, :], q_stack, preferred_element_type=F32)
            s = [s_all[:, hh * TQ:(hh + 1) * TQ] for hh in range(SLOTS)]
        for hh in range(SLOTS):
            m_new = m_ref[hh]
            for w in range(width):
                tile = s[hh][w * TK:(w + 1) * TK] + bias_ref[hh, qt - kt - w]
                sc_ref[hh, kt - lo + w] = tile
                m_new = jnp.maximum(m_new, _fold_rows(tile, jnp.max))
            m_ref[hh] = m_new

    def pass2(kt, width, ms):
        for hh in range(SLOTS):
            head = pl.ds(hh * HEAD_DIM, HEAD_DIM)
            l_new = l_ref[hh]
            o_new = acc_ref[head, :]
            for w in range(width):
                p = jnp.exp2(sc_ref[hh, kt - lo + w] - ms[hh])
                l_new = l_new + _fold_rows(p, jnp.sum)
                o_new = o_new + jnp.dot(vtb_ref[kt + w, head, :], p.astype(BF16),
                                        preferred_element_type=F32)
            l_ref[hh] = l_new
            acc_ref[head, :] = o_new

    def over_tiles(body):
        n = qt + 1 - lo
        start = lo
        for width in (1, 2):
            has = jnp.bitwise_and(n, width)

            @pl.when(has != 0)
            def _part(start=start, width=width):
                body(start, width)

            start = start + has

        def group(i, carry):
            body(start + 4 * i, 4)
            return carry

        lax.fori_loop(0, jnp.right_shift(n, 2), group, 0)

    over_tiles(pass1)
    ms = []
    for hh in range(SLOTS):
        m = jnp.max(m_ref[hh], axis=0, keepdims=True)
        ms.append(jnp.maximum(m, sinks[hh]) if cfg.sink else m)
    over_tiles(functools.partial(pass2, ms=ms))
    return (ms, [l_ref[hh] for hh in range(SLOTS)],
            [acc_ref[hh * HEAD_DIM:(hh + 1) * HEAD_DIM, :] for hh in range(SLOTS)])


def _attention(cfg, rows, cols, bias, *, k_blk, v_rblk, sel=None, gate_rblk0=None, sinks=None):
    bsz, seq, _ = rows.shape
    n_delta = bias.shape[2]
    n_q = seq // TQ
    nb = cfg.blocks_per_step
    n_kv = nb if cfg.kv_per_block else 1
    kw = n_kv * cfg.kv_width
    bias_tail = (0,) * (bias.ndim - 1)
    in_specs = [
        pl.BlockSpec((nb * HB_W, TQ), lambda hb, b, qt: (hb, b * n_q + qt)),
        pl.BlockSpec((1, seq, kw), lambda hb, b, qt: (b, 0, k_blk(hb))),
        pl.BlockSpec((kw, seq), lambda hb, b, qt: (v_rblk(hb), b)),
        pl.BlockSpec((nb,) + bias.shape[1:], lambda hb, b, qt: (hb,) + bias_tail,
                     pipeline_mode=pl.Buffered(1)),
    ]
    args = [cols, rows, cols, bias]
    if cfg.sel_from_input:
        in_specs.append(pl.BlockSpec((1, LANES, TQ), lambda hb, b, qt: (b, 0, qt)))
        args.append(sel)
    if cfg.gate_branch is not None:
        assert gate_rblk0 % nb == 0
        in_specs.append(pl.BlockSpec(
            (nb * GATE_ROWS, TQ), lambda hb, b, qt: (gate_rblk0 // nb + hb, b * n_q + qt)))
        args.append(cols)
    if cfg.sink:
        in_specs.append(pl.BlockSpec((nb, 1, LANES), lambda hb, b, qt: (hb, 0, 0)))
        args.append(sinks)
    vt_tile = LANES if cfg.band_rows is not None else TK
    scratch = [
        pltpu.VMEM((n_kv, SLOTS if cfg.selecting else 1, seq, HB_W), BF16),
        pltpu.VMEM((n_kv, seq // vt_tile, HB_W, vt_tile), BF16),
    ]
    if cfg.band_rows is None:
        scratch += [
            pltpu.VMEM((SLOTS, n_delta, TK, TQ), F32),
            pltpu.VMEM((SLOTS, SUBLANES, TQ), F32),
            pltpu.VMEM((SLOTS, SUBLANES, TQ), F32),
            pltpu.VMEM((HB_W, TQ), F32),
        ]
    if cfg.moba:
        assert SLOTS * (seq // cfg.sel_blk) <= LANES
        scratch.append(
            pltpu.VMEM((n_kv, SLOTS * (seq // cfg.sel_blk), HB_W), F32))
    return pl.pallas_call(
        functools.partial(_attn_kernel, cfg=cfg),
        grid=(N_HB // nb, bsz, n_q),
        in_specs=in_specs,
        out_specs=pl.BlockSpec((1, TQ, nb * HB_W), lambda hb, b, qt: (b, qt, hb)),
        out_shape=jax.ShapeDtypeStruct((bsz, seq, D_MODEL), BF16),
        scratch_shapes=scratch,
        compiler_params=_params(3),
        name="attention",
    )(*args)


def _gelu_tanh(x):
    cdf = 0.5 * (1.0 + jnp.tanh(math.sqrt(2.0 / math.pi) * (x + 0.044715 * (x * x * x))))
    return x * cdf


def _compress_kernel(kc_lo_ref, kc_hi_ref, vc_lo_ref, vc_hi_ref, pos_ref, w1_ref, w2k_ref,
                     w2vt_ref, kcb_ref, vcbt_ref):
    half = CMP_LEN // 2
    n_chunk = kc_lo_ref.shape[1] // CMP_STRIDE
    flat_lane = lax.broadcasted_iota(jnp.int32, (1, half * HB_W), 1)
    group_of_lane = _div_pow2(jnp.bitwise_and(flat_lane, HB_W - 1), HEAD_DIM)
    for kind, (lo_ref, hi_ref) in enumerate(((kc_lo_ref, kc_hi_ref), (vc_lo_ref, vc_hi_ref))):
        chunks = [jnp.concatenate(
            [lo_ref[0, pl.ds(l, n_chunk, stride=CMP_STRIDE), :],
             hi_ref[0, pl.ds(l, n_chunk, stride=CMP_STRIDE), :]], axis=1) for l in range(half)]
        parts = []
        for h in range(2):
            pieces = [chunks[l] + pos_ref[kind, pl.ds(h * half + l, 1), :] for l in range(half)]
            flat = jnp.concatenate(pieces, axis=1)
            stacked = jnp.concatenate(
                [jnp.where(group_of_lane == g, flat, 0.0)
                 for g in range(B_KV_GROUPS)], axis=0).astype(BF16)
            parts.append(jnp.dot(stacked, w1_ref[kind, h], preferred_element_type=F32))
        rows = B_KV_GROUPS * n_chunk
        hidden = parts[0] + pltpu.roll(parts[1], rows - 1, axis=0)
        act = _gelu_tanh(hidden).astype(BF16)
        out = None
        for g in range(B_KV_GROUPS):
            act_g = act[g * n_chunk:(g + 1) * n_chunk]
            if kind == 0:
                o = jnp.dot(act_g, w2k_ref[g], preferred_element_type=F32)
            else:
                o = lax.dot_general(w2vt_ref[g], act_g, _NT, preferred_element_type=F32)
            out = o if out is None else out + o
        if kind == 0:
            kcb_ref[0] = out
        else:
            vcbt_ref[0] = out


def _compress(rows, kc_blk, vc_blk, pos, w1, w2k, w2vt):
    bsz, seq, _ = rows.shape
    n_chunk = seq // CMP_STRIDE
    return pl.pallas_call(
        _compress_kernel,
        grid=(bsz,),
        in_specs=[
            pl.BlockSpec((1, seq, LANES), lambda b: (b, 0, 2 * kc_blk)),
            pl.BlockSpec((1, seq, LANES), lambda b: (b, 0, 2 * kc_blk + 1)),
            pl.BlockSpec((1, seq, LANES), lambda b: (b, 0, 2 * vc_blk)),
            pl.BlockSpec((1, seq, LANES), lambda b: (b, 0, 2 * vc_blk + 1)),
            _resident(pos.shape),
            _resident(w1.shape),
            _resident(w2k.shape),
            _resident(w2vt.shape),
        ],
        out_specs=[pl.BlockSpec((1, n_chunk, HB_W), lambda b: (b, 0, 0)),
                   pl.BlockSpec((1, HB_W, n_chunk), lambda b: (b, 0, 0))],
        out_shape=[jax.ShapeDtypeStruct((bsz, n_chunk, HB_W), F32),
                   jax.ShapeDtypeStruct((bsz, HB_W, n_chunk), F32)],
        compiler_params=_params(1),
        name="nsa_compress",
    )(rows, rows, rows, rows, pos, w1, w2k, w2vt)


def _cmp_kernel(qt_ref, kcb_ref, vcbt_ref, gate_ref, ovt_ref, o_ref, sel_ref):
    qt = pl.program_id(1)
    n_cmp_pad = kcb_ref.shape[1]
    n_slc = LANES // B_KV_GROUPS
    chan = lax.broadcasted_iota(jnp.int32, (HB_W, 1), 0)
    slot_of_chan = _div_pow2(chan, HEAD_DIM)
    t = qt * TQ + lax.broadcasted_iota(jnp.int32, (1, TQ), 1)
    cmp_end = (lax.broadcasted_iota(jnp.int32, (n_cmp_pad, 1), 0) * CMP_STRIDE + CMP_LEN - 1)
    ok = cmp_end <= jnp.concatenate([t] * N_HB, axis=1)
    kc = kcb_ref[0].astype(BF16)
    vct = vcbt_ref[0].astype(BF16)
    gsig = _sigmoid(gate_ref[...].astype(F32))
    outs = [[None] * B_KV_GROUPS for _ in range(N_HB)]
    imp = None
    for g in range(B_KV_GROUPS):
        own = jnp.concatenate(
            [qt_ref[r * HB_W + g * HEAD_DIM:r * HB_W + (g + 1) * HEAD_DIM, :].astype(F32)
             for r in range(N_HB)], axis=1) * SCALE
        qm = jnp.concatenate(
            [jnp.zeros((g * HEAD_DIM, N_HB * TQ), F32), own,
             jnp.zeros(((B_KV_GROUPS - 1 - g) * HEAD_DIM, N_HB * TQ), F32)][
                 (1 if g == 0 else 0):(2 if g == B_KV_GROUPS - 1 else 3)], axis=0)
        s = jnp.dot(kc, qm.astype(BF16), preferred_element_type=F32)
        s = jnp.where(ok, s, NEG)
        m = jnp.max(s, axis=0, keepdims=True)
        e = jnp.where(ok, jnp.exp(s - m), 0.0)
        p = e / jnp.maximum(jnp.sum(e, axis=0, keepdims=True), TINY)
        o = jnp.dot(vct[g * HEAD_DIM:(g + 1) * HEAD_DIM], p.astype(BF16),
                    preferred_element_type=F32)
        p_group = None
        for r in range(N_HB):
            p_r = p[:, r * TQ:(r + 1) * TQ]
            p_group = p_r if p_group is None else p_group + p_r
            row = r * GATE_ROWS + g
            outs[r][g] = o[:, r * TQ:(r + 1) * TQ] * gsig[row:row + 1, :]
        part = jnp.dot(ovt_ref[g], p_group, precision=lax.Precision.HIGHEST,
                       preferred_element_type=F32)
        imp = part if imp is None else imp + part
    for r in range(N_HB):
        o_ref[0, :, r * HB_W:(r + 1) * HB_W] = (
            jnp.concatenate(outs[r], axis=0).T.astype(o_ref.dtype))
    imp = imp.reshape(B_KV_GROUPS, n_slc, TQ)
    j = lax.broadcasted_iota(jnp.int32, (1, n_slc, 1), 1)
    cur = _div_pow2(t, SLC_BLK).reshape(1, 1, TQ)
    val = jnp.where(j == cur - 1, -NEG, imp)
    val = jnp.where(j == cur, -NEG, val)
    val = jnp.where(j == 0, -NEG, val)
    val = jnp.where(j > cur, NEG, val)
    chosen = jnp.where(_rank_rows(val) < SLC_TOPK, 1.0, 0.0)
    sel_ref[0] = chosen.reshape(LANES, TQ)


def _cmp_attention(cols, kcb, vcbt, gate_rblk, ovt, bsz, seq):
    n_cmp_pad = kcb.shape[1]
    n_q = seq // TQ
    return pl.pallas_call(
        _cmp_kernel,
        grid=(bsz, n_q),
        in_specs=[
            pl.BlockSpec((D_MODEL, TQ), lambda b, qt: (0, b * n_q + qt)),
            pl.BlockSpec((1, n_cmp_pad, HB_W), lambda b, qt: (b, 0, 0)),
            pl.BlockSpec((1, HB_W, n_cmp_pad), lambda b, qt: (b, 0, 0)),
            pl.BlockSpec((N_HB * GATE_ROWS, TQ), lambda b, qt: (gate_rblk, b * n_q + qt)),
            _resident(ovt.shape),
        ],
        out_specs=[
            pl.BlockSpec((1, TQ, D_MODEL), lambda b, qt: (b, qt, 0)),
            pl.BlockSpec((1, LANES, TQ), lambda b, qt: (b, 0, qt)),
        ],
        out_shape=[
            jax.ShapeDtypeStruct((bsz, seq, D_MODEL), BF16),
            jax.ShapeDtypeStruct((bsz, LANES, seq), F32),
        ],
        compiler_params=_params(2),
        name="nsa_cmp_select",
    )(cols, kcb, vcbt, cols, ovt)


def _mixer_a(project, bsz, seq, w_in, w_out, sinks, bias_nat, rel_bias):
    qw, kvw = N_HEADS * HEAD_DIM, A_KV_HEADS * HEAD_DIM
    w_q = _interleave_heads(w_in[:, :qw].reshape(D_MODEL, N_HEADS, HEAD_DIM), A_KV_HEADS, 1)
    w_cols = jnp.concatenate([w_q.reshape(D_MODEL, qw), w_in[:, qw + kvw:]], axis=1)
    rows, cols = project(w_in[:, qw:qw + kvw], w_cols)
    rows = rows.reshape(bsz, seq, kvw)
    band_rows = TQ + -(-(A_WINDOW - 1) // LANES) * LANES
    assert band_rows <= seq
    bias = _band_bias(rel_bias, A_KV_HEADS, band_rows, A_WINDOW)
    sink_rows = _interleave_heads(sinks.astype(F32), A_KV_HEADS, 0).reshape(N_HB, 1, SLOTS)
    sink_rows = jnp.pad(sink_rows, ((0, 0), (0, 0), (0, LANES - SLOTS)))
    cfg = _AttnCfg(kv_width=kvw, band_rows=band_rows, sink=True, blocks_per_step=N_HB)
    o = _attention(cfg, rows, cols, bias, k_blk=lambda hb: 0, v_rblk=lambda hb: qw // kvw,
                   sinks=sink_rows)
    w_o = _interleave_heads(w_out.reshape(N_HEADS, HEAD_DIM, D_MODEL), A_KV_HEADS, 0)
    return [o.reshape(bsz * seq, -1)], w_o.reshape(qw, D_MODEL)


def _nsa_overlap_t(seq):
    n_chunk = seq // CMP_STRIDE
    n_cmp = (seq - CMP_LEN) // CMP_STRIDE + 1
    n_slc = seq // SLC_BLK
    c_start = np.arange(n_cmp)[:, None] * CMP_STRIDE
    s_start = np.arange(n_slc)[None, :] * SLC_BLK
    overlap = ((c_start < s_start + SLC_BLK) & (c_start + CMP_LEN > s_start)).astype(np.float32)
    ovt = np.zeros((B_KV_GROUPS, LANES, n_chunk), np.float32)
    for g in range(B_KV_GROUPS):
        ovt[g, g * n_slc:(g + 1) * n_slc, :n_cmp] = overlap.T
    return jnp.asarray(ovt)


def _mixer_b(project, bsz, seq, w_in, w_out, cmp_pos, cmp_w1, cmp_w2, bias_nat, rel_bias):
    assert seq // SLC_BLK * B_KV_GROUPS == LANES and seq // CMP_STRIDE == LANES
    grp, dh = B_KV_GROUPS, HEAD_DIM
    qw, kvw = N_HEADS * dh, B_KV_GROUPS * dh
    w_q = _interleave_heads(w_in[:, :qw].reshape(D_MODEL, N_HEADS, dh), grp, 1)
    kv = [w_in[:, qw + i * kvw:qw + (i + 1) * kvw] for i in range(6)]
    w_g = w_in[:, qw + 6 * kvw:].reshape(D_MODEL, grp, N_HB, 3)
    w_g = w_g.transpose(0, 2, 3, 1).reshape(D_MODEL, N_HB, 3 * grp)
    w_g = jnp.pad(w_g, ((0, 0), (0, 0), (0, GATE_ROWS - 3 * grp)))
    w_rows = jnp.concatenate([kv[0], kv[1], kv[2], kv[4]], axis=1)
    w_cols = jnp.concatenate([w_q.reshape(D_MODEL, qw), kv[3], kv[5],
                              w_g.reshape(D_MODEL, N_HB * GATE_ROWS)], axis=1)
    rows, cols = project(w_rows, w_cols)
    rows = rows.reshape(bsz, seq, 4 * kvw)
    gate_row0 = qw + 2 * kvw

    half = CMP_LEN // 2
    pos = jnp.tile(cmp_pos.astype(F32), (1, 1, grp))
    w1 = cmp_w1.reshape(2, 2, half, 1, dh, CMP_HIDDEN)
    w1 = jnp.broadcast_to(w1, (2, 2, half, grp, dh, CMP_HIDDEN))
    w1 = w1.reshape(2, 2, half * HB_W, CMP_HIDDEN).astype(BF16)
    w2 = jnp.zeros((2, grp, CMP_HIDDEN, grp, dh), F32)
    for g in range(grp):
        w2 = w2.at[:, g, :, g, :].set(cmp_w2)
    w2 = w2.reshape(2, grp, CMP_HIDDEN, HB_W).astype(BF16)
    kcb, vcbt = _compress(rows, 0, 1, pos, w1, w2[0], jnp.swapaxes(w2[1], 1, 2))

    o_cmp, sel = _cmp_attention(cols, kcb, vcbt, gate_row0 // (N_HB * GATE_ROWS),
                                _nsa_overlap_t(seq), bsz, seq)

    gate_rblk0 = gate_row0 // GATE_ROWS
    bias = _block_bias(bias_nat, grp, seq // TK)
    cfg_slc = _AttnCfg(kv_width=kvw, sel_blk=SLC_BLK, sel_from_input=True, gate_branch=1,
                       blocks_per_step=2)
    o_slc = _attention(cfg_slc, rows, cols, bias, k_blk=lambda hb: 2,
                       v_rblk=lambda hb: qw // kvw, sel=sel, gate_rblk0=gate_rblk0)
    band_rows = TQ + -(-(B_WINDOW - 1) // LANES) * LANES
    assert band_rows <= seq
    cfg_win = _AttnCfg(kv_width=kvw, band_rows=band_rows, gate_branch=2, blocks_per_step=2)
    bias_win = _band_bias(rel_bias, grp, band_rows, B_WINDOW)
    o_win = _attention(cfg_win, rows, cols, bias_win, k_blk=lambda hb: 3,
                       v_rblk=lambda hb: qw // kvw + 1, gate_rblk0=gate_rblk0)
    w_o = _interleave_heads(w_out.reshape(N_HEADS, dh, D_MODEL), grp, 0).reshape(qw, D_MODEL)
    n = bsz * seq
    return [o_cmp.reshape(n, -1), o_slc.reshape(n, -1), o_win.reshape(n, -1)], w_o


def _mixer_c(project, bsz, seq, w_in, w_out, bias_nat, rel_bias):
    assert seq % MOBA_BLK == 0 and MOBA_BLK == TK
    qw = N_HEADS * HEAD_DIM
    w_cols = jnp.concatenate([w_in[:, :qw], w_in[:, 2 * qw:]], axis=1)
    rows, cols = project(w_in[:, qw:2 * qw], w_cols)
    rows = rows.reshape(bsz, seq, qw)
    bias = _block_bias(bias_nat, 1, seq // TK)
    nb = 2
    cfg = _AttnCfg(kv_width=HB_W, sel_blk=MOBA_BLK, blocks_per_step=nb, kv_per_block=True)
    o = _attention(cfg, rows, cols, bias, k_blk=lambda hb: hb,
                   v_rblk=lambda hb: N_HB // nb + hb)
    return [o.reshape(bsz * seq, -1)], w_out


def _layer(xf, bsz, seq, g, w_in, w_out, mixer, mixer_params, bias_nat, rel_bias):
    xf = _ffn(xf, g[0], w_in[0], w_out[0], g[1])
    project = functools.partial(_proj, xf, g[2])
    o_list, w_mix = mixer(project, bsz, seq, *mixer_params, bias_nat, rel_bias)
    return _ffn(xf, g[4], w_in[1], w_out[1], g[5], mix=(o_list, w_mix, g[3]))


def kernel(x, rel_bias, norm_g, ffn_w_in, ffn_w_out, a_w_in, a_w_out, a_sinks, b_w_in, b_w_out,
           b_cmp_pos, b_cmp_w1, b_cmp_w2, c_w_in, c_w_out):
    bsz, seq, d = x.shape
    assert d == D_MODEL and seq % TQ == 0 and (bsz * seq) % TM == 0
    depth = norm_g.shape[0]
    bias_nat = _bias_tiles(rel_bias, seq // TK)
    xf = x.reshape(bsz * seq, d)
    for i in range(depth):
        kind, j = i % N_MIXERS, i // N_MIXERS
        if kind == 0:
            mixer, params = _mixer_a, (a_w_in[j], a_w_out[j], a_sinks[j])
        elif kind == 1:
            mixer, params = _mixer_b, (b_w_in[j], b_w_out[j], b_cmp_pos[j], b_cmp_w1[j],
                                       b_cmp_w2[j])
        else:
            mixer, params = _mixer_c, (c_w_in[j], c_w_out[j])
        xf = _layer(xf, bsz, seq, norm_g[i], ffn_w_in[i], ffn_w_out[i], mixer, params,
                    bias_nat, rel_bias)
    return xf.reshape(bsz, seq, d)
```

```python
import dataclasses
import functools
import math

import numpy as np
import jax
import jax.numpy as jnp
from jax import lax
from jax.experimental import pallas as pl
from jax.experimental.pallas import tpu as pltpu

F32 = jnp.float32
BF16 = jnp.bfloat16

D_MODEL = 1024
HEAD_DIM = 64
N_HEADS = D_MODEL // HEAD_DIM
D_FF = 2816
NORM_EPS = 1e-6
N_MIXERS = 3
REL_BUCKETS = 32
REL_MAX_DIST = 1024
A_KV_HEADS = 2
A_WINDOW = 128
B_KV_GROUPS = 4
CMP_LEN = 32
CMP_STRIDE = 16
CMP_HIDDEN = 256
SLC_BLK = 64
SLC_TOPK = 16
B_WINDOW = 512
MOBA_BLK = 256
MOBA_TOPK = 3
NEG = -1e30
TINY = 1e-30

LANES = 128
SUBLANES = 8
MXU_DIM = 256
VMEM_LIMIT_BYTES = 56 * 1024 * 1024

SLOTS = 4
HB_W = SLOTS * HEAD_DIM
N_HB = N_HEADS // SLOTS
TQ = 256
TK = 256
TM = 512
FF_CHUNKS = ((0, 1536), (1536, D_FF))
GATE_ROWS = 16
SEL_BIG = float(2 ** 30)
SCALE = HEAD_DIM ** -0.5
LOG2E = math.log2(math.e)

_NT = (((1,), (1,)), ((), ()))


def _params(n_axes):
    return pltpu.CompilerParams(
        dimension_semantics=("arbitrary",) * n_axes,
        vmem_limit_bytes=VMEM_LIMIT_BYTES)


def _rms(x, g):
    return x * lax.rsqrt(jnp.mean(x * x, axis=-1, keepdims=True) + NORM_EPS) * g


def _sigmoid(x):
    return 1.0 / (1.0 + jnp.exp(-x))


def _div_pow2(x, d):
    assert d & (d - 1) == 0
    return jnp.right_shift(x, d.bit_length() - 1)


def _resident(shape):
    zeros = (0,) * len(shape)
    return pl.BlockSpec(shape, lambda *_: zeros, pipeline_mode=pl.Buffered(1))


def _ffn_kernel(*refs, n_mix):
    mix_refs, refs = refs[:n_mix], refs[n_mix:]
    if n_mix:
        wmix_ref, gmix_ref = refs[:2]
        refs = refs[2:]
    x_ref, gpre_ref, win_ref, wout_ref, gpost_ref, o_ref = refs
    x = x_ref[...]
    if n_mix:
        o = mix_refs[0][...]
        for r in mix_refs[1:]:
            o = o.astype(F32) + r[...].astype(F32)
        mixed = jnp.dot(o.astype(BF16), wmix_ref[...], preferred_element_type=F32)
        x = x + _rms(mixed, gmix_ref[...])
    hn = _rms(x, gpre_ref[...]).astype(BF16)
    y = None
    for lo, hi in FF_CHUNKS:
        gate = jnp.dot(hn, win_ref[:, lo:hi], preferred_element_type=F32)
        up = jnp.dot(hn, win_ref[:, D_FF + lo:D_FF + hi], preferred_element_type=F32)
        act = (gate * _sigmoid(gate) * up).astype(BF16)
        part = jnp.dot(act, wout_ref[lo:hi, :], preferred_element_type=F32)
        y = part if y is None else y + part
    o_ref[...] = x + 0.5 * _rms(y, gpost_ref[...])


def _ffn(xf, g_pre, w_in_all, w_out_all, which, g_post, mix=None):
    n, d = xf.shape
    tile = pl.BlockSpec((TM, d), lambda i: (i, 0))
    mix_specs, mix_args = [], []
    if mix is not None:
        o_list, w_mix, g_mix = mix
        mix_specs = [tile] * len(o_list) + [_resident(w_mix.shape), _resident((1, d))]
        mix_args = list(o_list) + [w_mix.astype(BF16), g_mix.reshape(1, d)]

    def picked(w_all):
        return pl.BlockSpec((None, None) + w_all.shape[2:], lambda i: which + (0, 0),
                            pipeline_mode=pl.Buffered(1))

    return pl.pallas_call(
        functools.partial(_ffn_kernel, n_mix=len(mix_args) - 2 if mix_args else 0),
        grid=(n // TM,),
        in_specs=mix_specs + [
            tile,
            _resident((1, d)),
            picked(w_in_all),
            picked(w_out_all),
            _resident((1, d)),
        ],
        out_specs=tile,
        out_shape=jax.ShapeDtypeStruct((n, d), F32),
        compiler_params=_params(1),
        name="ffn",
    )(*mix_args, xf, g_pre.reshape(1, d), w_in_all, w_out_all, g_post.reshape(1, d))


def _proj_kernel(x_ref, g_ref, wr_ref, wct_ref, rows_ref, cols_ref):
    hn = _rms(x_ref[...], g_ref[...]).astype(BF16)
    rows_ref[...] = jnp.dot(hn, wr_ref[...], preferred_element_type=F32)
    cols_ref[...] = lax.dot_general(wct_ref[...], hn, _NT,
                                    preferred_element_type=F32).astype(cols_ref.dtype)


def _proj(xf, g, w_rows, w_cols):
    n, d = xf.shape
    cr, cc = w_rows.shape[1], w_cols.shape[1]
    return pl.pallas_call(
        _proj_kernel,
        grid=(n // TM,),
        in_specs=[
            pl.BlockSpec((TM, d), lambda i: (i, 0)),
            _resident((1, d)),
            _resident((d, cr)),
            _resident((cc, d)),
        ],
        out_specs=[pl.BlockSpec((TM, cr), lambda i: (i, 0)),
                   pl.BlockSpec((cc, TM), lambda i: (0, i))],
        out_shape=[jax.ShapeDtypeStruct((n, cr), F32), jax.ShapeDtypeStruct((cc, n), BF16)],
        compiler_params=_params(1),
        name="proj",
    )(xf, g.reshape(1, d), w_rows.astype(BF16), w_cols.T.astype(BF16))


def _rel_bucket(dist):
    dist = jnp.maximum(dist, 0)
    exact = REL_BUCKETS // 2
    d = jnp.maximum(dist, 1).astype(jnp.float32)
    log_b = exact + (jnp.log(d / exact) / math.log(REL_MAX_DIST / exact)
                     * (REL_BUCKETS - exact)).astype(jnp.int32)
    return jnp.where(dist < exact, dist, jnp.minimum(log_b, REL_BUCKETS - 1))


def _tile_distance(n_delta, xp):
    delta = xp.arange(n_delta)[:, None, None]
    return delta * TK + xp.arange(TQ)[None, None, :] - xp.arange(TK)[None, :, None]


def _bias_of_distance(rel_bias, dist):
    bucket = _rel_bucket(dist).reshape(1, -1)
    onehot = (jnp.arange(REL_BUCKETS)[:, None] == bucket).astype(F32)
    tiles = jnp.dot(rel_bias.astype(F32).T, onehot, precision=lax.Precision.HIGHEST)
    return tiles.reshape((N_HEADS,) + dist.shape)


def _bias_tiles(rel_bias, n_delta):
    return _bias_of_distance(rel_bias, _tile_distance(n_delta, jnp))


def _band_distance(band_rows, xp):
    back = band_rows - TQ
    return back + xp.arange(TQ)[None, :] - xp.arange(band_rows + back)[:, None]


def _band_bias(rel_bias, n_groups, band_rows, window):
    bias = _bias_of_distance(rel_bias, _band_distance(band_rows, jnp))
    bias = _interleave_heads(bias, n_groups, 0)
    dist = _band_distance(band_rows, np)
    bias = jnp.where((dist >= 0) & (dist < window), bias * LOG2E, NEG)
    return bias.reshape((N_HB, SLOTS) + bias.shape[1:])


def _interleave_heads(w, n_groups, axis):
    r = N_HEADS // n_groups
    shape = w.shape
    w = w.reshape(shape[:axis] + (n_groups, r) + shape[axis + 1:])
    w = jnp.swapaxes(w, axis, axis + 1)
    return w.reshape(shape)


def _block_bias(bias_nat, n_groups, n_delta, window=None):
    dist = _tile_distance(n_delta, np)
    visible = dist >= 0
    if window is not None:
        visible &= dist < window
    bias = bias_nat[:, :n_delta]
    if n_groups > 1:
        bias = _interleave_heads(bias, n_groups, 0)
    bias = jnp.where(visible, bias * LOG2E, NEG)
    return bias.reshape((N_HB, SLOTS) + bias.shape[1:])


@dataclasses.dataclass(frozen=True)
class _AttnCfg:
    kv_width: int
    n_back: int | None = None
    band_rows: int | None = None
    blocks_per_step: int = 1
    kv_per_block: bool = False
    sel_blk: int | None = None
    sel_from_input: bool = False
    gate_branch: int | None = None
    sink: bool = False

    @property
    def selecting(self):
        return self.sel_blk is not None

    @property
    def moba(self):
        return self.selecting and not self.sel_from_input


def _rank_rows(val):
    width = val.shape[1]
    row = lax.broadcasted_iota(jnp.int32, (1, width, 1), 1)
    rank = jnp.zeros(val.shape, F32)
    for m in range(width):
        vm = val[:, m:m + 1, :]
        rank = rank + jnp.where(row > m, jnp.where(vm >= val, 1.0, 0.0),
                                jnp.where(vm > val, 1.0, 0.0))
    return rank


def _u_chan0(hh, n_blk):
    return hh * n_blk + (LANES if hh < SLOTS // 2 else 0)


def _fold_rows(x, op):
    return op(x.reshape(x.shape[0] // SUBLANES, SUBLANES, x.shape[1]), axis=0)


def _attn_kernel(*refs, cfg: _AttnCfg):
    it = iter(refs)
    qt_ref, k_ref, vt_ref, bias_ref = next(it), next(it), next(it), next(it)
    sel_ref = next(it) if cfg.sel_from_input else None
    gate_ref = next(it) if cfg.gate_branch is not None else None
    sink_ref = next(it) if cfg.sink else None
    o_ref = next(it)
    kb_ref, vtb_ref = next(it), next(it)
    scratch_refs = ((next(it), next(it), next(it), next(it))
                    if cfg.band_rows is None else ())
    kmx_ref = next(it) if cfg.moba else None
    vt_tile = LANES if cfg.band_rows is not None else TK

    qt = pl.program_id(2)
    seq = k_ref.shape[1]
    n_tiles = seq // TK
    lane = lax.broadcasted_iota(jnp.int32, (1, HB_W), 1)
    slot_of_lane = _div_pow2(lane, HEAD_DIM)
    n_sel_blk = seq // cfg.sel_blk if cfg.selecting else 0

    n_kv = kb_ref.shape[0]
    kw = k_ref.shape[2] // n_kv

    @pl.when(qt == 0)
    def _prepare():
        if cfg.moba:
            kmx_ref[...] = jnp.zeros(kmx_ref.shape, F32)
        for kv, c in [(kv, c) for kv in range(n_kv) for c in range(n_tiles)]:
            rows = pl.ds(c * TK, TK)
            k = k_ref[0, rows, kv * kw:(kv + 1) * kw]
            vt = vt_ref[kv * kw:(kv + 1) * kw, c * TK:(c + 1) * TK]
            if kw == LANES:
                k = jnp.concatenate([k, k], axis=1)
                vt = jnp.concatenate([vt, vt], axis=0)
            for i in range(TK // vt_tile):
                vtb_ref[kv, c * (TK // vt_tile) + i] = (
                    vt[:, i * vt_tile:(i + 1) * vt_tile].astype(BF16))
            if cfg.selecting:
                key_blk = _div_pow2(c * TK + lax.broadcasted_iota(jnp.int32, (TK, 1), 0),
                                    cfg.sel_blk)
                for hh in range(SLOTS):
                    onehot = (lane - _u_chan0(hh, n_sel_blk)) == key_blk
                    kb_ref[kv, hh, rows, :] = jnp.where(
                        slot_of_lane == hh, k, jnp.where(onehot, 1.0, 0.0)).astype(BF16)
            else:
                kb_ref[kv, 0, rows, :] = k.astype(BF16)
            if cfg.moba:
                k_mean = jnp.mean(k, axis=0, keepdims=True)
                for hh in range(SLOTS):
                    kmx_ref[kv, pl.ds(hh * n_sel_blk + c, 1), :] = jnp.where(
                        slot_of_lane == hh, k_mean, 0.0)

    for blk in range(qt_ref.shape[0] // HB_W):
        kv = blk if n_kv > 1 else 0
        refs_of_block = (qt_ref, bias_ref, sel_ref, gate_ref, sink_ref, o_ref, kb_ref.at[kv],
                         vtb_ref.at[kv], scratch_refs,
                         kmx_ref.at[kv] if cfg.moba else None)
        _attend_block(cfg, blk, qt, n_sel_blk, refs_of_block)


def _attend_block(cfg, blk, qt, n_sel_blk, refs):
    (qt_ref, bias_ref, sel_ref, gate_ref, sink_ref, o_ref, kb_ref, vtb_ref, scratch_refs,
     kmx_ref) = refs
    bias_blk = bias_ref.at[blk]
    q_t = qt_ref[blk * HB_W:(blk + 1) * HB_W, :].astype(F32)

    sel_rows = None
    if cfg.sel_from_input:
        sel_rows = sel_ref[0]
    elif cfg.moba:
        gate = jnp.dot(kmx_ref[...].astype(BF16), q_t.astype(BF16),
                       preferred_element_type=F32)
        gate = gate.reshape(SLOTS, n_sel_blk, TQ)
        key_blk = lax.broadcasted_iota(jnp.int32, (1, n_sel_blk, 1), 1)
        val = jnp.where(key_blk < qt, gate, NEG)
        rank = _rank_rows(val)
        chosen = jnp.where(key_blk < qt, jnp.where(rank < MOBA_TOPK, 1.0, 0.0), 0.0)
        chosen = jnp.where(key_blk == qt, 1.0, chosen)
        sel_rows = chosen.reshape(SLOTS * n_sel_blk, TQ)

    q_scaled = q_t * (SCALE * LOG2E)
    qms = []
    for hh in range(SLOTS):
        pieces = {hh * HEAD_DIM: q_scaled[hh * HEAD_DIM:(hh + 1) * HEAD_DIM]}
        if cfg.selecting:
            pieces[_u_chan0(hh, n_sel_blk)] = (
                sel_rows[hh * n_sel_blk:(hh + 1) * n_sel_blk] - 1.0) * SEL_BIG
        segs, row = [], 0
        for start in sorted(pieces):
            if start > row:
                segs.append(jnp.zeros((start - row, TQ), F32))
            segs.append(pieces[start])
            row = start + pieces[start].shape[0]
        if row < HB_W:
            segs.append(jnp.zeros((HB_W - row, TQ), F32))
        qms.append(jnp.concatenate(segs, axis=0).astype(BF16))
    q_stack = None if cfg.selecting else jnp.concatenate(qms, axis=1)

    sinks = ([sink_ref[blk, :, hh:hh + 1] * LOG2E for hh in range(SLOTS)]
             if cfg.sink else None)
    if cfg.band_rows is not None:
        ms, sums, accs = _band_softmax(cfg, qt, q_stack, kb_ref, vtb_ref, bias_blk, sinks)
    else:
        ms, sums, accs = _tiled_softmax(cfg, qt, qms, q_stack, kb_ref, vtb_ref, bias_blk,
                                        sinks, *scratch_refs)

    gsig = None
    if cfg.gate_branch is not None:
        gsig = _sigmoid(
            gate_ref[blk * GATE_ROWS:(blk + 1) * GATE_ROWS, :].astype(F32))
    outs = []
    for hh in range(SLOTS):
        denom = jnp.sum(sums[hh], axis=0, keepdims=True)
        if cfg.sink:
            denom = denom + jnp.exp2(sinks[hh] - ms[hh])
        o_h = accs[hh] / jnp.maximum(denom, TINY)
        if cfg.gate_branch is not None:
            row = cfg.gate_branch * SLOTS + hh
            o_h = o_h * gsig[row:row + 1, :]
        outs.append(o_h)
    o_ref[0, :, blk * HB_W:(blk + 1) * HB_W] = (
        jnp.concatenate(outs, axis=0).T.astype(o_ref.dtype))


def _band_softmax(cfg, qt, q_stack, kb_ref, vtb_ref, bias_ref, sinks):
    band = cfg.band_rows
    back = band - TQ
    start = pl.multiple_of(jnp.maximum(qt * TQ - back, 0), LANES)
    s_all = jnp.dot(kb_ref[0, pl.ds(start, band), :], q_stack, preferred_element_type=F32)
    bias_rows = pl.ds(pl.multiple_of(back - (qt * TQ - start), LANES), band)
    tile0 = _div_pow2(start, LANES)
    ms, sums, accs = [], [], []
    for hh in range(SLOTS):
        head = pl.ds(hh * HEAD_DIM, HEAD_DIM)
        sc = s_all[:, hh * TQ:(hh + 1) * TQ] + bias_ref[hh, bias_rows, :]
        m = jnp.max(_fold_rows(sc, jnp.max), axis=0, keepdims=True)
        if cfg.sink:
            m = jnp.maximum(m, sinks[hh])
        p = jnp.exp2(sc - m)
        vt = jnp.concatenate([vtb_ref[tile0 + c, head, :] for c in range(band // LANES)],
                             axis=1)
        ms.append(m)
        sums.append(_fold_rows(p, jnp.sum))
        accs.append(jnp.dot(vt, p.astype(BF16), preferred_element_type=F32))
    return ms, sums, accs


def _tiled_softmax(cfg, qt, qms, q_stack, kb_ref, vtb_ref, bias_ref, sinks,
                   sc_ref, m_ref, l_ref, acc_ref):
    lo = jnp.maximum(qt - cfg.n_back, 0) if cfg.n_back is not None else 0
    m_ref[...] = jnp.full(m_ref.shape, NEG, F32)
    l_ref[...] = jnp.zeros(l_ref.shape, F32)
    acc_ref[...] = jnp.zeros(acc_ref.shape, F32)

    def pass1(kt, width):
        rows = pl.ds(pl.multiple_of(kt * TK, TK), width * TK)
        if cfg.selecting:
            s = [jnp.dot(kb_ref[hh, rows, :], qms[hh], preferred_element_type=F32)
                 for hh in range(SLOTS)]
        else:
            s_all = jnp.dot(kb_ref[0, rows, :], q_stack, preferred_element_type=F32)
            s = [s_all[:, hh * TQ:(hh + 1) * TQ] for hh in range(SLOTS)]
        for hh in range(SLOTS):
            m_new = m_ref[hh]
            for w in range(width):
                tile = s[hh][w * TK:(w + 1) * TK] + bias_ref[hh, qt - kt - w]
                sc_ref[hh, kt - lo + w] = tile
                m_new = jnp.maximum(m_new, _fold_rows(tile, jnp.max))
            m_ref[hh] = m_new

    def pass2(kt, width, ms):
        for hh in range(SLOTS):
            head = pl.ds(hh * HEAD_DIM, HEAD_DIM)
            l_new = l_ref[hh]
            o_new = acc_ref[head, :]
            for w in range(width):
                p = jnp.exp2(sc_ref[hh, kt - lo + w] - ms[hh])
                l_new = l_new + _fold_rows(p, jnp.sum)
                o_new = o_new + jnp.dot(vtb_ref[kt + w, head, :], p.astype(BF16),
                                        preferred_element_type=F32)
            l_ref[hh] = l_new
            acc_ref[head, :] = o_new

    def over_tiles(body):
        n = qt + 1 - lo
        start = lo
        for width in (1, 2):
            has = jnp.bitwise_and(n, width)

            @pl.when(has != 0)
            def _part(start=start, width=width):
                body(start, width)

            start = start + has

        def group(i, carry):
            body(start + 4 * i, 4)
            return carry

        lax.fori_loop(0, jnp.right_shift(n, 2), group, 0)

    over_tiles(pass1)
    ms = []
    for hh in range(SLOTS):
        m = jnp.max(m_ref[hh], axis=0, keepdims=True)
        ms.append(jnp.maximum(m, sinks[hh]) if cfg.sink else m)
    over_tiles(functools.partial(pass2, ms=ms))
    return (ms, [l_ref[hh] for hh in range(SLOTS)],
            [acc_ref[hh * HEAD_DIM:(hh + 1) * HEAD_DIM, :] for hh in range(SLOTS)])


def _attention(cfg, rows, cols, bias, *, k_blk, v_rblk, sel=None, gate_rblk0=None, sinks=None):
    bsz, seq, _ = rows.shape
    n_delta = bias.shape[2]
    n_q = seq // TQ
    nb = cfg.blocks_per_step
    n_kv = nb if cfg.kv_per_block else 1
    kw = n_kv * cfg.kv_width
    bias_tail = (0,) * (bias.ndim - 1)
    in_specs = [
        pl.BlockSpec((nb * HB_W, TQ), lambda hb, b, qt: (hb, b * n_q + qt)),
        pl.BlockSpec((1, seq, kw), lambda hb, b, qt: (b, 0, k_blk(hb))),
        pl.BlockSpec((kw, seq), lambda hb, b, qt: (v_rblk(hb), b)),
        pl.BlockSpec((nb,) + bias.shape[1:], lambda hb, b, qt: (hb,) + bias_tail,
                     pipeline_mode=pl.Buffered(1)),
    ]
    args = [cols, rows, cols, bias]
    if cfg.sel_from_input:
        in_specs.append(pl.BlockSpec((1, LANES, TQ), lambda hb, b, qt: (b, 0, qt)))
        args.append(sel)
    if cfg.gate_branch is not None:
        assert gate_rblk0 % nb == 0
        in_specs.append(pl.BlockSpec(
            (nb * GATE_ROWS, TQ), lambda hb, b, qt: (gate_rblk0 // nb + hb, b * n_q + qt)))
        args.append(cols)
    if cfg.sink:
        in_specs.append(pl.BlockSpec((nb, 1, LANES), lambda hb, b, qt: (hb, 0, 0)))
        args.append(sinks)
    vt_tile = LANES if cfg.band_rows is not None else TK
    scratch = [
        pltpu.VMEM((n_kv, SLOTS if cfg.selecting else 1, seq, HB_W), BF16),
        pltpu.VMEM((n_kv, seq // vt_tile, HB_W, vt_tile), BF16),
    ]
    if cfg.band_rows is None:
        scratch += [
            pltpu.VMEM((SLOTS, n_delta, TK, TQ), F32),
            pltpu.VMEM((SLOTS, SUBLANES, TQ), F32),
            pltpu.VMEM((SLOTS, SUBLANES, TQ), F32),
            pltpu.VMEM((HB_W, TQ), F32),
        ]
    if cfg.moba:
        assert SLOTS * (seq // cfg.sel_blk) <= LANES
        scratch.append(
            pltpu.VMEM((n_kv, SLOTS * (seq // cfg.sel_blk), HB_W), F32))
    return pl.pallas_call(
        functools.partial(_attn_kernel, cfg=cfg),
        grid=(N_HB // nb, bsz, n_q),
        in_specs=in_specs,
        out_specs=pl.BlockSpec((1, TQ, nb * HB_W), lambda hb, b, qt: (b, qt, hb)),
        out_shape=jax.ShapeDtypeStruct((bsz, seq, D_MODEL), BF16),
        scratch_shapes=scratch,
        compiler_params=_params(3),
        name="attention",
    )(*args)


def _gelu_tanh(x):
    cdf = 0.5 * (1.0 + jnp.tanh(math.sqrt(2.0 / math.pi) * (x + 0.044715 * (x * x * x))))
    return x * cdf


def _compress_kernel(kc_lo_ref, kc_hi_ref, vc_lo_ref, vc_hi_ref, pos_ref, w1_ref, w2k_ref,
                     w2vt_ref, kcb_ref, vcbt_ref):
    half = CMP_LEN // 2
    n_chunk = kc_lo_ref.shape[1] // CMP_STRIDE
    flat_lane = lax.broadcasted_iota(jnp.int32, (1, half * HB_W), 1)
    group_of_lane = _div_pow2(jnp.bitwise_and(flat_lane, HB_W - 1), HEAD_DIM)
    for kind, (lo_ref, hi_ref) in enumerate(((kc_lo_ref, kc_hi_ref), (vc_lo_ref, vc_hi_ref))):
        chunks = [jnp.concatenate(
            [lo_ref[0, pl.ds(l, n_chunk, stride=CMP_STRIDE), :],
             hi_ref[0, pl.ds(l, n_chunk, stride=CMP_STRIDE), :]], axis=1) for l in range(half)]
        parts = []
        for h in range(2):
            pieces = [chunks[l] + pos_ref[kind, pl.ds(h * half + l, 1), :] for l in range(half)]
            flat = jnp.concatenate(pieces, axis=1)
            stacked = jnp.concatenate(
                [jnp.where(group_of_lane == g, flat, 0.0)
                 for g in range(B_KV_GROUPS)], axis=0).astype(BF16)
            parts.append(jnp.dot(stacked, w1_ref[kind, h], preferred_element_type=F32))
        rows = B_KV_GROUPS * n_chunk
        hidden = parts[0] + pltpu.roll(parts[1], rows - 1, axis=0)
        act = _gelu_tanh(hidden).astype(BF16)
        out = None
        for g in range(B_KV_GROUPS):
            act_g = act[g * n_chunk:(g + 1) * n_chunk]
            if kind == 0:
                o = jnp.dot(act_g, w2k_ref[g], preferred_element_type=F32)
            else:
                o = lax.dot_general(w2vt_ref[g], act_g, _NT, preferred_element_type=F32)
            out = o if out is None else out + o
        if kind == 0:
            kcb_ref[0] = out
        else:
            vcbt_ref[0] = out


def _compress(rows, kc_blk, vc_blk, pos, w1, w2k, w2vt):
    bsz, seq, _ = rows.shape
    n_chunk = seq // CMP_STRIDE
    return pl.pallas_call(
        _compress_kernel,
        grid=(bsz,),
        in_specs=[
            pl.BlockSpec((1, seq, LANES), lambda b: (b, 0, 2 * kc_blk)),
            pl.BlockSpec((1, seq, LANES), lambda b: (b, 0, 2 * kc_blk + 1)),
            pl.BlockSpec((1, seq, LANES), lambda b: (b, 0, 2 * vc_blk)),
            pl.BlockSpec((1, seq, LANES), lambda b: (b, 0, 2 * vc_blk + 1)),
            _resident(pos.shape),
            _resident(w1.shape),
            _resident(w2k.shape),
            _resident(w2vt.shape),
        ],
        out_specs=[pl.BlockSpec((1, n_chunk, HB_W), lambda b: (b, 0, 0)),
                   pl.BlockSpec((1, HB_W, n_chunk), lambda b: (b, 0, 0))],
        out_shape=[jax.ShapeDtypeStruct((bsz, n_chunk, HB_W), F32),
                   jax.ShapeDtypeStruct((bsz, HB_W, n_chunk), F32)],
        compiler_params=_params(1),
        name="nsa_compress",
    )(rows, rows, rows, rows, pos, w1, w2k, w2vt)


def _cmp_kernel(qt_ref, kcb_ref, vcbt_ref, gate_ref, ovt_ref, o_ref, sel_ref):
    qt = pl.program_id(1)
    n_cmp_pad = kcb_ref.shape[1]
    n_slc = LANES // B_KV_GROUPS
    chan = lax.broadcasted_iota(jnp.int32, (HB_W, 1), 0)
    slot_of_chan = _div_pow2(chan, HEAD_DIM)
    t = qt * TQ + lax.broadcasted_iota(jnp.int32, (1, TQ), 1)
    cmp_end = (lax.broadcasted_iota(jnp.int32, (n_cmp_pad, 1), 0) * CMP_STRIDE + CMP_LEN - 1)
    ok = cmp_end <= jnp.concatenate([t] * N_HB, axis=1)
    kc = kcb_ref[0].astype(BF16)
    vct = vcbt_ref[0].astype(BF16)
    gsig = _sigmoid(gate_ref[...].astype(F32))
    outs = [[None] * B_KV_GROUPS for _ in range(N_HB)]
    imp = None
    for g in range(B_KV_GROUPS):
        own = jnp.concatenate(
            [qt_ref[r * HB_W + g * HEAD_DIM:r * HB_W + (g + 1) * HEAD_DIM, :].astype(F32)
             for r in range(N_HB)], axis=1) * SCALE
        qm = jnp.concatenate(
            [jnp.zeros((g * HEAD_DIM, N_HB * TQ), F32), own,
             jnp.zeros(((B_KV_GROUPS - 1 - g) * HEAD_DIM, N_HB * TQ), F32)][
                 (1 if g == 0 else 0):(2 if g == B_KV_GROUPS - 1 else 3)], axis=0)
        s = jnp.dot(kc, qm.astype(BF16), preferred_element_type=F32)
        s = jnp.where(ok, s, NEG)
        m = jnp.max(s, axis=0, keepdims=True)
        e = jnp.where(ok, jnp.exp(s - m), 0.0)
        p = e / jnp.maximum(jnp.sum(e, axis=0, keepdims=True), TINY)
        o = jnp.dot(vct[g * HEAD_DIM:(g + 1) * HEAD_DIM], p.astype(BF16),
                    preferred_element_type=F32)
        p_group = None
        for r in range(N_HB):
            p_r = p[:, r * TQ:(r + 1) * TQ]
            p_group = p_r if p_group is None else p_group + p_r
            row = r * GATE_ROWS + g
            outs[r][g] = o[:, r * TQ:(r + 1) * TQ] * gsig[row:row + 1, :]
        part = jnp.dot(ovt_ref[g], p_group, precision=lax.Precision.HIGHEST,
                       preferred_element_type=F32)
        imp = part if imp is None else imp + part
    for r in range(N_HB):
        o_ref[0, :, r * HB_W:(r + 1) * HB_W] = (
            jnp.concatenate(outs[r], axis=0).T.astype(o_ref.dtype))
    imp = imp.reshape(B_KV_GROUPS, n_slc, TQ)
    j = lax.broadcasted_iota(jnp.int32, (1, n_slc, 1), 1)
    cur = _div_pow2(t, SLC_BLK).reshape(1, 1, TQ)
    val = jnp.where(j == cur - 1, -NEG, imp)
    val = jnp.where(j == cur, -NEG, val)
    val = jnp.where(j == 0, -NEG, val)
    val = jnp.where(j > cur, NEG, val)
    chosen = jnp.where(_rank_rows(val) < SLC_TOPK, 1.0, 0.0)
    sel_ref[0] = chosen.reshape(LANES, TQ)


def _cmp_attention(cols, kcb, vcbt, gate_rblk, ovt, bsz, seq):
    n_cmp_pad = kcb.shape[1]
    n_q = seq // TQ
    return pl.pallas_call(
        _cmp_kernel,
        grid=(bsz, n_q),
        in_specs=[
            pl.BlockSpec((D_MODEL, TQ), lambda b, qt: (0, b * n_q + qt)),
            pl.BlockSpec((1, n_cmp_pad, HB_W), lambda b, qt: (b, 0, 0)),
            pl.BlockSpec((1, HB_W, n_cmp_pad), lambda b, qt: (b, 0, 0)),
            pl.BlockSpec((N_HB * GATE_ROWS, TQ), lambda b, qt: (gate_rblk, b * n_q + qt)),
            _resident(ovt.shape),
        ],
        out_specs=[
            pl.BlockSpec((1, TQ, D_MODEL), lambda b, qt: (b, qt, 0)),
            pl.BlockSpec((1, LANES, TQ), lambda b, qt: (b, 0, qt)),
        ],
        out_shape=[
            jax.ShapeDtypeStruct((bsz, seq, D_MODEL), BF16),
            jax.ShapeDtypeStruct((bsz, LANES, seq), F32),
        ],
        compiler_params=_params(2),
        name="nsa_cmp_select",
    )(cols, kcb, vcbt, cols, ovt)


def _mixer_a(project, bsz, seq, w_in, w_out, sinks, bias_nat, rel_bias):
    qw, kvw = N_HEADS * HEAD_DIM, A_KV_HEADS * HEAD_DIM
    w_q = _interleave_heads(w_in[:, :qw].reshape(D_MODEL, N_HEADS, HEAD_DIM), A_KV_HEADS, 1)
    w_cols = jnp.concatenate([w_q.reshape(D_MODEL, qw), w_in[:, qw + kvw:]], axis=1)
    rows, cols = project(w_in[:, qw:qw + kvw], w_cols)
    rows = rows.reshape(bsz, seq, kvw)
    band_rows = TQ + -(-(A_WINDOW - 1) // LANES) * LANES
    assert band_rows <= seq
    bias = _band_bias(rel_bias, A_KV_HEADS, band_rows, A_WINDOW)
    sink_rows = _interleave_heads(sinks.astype(F32), A_KV_HEADS, 0).reshape(N_HB, 1, SLOTS)
    sink_rows = jnp.pad(sink_rows, ((0, 0), (0, 0), (0, LANES - SLOTS)))
    cfg = _AttnCfg(kv_width=kvw, band_rows=band_rows, sink=True, blocks_per_step=N_HB)
    o = _attention(cfg, rows, cols, bias, k_blk=lambda hb: 0, v_rblk=lambda hb: qw // kvw,
                   sinks=sink_rows)
    w_o = _interleave_heads(w_out.reshape(N_HEADS, HEAD_DIM, D_MODEL), A_KV_HEADS, 0)
    return [o.reshape(bsz * seq, -1)], w_o.reshape(qw, D_MODEL)


def _nsa_overlap_t(seq):
    n_chunk = seq // CMP_STRIDE
    n_cmp = (seq - CMP_LEN) // CMP_STRIDE + 1
    n_slc = seq // SLC_BLK
    c_start = np.arange(n_cmp)[:, None] * CMP_STRIDE
    s_start = np.arange(n_slc)[None, :] * SLC_BLK
    overlap = ((c_start < s_start + SLC_BLK) & (c_start + CMP_LEN > s_start)).astype(np.float32)
    ovt = np.zeros((B_KV_GROUPS, LANES, n_chunk), np.float32)
    for g in range(B_KV_GROUPS):
        ovt[g, g * n_slc:(g + 1) * n_slc, :n_cmp] = overlap.T
    return jnp.asarray(ovt)


def _mixer_b(project, bsz, seq, w_in, w_out, cmp_pos, cmp_w1, cmp_w2, bias_nat, rel_bias):
    assert seq // SLC_BLK * B_KV_GROUPS == LANES and seq // CMP_STRIDE == LANES
    grp, dh = B_KV_GROUPS, HEAD_DIM
    qw, kvw = N_HEADS * dh, B_KV_GROUPS * dh
    w_q = _interleave_heads(w_in[:, :qw].reshape(D_MODEL, N_HEADS, dh), grp, 1)
    kv = [w_in[:, qw + i * kvw:qw + (i + 1) * kvw] for i in range(6)]
    w_g = w_in[:, qw + 6 * kvw:].reshape(D_MODEL, grp, N_HB, 3)
    w_g = w_g.transpose(0, 2, 3, 1).reshape(D_MODEL, N_HB, 3 * grp)
    w_g = jnp.pad(w_g, ((0, 0), (0, 0), (0, GATE_ROWS - 3 * grp)))
    w_rows = jnp.concatenate([kv[0], kv[1], kv[2], kv[4]], axis=1)
    w_cols = jnp.concatenate([w_q.reshape(D_MODEL, qw), kv[3], kv[5],
                              w_g.reshape(D_MODEL, N_HB * GATE_ROWS)], axis=1)
    rows, cols = project(w_rows, w_cols)
    rows = rows.reshape(bsz, seq, 4 * kvw)
    gate_row0 = qw + 2 * kvw

    half = CMP_LEN // 2
    pos = jnp.tile(cmp_pos.astype(F32), (1, 1, grp))
    w1 = cmp_w1.reshape(2, 2, half, 1, dh, CMP_HIDDEN)
    w1 = jnp.broadcast_to(w1, (2, 2, half, grp, dh, CMP_HIDDEN))
    w1 = w1.reshape(2, 2, half * HB_W, CMP_HIDDEN).astype(BF16)
    w2 = jnp.zeros((2, grp, CMP_HIDDEN, grp, dh), F32)
    for g in range(grp):
        w2 = w2.at[:, g, :, g, :].set(cmp_w2)
    w2 = w2.reshape(2, grp, CMP_HIDDEN, HB_W).astype(BF16)
    kcb, vcbt = _compress(rows, 0, 1, pos, w1, w2[0], jnp.swapaxes(w2[1], 1, 2))

    o_cmp, sel = _cmp_attention(cols, kcb, vcbt, gate_row0 // (N_HB * GATE_ROWS),
                                _nsa_overlap_t(seq), bsz, seq)

    gate_rblk0 = gate_row0 // GATE_ROWS
    bias = _block_bias(bias_nat, grp, seq // TK)
    cfg_slc = _AttnCfg(kv_width=kvw, sel_blk=SLC_BLK, sel_from_input=True, gate_branch=1,
                       blocks_per_step=2)
    o_slc = _attention(cfg_slc, rows, cols, bias, k_blk=lambda hb: 2,
                       v_rblk=lambda hb: qw // kvw, sel=sel, gate_rblk0=gate_rblk0)
    band_rows = TQ + -(-(B_WINDOW - 1) // LANES) * LANES
    assert band_rows <= seq
    cfg_win = _AttnCfg(kv_width=kvw, band_rows=band_rows, gate_branch=2, blocks_per_step=2)
    bias_win = _band_bias(rel_bias, grp, band_rows, B_WINDOW)
    o_win = _attention(cfg_win, rows, cols, bias_win, k_blk=lambda hb: 3,
                       v_rblk=lambda hb: qw // kvw + 1, gate_rblk0=gate_rblk0)
    w_o = _interleave_heads(w_out.reshape(N_HEADS, dh, D_MODEL), grp, 0).reshape(qw, D_MODEL)
    n = bsz * seq
    return [o_cmp.reshape(n, -1), o_slc.reshape(n, -1), o_win.reshape(n, -1)], w_o


def _mixer_c(project, bsz, seq, w_in, w_out, bias_nat, rel_bias):
    assert seq % MOBA_BLK == 0 and MOBA_BLK == TK
    qw = N_HEADS * HEAD_DIM
    w_cols = jnp.concatenate([w_in[:, :qw], w_in[:, 2 * qw:]], axis=1)
    rows, cols = project(w_in[:, qw:2 * qw], w_cols)
    rows = rows.reshape(bsz, seq, qw)
    bias = _block_bias(bias_nat, 1, seq // TK)
    nb = 2
    cfg = _AttnCfg(kv_width=HB_W, sel_blk=MOBA_BLK, blocks_per_step=nb, kv_per_block=True)
    o = _attention(cfg, rows, cols, bias, k_blk=lambda hb: hb,
                   v_rblk=lambda hb: N_HB // nb + hb)
    return [o.reshape(bsz * seq, -1)], w_out


def _layer(xf, bsz, seq, layer, g, ffn_w_in, ffn_w_out, mixer, mixer_params, bias_nat,
           rel_bias):
    xf = _ffn(xf, g[0], ffn_w_in, ffn_w_out, (layer, 0), g[1])
    project = functools.partial(_proj, xf, g[2])
    o_list, w_mix = mixer(project, bsz, seq, *mixer_params, bias_nat, rel_bias)
    return _ffn(xf, g[4], ffn_w_in, ffn_w_out, (layer, 1), g[5], mix=(o_list, w_mix, g[3]))


def kernel(x, rel_bias, norm_g, ffn_w_in, ffn_w_out, a_w_in, a_w_out, a_sinks, b_w_in, b_w_out,
           b_cmp_pos, b_cmp_w1, b_cmp_w2, c_w_in, c_w_out):
    bsz, seq, d = x.shape
    assert d == D_MODEL and seq % TQ == 0 and (bsz * seq) % TM == 0
    depth = norm_g.shape[0]
    bias_nat = _bias_tiles(rel_bias, seq // TK)
    ffn_w_in, ffn_w_out = ffn_w_in.astype(BF16), ffn_w_out.astype(BF16)
    xf = x.reshape(bsz * seq, d)
    for i in range(depth):
        kind, j = i % N_MIXERS, i // N_MIXERS
        if kind == 0:
            mixer, params = _mixer_a, (a_w_in[j], a_w_out[j], a_sinks[j])
        elif kind == 1:
            mixer, params = _mixer_b, (b_w_in[j], b_w_out[j], b_cmp_pos[j], b_cmp_w1[j],
                                       b_cmp_w2[j])
        else:
            mixer, params = _mixer_c, (c_w_in[j], c_w_out[j])
        xf = _layer(xf, bsz, seq, i, norm_g[i], ffn_w_in, ffn_w_out, mixer, params,
                    bias_nat, rel_bias)
    return xf.reshape(bsz, seq, d)
```

```python
import dataclasses
import functools
import math

import numpy as np
import jax
import jax.numpy as jnp
from jax import lax
from jax.experimental import pallas as pl
from jax.experimental.pallas import tpu as pltpu

F32 = jnp.float32
BF16 = jnp.bfloat16

D_MODEL = 1024
HEAD_DIM = 64
N_HEADS = D_MODEL // HEAD_DIM
D_FF = 2816
NORM_EPS = 1e-6
N_MIXERS = 3
REL_BUCKETS = 32
REL_MAX_DIST = 1024
A_KV_HEADS = 2
A_WINDOW = 128
B_KV_GROUPS = 4
CMP_LEN = 32
CMP_STRIDE = 16
CMP_HIDDEN = 256
SLC_BLK = 64
SLC_TOPK = 16
B_WINDOW = 512
MOBA_BLK = 256
MOBA_TOPK = 3
NEG = -1e30
TINY = 1e-30

LANES = 128
SUBLANES = 8
MXU_DIM = 256
VMEM_LIMIT_BYTES = 56 * 1024 * 1024

SLOTS = 4
HB_W = SLOTS * HEAD_DIM
N_HB = N_HEADS // SLOTS
TQ = 256
TK = 256
TM = 512
FF_CHUNKS = ((0, 1536), (1536, D_FF))
GATE_ROWS = 16
SEL_BIG = float(2 ** 30)
SCALE = HEAD_DIM ** -0.5
LOG2E = math.log2(math.e)

_NT = (((1,), (1,)), ((), ()))


def _params(n_axes):
    return pltpu.CompilerParams(
        dimension_semantics=("arbitrary",) * n_axes,
        vmem_limit_bytes=VMEM_LIMIT_BYTES)


def _rms(x, g):
    return x * lax.rsqrt(jnp.mean(x * x, axis=-1, keepdims=True) + NORM_EPS) * g


def _sigmoid(x):
    return 1.0 / (1.0 + jnp.exp(-x))


def _div_pow2(x, d):
    assert d & (d - 1) == 0
    return jnp.right_shift(x, d.bit_length() - 1)


def _resident(shape):
    zeros = (0,) * len(shape)
    return pl.BlockSpec(shape, lambda *_: zeros, pipeline_mode=pl.Buffered(1))


def _ffn_kernel(*refs, n_mix):
    mix_refs, refs = refs[:n_mix], refs[n_mix:]
    if n_mix:
        wmix_ref, gmix_ref = refs[:2]
        refs = refs[2:]
    x_ref, gpre_ref, win_ref, wout_ref, gpost_ref, o_ref = refs
    x = x_ref[...]
    if n_mix:
        o = mix_refs[0][...]
        for r in mix_refs[1:]:
            o = o.astype(F32) + r[...].astype(F32)
        mixed = jnp.dot(o.astype(BF16), wmix_ref[...], preferred_element_type=F32)
        x = x + _rms(mixed, gmix_ref[...])
    hn = _rms(x, gpre_ref[...]).astype(BF16)
    y = None
    for lo, hi in FF_CHUNKS:
        gate = jnp.dot(hn, win_ref[:, lo:hi], preferred_element_type=F32)
        up = jnp.dot(hn, win_ref[:, D_FF + lo:D_FF + hi], preferred_element_type=F32)
        act = (gate * _sigmoid(gate) * up).astype(BF16)
        part = jnp.dot(act, wout_ref[lo:hi, :], preferred_element_type=F32)
        y = part if y is None else y + part
    o_ref[...] = x + 0.5 * _rms(y, gpost_ref[...])


def _ffn(xf, g_pre, w_in_all, w_out_all, which, g_post, mix=None):
    n, d = xf.shape
    tile = pl.BlockSpec((TM, d), lambda i: (i, 0))
    mix_specs, mix_args = [], []
    if mix is not None:
        o_list, w_mix, g_mix = mix
        mix_specs = [tile] * len(o_list) + [_resident(w_mix.shape), _resident((1, d))]
        mix_args = list(o_list) + [w_mix.astype(BF16), g_mix.reshape(1, d)]

    def picked(w_all):
        return pl.BlockSpec((None, None) + w_all.shape[2:], lambda i: which + (0, 0),
                            pipeline_mode=pl.Buffered(1))

    return pl.pallas_call(
        functools.partial(_ffn_kernel, n_mix=len(mix_args) - 2 if mix_args else 0),
        grid=(n // TM,),
        in_specs=mix_specs + [
            tile,
            _resident((1, d)),
            picked(w_in_all),
            picked(w_out_all),
            _resident((1, d)),
        ],
        out_specs=tile,
        out_shape=jax.ShapeDtypeStruct((n, d), F32),
        compiler_params=_params(1),
        name="ffn",
    )(*mix_args, xf, g_pre.reshape(1, d), w_in_all, w_out_all, g_post.reshape(1, d))


def _proj_kernel(x_ref, g_ref, wr_ref, wct_ref, rows_ref, cols_ref):
    hn = _rms(x_ref[...], g_ref[...]).astype(BF16)
    rows_ref[...] = jnp.dot(hn, wr_ref[...], preferred_element_type=F32)
    cols_ref[...] = lax.dot_general(wct_ref[...], hn, _NT,
                                    preferred_element_type=F32).astype(cols_ref.dtype)


def _proj(xf, g, w_rows, w_cols):
    n, d = xf.shape
    cr, cc = w_rows.shape[1], w_cols.shape[1]
    return pl.pallas_call(
        _proj_kernel,
        grid=(n // TM,),
        in_specs=[
            pl.BlockSpec((TM, d), lambda i: (i, 0)),
            _resident((1, d)),
            _resident((d, cr)),
            _resident((cc, d)),
        ],
        out_specs=[pl.BlockSpec((TM, cr), lambda i: (i, 0)),
                   pl.BlockSpec((cc, TM), lambda i: (0, i))],
        out_shape=[jax.ShapeDtypeStruct((n, cr), F32), jax.ShapeDtypeStruct((cc, n), BF16)],
        compiler_params=_params(1),
        name="proj",
    )(xf, g.reshape(1, d), w_rows.astype(BF16), w_cols.T.astype(BF16))


def _rel_bucket(dist):
    dist = jnp.maximum(dist, 0)
    exact = REL_BUCKETS // 2
    d = jnp.maximum(dist, 1).astype(jnp.float32)
    log_b = exact + (jnp.log(d / exact) / math.log(REL_MAX_DIST / exact)
                     * (REL_BUCKETS - exact)).astype(jnp.int32)
    return jnp.where(dist < exact, dist, jnp.minimum(log_b, REL_BUCKETS - 1))


def _tile_distance(n_delta, xp):
    delta = xp.arange(n_delta)[:, None, None]
    return delta * TK + xp.arange(TQ)[None, None, :] - xp.arange(TK)[None, :, None]


def _bias_of_distance(rel_bias, dist):
    bucket = _rel_bucket(dist).reshape(1, -1)
    onehot = (jnp.arange(REL_BUCKETS)[:, None] == bucket).astype(F32)
    tiles = jnp.dot(rel_bias.astype(F32).T, onehot, precision=lax.Precision.HIGHEST)
    return tiles.reshape((N_HEADS,) + dist.shape)


def _bias_tiles(rel_bias, n_delta):
    return _bias_of_distance(rel_bias, _tile_distance(n_delta, jnp))


def _band_distance(band_rows, xp):
    back = band_rows - TQ
    return back + xp.arange(TQ)[None, :] - xp.arange(band_rows + back)[:, None]


def _band_bias(rel_bias, n_groups, band_rows, window):
    bias = _bias_of_distance(rel_bias, _band_distance(band_rows, jnp))
    bias = _interleave_heads(bias, n_groups, 0)
    dist = _band_distance(band_rows, np)
    bias = jnp.where((dist >= 0) & (dist < window), bias * LOG2E, NEG)
    return bias.reshape((N_HB, SLOTS) + bias.shape[1:])


def _interleave_heads(w, n_groups, axis):
    r = N_HEADS // n_groups
    shape = w.shape
    w = w.reshape(shape[:axis] + (n_groups, r) + shape[axis + 1:])
    w = jnp.swapaxes(w, axis, axis + 1)
    return w.reshape(shape)


def _block_bias(bias_nat, n_groups, n_delta, window=None):
    dist = _tile_distance(n_delta, np)
    visible = dist >= 0
    if window is not None:
        visible &= dist < window
    bias = bias_nat[:, :n_delta]
    if n_groups > 1:
        bias = _interleave_heads(bias, n_groups, 0)
    bias = jnp.where(visible, bias * LOG2E, NEG)
    return bias.reshape((N_HB, SLOTS) + bias.shape[1:])


@dataclasses.dataclass(frozen=True)
class _AttnCfg:
    kv_width: int
    n_back: int | None = None
    band_rows: int | None = None
    blocks_per_step: int = 1
    kv_per_block: bool = False
    sel_blk: int | None = None
    sel_from_input: bool = False
    gate_branch: int | None = None
    sink: bool = False

    @property
    def selecting(self):
        return self.sel_blk is not None

    @property
    def moba(self):
        return self.selecting and not self.sel_from_input


def _rank_rows(val):
    width = val.shape[1]
    row = lax.broadcasted_iota(jnp.int32, (1, width, 1), 1)
    rank = jnp.zeros(val.shape, F32)
    for m in range(width):
        vm = val[:, m:m + 1, :]
        rank = rank + jnp.where(row > m, jnp.where(vm >= val, 1.0, 0.0),
                                jnp.where(vm > val, 1.0, 0.0))
    return rank


def _u_chan0(hh, n_blk):
    return hh * n_blk + (LANES if hh < SLOTS // 2 else 0)


def _fold_rows(x, op):
    return op(x.reshape(x.shape[0] // SUBLANES, SUBLANES, x.shape[1]), axis=0)


def _attn_kernel(*refs, cfg: _AttnCfg):
    it = iter(refs)
    qt_ref, k_ref, vt_ref, bias_ref = next(it), next(it), next(it), next(it)
    sel_ref = next(it) if cfg.sel_from_input else None
    gate_ref = next(it) if cfg.gate_branch is not None else None
    sink_ref = next(it) if cfg.sink else None
    o_ref = next(it)
    kb_ref, vtb_ref = next(it), next(it)
    scratch_refs = ((next(it), next(it), next(it), next(it))
                    if cfg.band_rows is None else ())
    kmx_ref = next(it) if cfg.moba else None
    vt_tile = LANES if cfg.band_rows is not None else TK

    qt = pl.program_id(2)
    seq = k_ref.shape[1]
    n_tiles = seq // TK
    lane = lax.broadcasted_iota(jnp.int32, (1, HB_W), 1)
    slot_of_lane = _div_pow2(lane, HEAD_DIM)
    n_sel_blk = seq // cfg.sel_blk if cfg.selecting else 0

    n_kv = kb_ref.shape[0]
    kw = k_ref.shape[2] // n_kv

    @pl.when(qt == 0)
    def _prepare():
        if cfg.moba:
            kmx_ref[...] = jnp.zeros(kmx_ref.shape, F32)
        for kv, c in [(kv, c) for kv in range(n_kv) for c in range(n_tiles)]:
            rows = pl.ds(c * TK, TK)
            k = k_ref[0, rows, kv * kw:(kv + 1) * kw]
            vt = vt_ref[kv * kw:(kv + 1) * kw, c * TK:(c + 1) * TK]
            if kw == LANES:
                k = jnp.concatenate([k, k], axis=1)
                vt = jnp.concatenate([vt, vt], axis=0)
            for i in range(TK // vt_tile):
                vtb_ref[kv, c * (TK // vt_tile) + i] = (
                    vt[:, i * vt_tile:(i + 1) * vt_tile].astype(BF16))
            if cfg.selecting:
                key_blk = _div_pow2(c * TK + lax.broadcasted_iota(jnp.int32, (TK, 1), 0),
                                    cfg.sel_blk)
                for hh in range(SLOTS):
                    onehot = (lane - _u_chan0(hh, n_sel_blk)) == key_blk
                    kb_ref[kv, hh, rows, :] = jnp.where(
                        slot_of_lane == hh, k, jnp.where(onehot, 1.0, 0.0)).astype(BF16)
            else:
                kb_ref[kv, 0, rows, :] = k.astype(BF16)
            if cfg.moba:
                k_mean = jnp.mean(k, axis=0, keepdims=True)
                for hh in range(SLOTS):
                    kmx_ref[kv, pl.ds(hh * n_sel_blk + c, 1), :] = jnp.where(
                        slot_of_lane == hh, k_mean, 0.0)

    for blk in range(qt_ref.shape[0] // HB_W):
        kv = blk if n_kv > 1 else 0
        refs_of_block = (qt_ref, bias_ref, sel_ref, gate_ref, sink_ref, o_ref, kb_ref.at[kv],
                         vtb_ref.at[kv], scratch_refs,
                         kmx_ref.at[kv] if cfg.moba else None)
        _attend_block(cfg, blk, qt, n_sel_blk, refs_of_block)


def _attend_block(cfg, blk, qt, n_sel_blk, refs):
    (qt_ref, bias_ref, sel_ref, gate_ref, sink_ref, o_ref, kb_ref, vtb_ref, scratch_refs,
     kmx_ref) = refs
    bias_blk = bias_ref.at[blk]
    q_t = qt_ref[blk * HB_W:(blk + 1) * HB_W, :].astype(F32)

    sel_rows = None
    if cfg.sel_from_input:
        sel_rows = sel_ref[0]
    elif cfg.moba:
        gate = jnp.dot(kmx_ref[...].astype(BF16), q_t.astype(BF16),
                       preferred_element_type=F32)
        gate = gate.reshape(SLOTS, n_sel_blk, TQ)
        key_blk = lax.broadcasted_iota(jnp.int32, (1, n_sel_blk, 1), 1)
        val = jnp.where(key_blk < qt, gate, NEG)
        rank = _rank_rows(val)
        chosen = jnp.where(key_blk < qt, jnp.where(rank < MOBA_TOPK, 1.0, 0.0), 0.0)
        chosen = jnp.where(key_blk == qt, 1.0, chosen)
        sel_rows = chosen.reshape(SLOTS * n_sel_blk, TQ)

    q_scaled = q_t * (SCALE * LOG2E)
    qms = []
    for hh in range(SLOTS):
        pieces = {hh * HEAD_DIM: q_scaled[hh * HEAD_DIM:(hh + 1) * HEAD_DIM]}
        if cfg.selecting:
            pieces[_u_chan0(hh, n_sel_blk)] = (
                sel_rows[hh * n_sel_blk:(hh + 1) * n_sel_blk] - 1.0) * SEL_BIG
        segs, row = [], 0
        for start in sorted(pieces):
            if start > row:
                segs.append(jnp.zeros((start - row, TQ), F32))
            segs.append(pieces[start])
            row = start + pieces[start].shape[0]
        if row < HB_W:
            segs.append(jnp.zeros((HB_W - row, TQ), F32))
        qms.append(jnp.concatenate(segs, axis=0).astype(BF16))
    q_stack = None if cfg.selecting else jnp.concatenate(qms, axis=1)

    sinks = ([sink_ref[blk, :, hh:hh + 1] * LOG2E for hh in range(SLOTS)]
             if cfg.sink else None)
    if cfg.band_rows is not None:
        ms, sums, accs = _band_softmax(cfg, qt, q_stack, kb_ref, vtb_ref, bias_blk, sinks)
    else:
        ms, sums, accs = _tiled_softmax(cfg, qt, qms, q_stack, kb_ref, vtb_ref, bias_blk,
                                        sinks, *scratch_refs)

    gsig = None
    if cfg.gate_branch is not None:
        gsig = _sigmoid(
            gate_ref[blk * GATE_ROWS:(blk + 1) * GATE_ROWS, :].astype(F32))
    outs = []
    for hh in range(SLOTS):
        denom = jnp.sum(sums[hh], axis=0, keepdims=True)
        if cfg.sink:
            denom = denom + jnp.exp2(sinks[hh] - ms[hh])
        o_h = accs[hh] / jnp.maximum(denom, TINY)
        if cfg.gate_branch is not None:
            row = cfg.gate_branch * SLOTS + hh
            o_h = o_h * gsig[row:row + 1, :]
        outs.append(o_h)
    o_ref[0, :, blk * HB_W:(blk + 1) * HB_W] = (
        jnp.concatenate(outs, axis=0).T.astype(o_ref.dtype))


def _band_softmax(cfg, qt, q_stack, kb_ref, vtb_ref, bias_ref, sinks):
    band = cfg.band_rows
    back = band - TQ
    start = pl.multiple_of(jnp.maximum(qt * TQ - back, 0), LANES)
    s_all = jnp.dot(kb_ref[0, pl.ds(start, band), :], q_stack, preferred_element_type=F32)
    bias_rows = pl.ds(pl.multiple_of(back - (qt * TQ - start), LANES), band)
    tile0 = _div_pow2(start, LANES)
    ms, sums, accs = [], [], []
    for hh in range(SLOTS):
        head = pl.ds(hh * HEAD_DIM, HEAD_DIM)
        sc = s_all[:, hh * TQ:(hh + 1) * TQ] + bias_ref[hh, bias_rows, :]
        m = jnp.max(_fold_rows(sc, jnp.max), axis=0, keepdims=True)
        if cfg.sink:
            m = jnp.maximum(m, sinks[hh])
        p = jnp.exp2(sc - m)
        vt = jnp.concatenate([vtb_ref[tile0 + c, head, :] for c in range(band // LANES)],
                             axis=1)
        ms.append(m)
        sums.append(_fold_rows(p, jnp.sum))
        accs.append(jnp.dot(vt, p.astype(BF16), preferred_element_type=F32))
    return ms, sums, accs


def _tiled_softmax(cfg, qt, qms, q_stack, kb_ref, vtb_ref, bias_ref, sinks,
                   sc_ref, m_ref, l_ref, acc_ref):
    lo = jnp.maximum(qt - cfg.n_back, 0) if cfg.n_back is not None else 0
    m_ref[...] = jnp.full(m_ref.shape, NEG, F32)
    l_ref[...] = jnp.zeros(l_ref.shape, F32)
    acc_ref[...] = jnp.zeros(acc_ref.shape, F32)

    def pass1(kt, width):
        rows = pl.ds(pl.multiple_of(kt * TK, TK), width * TK)
        if cfg.selecting:
            s = [jnp.dot(kb_ref[hh, rows, :], qms[hh], preferred_element_type=F32)
                 for hh in range(SLOTS)]
        else:
            s_all = jnp.dot(kb_ref[0, rows, :], q_stack, preferred_element_type=F32)
            s = [s_all[:, hh * TQ:(hh + 1) * TQ] for hh in range(SLOTS)]
        for hh in range(SLOTS):
            m_new = m_ref[hh]
            for w in range(width):
                tile = s[hh][w * TK:(w + 1) * TK] + bias_ref[hh, qt - kt - w]
                sc_ref[hh, kt - lo + w] = tile
                m_new = jnp.maximum(m_new, _fold_rows(tile, jnp.max))
            m_ref[hh] = m_new

    def pass2(kt, width, ms):
        for hh in range(SLOTS):
            head = pl.ds(hh * HEAD_DIM, HEAD_DIM)
            l_new = l_ref[hh]
            o_new = acc_ref[head, :]
            for w in range(width):
                p = jnp.exp2(sc_ref[hh, kt - lo + w] - ms[hh])
                l_new = l_new + _fold_rows(p, jnp.sum)
                o_new = o_new + jnp.dot(vtb_ref[kt + w, head, :], p.astype(BF16),
                                        preferred_element_type=F32)
            l_ref[hh] = l_new
            acc_ref[head, :] = o_new

    def over_tiles(body):
        n = qt + 1 - lo
        start = lo
        for width in (1, 2):
            has = jnp.bitwise_and(n, width)

            @pl.when(has != 0)
            def _part(start=start, width=width):
                body(start, width)

            start = start + has

        def group(i, carry):
            body(start + 4 * i, 4)
            return carry

        lax.fori_loop(0, jnp.right_shift(n, 2), group, 0)

    over_tiles(pass1)
    ms = []
    for hh in range(SLOTS):
        m = jnp.max(m_ref[hh], axis=0, keepdims=True)
        ms.append(jnp.maximum(m, sinks[hh]) if cfg.sink else m)
    over_tiles(functools.partial(pass2, ms=ms))
    return (ms, [l_ref[hh] for hh in range(SLOTS)],
            [acc_ref[hh * HEAD_DIM:(hh + 1) * HEAD_DIM, :] for hh in range(SLOTS)])


def _attention(cfg, rows, cols, bias, *, k_blk, v_rblk, sel=None, gate_rblk0=None, sinks=None):
    bsz, seq, _ = rows.shape
    n_delta = bias.shape[2]
    n_q = seq // TQ
    nb = cfg.blocks_per_step
    n_kv = nb if cfg.kv_per_block else 1
    kw = n_kv * cfg.kv_width
    bias_tail = (0,) * (bias.ndim - 1)
    in_specs = [
        pl.BlockSpec((nb * HB_W, TQ), lambda hb, b, qt: (hb, b * n_q + qt)),
        pl.BlockSpec((1, seq, kw), lambda hb, b, qt: (b, 0, k_blk(hb))),
        pl.BlockSpec((kw, seq), lambda hb, b, qt: (v_rblk(hb), b)),
        pl.BlockSpec((nb,) + bias.shape[1:], lambda hb, b, qt: (hb,) + bias_tail,
                     pipeline_mode=pl.Buffered(1)),
    ]
    args = [cols, rows, cols, bias]
    if cfg.sel_from_input:
        in_specs.append(pl.BlockSpec((1, LANES, TQ), lambda hb, b, qt: (b, 0, qt)))
        args.append(sel)
    if cfg.gate_branch is not None:
        assert gate_rblk0 % nb == 0
        in_specs.append(pl.BlockSpec(
            (nb * GATE_ROWS, TQ), lambda hb, b, qt: (gate_rblk0 // nb + hb, b * n_q + qt)))
        args.append(cols)
    if cfg.sink:
        in_specs.append(pl.BlockSpec((nb, 1, LANES), lambda hb, b, qt: (hb, 0, 0)))
        args.append(sinks)
    vt_tile = LANES if cfg.band_rows is not None else TK
    scratch = [
        pltpu.VMEM((n_kv, SLOTS if cfg.selecting else 1, seq, HB_W), BF16),
        pltpu.VMEM((n_kv, seq // vt_tile, HB_W, vt_tile), BF16),
    ]
    if cfg.band_rows is None:
        scratch += [
            pltpu.VMEM((SLOTS, n_delta, TK, TQ), F32),
            pltpu.VMEM((SLOTS, SUBLANES, TQ), F32),
            pltpu.VMEM((SLOTS, SUBLANES, TQ), F32),
            pltpu.VMEM((HB_W, TQ), F32),
        ]
    if cfg.moba:
        assert SLOTS * (seq // cfg.sel_blk) <= LANES
        scratch.append(
            pltpu.VMEM((n_kv, SLOTS * (seq // cfg.sel_blk), HB_W), F32))
    return pl.pallas_call(
        functools.partial(_attn_kernel, cfg=cfg),
        grid=(N_HB // nb, bsz, n_q),
        in_specs=in_specs,
        out_specs=pl.BlockSpec((1, TQ, nb * HB_W), lambda hb, b, qt: (b, qt, hb)),
        out_shape=jax.ShapeDtypeStruct((bsz, seq, D_MODEL), BF16),
        scratch_shapes=scratch,
        compiler_params=_params(3),
        name="attention",
    )(*args)


def _gelu_tanh(x):
    cdf = 0.5 * (1.0 + jnp.tanh(math.sqrt(2.0 / math.pi) * (x + 0.044715 * (x * x * x))))
    return x * cdf


def _compress_kernel(kc_lo_ref, kc_hi_ref, vc_lo_ref, vc_hi_ref, pos_ref, w1_ref, w2k_ref,
                     w2vt_ref, kcb_ref, vcbt_ref):
    half = CMP_LEN // 2
    n_chunk = kc_lo_ref.shape[1] // CMP_STRIDE
    flat_lane = lax.broadcasted_iota(jnp.int32, (1, half * HB_W), 1)
    group_of_lane = _div_pow2(jnp.bitwise_and(flat_lane, HB_W - 1), HEAD_DIM)
    for kind, (lo_ref, hi_ref) in enumerate(((kc_lo_ref, kc_hi_ref), (vc_lo_ref, vc_hi_ref))):
        chunks = [jnp.concatenate(
            [lo_ref[0, pl.ds(l, n_chunk, stride=CMP_STRIDE), :],
             hi_ref[0, pl.ds(l, n_chunk, stride=CMP_STRIDE), :]], axis=1) for l in range(half)]
        parts = []
        for h in range(2):
            pieces = [chunks[l] + pos_ref[kind, pl.ds(h * half + l, 1), :] for l in range(half)]
            flat = jnp.concatenate(pieces, axis=1)
            stacked = jnp.concatenate(
                [jnp.where(group_of_lane == g, flat, 0.0)
                 for g in range(B_KV_GROUPS)], axis=0).astype(BF16)
            parts.append(jnp.dot(stacked, w1_ref[kind, h], preferred_element_type=F32))
        rows = B_KV_GROUPS * n_chunk
        hidden = parts[0] + pltpu.roll(parts[1], rows - 1, axis=0)
        act = _gelu_tanh(hidden).astype(BF16)
        out = None
        for g in range(B_KV_GROUPS):
            act_g = act[g * n_chunk:(g + 1) * n_chunk]
            if kind == 0:
                o = jnp.dot(act_g, w2k_ref[g], preferred_element_type=F32)
            else:
                o = lax.dot_general(w2vt_ref[g], act_g, _NT, preferred_element_type=F32)
            out = o if out is None else out + o
        if kind == 0:
            kcb_ref[0] = out
        else:
            vcbt_ref[0] = out


def _compress(rows, kc_blk, vc_blk, pos, w1, w2k, w2vt):
    bsz, seq, _ = rows.shape
    n_chunk = seq // CMP_STRIDE
    return pl.pallas_call(
        _compress_kernel,
        grid=(bsz,),
        in_specs=[
            pl.BlockSpec((1, seq, LANES), lambda b: (b, 0, 2 * kc_blk)),
            pl.BlockSpec((1, seq, LANES), lambda b: (b, 0, 2 * kc_blk + 1)),
            pl.BlockSpec((1, seq, LANES), lambda b: (b, 0, 2 * vc_blk)),
            pl.BlockSpec((1, seq, LANES), lambda b: (b, 0, 2 * vc_blk + 1)),
            _resident(pos.shape),
            _resident(w1.shape),
            _resident(w2k.shape),
            _resident(w2vt.shape),
        ],
        out_specs=[pl.BlockSpec((1, n_chunk, HB_W), lambda b: (b, 0, 0)),
                   pl.BlockSpec((1, HB_W, n_chunk), lambda b: (b, 0, 0))],
        out_shape=[jax.ShapeDtypeStruct((bsz, n_chunk, HB_W), F32),
                   jax.ShapeDtypeStruct((bsz, HB_W, n_chunk), F32)],
        compiler_params=_params(1),
        name="nsa_compress",
    )(rows, rows, rows, rows, pos, w1, w2k, w2vt)


def _cmp_kernel(qt_ref, kcb_ref, vcbt_ref, gate_ref, ovt_ref, o_ref, sel_ref):
    qt = pl.program_id(1)
    n_cmp_pad = kcb_ref.shape[1]
    n_slc = LANES // B_KV_GROUPS
    chan = lax.broadcasted_iota(jnp.int32, (HB_W, 1), 0)
    slot_of_chan = _div_pow2(chan, HEAD_DIM)
    t = qt * TQ + lax.broadcasted_iota(jnp.int32, (1, TQ), 1)
    cmp_end = (lax.broadcasted_iota(jnp.int32, (n_cmp_pad, 1), 0) * CMP_STRIDE + CMP_LEN - 1)
    ok = cmp_end <= jnp.concatenate([t] * N_HB, axis=1)
    kc = kcb_ref[0].astype(BF16)
    vct = vcbt_ref[0].astype(BF16)
    gsig = _sigmoid(gate_ref[...].astype(F32))
    outs = [[None] * B_KV_GROUPS for _ in range(N_HB)]
    imp = None
    for g in range(B_KV_GROUPS):
        own = jnp.concatenate(
            [qt_ref[r * HB_W + g * HEAD_DIM:r * HB_W + (g + 1) * HEAD_DIM, :].astype(F32)
             for r in range(N_HB)], axis=1) * SCALE
        qm = jnp.concatenate(
            [jnp.zeros((g * HEAD_DIM, N_HB * TQ), F32), own,
             jnp.zeros(((B_KV_GROUPS - 1 - g) * HEAD_DIM, N_HB * TQ), F32)][
                 (1 if g == 0 else 0):(2 if g == B_KV_GROUPS - 1 else 3)], axis=0)
        s = jnp.dot(kc, qm.astype(BF16), preferred_element_type=F32)
        s = jnp.where(ok, s, NEG)
        m = jnp.max(s, axis=0, keepdims=True)
        e = jnp.where(ok, jnp.exp(s - m), 0.0)
        p = e / jnp.maximum(jnp.sum(e, axis=0, keepdims=True), TINY)
        o = jnp.dot(vct[g * HEAD_DIM:(g + 1) * HEAD_DIM], p.astype(BF16),
                    preferred_element_type=F32)
        p_group = None
        for r in range(N_HB):
            p_r = p[:, r * TQ:(r + 1) * TQ]
            p_group = p_r if p_group is None else p_group + p_r
            row = r * GATE_ROWS + g
            outs[r][g] = o[:, r * TQ:(r + 1) * TQ] * gsig[row:row + 1, :]
        part = jnp.dot(ovt_ref[g], p_group, precision=lax.Precision.HIGHEST,
                       preferred_element_type=F32)
        imp = part if imp is None else imp + part
    for r in range(N_HB):
        o_ref[0, :, r * HB_W:(r + 1) * HB_W] = (
            jnp.concatenate(outs[r], axis=0).T.astype(o_ref.dtype))
    imp = imp.reshape(B_KV_GROUPS, n_slc, TQ)
    j = lax.broadcasted_iota(jnp.int32, (1, n_slc, 1), 1)
    cur = _div_pow2(t, SLC_BLK).reshape(1, 1, TQ)
    val = jnp.where(j == cur - 1, -NEG, imp)
    val = jnp.where(j == cur, -NEG, val)
    val = jnp.where(j == 0, -NEG, val)
    val = jnp.where(j > cur, NEG, val)
    chosen = jnp.where(_rank_rows(val) < SLC_TOPK, 1.0, 0.0)
    sel_ref[0] = chosen.reshape(LANES, TQ)


def _cmp_attention(cols, kcb, vcbt, gate_rblk, ovt, bsz, seq):
    n_cmp_pad = kcb.shape[1]
    n_q = seq // TQ
    return pl.pallas_call(
        _cmp_kernel,
        grid=(bsz, n_q),
        in_specs=[
            pl.BlockSpec((D_MODEL, TQ), lambda b, qt: (0, b * n_q + qt)),
            pl.BlockSpec((1, n_cmp_pad, HB_W), lambda b, qt: (b, 0, 0)),
            pl.BlockSpec((1, HB_W, n_cmp_pad), lambda b, qt: (b, 0, 0)),
            pl.BlockSpec((N_HB * GATE_ROWS, TQ), lambda b, qt: (gate_rblk, b * n_q + qt)),
            _resident(ovt.shape),
        ],
        out_specs=[
            pl.BlockSpec((1, TQ, D_MODEL), lambda b, qt: (b, qt, 0)),
            pl.BlockSpec((1, LANES, TQ), lambda b, qt: (b, 0, qt)),
        ],
        out_shape=[
            jax.ShapeDtypeStruct((bsz, seq, D_MODEL), BF16),
            jax.ShapeDtypeStruct((bsz, LANES, seq), F32),
        ],
        compiler_params=_params(2),
        name="nsa_cmp_select",
    )(cols, kcb, vcbt, cols, ovt)


def _mixer_a(project, bsz, seq, w_in, w_out, sinks, bias_nat, rel_bias):
    qw, kvw = N_HEADS * HEAD_DIM, A_KV_HEADS * HEAD_DIM
    w_q = _interleave_heads(w_in[:, :qw].reshape(D_MODEL, N_HEADS, HEAD_DIM), A_KV_HEADS, 1)
    w_cols = jnp.concatenate([w_q.reshape(D_MODEL, qw), w_in[:, qw + kvw:]], axis=1)
    rows, cols = project(w_in[:, qw:qw + kvw], w_cols)
    rows = rows.reshape(bsz, seq, kvw)
    band_rows = TQ + -(-(A_WINDOW - 1) // LANES) * LANES
    assert band_rows <= seq
    bias = _band_bias(rel_bias, A_KV_HEADS, band_rows, A_WINDOW)
    sink_rows = _interleave_heads(sinks.astype(F32), A_KV_HEADS, 0).reshape(N_HB, 1, SLOTS)
    sink_rows = jnp.pad(sink_rows, ((0, 0), (0, 0), (0, LANES - SLOTS)))
    cfg = _AttnCfg(kv_width=kvw, band_rows=band_rows, sink=True, blocks_per_step=N_HB)
    o = _attention(cfg, rows, cols, bias, k_blk=lambda hb: 0, v_rblk=lambda hb: qw // kvw,
                   sinks=sink_rows)
    w_o = _interleave_heads(w_out.reshape(N_HEADS, HEAD_DIM, D_MODEL), A_KV_HEADS, 0)
    return [o.reshape(bsz * seq, -1)], w_o.reshape(qw, D_MODEL)


def _nsa_overlap_t(seq):
    n_chunk = seq // CMP_STRIDE
    n_cmp = (seq - CMP_LEN) // CMP_STRIDE + 1
    n_slc = seq // SLC_BLK
    c_start = np.arange(n_cmp)[:, None] * CMP_STRIDE
    s_start = np.arange(n_slc)[None, :] * SLC_BLK
    overlap = ((c_start < s_start + SLC_BLK) & (c_start + CMP_LEN > s_start)).astype(np.float32)
    ovt = np.zeros((B_KV_GROUPS, LANES, n_chunk), np.float32)
    for g in range(B_KV_GROUPS):
        ovt[g, g * n_slc:(g + 1) * n_slc, :n_cmp] = overlap.T
    return jnp.asarray(ovt)


def _mixer_b(project, bsz, seq, w_in, w_out, cmp_pos, cmp_w1, cmp_w2, bias_nat, rel_bias):
    assert seq // SLC_BLK * B_KV_GROUPS == LANES and seq // CMP_STRIDE == LANES
    grp, dh = B_KV_GROUPS, HEAD_DIM
    qw, kvw = N_HEADS * dh, B_KV_GROUPS * dh
    w_q = _interleave_heads(w_in[:, :qw].reshape(D_MODEL, N_HEADS, dh), grp, 1)
    kv = [w_in[:, qw + i * kvw:qw + (i + 1) * kvw] for i in range(6)]
    w_g = w_in[:, qw + 6 * kvw:].reshape(D_MODEL, grp, N_HB, 3)
    w_g = w_g.transpose(0, 2, 3, 1).reshape(D_MODEL, N_HB, 3 * grp)
    w_g = jnp.pad(w_g, ((0, 0), (0, 0), (0, GATE_ROWS - 3 * grp)))
    w_rows = jnp.concatenate([kv[0], kv[1], kv[2], kv[4]], axis=1)
    w_cols = jnp.concatenate([w_q.reshape(D_MODEL, qw), kv[3], kv[5],
                              w_g.reshape(D_MODEL, N_HB * GATE_ROWS)], axis=1)
    rows, cols = project(w_rows, w_cols)
    rows = rows.reshape(bsz, seq, 4 * kvw)
    gate_row0 = qw + 2 * kvw

    half = CMP_LEN // 2
    pos = jnp.tile(cmp_pos.astype(F32), (1, 1, grp))
    w1 = cmp_w1.reshape(2, 2, half, 1, dh, CMP_HIDDEN)
    w1 = jnp.broadcast_to(w1, (2, 2, half, grp, dh, CMP_HIDDEN))
    w1 = w1.reshape(2, 2, half * HB_W, CMP_HIDDEN).astype(BF16)
    w2 = jnp.zeros((2, grp, CMP_HIDDEN, grp, dh), F32)
    for g in range(grp):
        w2 = w2.at[:, g, :, g, :].set(cmp_w2)
    w2 = w2.reshape(2, grp, CMP_HIDDEN, HB_W).astype(BF16)
    kcb, vcbt = _compress(rows, 0, 1, pos, w1, w2[0], jnp.swapaxes(w2[1], 1, 2))

    o_cmp, sel = _cmp_attention(cols, kcb, vcbt, gate_row0 // (N_HB * GATE_ROWS),
                                _nsa_overlap_t(seq), bsz, seq)

    gate_rblk0 = gate_row0 // GATE_ROWS
    bias = _block_bias(bias_nat, grp, seq // TK)
    cfg_slc = _AttnCfg(kv_width=kvw, sel_blk=SLC_BLK, sel_from_input=True, gate_branch=1,
                       blocks_per_step=2)
    o_slc = _attention(cfg_slc, rows, cols, bias, k_blk=lambda hb: 2,
                       v_rblk=lambda hb: qw // kvw, sel=sel, gate_rblk0=gate_rblk0)
    band_rows = TQ + -(-(B_WINDOW - 1) // LANES) * LANES
    assert band_rows <= seq
    cfg_win = _AttnCfg(kv_width=kvw, band_rows=band_rows, gate_branch=2, blocks_per_step=N_HB)
    bias_win = _band_bias(rel_bias, grp, band_rows, B_WINDOW)
    o_win = _attention(cfg_win, rows, cols, bias_win, k_blk=lambda hb: 3,
                       v_rblk=lambda hb: qw // kvw + 1, gate_rblk0=gate_rblk0)
    w_o = _interleave_heads(w_out.reshape(N_HEADS, dh, D_MODEL), grp, 0).reshape(qw, D_MODEL)
    n = bsz * seq
    return [o_cmp.reshape(n, -1), o_slc.reshape(n, -1), o_win.reshape(n, -1)], w_o


def _mixer_c(project, bsz, seq, w_in, w_out, bias_nat, rel_bias):
    assert seq % MOBA_BLK == 0 and MOBA_BLK == TK
    qw = N_HEADS * HEAD_DIM
    w_cols = jnp.concatenate([w_in[:, :qw], w_in[:, 2 * qw:]], axis=1)
    rows, cols = project(w_in[:, qw:2 * qw], w_cols)
    rows = rows.reshape(bsz, seq, qw)
    bias = _block_bias(bias_nat, 1, seq // TK)
    nb = 2
    cfg = _AttnCfg(kv_width=HB_W, sel_blk=MOBA_BLK, blocks_per_step=nb, kv_per_block=True)
    o = _attention(cfg, rows, cols, bias, k_blk=lambda hb: hb,
                   v_rblk=lambda hb: N_HB // nb + hb)
    return [o.reshape(bsz * seq, -1)], w_out


def _layer(xf, bsz, seq, layer, g, ffn_w_in, ffn_w_out, mixer, mixer_params, bias_nat,
           rel_bias):
    xf = _ffn(xf, g[0], ffn_w_in, ffn_w_out, (layer, 0), g[1])
    project = functools.partial(_proj, xf, g[2])
    o_list, w_mix = mixer(project, bsz, seq, *mixer_params, bias_nat, rel_bias)
    return _ffn(xf, g[4], ffn_w_in, ffn_w_out, (layer, 1), g[5], mix=(o_list, w_mix, g[3]))


def kernel(x, rel_bias, norm_g, ffn_w_in, ffn_w_out, a_w_in, a_w_out, a_sinks, b_w_in, b_w_out,
           b_cmp_pos, b_cmp_w1, b_cmp_w2, c_w_in, c_w_out):
    bsz, seq, d = x.shape
    assert d == D_MODEL and seq % TQ == 0 and (bsz * seq) % TM == 0
    depth = norm_g.shape[0]
    bias_nat = _bias_tiles(rel_bias, seq // TK)
    ffn_w_in, ffn_w_out = ffn_w_in.astype(BF16), ffn_w_out.astype(BF16)
    xf = x.reshape(bsz * seq, d)
    for i in range(depth):
        kind, j = i % N_MIXERS, i // N_MIXERS
        if kind == 0:
            mixer, params = _mixer_a, (a_w_in[j], a_w_out[j], a_sinks[j])
        elif kind == 1:
            mixer, params = _mixer_b, (b_w_in[j], b_w_out[j], b_cmp_pos[j], b_cmp_w1[j],
                                       b_cmp_w2[j])
        else:
            mixer, params = _mixer_c, (c_w_in[j], c_w_out[j])
        xf = _layer(xf, bsz, seq, i, norm_g[i], ffn_w_in, ffn_w_out, mixer, params,
                    bias_nat, rel_bias)
    return xf.reshape(bsz, seq, d)
```

```python
import dataclasses
import functools
import math

import numpy as np
import jax
import jax.numpy as jnp
from jax import lax
from jax.experimental import pallas as pl
from jax.experimental.pallas import tpu as pltpu

F32 = jnp.float32
BF16 = jnp.bfloat16

D_MODEL = 1024
HEAD_DIM = 64
N_HEADS = D_MODEL // HEAD_DIM
D_FF = 2816
NORM_EPS = 1e-6
N_MIXERS = 3
REL_BUCKETS = 32
REL_MAX_DIST = 1024
A_KV_HEADS = 2
A_WINDOW = 128
B_KV_GROUPS = 4
CMP_LEN = 32
CMP_STRIDE = 16
CMP_HIDDEN = 256
SLC_BLK = 64
SLC_TOPK = 16
B_WINDOW = 512
MOBA_BLK = 256
MOBA_TOPK = 3
NEG = -1e30
TINY = 1e-30

LANES = 128
SUBLANES = 8
MXU_DIM = 256
VMEM_LIMIT_BYTES = 56 * 1024 * 1024

SLOTS = 4
HB_W = SLOTS * HEAD_DIM
assert HB_W == MXU_DIM
N_HB = N_HEADS // SLOTS
TQ = 256
TK = 256
TM = 512
FF_CHUNKS = ((0, 1536), (1536, D_FF))
GATE_ROWS = 16
SEL_BIG = float(2 ** 30)
SCALE = HEAD_DIM ** -0.5
LOG2E = math.log2(math.e)

_NT = (((1,), (1,)), ((), ()))


def _params(n_axes):
    return pltpu.CompilerParams(
        dimension_semantics=("arbitrary",) * n_axes,
        vmem_limit_bytes=VMEM_LIMIT_BYTES)


def _rms(x, g):
    return x * lax.rsqrt(jnp.mean(x * x, axis=-1, keepdims=True) + NORM_EPS) * g


def _sigmoid(x):
    return 1.0 / (1.0 + jnp.exp(-x))


def _div_pow2(x, d):
    assert d & (d - 1) == 0
    return jnp.right_shift(x, d.bit_length() - 1)


def _resident(shape):
    zeros = (0,) * len(shape)
    return pl.BlockSpec(shape, lambda *_: zeros, pipeline_mode=pl.Buffered(1))


def _ffn_kernel(*refs, n_mix):
    mix_refs, refs = refs[:n_mix], refs[n_mix:]
    if n_mix:
        wmix_ref, gmix_ref = refs[:2]
        refs = refs[2:]
    x_ref, gpre_ref, win_ref, wout_ref, gpost_ref, o_ref = refs
    x = x_ref[...]
    if n_mix:
        o = mix_refs[0][...]
        for r in mix_refs[1:]:
            o = o.astype(F32) + r[...].astype(F32)
        mixed = jnp.dot(o.astype(BF16), wmix_ref[...], preferred_element_type=F32)
        x = x + _rms(mixed, gmix_ref[...])
    hn = _rms(x, gpre_ref[...]).astype(BF16)
    y = None
    for lo, hi in FF_CHUNKS:
        gate = jnp.dot(hn, win_ref[:, lo:hi], preferred_element_type=F32)
        up = jnp.dot(hn, win_ref[:, D_FF + lo:D_FF + hi], preferred_element_type=F32)
        act = (gate * _sigmoid(gate) * up).astype(BF16)
        part = jnp.dot(act, wout_ref[lo:hi, :], preferred_element_type=F32)
        y = part if y is None else y + part
    o_ref[...] = x + 0.5 * _rms(y, gpost_ref[...])


def _ffn(xf, g_pre, w_in_all, w_out_all, which, g_post, mix=None):
    n, d = xf.shape
    tile = pl.BlockSpec((TM, d), lambda i: (i, 0))
    mix_specs, mix_args = [], []
    if mix is not None:
        o_list, w_mix, g_mix = mix
        mix_specs = [tile] * len(o_list) + [_resident(w_mix.shape), _resident((1, d))]
        mix_args = list(o_list) + [w_mix.astype(BF16), g_mix.reshape(1, d)]

    def picked(w_all):
        return pl.BlockSpec((None, None) + w_all.shape[2:], lambda i: which + (0, 0),
                            pipeline_mode=pl.Buffered(1))

    return pl.pallas_call(
        functools.partial(_ffn_kernel, n_mix=len(mix_args) - 2 if mix_args else 0),
        grid=(n // TM,),
        in_specs=mix_specs + [
            tile,
            _resident((1, d)),
            picked(w_in_all),
            picked(w_out_all),
            _resident((1, d)),
        ],
        out_specs=tile,
        out_shape=jax.ShapeDtypeStruct((n, d), F32),
        compiler_params=_params(1),
        name="ffn",
    )(*mix_args, xf, g_pre.reshape(1, d), w_in_all, w_out_all, g_post.reshape(1, d))


def _proj_kernel(x_ref, g_ref, wr_ref, wct_ref, rows_ref, cols_ref):
    hn = _rms(x_ref[...], g_ref[...]).astype(BF16)
    rows_ref[...] = jnp.dot(hn, wr_ref[...], preferred_element_type=F32)
    cols_ref[...] = lax.dot_general(wct_ref[...], hn, _NT,
                                    preferred_element_type=F32).astype(cols_ref.dtype)


def _proj(xf, g, w_rows, w_cols):
    n, d = xf.shape
    cr, cc = w_rows.shape[1], w_cols.shape[1]
    return pl.pallas_call(
        _proj_kernel,
        grid=(n // TM,),
        in_specs=[
            pl.BlockSpec((TM, d), lambda i: (i, 0)),
            _resident((1, d)),
            _resident((d, cr)),
            _resident((cc, d)),
        ],
        out_specs=[pl.BlockSpec((TM, cr), lambda i: (i, 0)),
                   pl.BlockSpec((cc, TM), lambda i: (0, i))],
        out_shape=[jax.ShapeDtypeStruct((n, cr), F32), jax.ShapeDtypeStruct((cc, n), BF16)],
        compiler_params=_params(1),
        name="proj",
    )(xf, g.reshape(1, d), w_rows.astype(BF16), w_cols.T.astype(BF16))


def _rel_bucket(dist):
    dist = jnp.maximum(dist, 0)
    exact = REL_BUCKETS // 2
    d = jnp.maximum(dist, 1).astype(jnp.float32)
    log_b = exact + (jnp.log(d / exact) / math.log(REL_MAX_DIST / exact)
                     * (REL_BUCKETS - exact)).astype(jnp.int32)
    return jnp.where(dist < exact, dist, jnp.minimum(log_b, REL_BUCKETS - 1))


def _tile_distance(n_delta, xp):
    delta = xp.arange(n_delta)[:, None, None]
    return delta * TK + xp.arange(TQ)[None, None, :] - xp.arange(TK)[None, :, None]


def _bias_of_distance(rel_bias, dist):
    bucket = _rel_bucket(dist).reshape(1, -1)
    onehot = (jnp.arange(REL_BUCKETS)[:, None] == bucket).astype(F32)
    tiles = jnp.dot(rel_bias.astype(F32).T, onehot, precision=lax.Precision.HIGHEST)
    return tiles.reshape((N_HEADS,) + dist.shape)


def _bias_tiles(rel_bias, n_delta):
    return _bias_of_distance(rel_bias, _tile_distance(n_delta, jnp))


def _band_distance(band_rows, xp):
    back = band_rows - TQ
    return back + xp.arange(TQ)[None, :] - xp.arange(band_rows + back)[:, None]


def _band_bias(rel_bias, n_groups, band_rows, window):
    bias = _bias_of_distance(rel_bias, _band_distance(band_rows, jnp))
    bias = _interleave_heads(bias, n_groups, 0)
    dist = _band_distance(band_rows, np)
    bias = jnp.where((dist >= 0) & (dist < window), bias * LOG2E, NEG)
    return bias.reshape((N_HB, SLOTS) + bias.shape[1:])


def _interleave_heads(w, n_groups, axis):
    r = N_HEADS // n_groups
    shape = w.shape
    w = w.reshape(shape[:axis] + (n_groups, r) + shape[axis + 1:])
    w = jnp.swapaxes(w, axis, axis + 1)
    return w.reshape(shape)


def _block_bias(bias_nat, n_groups, n_delta):
    bias = bias_nat[:, :n_delta]
    if n_groups > 1:
        bias = _interleave_heads(bias, n_groups, 0)
    bias = jnp.where(_tile_distance(n_delta, np) >= 0, bias * LOG2E, NEG)
    return bias.reshape((N_HB, SLOTS) + bias.shape[1:])


@dataclasses.dataclass(frozen=True)
class _AttnCfg:
    kv_width: int
    band_rows: int | None = None
    blocks_per_step: int = 1
    kv_per_block: bool = False
    sel_blk: int | None = None
    sel_from_input: bool = False
    gate_branch: int | None = None
    sink: bool = False

    @property
    def selecting(self):
        return self.sel_blk is not None

    @property
    def moba(self):
        return self.selecting and not self.sel_from_input


def _rank_rows(val):
    width = val.shape[1]
    row = lax.broadcasted_iota(jnp.int32, (1, width, 1), 1)
    rank = jnp.zeros(val.shape, F32)
    for m in range(width):
        vm = val[:, m:m + 1, :]
        rank = rank + jnp.where(row > m, jnp.where(vm >= val, 1.0, 0.0),
                                jnp.where(vm > val, 1.0, 0.0))
    return rank


def _u_chan0(hh, n_blk):
    return hh * n_blk + (LANES if hh < SLOTS // 2 else 0)


def _fold_rows(x, op):
    return op(x.reshape(x.shape[0] // SUBLANES, SUBLANES, x.shape[1]), axis=0)


def _attn_kernel(*refs, cfg: _AttnCfg):
    it = iter(refs)
    qt_ref, k_ref, vt_ref, bias_ref = next(it), next(it), next(it), next(it)
    sel_ref = next(it) if cfg.sel_from_input else None
    gate_ref = next(it) if cfg.gate_branch is not None else None
    sink_ref = next(it) if cfg.sink else None
    o_ref = next(it)
    kb_ref, vtb_ref = next(it), next(it)
    scratch_refs = ((next(it), next(it), next(it), next(it))
                    if cfg.band_rows is None else ())
    kmx_ref = next(it) if cfg.moba else None
    vt_tile = LANES if cfg.band_rows is not None else TK

    qt = pl.program_id(2)
    seq = k_ref.shape[1]
    n_tiles = seq // TK
    lane = lax.broadcasted_iota(jnp.int32, (1, HB_W), 1)
    slot_of_lane = _div_pow2(lane, HEAD_DIM)
    n_sel_blk = seq // cfg.sel_blk if cfg.selecting else 0

    n_kv = kb_ref.shape[0]
    kw = k_ref.shape[2] // n_kv

    @pl.when(qt == 0)
    def _prepare():
        if cfg.moba:
            kmx_ref[...] = jnp.zeros(kmx_ref.shape, F32)
        for kv, c in [(kv, c) for kv in range(n_kv) for c in range(n_tiles)]:
            rows = pl.ds(c * TK, TK)
            k = k_ref[0, rows, kv * kw:(kv + 1) * kw]
            vt = vt_ref[kv * kw:(kv + 1) * kw, c * TK:(c + 1) * TK]
            if kw == LANES:
                k = jnp.concatenate([k, k], axis=1)
                vt = jnp.concatenate([vt, vt], axis=0)
            for i in range(TK // vt_tile):
                vtb_ref[kv, c * (TK // vt_tile) + i] = (
                    vt[:, i * vt_tile:(i + 1) * vt_tile].astype(BF16))
            if cfg.selecting:
                key_blk = _div_pow2(c * TK + lax.broadcasted_iota(jnp.int32, (TK, 1), 0),
                                    cfg.sel_blk)
                for hh in range(SLOTS):
                    onehot = (lane - _u_chan0(hh, n_sel_blk)) == key_blk
                    kb_ref[kv, hh, rows, :] = jnp.where(
                        slot_of_lane == hh, k, jnp.where(onehot, 1.0, 0.0)).astype(BF16)
            else:
                kb_ref[kv, 0, rows, :] = k.astype(BF16)
            if cfg.moba:
                k_mean = jnp.mean(k, axis=0, keepdims=True)
                for hh in range(SLOTS):
                    kmx_ref[kv, pl.ds(hh * n_sel_blk + c, 1), :] = jnp.where(
                        slot_of_lane == hh, k_mean, 0.0)

    for blk in range(qt_ref.shape[0] // HB_W):
        kv = blk if n_kv > 1 else 0
        refs_of_block = (qt_ref, bias_ref, sel_ref, gate_ref, sink_ref, o_ref, kb_ref.at[kv],
                         vtb_ref.at[kv], scratch_refs,
                         kmx_ref.at[kv] if cfg.moba else None)
        _attend_block(cfg, blk, qt, n_sel_blk, refs_of_block)


def _attend_block(cfg, blk, qt, n_sel_blk, refs):
    (qt_ref, bias_ref, sel_ref, gate_ref, sink_ref, o_ref, kb_ref, vtb_ref, scratch_refs,
     kmx_ref) = refs
    bias_blk = bias_ref.at[blk]
    q_t = qt_ref[blk * HB_W:(blk + 1) * HB_W, :].astype(F32)

    sel_rows = None
    if cfg.sel_from_input:
        sel_rows = sel_ref[0]
    elif cfg.moba:
        gate = jnp.dot(kmx_ref[...].astype(BF16), q_t.astype(BF16),
                       preferred_element_type=F32)
        gate = gate.reshape(SLOTS, n_sel_blk, TQ)
        key_blk = lax.broadcasted_iota(jnp.int32, (1, n_sel_blk, 1), 1)
        val = jnp.where(key_blk < qt, gate, NEG)
        rank = _rank_rows(val)
        chosen = jnp.where(key_blk < qt, jnp.where(rank < MOBA_TOPK, 1.0, 0.0), 0.0)
        chosen = jnp.where(key_blk == qt, 1.0, chosen)
        sel_rows = chosen.reshape(SLOTS * n_sel_blk, TQ)

    q_scaled = q_t * (SCALE * LOG2E)
    qms = []
    for hh in range(SLOTS):
        pieces = {hh * HEAD_DIM: q_scaled[hh * HEAD_DIM:(hh + 1) * HEAD_DIM]}
        if cfg.selecting:
            pieces[_u_chan0(hh, n_sel_blk)] = (
                sel_rows[hh * n_sel_blk:(hh + 1) * n_sel_blk] - 1.0) * SEL_BIG
        segs, row = [], 0
        for start in sorted(pieces):
            if start > row:
                segs.append(jnp.zeros((start - row, TQ), F32))
            segs.append(pieces[start])
            row = start + pieces[start].shape[0]
        if row < HB_W:
            segs.append(jnp.zeros((HB_W - row, TQ), F32))
        qms.append(jnp.concatenate(segs, axis=0).astype(BF16))
    q_stack = None if cfg.selecting else jnp.concatenate(qms, axis=1)

    sinks = ([sink_ref[blk, :, hh:hh + 1] * LOG2E for hh in range(SLOTS)]
             if cfg.sink else None)
    if cfg.band_rows is not None:
        ms, sums, accs = _band_softmax(cfg, qt, q_stack, kb_ref, vtb_ref, bias_blk, sinks)
    else:
        ms, sums, accs = _tiled_softmax(cfg, qt, qms, q_stack, kb_ref, vtb_ref, bias_blk,
                                        sinks, *scratch_refs)

    gsig = None
    if cfg.gate_branch is not None:
        gsig = _sigmoid(
            gate_ref[blk * GATE_ROWS:(blk + 1) * GATE_ROWS, :].astype(F32))
    outs = []
    for hh in range(SLOTS):
        denom = jnp.sum(sums[hh], axis=0, keepdims=True)
        if cfg.sink:
            denom = denom + jnp.exp2(sinks[hh] - ms[hh])
        o_h = accs[hh] / jnp.maximum(denom, TINY)
        if cfg.gate_branch is not None:
            row = cfg.gate_branch * SLOTS + hh
            o_h = o_h * gsig[row:row + 1, :]
        outs.append(o_h)
    o_ref[0, :, blk * HB_W:(blk + 1) * HB_W] = (
        jnp.concatenate(outs, axis=0).T.astype(o_ref.dtype))


def _band_softmax(cfg, qt, q_stack, kb_ref, vtb_ref, bias_ref, sinks):
    band = cfg.band_rows
    back = band - TQ
    start = pl.multiple_of(jnp.maximum(qt * TQ - back, 0), LANES)
    s_all = jnp.dot(kb_ref[0, pl.ds(start, band), :], q_stack, preferred_element_type=F32)
    bias_rows = pl.ds(pl.multiple_of(back - (qt * TQ - start), LANES), band)
    tile0 = _div_pow2(start, LANES)
    ms, sums, accs = [], [], []
    for hh in range(SLOTS):
        head = pl.ds(hh * HEAD_DIM, HEAD_DIM)
        sc = s_all[:, hh * TQ:(hh + 1) * TQ] + bias_ref[hh, bias_rows, :]
        m = jnp.max(_fold_rows(sc, jnp.max), axis=0, keepdims=True)
        if cfg.sink:
            m = jnp.maximum(m, sinks[hh])
        p = jnp.exp2(sc - m)
        vt = jnp.concatenate([vtb_ref[tile0 + c, head, :] for c in range(band // LANES)],
                             axis=1)
        ms.append(m)
        sums.append(_fold_rows(p, jnp.sum))
        accs.append(jnp.dot(vt, p.astype(BF16), preferred_element_type=F32))
    return ms, sums, accs


def _tiled_softmax(cfg, qt, qms, q_stack, kb_ref, vtb_ref, bias_ref, sinks,
                   sc_ref, m_ref, l_ref, acc_ref):
    m_ref[...] = jnp.full(m_ref.shape, NEG, F32)
    l_ref[...] = jnp.zeros(l_ref.shape, F32)
    acc_ref[...] = jnp.zeros(acc_ref.shape, F32)

    def pass1(kt, width):
        rows = pl.ds(pl.multiple_of(kt * TK, TK), width * TK)
        if cfg.selecting:
            s = [jnp.dot(kb_ref[hh, rows, :], qms[hh], preferred_element_type=F32)
                 for hh in range(SLOTS)]
        else:
            s_all = jnp.dot(kb_ref[0, rows, :], q_stack, preferred_element_type=F32)
            s = [s_all[:, hh * TQ:(hh + 1) * TQ] for hh in range(SLOTS)]
        for hh in range(SLOTS):
            m_new = m_ref[hh]
            for w in range(width):
                tile = s[hh][w * TK:(w + 1) * TK] + bias_ref[hh, qt - kt - w]
                sc_ref[hh, kt + w] = tile
                m_new = jnp.maximum(m_new, _fold_rows(tile, jnp.max))
            m_ref[hh] = m_new

    def pass2(kt, width, ms):
        for hh in range(SLOTS):
            head = pl.ds(hh * HEAD_DIM, HEAD_DIM)
            l_new = l_ref[hh]
            o_new = acc_ref[head, :]
            for w in range(width):
                p = jnp.exp2(sc_ref[hh, kt + w] - ms[hh])
                l_new = l_new + _fold_rows(p, jnp.sum)
                o_new = o_new + jnp.dot(vtb_ref[kt + w, head, :], p.astype(BF16),
                                        preferred_element_type=F32)
            l_ref[hh] = l_new
            acc_ref[head, :] = o_new

    def over_tiles(body):
        n = qt + 1
        start = 0
        for width in (1, 2):
            has = jnp.bitwise_and(n, width)

            @pl.when(has != 0)
            def _part(start=start, width=width):
                body(start, width)

            start = start + has

        def group(i, carry):
            body(start + 4 * i, 4)
            return carry

        lax.fori_loop(0, jnp.right_shift(n, 2), group, 0)

    over_tiles(pass1)
    ms = []
    for hh in range(SLOTS):
        m = jnp.max(m_ref[hh], axis=0, keepdims=True)
        ms.append(jnp.maximum(m, sinks[hh]) if cfg.sink else m)
    over_tiles(functools.partial(pass2, ms=ms))
    return (ms, [l_ref[hh] for hh in range(SLOTS)],
            [acc_ref[hh * HEAD_DIM:(hh + 1) * HEAD_DIM, :] for hh in range(SLOTS)])


def _attention(cfg, rows, cols, bias, *, k_blk, v_rblk, sel=None, gate_rblk0=None, sinks=None):
    bsz, seq, _ = rows.shape
    n_delta = bias.shape[2]
    n_q = seq // TQ
    nb = cfg.blocks_per_step
    n_kv = nb if cfg.kv_per_block else 1
    kw = n_kv * cfg.kv_width
    bias_tail = (0,) * (bias.ndim - 1)
    in_specs = [
        pl.BlockSpec((nb * HB_W, TQ), lambda hb, b, qt: (hb, b * n_q + qt)),
        pl.BlockSpec((1, seq, kw), lambda hb, b, qt: (b, 0, k_blk(hb))),
        pl.BlockSpec((kw, seq), lambda hb, b, qt: (v_rblk(hb), b)),
        pl.BlockSpec((nb,) + bias.shape[1:], lambda hb, b, qt: (hb,) + bias_tail,
                     pipeline_mode=pl.Buffered(1)),
    ]
    args = [cols, rows, cols, bias]
    if cfg.sel_from_input:
        in_specs.append(pl.BlockSpec((1, LANES, TQ), lambda hb, b, qt: (b, 0, qt)))
        args.append(sel)
    if cfg.gate_branch is not None:
        assert gate_rblk0 % nb == 0
        in_specs.append(pl.BlockSpec(
            (nb * GATE_ROWS, TQ), lambda hb, b, qt: (gate_rblk0 // nb + hb, b * n_q + qt)))
        args.append(cols)
    if cfg.sink:
        in_specs.append(pl.BlockSpec((nb, 1, LANES), lambda hb, b, qt: (hb, 0, 0)))
        args.append(sinks)
    vt_tile = LANES if cfg.band_rows is not None else TK
    scratch = [
        pltpu.VMEM((n_kv, SLOTS if cfg.selecting else 1, seq, HB_W), BF16),
        pltpu.VMEM((n_kv, seq // vt_tile, HB_W, vt_tile), BF16),
    ]
    if cfg.band_rows is None:
        scratch += [
            pltpu.VMEM((SLOTS, n_delta, TK, TQ), F32),
            pltpu.VMEM((SLOTS, SUBLANES, TQ), F32),
            pltpu.VMEM((SLOTS, SUBLANES, TQ), F32),
            pltpu.VMEM((HB_W, TQ), F32),
        ]
    if cfg.moba:
        assert SLOTS * (seq // cfg.sel_blk) <= LANES
        scratch.append(
            pltpu.VMEM((n_kv, SLOTS * (seq // cfg.sel_blk), HB_W), F32))
    return pl.pallas_call(
        functools.partial(_attn_kernel, cfg=cfg),
        grid=(N_HB // nb, bsz, n_q),
        in_specs=in_specs,
        out_specs=pl.BlockSpec((1, TQ, nb * HB_W), lambda hb, b, qt: (b, qt, hb)),
        out_shape=jax.ShapeDtypeStruct((bsz, seq, D_MODEL), BF16),
        scratch_shapes=scratch,
        compiler_params=_params(3),
        name="attention",
    )(*args)


def _gelu_tanh(x):
    cdf = 0.5 * (1.0 + jnp.tanh(math.sqrt(2.0 / math.pi) * (x + 0.044715 * (x * x * x))))
    return x * cdf


def _compress_kernel(kc_lo_ref, kc_hi_ref, vc_lo_ref, vc_hi_ref, pos_ref, w1_ref, w2k_ref,
                     w2vt_ref, kcb_ref, vcbt_ref):
    half = CMP_LEN // 2
    n_chunk = kc_lo_ref.shape[1] // CMP_STRIDE
    flat_lane = lax.broadcasted_iota(jnp.int32, (1, half * HB_W), 1)
    group_of_lane = _div_pow2(jnp.bitwise_and(flat_lane, HB_W - 1), HEAD_DIM)
    for kind, (lo_ref, hi_ref) in enumerate(((kc_lo_ref, kc_hi_ref), (vc_lo_ref, vc_hi_ref))):
        chunks = [jnp.concatenate(
            [lo_ref[0, pl.ds(l, n_chunk, stride=CMP_STRIDE), :],
             hi_ref[0, pl.ds(l, n_chunk, stride=CMP_STRIDE), :]], axis=1) for l in range(half)]
        parts = []
        for h in range(2):
            pieces = [chunks[l] + pos_ref[kind, pl.ds(h * half + l, 1), :] for l in range(half)]
            flat = jnp.concatenate(pieces, axis=1)
            stacked = jnp.concatenate(
                [jnp.where(group_of_lane == g, flat, 0.0)
                 for g in range(B_KV_GROUPS)], axis=0).astype(BF16)
            parts.append(jnp.dot(stacked, w1_ref[kind, h], preferred_element_type=F32))
        rows = B_KV_GROUPS * n_chunk
        hidden = parts[0] + pltpu.roll(parts[1], rows - 1, axis=0)
        act = _gelu_tanh(hidden).astype(BF16)
        out = None
        for g in range(B_KV_GROUPS):
            act_g = act[g * n_chunk:(g + 1) * n_chunk]
            if kind == 0:
                o = jnp.dot(act_g, w2k_ref[g], preferred_element_type=F32)
            else:
                o = lax.dot_general(w2vt_ref[g], act_g, _NT, preferred_element_type=F32)
            out = o if out is None else out + o
        if kind == 0:
            kcb_ref[0] = out
        else:
            vcbt_ref[0] = out


def _compress(rows, kc_blk, vc_blk, pos, w1, w2k, w2vt):
    bsz, seq, _ = rows.shape
    n_chunk = seq // CMP_STRIDE
    return pl.pallas_call(
        _compress_kernel,
        grid=(bsz,),
        in_specs=[
            pl.BlockSpec((1, seq, LANES), lambda b: (b, 0, 2 * kc_blk)),
            pl.BlockSpec((1, seq, LANES), lambda b: (b, 0, 2 * kc_blk + 1)),
            pl.BlockSpec((1, seq, LANES), lambda b: (b, 0, 2 * vc_blk)),
            pl.BlockSpec((1, seq, LANES), lambda b: (b, 0, 2 * vc_blk + 1)),
            _resident(pos.shape),
            _resident(w1.shape),
            _resident(w2k.shape),
            _resident(w2vt.shape),
        ],
        out_specs=[pl.BlockSpec((1, n_chunk, HB_W), lambda b: (b, 0, 0)),
                   pl.BlockSpec((1, HB_W, n_chunk), lambda b: (b, 0, 0))],
        out_shape=[jax.ShapeDtypeStruct((bsz, n_chunk, HB_W), F32),
                   jax.ShapeDtypeStruct((bsz, HB_W, n_chunk), F32)],
        compiler_params=_params(1),
        name="nsa_compress",
    )(rows, rows, rows, rows, pos, w1, w2k, w2vt)


def _cmp_kernel(qt_ref, kcb_ref, vcbt_ref, gate_ref, ovt_ref, o_ref, sel_ref):
    qt = pl.program_id(1)
    n_cmp_pad = kcb_ref.shape[1]
    n_slc = LANES // B_KV_GROUPS
    chan = lax.broadcasted_iota(jnp.int32, (HB_W, 1), 0)
    slot_of_chan = _div_pow2(chan, HEAD_DIM)
    t = qt * TQ + lax.broadcasted_iota(jnp.int32, (1, TQ), 1)
    cmp_end = (lax.broadcasted_iota(jnp.int32, (n_cmp_pad, 1), 0) * CMP_STRIDE + CMP_LEN - 1)
    ok = cmp_end <= jnp.concatenate([t] * N_HB, axis=1)
    kc = kcb_ref[0].astype(BF16)
    vct = vcbt_ref[0].astype(BF16)
    gsig = _sigmoid(gate_ref[...].astype(F32))
    outs = [[None] * B_KV_GROUPS for _ in range(N_HB)]
    imp = None
    for g in range(B_KV_GROUPS):
        own = jnp.concatenate(
            [qt_ref[r * HB_W + g * HEAD_DIM:r * HB_W + (g + 1) * HEAD_DIM, :].astype(F32)
             for r in range(N_HB)], axis=1) * SCALE
        qm = jnp.concatenate(
            [jnp.zeros((g * HEAD_DIM, N_HB * TQ), F32), own,
             jnp.zeros(((B_KV_GROUPS - 1 - g) * HEAD_DIM, N_HB * TQ), F32)][
                 (1 if g == 0 else 0):(2 if g == B_KV_GROUPS - 1 else 3)], axis=0)
        s = jnp.dot(kc, qm.astype(BF16), preferred_element_type=F32)
        s = jnp.where(ok, s, NEG)
        m = jnp.max(s, axis=0, keepdims=True)
        e = jnp.where(ok, jnp.exp(s - m), 0.0)
        p = e / jnp.maximum(jnp.sum(e, axis=0, keepdims=True), TINY)
        o = jnp.dot(vct[g * HEAD_DIM:(g + 1) * HEAD_DIM], p.astype(BF16),
                    preferred_element_type=F32)
        p_group = None
        for r in range(N_HB):
            p_r = p[:, r * TQ:(r + 1) * TQ]
            p_group = p_r if p_group is None else p_group + p_r
            row = r * GATE_ROWS + g
            outs[r][g] = o[:, r * TQ:(r + 1) * TQ] * gsig[row:row + 1, :]
        part = jnp.dot(ovt_ref[g], p_group, precision=lax.Precision.HIGHEST,
                       preferred_element_type=F32)
        imp = part if imp is None else imp + part
    for r in range(N_HB):
        o_ref[0, :, r * HB_W:(r + 1) * HB_W] = (
            jnp.concatenate(outs[r], axis=0).T.astype(o_ref.dtype))
    imp = imp.reshape(B_KV_GROUPS, n_slc, TQ)
    j = lax.broadcasted_iota(jnp.int32, (1, n_slc, 1), 1)
    cur = _div_pow2(t, SLC_BLK).reshape(1, 1, TQ)
    val = jnp.where(j == cur - 1, -NEG, imp)
    val = jnp.where(j == cur, -NEG, val)
    val = jnp.where(j == 0, -NEG, val)
    val = jnp.where(j > cur, NEG, val)
    chosen = jnp.where(_rank_rows(val) < SLC_TOPK, 1.0, 0.0)
    sel_ref[0] = chosen.reshape(LANES, TQ)


def _cmp_attention(cols, kcb, vcbt, gate_rblk, ovt, bsz, seq):
    n_cmp_pad = kcb.shape[1]
    n_q = seq // TQ
    return pl.pallas_call(
        _cmp_kernel,
        grid=(bsz, n_q),
        in_specs=[
            pl.BlockSpec((D_MODEL, TQ), lambda b, qt: (0, b * n_q + qt)),
            pl.BlockSpec((1, n_cmp_pad, HB_W), lambda b, qt: (b, 0, 0)),
            pl.BlockSpec((1, HB_W, n_cmp_pad), lambda b, qt: (b, 0, 0)),
            pl.BlockSpec((N_HB * GATE_ROWS, TQ), lambda b, qt: (gate_rblk, b * n_q + qt)),
            _resident(ovt.shape),
        ],
        out_specs=[
            pl.BlockSpec((1, TQ, D_MODEL), lambda b, qt: (b, qt, 0)),
            pl.BlockSpec((1, LANES, TQ), lambda b, qt: (b, 0, qt)),
        ],
        out_shape=[
            jax.ShapeDtypeStruct((bsz, seq, D_MODEL), BF16),
            jax.ShapeDtypeStruct((bsz, LANES, seq), F32),
        ],
        compiler_params=_params(2),
        name="nsa_cmp_select",
    )(cols, kcb, vcbt, cols, ovt)


def _mixer_a(project, bsz, seq, w_in, w_out, sinks, bias_nat, rel_bias):
    qw, kvw = N_HEADS * HEAD_DIM, A_KV_HEADS * HEAD_DIM
    w_q = _interleave_heads(w_in[:, :qw].reshape(D_MODEL, N_HEADS, HEAD_DIM), A_KV_HEADS, 1)
    w_cols = jnp.concatenate([w_q.reshape(D_MODEL, qw), w_in[:, qw + kvw:]], axis=1)
    rows, cols = project(w_in[:, qw:qw + kvw], w_cols)
    rows = rows.reshape(bsz, seq, kvw)
    band_rows = TQ + -(-(A_WINDOW - 1) // LANES) * LANES
    assert band_rows <= seq
    bias = _band_bias(rel_bias, A_KV_HEADS, band_rows, A_WINDOW)
    sink_rows = _interleave_heads(sinks.astype(F32), A_KV_HEADS, 0).reshape(N_HB, 1, SLOTS)
    sink_rows = jnp.pad(sink_rows, ((0, 0), (0, 0), (0, LANES - SLOTS)))
    cfg = _AttnCfg(kv_width=kvw, band_rows=band_rows, sink=True, blocks_per_step=N_HB)
    o = _attention(cfg, rows, cols, bias, k_blk=lambda hb: 0, v_rblk=lambda hb: qw // kvw,
                   sinks=sink_rows)
    w_o = _interleave_heads(w_out.reshape(N_HEADS, HEAD_DIM, D_MODEL), A_KV_HEADS, 0)
    return [o.reshape(bsz * seq, -1)], w_o.reshape(qw, D_MODEL)


def _nsa_overlap_t(seq):
    n_chunk = seq // CMP_STRIDE
    n_cmp = (seq - CMP_LEN) // CMP_STRIDE + 1
    n_slc = seq // SLC_BLK
    c_start = np.arange(n_cmp)[:, None] * CMP_STRIDE
    s_start = np.arange(n_slc)[None, :] * SLC_BLK
    overlap = ((c_start < s_start + SLC_BLK) & (c_start + CMP_LEN > s_start)).astype(np.float32)
    ovt = np.zeros((B_KV_GROUPS, LANES, n_chunk), np.float32)
    for g in range(B_KV_GROUPS):
        ovt[g, g * n_slc:(g + 1) * n_slc, :n_cmp] = overlap.T
    return jnp.asarray(ovt)


def _mixer_b(project, bsz, seq, w_in, w_out, cmp_pos, cmp_w1, cmp_w2, bias_nat, rel_bias):
    assert seq // SLC_BLK * B_KV_GROUPS == LANES and seq // CMP_STRIDE == LANES
    grp, dh = B_KV_GROUPS, HEAD_DIM
    qw, kvw = N_HEADS * dh, B_KV_GROUPS * dh
    w_q = _interleave_heads(w_in[:, :qw].reshape(D_MODEL, N_HEADS, dh), grp, 1)
    kv = [w_in[:, qw + i * kvw:qw + (i + 1) * kvw] for i in range(6)]
    w_g = w_in[:, qw + 6 * kvw:].reshape(D_MODEL, grp, N_HB, 3)
    w_g = w_g.transpose(0, 2, 3, 1).reshape(D_MODEL, N_HB, 3 * grp)
    w_g = jnp.pad(w_g, ((0, 0), (0, 0), (0, GATE_ROWS - 3 * grp)))
    w_rows = jnp.concatenate([kv[0], kv[1], kv[2], kv[4]], axis=1)
    w_cols = jnp.concatenate([w_q.reshape(D_MODEL, qw), kv[3], kv[5],
                              w_g.reshape(D_MODEL, N_HB * GATE_ROWS)], axis=1)
    rows, cols = project(w_rows, w_cols)
    rows = rows.reshape(bsz, seq, 4 * kvw)
    gate_row0 = qw + 2 * kvw

    half = CMP_LEN // 2
    pos = jnp.tile(cmp_pos.astype(F32), (1, 1, grp))
    w1 = cmp_w1.reshape(2, 2, half, 1, dh, CMP_HIDDEN)
    w1 = jnp.broadcast_to(w1, (2, 2, half, grp, dh, CMP_HIDDEN))
    w1 = w1.reshape(2, 2, half * HB_W, CMP_HIDDEN).astype(BF16)
    w2 = jnp.zeros((2, grp, CMP_HIDDEN, grp, dh), F32)
    for g in range(grp):
        w2 = w2.at[:, g, :, g, :].set(cmp_w2)
    w2 = w2.reshape(2, grp, CMP_HIDDEN, HB_W).astype(BF16)
    kcb, vcbt = _compress(rows, 0, 1, pos, w1, w2[0], jnp.swapaxes(w2[1], 1, 2))

    o_cmp, sel = _cmp_attention(cols, kcb, vcbt, gate_row0 // (N_HB * GATE_ROWS),
                                _nsa_overlap_t(seq), bsz, seq)

    gate_rblk0 = gate_row0 // GATE_ROWS
    bias = _block_bias(bias_nat, grp, seq // TK)
    cfg_slc = _AttnCfg(kv_width=kvw, sel_blk=SLC_BLK, sel_from_input=True, gate_branch=1,
                       blocks_per_step=2)
    o_slc = _attention(cfg_slc, rows, cols, bias, k_blk=lambda hb: 2,
                       v_rblk=lambda hb: qw // kvw, sel=sel, gate_rblk0=gate_rblk0)
    band_rows = TQ + -(-(B_WINDOW - 1) // LANES) * LANES
    assert band_rows <= seq
    cfg_win = _AttnCfg(kv_width=kvw, band_rows=band_rows, gate_branch=2, blocks_per_step=N_HB)
    bias_win = _band_bias(rel_bias, grp, band_rows, B_WINDOW)
    o_win = _attention(cfg_win, rows, cols, bias_win, k_blk=lambda hb: 3,
                       v_rblk=lambda hb: qw // kvw + 1, gate_rblk0=gate_rblk0)
    w_o = _interleave_heads(w_out.reshape(N_HEADS, dh, D_MODEL), grp, 0).reshape(qw, D_MODEL)
    n = bsz * seq
    return [o_cmp.reshape(n, -1), o_slc.reshape(n, -1), o_win.reshape(n, -1)], w_o


def _mixer_c(project, bsz, seq, w_in, w_out, bias_nat, rel_bias):
    assert seq % MOBA_BLK == 0 and MOBA_BLK == TK
    qw = N_HEADS * HEAD_DIM
    w_cols = jnp.concatenate([w_in[:, :qw], w_in[:, 2 * qw:]], axis=1)
    rows, cols = project(w_in[:, qw:2 * qw], w_cols)
    rows = rows.reshape(bsz, seq, qw)
    bias = _block_bias(bias_nat, 1, seq // TK)
    nb = 2
    cfg = _AttnCfg(kv_width=HB_W, sel_blk=MOBA_BLK, blocks_per_step=nb, kv_per_block=True)
    o = _attention(cfg, rows, cols, bias, k_blk=lambda hb: hb,
                   v_rblk=lambda hb: N_HB // nb + hb)
    return [o.reshape(bsz * seq, -1)], w_out


def _layer(xf, bsz, seq, layer, g, ffn_w_in, ffn_w_out, mixer, mixer_params, bias_nat,
           rel_bias):
    xf = _ffn(xf, g[0], ffn_w_in, ffn_w_out, (layer, 0), g[1])
    project = functools.partial(_proj, xf, g[2])
    o_list, w_mix = mixer(project, bsz, seq, *mixer_params, bias_nat, rel_bias)
    return _ffn(xf, g[4], ffn_w_in, ffn_w_out, (layer, 1), g[5], mix=(o_list, w_mix, g[3]))


def kernel(x, rel_bias, norm_g, ffn_w_in, ffn_w_out, a_w_in, a_w_out, a_sinks, b_w_in, b_w_out,
           b_cmp_pos, b_cmp_w1, b_cmp_w2, c_w_in, c_w_out):
    bsz, seq, d = x.shape
    assert d == D_MODEL and seq % TQ == 0 and (bsz * seq) % TM == 0
    depth = norm_g.shape[0]
    bias_nat = _bias_tiles(rel_bias, seq // TK)
    ffn_w_in, ffn_w_out = ffn_w_in.astype(BF16), ffn_w_out.astype(BF16)
    xf = x.reshape(bsz * seq, d)
    for i in range(depth):
        kind, j = i % N_MIXERS, i // N_MIXERS
        if kind == 0:
            mixer, params = _mixer_a, (a_w_in[j], a_w_out[j], a_sinks[j])
        elif kind == 1:
            mixer, params = _mixer_b, (b_w_in[j], b_w_out[j], b_cmp_pos[j], b_cmp_w1[j],
                                       b_cmp_w2[j])
        else:
            mixer, params = _mixer_c, (c_w_in[j], c_w_out[j])
        xf = _layer(xf, bsz, seq, i, norm_g[i], ffn_w_in, ffn_w_out, mixer, params,
                    bias_nat, rel_bias)
    return xf.reshape(bsz, seq, d)
```

```python
import dataclasses
import functools
import math

import numpy as np
import jax
import jax.numpy as jnp
from jax import lax
from jax.experimental import pallas as pl
from jax.experimental.pallas import tpu as pltpu

F32 = jnp.float32
BF16 = jnp.bfloat16

D_MODEL = 1024
HEAD_DIM = 64
N_HEADS = D_MODEL // HEAD_DIM
D_FF = 2816
NORM_EPS = 1e-6
N_MIXERS = 3
REL_BUCKETS = 32
REL_MAX_DIST = 1024
A_KV_HEADS = 2
A_WINDOW = 128
B_KV_GROUPS = 4
CMP_LEN = 32
CMP_STRIDE = 16
CMP_HIDDEN = 256
SLC_BLK = 64
SLC_TOPK = 16
B_WINDOW = 512
MOBA_BLK = 256
MOBA_TOPK = 3
NEG = -1e30
TINY = 1e-30

LANES = 128
SUBLANES = 8
MXU_DIM = 256
VMEM_LIMIT_BYTES = 56 * 1024 * 1024

SLOTS = 4
HB_W = SLOTS * HEAD_DIM
assert HB_W == MXU_DIM
N_HB = N_HEADS // SLOTS
TQ = 256
TK = 256
TM = 512
TM_PROJ = 1024
FF_CHUNKS = ((0, 1536), (1536, D_FF))
GATE_ROWS = 16
SEL_BIG = float(2 ** 30)
SCALE = HEAD_DIM ** -0.5
LOG2E = math.log2(math.e)

_NT = (((1,), (1,)), ((), ()))


def _params(n_axes):
    return pltpu.CompilerParams(
        dimension_semantics=("arbitrary",) * n_axes,
        vmem_limit_bytes=VMEM_LIMIT_BYTES)


def _rms(x, g):
    return x * lax.rsqrt(jnp.mean(x * x, axis=-1, keepdims=True) + NORM_EPS) * g


def _sigmoid(x):
    return 1.0 / (1.0 + jnp.exp(-x))


def _div_pow2(x, d):
    assert d & (d - 1) == 0
    return jnp.right_shift(x, d.bit_length() - 1)


def _resident(shape):
    zeros = (0,) * len(shape)
    return pl.BlockSpec(shape, lambda *_: zeros, pipeline_mode=pl.Buffered(1))


def _ffn_kernel(*refs, n_mix):
    mix_refs, refs = refs[:n_mix], refs[n_mix:]
    if n_mix:
        wmix_ref, gmix_ref = refs[:2]
        refs = refs[2:]
    x_ref, gpre_ref, win_ref, wout_ref, gpost_ref, o_ref = refs
    x = x_ref[...]
    if n_mix:
        o = mix_refs[0][...]
        for r in mix_refs[1:]:
            o = o.astype(F32) + r[...].astype(F32)
        mixed = jnp.dot(o.astype(BF16), wmix_ref[...], preferred_element_type=F32)
        x = x + _rms(mixed, gmix_ref[...])
    hn = _rms(x, gpre_ref[...]).astype(BF16)
    y = None
    for lo, hi in FF_CHUNKS:
        gate = jnp.dot(hn, win_ref[:, lo:hi], preferred_element_type=F32)
        up = jnp.dot(hn, win_ref[:, D_FF + lo:D_FF + hi], preferred_element_type=F32)
        act = (gate * _sigmoid(gate) * up).astype(BF16)
        part = jnp.dot(act, wout_ref[lo:hi, :], preferred_element_type=F32)
        y = part if y is None else y + part
    o_ref[...] = x + 0.5 * _rms(y, gpost_ref[...])


def _ffn(xf, g_pre, w_in_all, w_out_all, which, g_post, mix=None):
    n, d = xf.shape
    tile = pl.BlockSpec((TM, d), lambda i: (i, 0))
    mix_specs, mix_args = [], []
    if mix is not None:
        o_list, w_mix, g_mix = mix
        mix_specs = [tile] * len(o_list) + [_resident(w_mix.shape), _resident((1, d))]
        mix_args = list(o_list) + [w_mix.astype(BF16), g_mix.reshape(1, d)]

    def picked(w_all):
        return pl.BlockSpec((None, None) + w_all.shape[2:], lambda i: which + (0, 0),
                            pipeline_mode=pl.Buffered(1))

    return pl.pallas_call(
        functools.partial(_ffn_kernel, n_mix=len(mix_args) - 2 if mix_args else 0),
        grid=(n // TM,),
        in_specs=mix_specs + [
            tile,
            _resident((1, d)),
            picked(w_in_all),
            picked(w_out_all),
            _resident((1, d)),
        ],
        out_specs=tile,
        out_shape=jax.ShapeDtypeStruct((n, d), F32),
        compiler_params=_params(1),
        name="ffn",
    )(*mix_args, xf, g_pre.reshape(1, d), w_in_all, w_out_all, g_post.reshape(1, d))


def _proj_kernel(x_ref, g_ref, wr_ref, wct_ref, rows_ref, cols_ref):
    hn = _rms(x_ref[...], g_ref[...]).astype(BF16)
    rows_ref[...] = jnp.dot(hn, wr_ref[...], preferred_element_type=F32)
    cols_ref[...] = lax.dot_general(wct_ref[...], hn, _NT,
                                    preferred_element_type=F32).astype(cols_ref.dtype)


def _proj(xf, g, w_rows, w_cols):
    n, d = xf.shape
    cr, cc = w_rows.shape[1], w_cols.shape[1]
    return pl.pallas_call(
        _proj_kernel,
        grid=(n // TM_PROJ,),
        in_specs=[
            pl.BlockSpec((TM_PROJ, d), lambda i: (i, 0)),
            _resident((1, d)),
            _resident((d, cr)),
            _resident((cc, d)),
        ],
        out_specs=[pl.BlockSpec((TM_PROJ, cr), lambda i: (i, 0)),
                   pl.BlockSpec((cc, TM_PROJ), lambda i: (0, i))],
        out_shape=[jax.ShapeDtypeStruct((n, cr), F32), jax.ShapeDtypeStruct((cc, n), BF16)],
        compiler_params=_params(1),
        name="proj",
    )(xf, g.reshape(1, d), w_rows.astype(BF16), w_cols.T.astype(BF16))


def _rel_bucket(dist):
    dist = jnp.maximum(dist, 0)
    exact = REL_BUCKETS // 2
    d = jnp.maximum(dist, 1).astype(jnp.float32)
    log_b = exact + (jnp.log(d / exact) / math.log(REL_MAX_DIST / exact)
                     * (REL_BUCKETS - exact)).astype(jnp.int32)
    return jnp.where(dist < exact, dist, jnp.minimum(log_b, REL_BUCKETS - 1))


def _tile_distance(n_delta, xp):
    delta = xp.arange(n_delta)[:, None, None]
    return delta * TK + xp.arange(TQ)[None, None, :] - xp.arange(TK)[None, :, None]


def _bias_of_distance(rel_bias, dist):
    bucket = _rel_bucket(dist).reshape(1, -1)
    onehot = (jnp.arange(REL_BUCKETS)[:, None] == bucket).astype(F32)
    tiles = jnp.dot(rel_bias.astype(F32).T, onehot, precision=lax.Precision.HIGHEST)
    return tiles.reshape((N_HEADS,) + dist.shape)


def _bias_tiles(rel_bias, n_delta):
    return _bias_of_distance(rel_bias, _tile_distance(n_delta, jnp))


def _band_distance(band_rows, xp):
    back = band_rows - TQ
    return back + xp.arange(TQ)[None, :] - xp.arange(band_rows + back)[:, None]


def _band_bias(rel_bias, n_groups, band_rows, window):
    bias = _bias_of_distance(rel_bias, _band_distance(band_rows, jnp))
    bias = _interleave_heads(bias, n_groups, 0)
    dist = _band_distance(band_rows, np)
    bias = jnp.where((dist >= 0) & (dist < window), bias * LOG2E, NEG)
    return bias.reshape((N_HB, SLOTS) + bias.shape[1:])


def _interleave_heads(w, n_groups, axis):
    r = N_HEADS // n_groups
    shape = w.shape
    w = w.reshape(shape[:axis] + (n_groups, r) + shape[axis + 1:])
    w = jnp.swapaxes(w, axis, axis + 1)
    return w.reshape(shape)


def _block_bias(bias_nat, n_groups, n_delta):
    bias = bias_nat[:, :n_delta]
    if n_groups > 1:
        bias = _interleave_heads(bias, n_groups, 0)
    bias = jnp.where(_tile_distance(n_delta, np) >= 0, bias * LOG2E, NEG)
    return bias.reshape((N_HB, SLOTS) + bias.shape[1:])


@dataclasses.dataclass(frozen=True)
class _AttnCfg:
    kv_width: int
    band_rows: int | None = None
    blocks_per_step: int = 1
    kv_per_block: bool = False
    sel_blk: int | None = None
    sel_from_input: bool = False
    gate_branch: int | None = None
    sink: bool = False

    @property
    def selecting(self):
        return self.sel_blk is not None

    @property
    def moba(self):
        return self.selecting and not self.sel_from_input


def _rank_rows(val):
    width = val.shape[1]
    row = lax.broadcasted_iota(jnp.int32, (1, width, 1), 1)
    rank = jnp.zeros(val.shape, F32)
    for m in range(width):
        vm = val[:, m:m + 1, :]
        rank = rank + jnp.where(row > m, jnp.where(vm >= val, 1.0, 0.0),
                                jnp.where(vm > val, 1.0, 0.0))
    return rank


def _u_chan0(hh, n_blk):
    return hh * n_blk + (LANES if hh < SLOTS // 2 else 0)


def _fold_rows(x, op):
    return op(x.reshape(x.shape[0] // SUBLANES, SUBLANES, x.shape[1]), axis=0)


def _attn_kernel(*refs, cfg: _AttnCfg):
    it = iter(refs)
    qt_ref, k_ref, vt_ref, bias_ref = next(it), next(it), next(it), next(it)
    sel_ref = next(it) if cfg.sel_from_input else None
    gate_ref = next(it) if cfg.gate_branch is not None else None
    sink_ref = next(it) if cfg.sink else None
    o_ref = next(it)
    kb_ref, vtb_ref = next(it), next(it)
    scratch_refs = ((next(it), next(it), next(it), next(it))
                    if cfg.band_rows is None else ())
    kmx_ref = next(it) if cfg.moba else None
    vt_tile = LANES if cfg.band_rows is not None else TK

    qt = pl.program_id(2)
    seq = k_ref.shape[1]
    n_tiles = seq // TK
    lane = lax.broadcasted_iota(jnp.int32, (1, HB_W), 1)
    slot_of_lane = _div_pow2(lane, HEAD_DIM)
    n_sel_blk = seq // cfg.sel_blk if cfg.selecting else 0

    n_kv = kb_ref.shape[0]
    kw = k_ref.shape[2] // n_kv

    @pl.when(qt == 0)
    def _prepare():
        if cfg.moba:
            kmx_ref[...] = jnp.zeros(kmx_ref.shape, F32)
        for kv, c in [(kv, c) for kv in range(n_kv) for c in range(n_tiles)]:
            rows = pl.ds(c * TK, TK)
            k = k_ref[0, rows, kv * kw:(kv + 1) * kw]
            vt = vt_ref[kv * kw:(kv + 1) * kw, c * TK:(c + 1) * TK]
            if kw == LANES:
                k = jnp.concatenate([k, k], axis=1)
                vt = jnp.concatenate([vt, vt], axis=0)
            for i in range(TK // vt_tile):
                vtb_ref[kv, c * (TK // vt_tile) + i] = (
                    vt[:, i * vt_tile:(i + 1) * vt_tile].astype(BF16))
            if cfg.selecting:
                key_blk = _div_pow2(c * TK + lax.broadcasted_iota(jnp.int32, (TK, 1), 0),
                                    cfg.sel_blk)
                for hh in range(SLOTS):
                    onehot = (lane - _u_chan0(hh, n_sel_blk)) == key_blk
                    kb_ref[kv, hh, rows, :] = jnp.where(
                        slot_of_lane == hh, k, jnp.where(onehot, 1.0, 0.0)).astype(BF16)
            else:
                kb_ref[kv, 0, rows, :] = k.astype(BF16)
            if cfg.moba:
                k_mean = jnp.mean(k, axis=0, keepdims=True)
                for hh in range(SLOTS):
                    kmx_ref[kv, pl.ds(hh * n_sel_blk + c, 1), :] = jnp.where(
                        slot_of_lane == hh, k_mean, 0.0)

    for blk in range(qt_ref.shape[0] // HB_W):
        kv = blk if n_kv > 1 else 0
        refs_of_block = (qt_ref, bias_ref, sel_ref, gate_ref, sink_ref, o_ref, kb_ref.at[kv],
                         vtb_ref.at[kv], scratch_refs,
                         kmx_ref.at[kv] if cfg.moba else None)
        _attend_block(cfg, blk, qt, n_sel_blk, refs_of_block)


def _attend_block(cfg, blk, qt, n_sel_blk, refs):
    (qt_ref, bias_ref, sel_ref, gate_ref, sink_ref, o_ref, kb_ref, vtb_ref, scratch_refs,
     kmx_ref) = refs
    bias_blk = bias_ref.at[blk]
    q_t = qt_ref[blk * HB_W:(blk + 1) * HB_W, :].astype(F32)

    sel_rows = None
    if cfg.sel_from_input:
        sel_rows = sel_ref[0]
    elif cfg.moba:
        gate = jnp.dot(kmx_ref[...].astype(BF16), q_t.astype(BF16),
                       preferred_element_type=F32)
        gate = gate.reshape(SLOTS, n_sel_blk, TQ)
        key_blk = lax.broadcasted_iota(jnp.int32, (1, n_sel_blk, 1), 1)
        val = jnp.where(key_blk < qt, gate, NEG)
        rank = _rank_rows(val)
        chosen = jnp.where(key_blk < qt, jnp.where(rank < MOBA_TOPK, 1.0, 0.0), 0.0)
        chosen = jnp.where(key_blk == qt, 1.0, chosen)
        sel_rows = chosen.reshape(SLOTS * n_sel_blk, TQ)

    q_scaled = q_t * (SCALE * LOG2E)
    qms = []
    for hh in range(SLOTS):
        pieces = {hh * HEAD_DIM: q_scaled[hh * HEAD_DIM:(hh + 1) * HEAD_DIM]}
        if cfg.selecting:
            pieces[_u_chan0(hh, n_sel_blk)] = (
                sel_rows[hh * n_sel_blk:(hh + 1) * n_sel_blk] - 1.0) * SEL_BIG
        segs, row = [], 0
        for start in sorted(pieces):
            if start > row:
                segs.append(jnp.zeros((start - row, TQ), F32))
            segs.append(pieces[start])
            row = start + pieces[start].shape[0]
        if row < HB_W:
            segs.append(jnp.zeros((HB_W - row, TQ), F32))
        qms.append(jnp.concatenate(segs, axis=0).astype(BF16))
    q_stack = None if cfg.selecting else jnp.concatenate(qms, axis=1)

    sinks = ([sink_ref[blk, :, hh:hh + 1] * LOG2E for hh in range(SLOTS)]
             if cfg.sink else None)
    if cfg.band_rows is not None:
        ms, sums, accs = _band_softmax(cfg, qt, q_stack, kb_ref, vtb_ref, bias_blk, sinks)
    else:
        ms, sums, accs = _tiled_softmax(cfg, qt, qms, q_stack, kb_ref, vtb_ref, bias_blk,
                                        sinks, *scratch_refs)

    gsig = None
    if cfg.gate_branch is not None:
        gsig = _sigmoid(
            gate_ref[blk * GATE_ROWS:(blk + 1) * GATE_ROWS, :].astype(F32))
    outs = []
    for hh in range(SLOTS):
        denom = jnp.sum(sums[hh], axis=0, keepdims=True)
        if cfg.sink:
            denom = denom + jnp.exp2(sinks[hh] - ms[hh])
        o_h = accs[hh] / jnp.maximum(denom, TINY)
        if cfg.gate_branch is not None:
            row = cfg.gate_branch * SLOTS + hh
            o_h = o_h * gsig[row:row + 1, :]
        outs.append(o_h)
    o_ref[0, :, blk * HB_W:(blk + 1) * HB_W] = (
        jnp.concatenate(outs, axis=0).T.astype(o_ref.dtype))


def _band_softmax(cfg, qt, q_stack, kb_ref, vtb_ref, bias_ref, sinks):
    band = cfg.band_rows
    back = band - TQ
    start = pl.multiple_of(jnp.maximum(qt * TQ - back, 0), LANES)
    s_all = jnp.dot(kb_ref[0, pl.ds(start, band), :], q_stack, preferred_element_type=F32)
    bias_rows = pl.ds(pl.multiple_of(back - (qt * TQ - start), LANES), band)
    tile0 = _div_pow2(start, LANES)
    ms, sums, accs = [], [], []
    for hh in range(SLOTS):
        head = pl.ds(hh * HEAD_DIM, HEAD_DIM)
        sc = s_all[:, hh * TQ:(hh + 1) * TQ] + bias_ref[hh, bias_rows, :]
        m = jnp.max(_fold_rows(sc, jnp.max), axis=0, keepdims=True)
        if cfg.sink:
            m = jnp.maximum(m, sinks[hh])
        p = jnp.exp2(sc - m)
        vt = jnp.concatenate([vtb_ref[tile0 + c, head, :] for c in range(band // LANES)],
                             axis=1)
        ms.append(m)
        sums.append(_fold_rows(p, jnp.sum))
        accs.append(jnp.dot(vt, p.astype(BF16), preferred_element_type=F32))
    return ms, sums, accs


def _tiled_softmax(cfg, qt, qms, q_stack, kb_ref, vtb_ref, bias_ref, sinks,
                   sc_ref, m_ref, l_ref, acc_ref):
    m_ref[...] = jnp.full(m_ref.shape, NEG, F32)
    l_ref[...] = jnp.zeros(l_ref.shape, F32)
    acc_ref[...] = jnp.zeros(acc_ref.shape, F32)

    def pass1(kt, width):
        rows = pl.ds(pl.multiple_of(kt * TK, TK), width * TK)
        if cfg.selecting:
            s = [jnp.dot(kb_ref[hh, rows, :], qms[hh], preferred_element_type=F32)
                 for hh in range(SLOTS)]
        else:
            s_all = jnp.dot(kb_ref[0, rows, :], q_stack, preferred_element_type=F32)
            s = [s_all[:, hh * TQ:(hh + 1) * TQ] for hh in range(SLOTS)]
        for hh in range(SLOTS):
            m_new = m_ref[hh]
            for w in range(width):
                tile = s[hh][w * TK:(w + 1) * TK] + bias_ref[hh, qt - kt - w]
                sc_ref[hh, kt + w] = tile
                m_new = jnp.maximum(m_new, _fold_rows(tile, jnp.max))
            m_ref[hh] = m_new

    def pass2(kt, width, ms):
        for hh in range(SLOTS):
            head = pl.ds(hh * HEAD_DIM, HEAD_DIM)
            l_new = l_ref[hh]
            o_new = acc_ref[head, :]
            for w in range(width):
                p = jnp.exp2(sc_ref[hh, kt + w] - ms[hh])
                l_new = l_new + _fold_rows(p, jnp.sum)
                o_new = o_new + jnp.dot(vtb_ref[kt + w, head, :], p.astype(BF16),
                                        preferred_element_type=F32)
            l_ref[hh] = l_new
            acc_ref[head, :] = o_new

    def over_tiles(body):
        n = qt + 1
        start = 0
        for width in (1, 2):
            has = jnp.bitwise_and(n, width)

            @pl.when(has != 0)
            def _part(start=start, width=width):
                body(start, width)

            start = start + has

        def group(i, carry):
            body(start + 4 * i, 4)
            return carry

        lax.fori_loop(0, jnp.right_shift(n, 2), group, 0)

    over_tiles(pass1)
    ms = []
    for hh in range(SLOTS):
        m = jnp.max(m_ref[hh], axis=0, keepdims=True)
        ms.append(jnp.maximum(m, sinks[hh]) if cfg.sink else m)
    over_tiles(functools.partial(pass2, ms=ms))
    return (ms, [l_ref[hh] for hh in range(SLOTS)],
            [acc_ref[hh * HEAD_DIM:(hh + 1) * HEAD_DIM, :] for hh in range(SLOTS)])


def _attention(cfg, rows, cols, bias, *, k_blk, v_rblk, sel=None, gate_rblk0=None, sinks=None):
    bsz, seq, _ = rows.shape
    n_delta = bias.shape[2]
    n_q = seq // TQ
    nb = cfg.blocks_per_step
    n_kv = nb if cfg.kv_per_block else 1
    kw = n_kv * cfg.kv_width
    bias_tail = (0,) * (bias.ndim - 1)
    in_specs = [
        pl.BlockSpec((nb * HB_W, TQ), lambda hb, b, qt: (hb, b * n_q + qt)),
        pl.BlockSpec((1, seq, kw), lambda hb, b, qt: (b, 0, k_blk(hb))),
        pl.BlockSpec((kw, seq), lambda hb, b, qt: (v_rblk(hb), b)),
        pl.BlockSpec((nb,) + bias.shape[1:], lambda hb, b, qt: (hb,) + bias_tail,
                     pipeline_mode=pl.Buffered(1)),
    ]
    args = [cols, rows, cols, bias]
    if cfg.sel_from_input:
        in_specs.append(pl.BlockSpec((1, LANES, TQ), lambda hb, b, qt: (b, 0, qt)))
        args.append(sel)
    if cfg.gate_branch is not None:
        assert gate_rblk0 % nb == 0
        in_specs.append(pl.BlockSpec(
            (nb * GATE_ROWS, TQ), lambda hb, b, qt: (gate_rblk0 // nb + hb, b * n_q + qt)))
        args.append(cols)
    if cfg.sink:
        in_specs.append(pl.BlockSpec((nb, 1, LANES), lambda hb, b, qt: (hb, 0, 0)))
        args.append(sinks)
    vt_tile = LANES if cfg.band_rows is not None else TK
    scratch = [
        pltpu.VMEM((n_kv, SLOTS if cfg.selecting else 1, seq, HB_W), BF16),
        pltpu.VMEM((n_kv, seq // vt_tile, HB_W, vt_tile), BF16),
    ]
    if cfg.band_rows is None:
        scratch += [
            pltpu.VMEM((SLOTS, n_delta, TK, TQ), F32),
            pltpu.VMEM((SLOTS, SUBLANES, TQ), F32),
            pltpu.VMEM((SLOTS, SUBLANES, TQ), F32),
            pltpu.VMEM((HB_W, TQ), F32),
        ]
    if cfg.moba:
        assert SLOTS * (seq // cfg.sel_blk) <= LANES
        scratch.append(
            pltpu.VMEM((n_kv, SLOTS * (seq // cfg.sel_blk), HB_W), F32))
    return pl.pallas_call(
        functools.partial(_attn_kernel, cfg=cfg),
        grid=(N_HB // nb, bsz, n_q),
        in_specs=in_specs,
        out_specs=pl.BlockSpec((1, TQ, nb * HB_W), lambda hb, b, qt: (b, qt, hb)),
        out_shape=jax.ShapeDtypeStruct((bsz, seq, D_MODEL), BF16),
        scratch_shapes=scratch,
        compiler_params=_params(3),
        name="attention",
    )(*args)


def _gelu_tanh(x):
    cdf = 0.5 * (1.0 + jnp.tanh(math.sqrt(2.0 / math.pi) * (x + 0.044715 * (x * x * x))))
    return x * cdf


def _compress_kernel(kc_lo_ref, kc_hi_ref, vc_lo_ref, vc_hi_ref, pos_ref, w1_ref, w2k_ref,
                     w2vt_ref, kcb_ref, vcbt_ref):
    half = CMP_LEN // 2
    n_chunk = kc_lo_ref.shape[1] // CMP_STRIDE
    flat_lane = lax.broadcasted_iota(jnp.int32, (1, half * HB_W), 1)
    group_of_lane = _div_pow2(jnp.bitwise_and(flat_lane, HB_W - 1), HEAD_DIM)
    for kind, (lo_ref, hi_ref) in enumerate(((kc_lo_ref, kc_hi_ref), (vc_lo_ref, vc_hi_ref))):
        chunks = [jnp.concatenate(
            [lo_ref[0, pl.ds(l, n_chunk, stride=CMP_STRIDE), :],
             hi_ref[0, pl.ds(l, n_chunk, stride=CMP_STRIDE), :]], axis=1) for l in range(half)]
        parts = []
        for h in range(2):
            pieces = [chunks[l] + pos_ref[kind, pl.ds(h * half + l, 1), :] for l in range(half)]
            flat = jnp.concatenate(pieces, axis=1)
            stacked = jnp.concatenate(
                [jnp.where(group_of_lane == g, flat, 0.0)
                 for g in range(B_KV_GROUPS)], axis=0).astype(BF16)
            parts.append(jnp.dot(stacked, w1_ref[kind, h], preferred_element_type=F32))
        rows = B_KV_GROUPS * n_chunk
        hidden = parts[0] + pltpu.roll(parts[1], rows - 1, axis=0)
        act = _gelu_tanh(hidden).astype(BF16)
        out = None
        for g in range(B_KV_GROUPS):
            act_g = act[g * n_chunk:(g + 1) * n_chunk]
            if kind == 0:
                o = jnp.dot(act_g, w2k_ref[g], preferred_element_type=F32)
            else:
                o = lax.dot_general(w2vt_ref[g], act_g, _NT, preferred_element_type=F32)
            out = o if out is None else out + o
        if kind == 0:
            kcb_ref[0] = out
        else:
            vcbt_ref[0] = out


def _compress(rows, kc_blk, vc_blk, pos, w1, w2k, w2vt):
    bsz, seq, _ = rows.shape
    n_chunk = seq // CMP_STRIDE
    return pl.pallas_call(
        _compress_kernel,
        grid=(bsz,),
        in_specs=[
            pl.BlockSpec((1, seq, LANES), lambda b: (b, 0, 2 * kc_blk)),
            pl.BlockSpec((1, seq, LANES), lambda b: (b, 0, 2 * kc_blk + 1)),
            pl.BlockSpec((1, seq, LANES), lambda b: (b, 0, 2 * vc_blk)),
            pl.BlockSpec((1, seq, LANES), lambda b: (b, 0, 2 * vc_blk + 1)),
            _resident(pos.shape),
            _resident(w1.shape),
            _resident(w2k.shape),
            _resident(w2vt.shape),
        ],
        out_specs=[pl.BlockSpec((1, n_chunk, HB_W), lambda b: (b, 0, 0)),
                   pl.BlockSpec((1, HB_W, n_chunk), lambda b: (b, 0, 0))],
        out_shape=[jax.ShapeDtypeStruct((bsz, n_chunk, HB_W), F32),
                   jax.ShapeDtypeStruct((bsz, HB_W, n_chunk), F32)],
        compiler_params=_params(1),
        name="nsa_compress",
    )(rows, rows, rows, rows, pos, w1, w2k, w2vt)


def _cmp_kernel(qt_ref, kcb_ref, vcbt_ref, gate_ref, ovt_ref, o_ref, sel_ref):
    qt = pl.program_id(1)
    n_cmp_pad = kcb_ref.shape[1]
    n_slc = LANES // B_KV_GROUPS
    chan = lax.broadcasted_iota(jnp.int32, (HB_W, 1), 0)
    slot_of_chan = _div_pow2(chan, HEAD_DIM)
    t = qt * TQ + lax.broadcasted_iota(jnp.int32, (1, TQ), 1)
    cmp_end = (lax.broadcasted_iota(jnp.int32, (n_cmp_pad, 1), 0) * CMP_STRIDE + CMP_LEN - 1)
    ok = cmp_end <= jnp.concatenate([t] * N_HB, axis=1)
    kc = kcb_ref[0].astype(BF16)
    vct = vcbt_ref[0].astype(BF16)
    gsig = _sigmoid(gate_ref[...].astype(F32))
    outs = [[None] * B_KV_GROUPS for _ in range(N_HB)]
    imp = None
    for g in range(B_KV_GROUPS):
        own = jnp.concatenate(
            [qt_ref[r * HB_W + g * HEAD_DIM:r * HB_W + (g + 1) * HEAD_DIM, :].astype(F32)
             for r in range(N_HB)], axis=1) * SCALE
        qm = jnp.concatenate(
            [jnp.zeros((g * HEAD_DIM, N_HB * TQ), F32), own,
             jnp.zeros(((B_KV_GROUPS - 1 - g) * HEAD_DIM, N_HB * TQ), F32)][
                 (1 if g == 0 else 0):(2 if g == B_KV_GROUPS - 1 else 3)], axis=0)
        s = jnp.dot(kc, qm.astype(BF16), preferred_element_type=F32)
        s = jnp.where(ok, s, NEG)
        m = jnp.max(s, axis=0, keepdims=True)
        e = jnp.where(ok, jnp.exp(s - m), 0.0)
        p = e / jnp.maximum(jnp.sum(e, axis=0, keepdims=True), TINY)
        o = jnp.dot(vct[g * HEAD_DIM:(g + 1) * HEAD_DIM], p.astype(BF16),
                    preferred_element_type=F32)
        p_group = None
        for r in range(N_HB):
            p_r = p[:, r * TQ:(r + 1) * TQ]
            p_group = p_r if p_group is None else p_group + p_r
            row = r * GATE_ROWS + g
            outs[r][g] = o[:, r * TQ:(r + 1) * TQ] * gsig[row:row + 1, :]
        part = jnp.dot(ovt_ref[g], p_group, precision=lax.Precision.HIGHEST,
                       preferred_element_type=F32)
        imp = part if imp is None else imp + part
    for r in range(N_HB):
        o_ref[0, :, r * HB_W:(r + 1) * HB_W] = (
            jnp.concatenate(outs[r], axis=0).T.astype(o_ref.dtype))
    imp = imp.reshape(B_KV_GROUPS, n_slc, TQ)
    j = lax.broadcasted_iota(jnp.int32, (1, n_slc, 1), 1)
    cur = _div_pow2(t, SLC_BLK).reshape(1, 1, TQ)
    val = jnp.where(j == cur - 1, -NEG, imp)
    val = jnp.where(j == cur, -NEG, val)
    val = jnp.where(j == 0, -NEG, val)
    val = jnp.where(j > cur, NEG, val)
    chosen = jnp.where(_rank_rows(val) < SLC_TOPK, 1.0, 0.0)
    sel_ref[0] = chosen.reshape(LANES, TQ)


def _cmp_attention(cols, kcb, vcbt, gate_rblk, ovt, bsz, seq):
    n_cmp_pad = kcb.shape[1]
    n_q = seq // TQ
    return pl.pallas_call(
        _cmp_kernel,
        grid=(bsz, n_q),
        in_specs=[
            pl.BlockSpec((D_MODEL, TQ), lambda b, qt: (0, b * n_q + qt)),
            pl.BlockSpec((1, n_cmp_pad, HB_W), lambda b, qt: (b, 0, 0)),
            pl.BlockSpec((1, HB_W, n_cmp_pad), lambda b, qt: (b, 0, 0)),
            pl.BlockSpec((N_HB * GATE_ROWS, TQ), lambda b, qt: (gate_rblk, b * n_q + qt)),
            _resident(ovt.shape),
        ],
        out_specs=[
            pl.BlockSpec((1, TQ, D_MODEL), lambda b, qt: (b, qt, 0)),
            pl.BlockSpec((1, LANES, TQ), lambda b, qt: (b, 0, qt)),
        ],
        out_shape=[
            jax.ShapeDtypeStruct((bsz, seq, D_MODEL), BF16),
            jax.ShapeDtypeStruct((bsz, LANES, seq), F32),
        ],
        compiler_params=_params(2),
        name="nsa_cmp_select",
    )(cols, kcb, vcbt, cols, ovt)


def _mixer_a(project, bsz, seq, w_in, w_out, sinks, bias_nat, rel_bias):
    qw, kvw = N_HEADS * HEAD_DIM, A_KV_HEADS * HEAD_DIM
    w_q = _interleave_heads(w_in[:, :qw].reshape(D_MODEL, N_HEADS, HEAD_DIM), A_KV_HEADS, 1)
    w_cols = jnp.concatenate([w_q.reshape(D_MODEL, qw), w_in[:, qw + kvw:]], axis=1)
    rows, cols = project(w_in[:, qw:qw + kvw], w_cols)
    rows = rows.reshape(bsz, seq, kvw)
    band_rows = TQ + -(-(A_WINDOW - 1) // LANES) * LANES
    assert band_rows <= seq
    bias = _band_bias(rel_bias, A_KV_HEADS, band_rows, A_WINDOW)
    sink_rows = _interleave_heads(sinks.astype(F32), A_KV_HEADS, 0).reshape(N_HB, 1, SLOTS)
    sink_rows = jnp.pad(sink_rows, ((0, 0), (0, 0), (0, LANES - SLOTS)))
    cfg = _AttnCfg(kv_width=kvw, band_rows=band_rows, sink=True, blocks_per_step=N_HB)
    o = _attention(cfg, rows, cols, bias, k_blk=lambda hb: 0, v_rblk=lambda hb: qw // kvw,
                   sinks=sink_rows)
    w_o = _interleave_heads(w_out.reshape(N_HEADS, HEAD_DIM, D_MODEL), A_KV_HEADS, 0)
    return [o.reshape(bsz * seq, -1)], w_o.reshape(qw, D_MODEL)


def _nsa_overlap_t(seq):
    n_chunk = seq // CMP_STRIDE
    n_cmp = (seq - CMP_LEN) // CMP_STRIDE + 1
    n_slc = seq // SLC_BLK
    c_start = np.arange(n_cmp)[:, None] * CMP_STRIDE
    s_start = np.arange(n_slc)[None, :] * SLC_BLK
    overlap = ((c_start < s_start + SLC_BLK) & (c_start + CMP_LEN > s_start)).astype(np.float32)
    ovt = np.zeros((B_KV_GROUPS, LANES, n_chunk), np.float32)
    for g in range(B_KV_GROUPS):
        ovt[g, g * n_slc:(g + 1) * n_slc, :n_cmp] = overlap.T
    return jnp.asarray(ovt)


def _mixer_b(project, bsz, seq, w_in, w_out, cmp_pos, cmp_w1, cmp_w2, bias_nat, rel_bias):
    assert seq // SLC_BLK * B_KV_GROUPS == LANES and seq // CMP_STRIDE == LANES
    grp, dh = B_KV_GROUPS, HEAD_DIM
    qw, kvw = N_HEADS * dh, B_KV_GROUPS * dh
    w_q = _interleave_heads(w_in[:, :qw].reshape(D_MODEL, N_HEADS, dh), grp, 1)
    kv = [w_in[:, qw + i * kvw:qw + (i + 1) * kvw] for i in range(6)]
    w_g = w_in[:, qw + 6 * kvw:].reshape(D_MODEL, grp, N_HB, 3)
    w_g = w_g.transpose(0, 2, 3, 1).reshape(D_MODEL, N_HB, 3 * grp)
    w_g = jnp.pad(w_g, ((0, 0), (0, 0), (0, GATE_ROWS - 3 * grp)))
    w_rows = jnp.concatenate([kv[0], kv[1], kv[2], kv[4]], axis=1)
    w_cols = jnp.concatenate([w_q.reshape(D_MODEL, qw), kv[3], kv[5],
                              w_g.reshape(D_MODEL, N_HB * GATE_ROWS)], axis=1)
    rows, cols = project(w_rows, w_cols)
    rows = rows.reshape(bsz, seq, 4 * kvw)
    gate_row0 = qw + 2 * kvw

    half = CMP_LEN // 2
    pos = jnp.tile(cmp_pos.astype(F32), (1, 1, grp))
    w1 = cmp_w1.reshape(2, 2, half, 1, dh, CMP_HIDDEN)
    w1 = jnp.broadcast_to(w1, (2, 2, half, grp, dh, CMP_HIDDEN))
    w1 = w1.reshape(2, 2, half * HB_W, CMP_HIDDEN).astype(BF16)
    w2 = jnp.zeros((2, grp, CMP_HIDDEN, grp, dh), F32)
    for g in range(grp):
        w2 = w2.at[:, g, :, g, :].set(cmp_w2)
    w2 = w2.reshape(2, grp, CMP_HIDDEN, HB_W).astype(BF16)
    kcb, vcbt = _compress(rows, 0, 1, pos, w1, w2[0], jnp.swapaxes(w2[1], 1, 2))

    o_cmp, sel = _cmp_attention(cols, kcb, vcbt, gate_row0 // (N_HB * GATE_ROWS),
                                _nsa_overlap_t(seq), bsz, seq)

    gate_rblk0 = gate_row0 // GATE_ROWS
    bias = _block_bias(bias_nat, grp, seq // TK)
    cfg_slc = _AttnCfg(kv_width=kvw, sel_blk=SLC_BLK, sel_from_input=True, gate_branch=1,
                       blocks_per_step=2)
    o_slc = _attention(cfg_slc, rows, cols, bias, k_blk=lambda hb: 2,
                       v_rblk=lambda hb: qw // kvw, sel=sel, gate_rblk0=gate_rblk0)
    band_rows = TQ + -(-(B_WINDOW - 1) // LANES) * LANES
    assert band_rows <= seq
    cfg_win = _AttnCfg(kv_width=kvw, band_rows=band_rows, gate_branch=2, blocks_per_step=N_HB)
    bias_win = _band_bias(rel_bias, grp, band_rows, B_WINDOW)
    o_win = _attention(cfg_win, rows, cols, bias_win, k_blk=lambda hb: 3,
                       v_rblk=lambda hb: qw // kvw + 1, gate_rblk0=gate_rblk0)
    w_o = _interleave_heads(w_out.reshape(N_HEADS, dh, D_MODEL), grp, 0).reshape(qw, D_MODEL)
    n = bsz * seq
    return [o_cmp.reshape(n, -1), o_slc.reshape(n, -1), o_win.reshape(n, -1)], w_o


def _mixer_c(project, bsz, seq, w_in, w_out, bias_nat, rel_bias):
    assert seq % MOBA_BLK == 0 and MOBA_BLK == TK
    qw = N_HEADS * HEAD_DIM
    w_cols = jnp.concatenate([w_in[:, :qw], w_in[:, 2 * qw:]], axis=1)
    rows, cols = project(w_in[:, qw:2 * qw], w_cols)
    rows = rows.reshape(bsz, seq, qw)
    bias = _block_bias(bias_nat, 1, seq // TK)
    nb = 2
    cfg = _AttnCfg(kv_width=HB_W, sel_blk=MOBA_BLK, blocks_per_step=nb, kv_per_block=True)
    o = _attention(cfg, rows, cols, bias, k_blk=lambda hb: hb,
                   v_rblk=lambda hb: N_HB // nb + hb)
    return [o.reshape(bsz * seq, -1)], w_out


def _layer(xf, bsz, seq, layer, g, ffn_w_in, ffn_w_out, mixer, mixer_params, bias_nat,
           rel_bias):
    xf = _ffn(xf, g[0], ffn_w_in, ffn_w_out, (layer, 0), g[1])
    project = functools.partial(_proj, xf, g[2])
    o_list, w_mix = mixer(project, bsz, seq, *mixer_params, bias_nat, rel_bias)
    return _ffn(xf, g[4], ffn_w_in, ffn_w_out, (layer, 1), g[5], mix=(o_list, w_mix, g[3]))


def kernel(x, rel_bias, norm_g, ffn_w_in, ffn_w_out, a_w_in, a_w_out, a_sinks, b_w_in, b_w_out,
           b_cmp_pos, b_cmp_w1, b_cmp_w2, c_w_in, c_w_out):
    bsz, seq, d = x.shape
    assert d == D_MODEL and seq % TQ == 0 and (bsz * seq) % max(TM, TM_PROJ) == 0
    depth = norm_g.shape[0]
    bias_nat = _bias_tiles(rel_bias, seq // TK)
    ffn_w_in, ffn_w_out = ffn_w_in.astype(BF16), ffn_w_out.astype(BF16)
    xf = x.reshape(bsz * seq, d)
    for i in range(depth):
        kind, j = i % N_MIXERS, i // N_MIXERS
        if kind == 0:
            mixer, params = _mixer_a, (a_w_in[j], a_w_out[j], a_sinks[j])
        elif kind == 1:
            mixer, params = _mixer_b, (b_w_in[j], b_w_out[j], b_cmp_pos[j], b_cmp_w1[j],
                                       b_cmp_w2[j])
        else:
            mixer, params = _mixer_c, (c_w_in[j], c_w_out[j])
        xf = _layer(xf, bsz, seq, i, norm_g[i], ffn_w_in, ffn_w_out, mixer, params,
                    bias_nat, rel_bias)
    return xf.reshape(bsz, seq, d)
```

```python
import dataclasses
import functools
import math

import numpy as np
import jax
import jax.numpy as jnp
from jax import lax
from jax.experimental import pallas as pl
from jax.experimental.pallas import tpu as pltpu

F32 = jnp.float32
BF16 = jnp.bfloat16

D_MODEL = 1024
HEAD_DIM = 64
N_HEADS = D_MODEL // HEAD_DIM
D_FF = 2816
NORM_EPS = 1e-6
N_MIXERS = 3
REL_BUCKETS = 32
REL_MAX_DIST = 1024
A_KV_HEADS = 2
A_WINDOW = 128
B_KV_GROUPS = 4
CMP_LEN = 32
CMP_STRIDE = 16
CMP_HIDDEN = 256
SLC_BLK = 64
SLC_TOPK = 16
B_WINDOW = 512
MOBA_BLK = 256
MOBA_TOPK = 3
NEG = -1e30
TINY = 1e-30

LANES = 128
SUBLANES = 8
MXU_DIM = 256
VMEM_LIMIT_BYTES = 56 * 1024 * 1024

SLOTS = 4
HB_W = SLOTS * HEAD_DIM
assert HB_W == MXU_DIM
N_HB = N_HEADS // SLOTS
TQ = 256
TK = 256
TM = 512
TM_PROJ = 1024
FF_CHUNKS = ((0, 1536), (1536, D_FF))
GATE_ROWS = 16
SEL_BIG = float(2 ** 30)
SCALE = HEAD_DIM ** -0.5
LOG2E = math.log2(math.e)

_NT = (((1,), (1,)), ((), ()))


def _params(n_axes):
    return pltpu.CompilerParams(
        dimension_semantics=("arbitrary",) * n_axes,
        vmem_limit_bytes=VMEM_LIMIT_BYTES)


def _rms(x, g):
    return x * lax.rsqrt(jnp.mean(x * x, axis=-1, keepdims=True) + NORM_EPS) * g


def _sigmoid(x):
    return 1.0 / (1.0 + jnp.exp(-x))


def _div_pow2(x, d):
    assert d & (d - 1) == 0
    return jnp.right_shift(x, d.bit_length() - 1)


def _resident(shape):
    zeros = (0,) * len(shape)
    return pl.BlockSpec(shape, lambda *_: zeros, pipeline_mode=pl.Buffered(1))


def _ffn_kernel(*refs, n_mix):
    mix_refs, refs = refs[:n_mix], refs[n_mix:]
    if n_mix:
        wmix_ref, gmix_ref = refs[:2]
        refs = refs[2:]
    x_ref, gpre_ref, win_ref, wout_ref, gpost_ref, o_ref = refs
    x = x_ref[...]
    if n_mix:
        o = mix_refs[0][...]
        for r in mix_refs[1:]:
            o = o.astype(F32) + r[...].astype(F32)
        mixed = jnp.dot(o.astype(BF16), wmix_ref[...], preferred_element_type=F32)
        x = x + _rms(mixed, gmix_ref[...])
    hn = _rms(x, gpre_ref[...]).astype(BF16)
    y = None
    for lo, hi in FF_CHUNKS:
        gate = jnp.dot(hn, win_ref[:, lo:hi], preferred_element_type=F32)
        up = jnp.dot(hn, win_ref[:, D_FF + lo:D_FF + hi], preferred_element_type=F32)
        act = (gate * _sigmoid(gate) * up).astype(BF16)
        part = jnp.dot(act, wout_ref[lo:hi, :], preferred_element_type=F32)
        y = part if y is None else y + part
    o_ref[...] = x + 0.5 * _rms(y, gpost_ref[...])


def _ffn(xf, g_pre, w_in_all, w_out_all, which, g_post, mix=None):
    n, d = xf.shape
    tile = pl.BlockSpec((TM, d), lambda i: (i, 0))
    mix_specs, mix_args = [], []
    if mix is not None:
        o_list, w_mix, g_mix = mix
        mix_specs = [tile] * len(o_list) + [_resident(w_mix.shape), _resident((1, d))]
        mix_args = list(o_list) + [w_mix.astype(BF16), g_mix.reshape(1, d)]

    def picked(w_all):
        return pl.BlockSpec((None, None) + w_all.shape[2:], lambda i: which + (0, 0),
                            pipeline_mode=pl.Buffered(1))

    return pl.pallas_call(
        functools.partial(_ffn_kernel, n_mix=len(mix_args) - 2 if mix_args else 0),
        grid=(n // TM,),
        in_specs=mix_specs + [
            tile,
            _resident((1, d)),
            picked(w_in_all),
            picked(w_out_all),
            _resident((1, d)),
        ],
        out_specs=tile,
        out_shape=jax.ShapeDtypeStruct((n, d), F32),
        compiler_params=_params(1),
        name="ffn",
    )(*mix_args, xf, g_pre.reshape(1, d), w_in_all, w_out_all, g_post.reshape(1, d))


def _proj_kernel(x_ref, g_ref, wr_ref, wct_ref, rows_ref, cols_ref):
    hn = _rms(x_ref[...], g_ref[...]).astype(BF16)
    rows_ref[...] = jnp.dot(hn, wr_ref[...], preferred_element_type=F32)
    cols_ref[...] = lax.dot_general(wct_ref[...], hn, _NT,
                                    preferred_element_type=F32).astype(cols_ref.dtype)


def _proj(xf, g, w_rows, w_cols):
    n, d = xf.shape
    cr, cc = w_rows.shape[1], w_cols.shape[1]
    return pl.pallas_call(
        _proj_kernel,
        grid=(n // TM_PROJ,),
        in_specs=[
            pl.BlockSpec((TM_PROJ, d), lambda i: (i, 0)),
            _resident((1, d)),
            _resident((d, cr)),
            _resident((cc, d)),
        ],
        out_specs=[pl.BlockSpec((TM_PROJ, cr), lambda i: (i, 0)),
                   pl.BlockSpec((cc, TM_PROJ), lambda i: (0, i))],
        out_shape=[jax.ShapeDtypeStruct((n, cr), F32), jax.ShapeDtypeStruct((cc, n), BF16)],
        compiler_params=_params(1),
        name="proj",
    )(xf, g.reshape(1, d), w_rows.astype(BF16), w_cols.T.astype(BF16))


def _rel_bucket(dist):
    dist = jnp.maximum(dist, 0)
    exact = REL_BUCKETS // 2
    d = jnp.maximum(dist, 1).astype(jnp.float32)
    log_b = exact + (jnp.log(d / exact) / math.log(REL_MAX_DIST / exact)
                     * (REL_BUCKETS - exact)).astype(jnp.int32)
    return jnp.where(dist < exact, dist, jnp.minimum(log_b, REL_BUCKETS - 1))


def _tile_distance(n_delta, xp):
    delta = xp.arange(n_delta)[:, None, None]
    return delta * TK + xp.arange(TQ)[None, None, :] - xp.arange(TK)[None, :, None]


def _bias_of_distance(rel_bias, dist):
    bucket = _rel_bucket(dist).reshape(1, -1)
    onehot = (jnp.arange(REL_BUCKETS)[:, None] == bucket).astype(F32)
    tiles = jnp.dot(rel_bias.astype(F32).T, onehot, precision=lax.Precision.HIGHEST)
    return tiles.reshape((N_HEADS,) + dist.shape)


def _bias_tiles(rel_bias, n_delta):
    return _bias_of_distance(rel_bias, _tile_distance(n_delta, jnp))


def _band_distance(band_rows, xp):
    back = band_rows - TQ
    return back + xp.arange(TQ)[None, :] - xp.arange(band_rows + back)[:, None]


def _band_bias(rel_bias, n_groups, band_rows, window):
    bias = _bias_of_distance(rel_bias, _band_distance(band_rows, jnp))
    bias = _interleave_heads(bias, n_groups, 0)
    dist = _band_distance(band_rows, np)
    bias = jnp.where((dist >= 0) & (dist < window), bias * LOG2E, NEG)
    return bias.reshape((N_HB, SLOTS) + bias.shape[1:])


def _interleave_heads(w, n_groups, axis):
    r = N_HEADS // n_groups
    shape = w.shape
    w = w.reshape(shape[:axis] + (n_groups, r) + shape[axis + 1:])
    w = jnp.swapaxes(w, axis, axis + 1)
    return w.reshape(shape)


def _block_bias(bias_nat, n_groups, n_delta):
    bias = bias_nat[:, :n_delta]
    if n_groups > 1:
        bias = _interleave_heads(bias, n_groups, 0)
    bias = jnp.where(_tile_distance(n_delta, np) >= 0, bias * LOG2E, NEG)
    return bias.reshape((N_HB, SLOTS) + bias.shape[1:])


@dataclasses.dataclass(frozen=True)
class _AttnCfg:
    kv_width: int
    band_rows: int | None = None
    blocks_per_step: int = 1
    kv_per_block: bool = False
    sel_blk: int | None = None
    sel_from_input: bool = False
    gate_branch: int | None = None
    sink: bool = False

    @property
    def selecting(self):
        return self.sel_blk is not None

    @property
    def moba(self):
        return self.selecting and not self.sel_from_input


def _rank_rows(val):
    width = val.shape[1]
    row = lax.broadcasted_iota(jnp.int32, (1, width, 1), 1)
    rank = jnp.zeros(val.shape, F32)
    for m in range(width):
        vm = val[:, m:m + 1, :]
        rank = rank + jnp.where(row > m, jnp.where(vm >= val, 1.0, 0.0),
                                jnp.where(vm > val, 1.0, 0.0))
    return rank


def _u_chan0(hh, n_blk):
    return hh * n_blk + (LANES if hh < SLOTS // 2 else 0)


def _fold_rows(x, op):
    return op(x.reshape(x.shape[0] // SUBLANES, SUBLANES, x.shape[1]), axis=0)


def _attn_kernel(*refs, cfg: _AttnCfg):
    it = iter(refs)
    qt_ref, k_ref, vt_ref, bias_ref = next(it), next(it), next(it), next(it)
    sel_ref = next(it) if cfg.sel_from_input else None
    gate_ref = next(it) if cfg.gate_branch is not None else None
    sink_ref = next(it) if cfg.sink else None
    o_ref = next(it)
    kb_ref, vtb_ref = next(it), next(it)
    scratch_refs = ((next(it), next(it), next(it), next(it))
                    if cfg.band_rows is None else ())
    kmx_ref = next(it) if cfg.moba else None
    vt_tile = LANES if cfg.band_rows is not None else TK

    qt = pl.program_id(2)
    seq = k_ref.shape[1]
    n_tiles = seq // TK
    lane = lax.broadcasted_iota(jnp.int32, (1, HB_W), 1)
    slot_of_lane = _div_pow2(lane, HEAD_DIM)
    n_sel_blk = seq // cfg.sel_blk if cfg.selecting else 0

    n_kv = kb_ref.shape[0]
    kw = k_ref.shape[2] // n_kv

    @pl.when(qt == 0)
    def _prepare():
        if cfg.moba:
            kmx_ref[...] = jnp.zeros(kmx_ref.shape, F32)
        for kv, c in [(kv, c) for kv in range(n_kv) for c in range(n_tiles)]:
            rows = pl.ds(c * TK, TK)
            k = k_ref[0, rows, kv * kw:(kv + 1) * kw]
            vt = vt_ref[kv * kw:(kv + 1) * kw, c * TK:(c + 1) * TK]
            if kw == LANES:
                k = jnp.concatenate([k, k], axis=1)
                vt = jnp.concatenate([vt, vt], axis=0)
            for i in range(TK // vt_tile):
                vtb_ref[kv, c * (TK // vt_tile) + i] = (
                    vt[:, i * vt_tile:(i + 1) * vt_tile].astype(BF16))
            if cfg.selecting:
                key_blk = _div_pow2(c * TK + lax.broadcasted_iota(jnp.int32, (TK, 1), 0),
                                    cfg.sel_blk)
                for hh in range(SLOTS):
                    onehot = (lane - _u_chan0(hh, n_sel_blk)) == key_blk
                    kb_ref[kv, hh, rows, :] = jnp.where(
                        slot_of_lane == hh, k, jnp.where(onehot, 1.0, 0.0)).astype(BF16)
            else:
                kb_ref[kv, 0, rows, :] = k.astype(BF16)
            if cfg.moba:
                k_mean = jnp.mean(k, axis=0, keepdims=True)
                for hh in range(SLOTS):
                    kmx_ref[kv, pl.ds(hh * n_sel_blk + c, 1), :] = jnp.where(
                        slot_of_lane == hh, k_mean, 0.0)

    for blk in range(qt_ref.shape[0] // HB_W):
        kv = blk if n_kv > 1 else 0
        refs_of_block = (qt_ref, bias_ref, sel_ref, gate_ref, sink_ref, o_ref, kb_ref.at[kv],
                         vtb_ref.at[kv], scratch_refs,
                         kmx_ref.at[kv] if cfg.moba else None)
        _attend_block(cfg, blk, qt, n_sel_blk, refs_of_block)


def _attend_block(cfg, blk, qt, n_sel_blk, refs):
    (qt_ref, bias_ref, sel_ref, gate_ref, sink_ref, o_ref, kb_ref, vtb_ref, scratch_refs,
     kmx_ref) = refs
    bias_blk = bias_ref.at[blk]
    q_t = qt_ref[blk * HB_W:(blk + 1) * HB_W, :].astype(F32)

    sel_rows = None
    if cfg.sel_from_input:
        sel_rows = sel_ref[0]
    elif cfg.moba:
        gate = jnp.dot(kmx_ref[...].astype(BF16), q_t.astype(BF16),
                       preferred_element_type=F32)
        gate = gate.reshape(SLOTS, n_sel_blk, TQ)
        key_blk = lax.broadcasted_iota(jnp.int32, (1, n_sel_blk, 1), 1)
        val = jnp.where(key_blk < qt, gate, NEG)
        rank = _rank_rows(val)
        chosen = jnp.where(key_blk < qt, jnp.where(rank < MOBA_TOPK, 1.0, 0.0), 0.0)
        chosen = jnp.where(key_blk == qt, 1.0, chosen)
        sel_rows = chosen.reshape(SLOTS * n_sel_blk, TQ)

    q_scaled = q_t * (SCALE * LOG2E)
    qms = []
    for hh in range(SLOTS):
        pieces = {hh * HEAD_DIM: q_scaled[hh * HEAD_DIM:(hh + 1) * HEAD_DIM]}
        if cfg.selecting:
            pieces[_u_chan0(hh, n_sel_blk)] = (
                sel_rows[hh * n_sel_blk:(hh + 1) * n_sel_blk] - 1.0) * SEL_BIG
        segs, row = [], 0
        for start in sorted(pieces):
            if start > row:
                segs.append(jnp.zeros((start - row, TQ), F32))
            segs.append(pieces[start])
            row = start + pieces[start].shape[0]
        if row < HB_W:
            segs.append(jnp.zeros((HB_W - row, TQ), F32))
        qms.append(jnp.concatenate(segs, axis=0).astype(BF16))
    q_stack = None if cfg.selecting else jnp.concatenate(qms, axis=1)

    sinks = ([sink_ref[blk, :, hh:hh + 1] * LOG2E for hh in range(SLOTS)]
             if cfg.sink else None)
    if cfg.band_rows is not None:
        ms, sums, accs = _band_softmax(cfg, qt, q_stack, kb_ref, vtb_ref, bias_blk, sinks)
    else:
        ms, sums, accs = _tiled_softmax(cfg, qt, qms, q_stack, kb_ref, vtb_ref, bias_blk,
                                        sinks, *scratch_refs)

    gsig = None
    if cfg.gate_branch is not None:
        gsig = _sigmoid(
            gate_ref[blk * GATE_ROWS:(blk + 1) * GATE_ROWS, :].astype(F32))
    outs = []
    for hh in range(SLOTS):
        denom = jnp.sum(sums[hh], axis=0, keepdims=True)
        if cfg.sink:
            denom = denom + jnp.exp2(sinks[hh] - ms[hh])
        o_h = accs[hh] / jnp.maximum(denom, TINY)
        if cfg.gate_branch is not None:
            row = cfg.gate_branch * SLOTS + hh
            o_h = o_h * gsig[row:row + 1, :]
        outs.append(o_h)
    o_ref[0, :, blk * HB_W:(blk + 1) * HB_W] = (
        jnp.concatenate(outs, axis=0).T.astype(o_ref.dtype))


def _band_softmax(cfg, qt, q_stack, kb_ref, vtb_ref, bias_ref, sinks):
    band = cfg.band_rows
    back = band - TQ
    start = pl.multiple_of(jnp.maximum(qt * TQ - back, 0), LANES)
    s_all = jnp.dot(kb_ref[0, pl.ds(start, band), :], q_stack, preferred_element_type=F32)
    bias_rows = pl.ds(pl.multiple_of(back - (qt * TQ - start), LANES), band)
    tile0 = _div_pow2(start, LANES)
    ms, sums, accs = [], [], []
    for hh in range(SLOTS):
        head = pl.ds(hh * HEAD_DIM, HEAD_DIM)
        sc = s_all[:, hh * TQ:(hh + 1) * TQ] + bias_ref[hh, bias_rows, :]
        m = jnp.max(_fold_rows(sc, jnp.max), axis=0, keepdims=True)
        if cfg.sink:
            m = jnp.maximum(m, sinks[hh])
        p = jnp.exp2(sc - m)
        vt = jnp.concatenate([vtb_ref[tile0 + c, head, :] for c in range(band // LANES)],
                             axis=1)
        ms.append(m)
        sums.append(_fold_rows(p, jnp.sum))
        accs.append(jnp.dot(vt, p.astype(BF16), preferred_element_type=F32))
    return ms, sums, accs


def _tiled_softmax(cfg, qt, qms, q_stack, kb_ref, vtb_ref, bias_ref, sinks,
                   sc_ref, m_ref, l_ref, acc_ref):
    m_ref[...] = jnp.full(m_ref.shape, NEG, F32)
    l_ref[...] = jnp.zeros(l_ref.shape, F32)
    acc_ref[...] = jnp.zeros(acc_ref.shape, F32)

    def pass1(kt, width):
        rows = pl.ds(pl.multiple_of(kt * TK, TK), width * TK)
        if cfg.selecting:
            s = [jnp.dot(kb_ref[hh, rows, :], qms[hh], preferred_element_type=F32)
                 for hh in range(SLOTS)]
        else:
            s_all = jnp.dot(kb_ref[0, rows, :], q_stack, preferred_element_type=F32)
            s = [s_all[:, hh * TQ:(hh + 1) * TQ] for hh in range(SLOTS)]
        for hh in range(SLOTS):
            m_new = m_ref[hh]
            for w in range(width):
                tile = s[hh][w * TK:(w + 1) * TK] + bias_ref[hh, qt - kt - w]
                sc_ref[hh, kt + w] = tile
                m_new = jnp.maximum(m_new, _fold_rows(tile, jnp.max))
            m_ref[hh] = m_new

    def pass2(kt, width, ms):
        for hh in range(SLOTS):
            head = pl.ds(hh * HEAD_DIM, HEAD_DIM)
            l_new = l_ref[hh]
            o_new = acc_ref[head, :]
            for w in range(width):
                p = jnp.exp2(sc_ref[hh, kt + w] - ms[hh])
                l_new = l_new + _fold_rows(p, jnp.sum)
                o_new = o_new + jnp.dot(vtb_ref[kt + w, head, :], p.astype(BF16),
                                        preferred_element_type=F32)
            l_ref[hh] = l_new
            acc_ref[head, :] = o_new

    def over_tiles(body):
        n = qt + 1
        start = 0
        for width in (1, 2):
            has = jnp.bitwise_and(n, width)

            @pl.when(has != 0)
            def _part(start=start, width=width):
                body(start, width)

            start = start + has

        def group(i, carry):
            body(start + 4 * i, 4)
            return carry

        lax.fori_loop(0, jnp.right_shift(n, 2), group, 0)

    over_tiles(pass1)
    ms = []
    for hh in range(SLOTS):
        m = jnp.max(m_ref[hh], axis=0, keepdims=True)
        ms.append(jnp.maximum(m, sinks[hh]) if cfg.sink else m)
    over_tiles(functools.partial(pass2, ms=ms))
    return (ms, [l_ref[hh] for hh in range(SLOTS)],
            [acc_ref[hh * HEAD_DIM:(hh + 1) * HEAD_DIM, :] for hh in range(SLOTS)])


def _attention(cfg, rows, cols, bias, *, k_blk, v_rblk, sel=None, gate_rblk0=None, sinks=None):
    bsz, seq, _ = rows.shape
    n_delta = bias.shape[2]
    n_q = seq // TQ
    nb = cfg.blocks_per_step
    n_kv = nb if cfg.kv_per_block else 1
    kw = n_kv * cfg.kv_width
    bias_tail = (0,) * (bias.ndim - 1)
    in_specs = [
        pl.BlockSpec((nb * HB_W, TQ), lambda hb, b, qt: (hb, b * n_q + qt)),
        pl.BlockSpec((1, seq, kw), lambda hb, b, qt: (b, 0, k_blk(hb))),
        pl.BlockSpec((kw, seq), lambda hb, b, qt: (v_rblk(hb), b)),
        pl.BlockSpec((nb,) + bias.shape[1:], lambda hb, b, qt: (hb,) + bias_tail,
                     pipeline_mode=pl.Buffered(1)),
    ]
    args = [cols, rows, cols, bias]
    if cfg.sel_from_input:
        in_specs.append(pl.BlockSpec((1, LANES, TQ), lambda hb, b, qt: (b, 0, qt)))
        args.append(sel)
    if cfg.gate_branch is not None:
        assert gate_rblk0 % nb == 0
        in_specs.append(pl.BlockSpec(
            (nb * GATE_ROWS, TQ), lambda hb, b, qt: (gate_rblk0 // nb + hb, b * n_q + qt)))
        args.append(cols)
    if cfg.sink:
        in_specs.append(pl.BlockSpec((nb, 1, LANES), lambda hb, b, qt: (hb, 0, 0)))
        args.append(sinks)
    vt_tile = LANES if cfg.band_rows is not None else TK
    scratch = [
        pltpu.VMEM((n_kv, SLOTS if cfg.selecting else 1, seq, HB_W), BF16),
        pltpu.VMEM((n_kv, seq // vt_tile, HB_W, vt_tile), BF16),
    ]
    if cfg.band_rows is None:
        scratch += [
            pltpu.VMEM((SLOTS, n_delta, TK, TQ), F32),
            pltpu.VMEM((SLOTS, SUBLANES, TQ), F32),
            pltpu.VMEM((SLOTS, SUBLANES, TQ), F32),
            pltpu.VMEM((HB_W, TQ), F32),
        ]
    if cfg.moba:
        assert SLOTS * (seq // cfg.sel_blk) <= LANES
        scratch.append(
            pltpu.VMEM((n_kv, SLOTS * (seq // cfg.sel_blk), HB_W), F32))
    return pl.pallas_call(
        functools.partial(_attn_kernel, cfg=cfg),
        grid=(N_HB // nb, bsz, n_q),
        in_specs=in_specs,
        out_specs=pl.BlockSpec((1, TQ, nb * HB_W), lambda hb, b, qt: (b, qt, hb)),
        out_shape=jax.ShapeDtypeStruct((bsz, seq, D_MODEL), BF16),
        scratch_shapes=scratch,
        compiler_params=_params(3),
        name="attention",
    )(*args)


def _gelu_tanh(x):
    cdf = 0.5 * (1.0 + jnp.tanh(math.sqrt(2.0 / math.pi) * (x + 0.044715 * (x * x * x))))
    return x * cdf


def _compress_kernel(kc_lo_ref, kc_hi_ref, vc_lo_ref, vc_hi_ref, pos_ref, w1_ref, w2k_ref,
                     w2vt_ref, kcb_ref, vcbt_ref):
    half = CMP_LEN // 2
    n_chunk = kc_lo_ref.shape[1] // CMP_STRIDE
    flat_lane = lax.broadcasted_iota(jnp.int32, (1, half * HB_W), 1)
    group_of_lane = _div_pow2(jnp.bitwise_and(flat_lane, HB_W - 1), HEAD_DIM)
    for kind, (lo_ref, hi_ref) in enumerate(((kc_lo_ref, kc_hi_ref), (vc_lo_ref, vc_hi_ref))):
        chunks = [jnp.concatenate(
            [lo_ref[0, pl.ds(l, n_chunk, stride=CMP_STRIDE), :],
             hi_ref[0, pl.ds(l, n_chunk, stride=CMP_STRIDE), :]], axis=1) for l in range(half)]
        parts = []
        for h in range(2):
            pieces = [chunks[l] + pos_ref[kind, pl.ds(h * half + l, 1), :] for l in range(half)]
            flat = jnp.concatenate(pieces, axis=1)
            stacked = jnp.concatenate(
                [jnp.where(group_of_lane == g, flat, 0.0)
                 for g in range(B_KV_GROUPS)], axis=0).astype(BF16)
            parts.append(jnp.dot(stacked, w1_ref[kind, h], preferred_element_type=F32))
        rows = B_KV_GROUPS * n_chunk
        hidden = parts[0] + pltpu.roll(parts[1], rows - 1, axis=0)
        act = _gelu_tanh(hidden).astype(BF16)
        out = None
        for g in range(B_KV_GROUPS):
            act_g = act[g * n_chunk:(g + 1) * n_chunk]
            if kind == 0:
                o = jnp.dot(act_g, w2k_ref[g], preferred_element_type=F32)
            else:
                o = lax.dot_general(w2vt_ref[g], act_g, _NT, preferred_element_type=F32)
            out = o if out is None else out + o
        if kind == 0:
            kcb_ref[0] = out
        else:
            vcbt_ref[0] = out


def _compress(rows, kc_blk, vc_blk, pos, w1, w2k, w2vt):
    bsz, seq, _ = rows.shape
    n_chunk = seq // CMP_STRIDE
    return pl.pallas_call(
        _compress_kernel,
        grid=(bsz,),
        in_specs=[
            pl.BlockSpec((1, seq, LANES), lambda b: (b, 0, 2 * kc_blk)),
            pl.BlockSpec((1, seq, LANES), lambda b: (b, 0, 2 * kc_blk + 1)),
            pl.BlockSpec((1, seq, LANES), lambda b: (b, 0, 2 * vc_blk)),
            pl.BlockSpec((1, seq, LANES), lambda b: (b, 0, 2 * vc_blk + 1)),
            _resident(pos.shape),
            _resident(w1.shape),
            _resident(w2k.shape),
            _resident(w2vt.shape),
        ],
        out_specs=[pl.BlockSpec((1, n_chunk, HB_W), lambda b: (b, 0, 0)),
                   pl.BlockSpec((1, HB_W, n_chunk), lambda b: (b, 0, 0))],
        out_shape=[jax.ShapeDtypeStruct((bsz, n_chunk, HB_W), F32),
                   jax.ShapeDtypeStruct((bsz, HB_W, n_chunk), F32)],
        compiler_params=_params(1),
        name="nsa_compress",
    )(rows, rows, rows, rows, pos, w1, w2k, w2vt)


def _cmp_kernel(qt_ref, kcb_ref, vcbt_ref, gate_ref, ovt_ref, o_ref, sel_ref):
    qt = pl.program_id(1)
    n_cmp_pad = kcb_ref.shape[1]
    n_slc = LANES // B_KV_GROUPS
    chan = lax.broadcasted_iota(jnp.int32, (HB_W, 1), 0)
    slot_of_chan = _div_pow2(chan, HEAD_DIM)
    t = qt * TQ + lax.broadcasted_iota(jnp.int32, (1, TQ), 1)
    cmp_end = (lax.broadcasted_iota(jnp.int32, (n_cmp_pad, 1), 0) * CMP_STRIDE + CMP_LEN - 1)
    ok = cmp_end <= jnp.concatenate([t] * N_HB, axis=1)
    kc = kcb_ref[0].astype(BF16)
    vct = vcbt_ref[0].astype(BF16)
    gsig = _sigmoid(gate_ref[...].astype(F32))
    outs = [[None] * B_KV_GROUPS for _ in range(N_HB)]
    imp = None
    for g in range(B_KV_GROUPS):
        own = jnp.concatenate(
            [qt_ref[r * HB_W + g * HEAD_DIM:r * HB_W + (g + 1) * HEAD_DIM, :].astype(F32)
             for r in range(N_HB)], axis=1) * SCALE
        qm = jnp.concatenate(
            [jnp.zeros((g * HEAD_DIM, N_HB * TQ), F32), own,
             jnp.zeros(((B_KV_GROUPS - 1 - g) * HEAD_DIM, N_HB * TQ), F32)][
                 (1 if g == 0 else 0):(2 if g == B_KV_GROUPS - 1 else 3)], axis=0)
        s = jnp.dot(kc, qm.astype(BF16), preferred_element_type=F32)
        s = jnp.where(ok, s, NEG)
        m = jnp.max(s, axis=0, keepdims=True)
        e = jnp.where(ok, jnp.exp(s - m), 0.0)
        p = e / jnp.maximum(jnp.sum(e, axis=0, keepdims=True), TINY)
        o = jnp.dot(vct[g * HEAD_DIM:(g + 1) * HEAD_DIM], p.astype(BF16),
                    preferred_element_type=F32)
        p_group = None
        for r in range(N_HB):
            p_r = p[:, r * TQ:(r + 1) * TQ]
            p_group = p_r if p_group is None else p_group + p_r
            row = r * GATE_ROWS + g
            outs[r][g] = o[:, r * TQ:(r + 1) * TQ] * gsig[row:row + 1, :]
        part = jnp.dot(ovt_ref[g], p_group, precision=lax.Precision.HIGHEST,
                       preferred_element_type=F32)
        imp = part if imp is None else imp + part
    for r in range(N_HB):
        o_ref[0, :, r * HB_W:(r + 1) * HB_W] = (
            jnp.concatenate(outs[r], axis=0).T.astype(o_ref.dtype))
    imp = imp.reshape(B_KV_GROUPS, n_slc, TQ)
    j = lax.broadcasted_iota(jnp.int32, (1, n_slc, 1), 1)
    cur = _div_pow2(t, SLC_BLK).reshape(1, 1, TQ)
    val = jnp.where(j == cur - 1, -NEG, imp)
    val = jnp.where(j == cur, -NEG, val)
    val = jnp.where(j == 0, -NEG, val)
    val = jnp.where(j > cur, NEG, val)
    chosen = jnp.where(_rank_rows(val) < SLC_TOPK, 1.0, 0.0)
    sel_ref[0] = chosen.reshape(LANES, TQ)


def _cmp_attention(cols, kcb, vcbt, gate_rblk, ovt, bsz, seq):
    n_cmp_pad = kcb.shape[1]
    n_q = seq // TQ
    return pl.pallas_call(
        _cmp_kernel,
        grid=(bsz, n_q),
        in_specs=[
            pl.BlockSpec((D_MODEL, TQ), lambda b, qt: (0, b * n_q + qt)),
            pl.BlockSpec((1, n_cmp_pad, HB_W), lambda b, qt: (b, 0, 0)),
            pl.BlockSpec((1, HB_W, n_cmp_pad), lambda b, qt: (b, 0, 0)),
            pl.BlockSpec((N_HB * GATE_ROWS, TQ), lambda b, qt: (gate_rblk, b * n_q + qt)),
            _resident(ovt.shape),
        ],
        out_specs=[
            pl.BlockSpec((1, TQ, D_MODEL), lambda b, qt: (b, qt, 0)),
            pl.BlockSpec((1, LANES, TQ), lambda b, qt: (b, 0, qt)),
        ],
        out_shape=[
            jax.ShapeDtypeStruct((bsz, seq, D_MODEL), BF16),
            jax.ShapeDtypeStruct((bsz, LANES, seq), F32),
        ],
        compiler_params=_params(2),
        name="nsa_cmp_select",
    )(cols, kcb, vcbt, cols, ovt)


def _mixer_a(project, bsz, seq, w_in, w_out, sinks, bias_nat, rel_bias):
    qw, kvw = N_HEADS * HEAD_DIM, A_KV_HEADS * HEAD_DIM
    w_q = _interleave_heads(w_in[:, :qw].reshape(D_MODEL, N_HEADS, HEAD_DIM), A_KV_HEADS, 1)
    w_cols = jnp.concatenate([w_q.reshape(D_MODEL, qw), w_in[:, qw + kvw:]], axis=1)
    rows, cols = project(w_in[:, qw:qw + kvw], w_cols)
    rows = rows.reshape(bsz, seq, kvw)
    band_rows = TQ + -(-(A_WINDOW - 1) // LANES) * LANES
    assert band_rows <= seq
    bias = _band_bias(rel_bias, A_KV_HEADS, band_rows, A_WINDOW)
    sink_rows = _interleave_heads(sinks.astype(F32), A_KV_HEADS, 0).reshape(N_HB, 1, SLOTS)
    sink_rows = jnp.pad(sink_rows, ((0, 0), (0, 0), (0, LANES - SLOTS)))
    cfg = _AttnCfg(kv_width=kvw, band_rows=band_rows, sink=True, blocks_per_step=N_HB)
    o = _attention(cfg, rows, cols, bias, k_blk=lambda hb: 0, v_rblk=lambda hb: qw // kvw,
                   sinks=sink_rows)
    w_o = _interleave_heads(w_out.reshape(N_HEADS, HEAD_DIM, D_MODEL), A_KV_HEADS, 0)
    return [o.reshape(bsz * seq, -1)], w_o.reshape(qw, D_MODEL)


def _nsa_overlap_t(seq):
    n_chunk = seq // CMP_STRIDE
    n_cmp = (seq - CMP_LEN) // CMP_STRIDE + 1
    n_slc = seq // SLC_BLK
    c_start = np.arange(n_cmp)[:, None] * CMP_STRIDE
    s_start = np.arange(n_slc)[None, :] * SLC_BLK
    overlap = ((c_start < s_start + SLC_BLK) & (c_start + CMP_LEN > s_start)).astype(np.float32)
    ovt = np.zeros((B_KV_GROUPS, LANES, n_chunk), np.float32)
    for g in range(B_KV_GROUPS):
        ovt[g, g * n_slc:(g + 1) * n_slc, :n_cmp] = overlap.T
    return jnp.asarray(ovt)


def _mixer_b(project, bsz, seq, w_in, w_out, cmp_pos, cmp_w1, cmp_w2, bias_nat, rel_bias):
    assert seq // SLC_BLK * B_KV_GROUPS == LANES and seq // CMP_STRIDE == LANES
    grp, dh = B_KV_GROUPS, HEAD_DIM
    qw, kvw = N_HEADS * dh, B_KV_GROUPS * dh
    w_q = _interleave_heads(w_in[:, :qw].reshape(D_MODEL, N_HEADS, dh), grp, 1)
    kv = [w_in[:, qw + i * kvw:qw + (i + 1) * kvw] for i in range(6)]
    w_g = w_in[:, qw + 6 * kvw:].reshape(D_MODEL, grp, N_HB, 3)
    w_g = w_g.transpose(0, 2, 3, 1).reshape(D_MODEL, N_HB, 3 * grp)
    w_g = jnp.pad(w_g, ((0, 0), (0, 0), (0, GATE_ROWS - 3 * grp)))
    w_rows = jnp.concatenate([kv[0], kv[1], kv[2], kv[4]], axis=1)
    w_cols = jnp.concatenate([w_q.reshape(D_MODEL, qw), kv[3], kv[5],
                              w_g.reshape(D_MODEL, N_HB * GATE_ROWS)], axis=1)
    rows, cols = project(w_rows, w_cols)
    rows = rows.reshape(bsz, seq, 4 * kvw)
    gate_row0 = qw + 2 * kvw

    half = CMP_LEN // 2
    pos = jnp.tile(cmp_pos.astype(F32), (1, 1, grp))
    w1 = cmp_w1.reshape(2, 2, half, 1, dh, CMP_HIDDEN)
    w1 = jnp.broadcast_to(w1, (2, 2, half, grp, dh, CMP_HIDDEN))
    w1 = w1.reshape(2, 2, half * HB_W, CMP_HIDDEN).astype(BF16)
    w2 = jnp.zeros((2, grp, CMP_HIDDEN, grp, dh), F32)
    for g in range(grp):
        w2 = w2.at[:, g, :, g, :].set(cmp_w2)
    w2 = w2.reshape(2, grp, CMP_HIDDEN, HB_W).astype(BF16)
    kcb, vcbt = _compress(rows, 0, 1, pos, w1, w2[0], jnp.swapaxes(w2[1], 1, 2))

    o_cmp, sel = _cmp_attention(cols, kcb, vcbt, gate_row0 // (N_HB * GATE_ROWS),
                                _nsa_overlap_t(seq), bsz, seq)

    gate_rblk0 = gate_row0 // GATE_ROWS
    bias = _block_bias(bias_nat, grp, seq // TK)
    cfg_slc = _AttnCfg(kv_width=kvw, sel_blk=SLC_BLK, sel_from_input=True, gate_branch=1,
                       blocks_per_step=N_HB)
    o_slc = _attention(cfg_slc, rows, cols, bias, k_blk=lambda hb: 2,
                       v_rblk=lambda hb: qw // kvw, sel=sel, gate_rblk0=gate_rblk0)
    band_rows = TQ + -(-(B_WINDOW - 1) // LANES) * LANES
    assert band_rows <= seq
    cfg_win = _AttnCfg(kv_width=kvw, band_rows=band_rows, gate_branch=2, blocks_per_step=N_HB)
    bias_win = _band_bias(rel_bias, grp, band_rows, B_WINDOW)
    o_win = _attention(cfg_win, rows, cols, bias_win, k_blk=lambda hb: 3,
                       v_rblk=lambda hb: qw // kvw + 1, gate_rblk0=gate_rblk0)
    w_o = _interleave_heads(w_out.reshape(N_HEADS, dh, D_MODEL), grp, 0).reshape(qw, D_MODEL)
    n = bsz * seq
    return [o_cmp.reshape(n, -1), o_slc.reshape(n, -1), o_win.reshape(n, -1)], w_o


def _mixer_c(project, bsz, seq, w_in, w_out, bias_nat, rel_bias):
    assert seq % MOBA_BLK == 0 and MOBA_BLK == TK
    qw = N_HEADS * HEAD_DIM
    w_cols = jnp.concatenate([w_in[:, :qw], w_in[:, 2 * qw:]], axis=1)
    rows, cols = project(w_in[:, qw:2 * qw], w_cols)
    rows = rows.reshape(bsz, seq, qw)
    bias = _block_bias(bias_nat, 1, seq // TK)
    nb = 2
    cfg = _AttnCfg(kv_width=HB_W, sel_blk=MOBA_BLK, blocks_per_step=nb, kv_per_block=True)
    o = _attention(cfg, rows, cols, bias, k_blk=lambda hb: hb,
                   v_rblk=lambda hb: N_HB // nb + hb)
    return [o.reshape(bsz * seq, -1)], w_out


def _layer(xf, bsz, seq, layer, g, ffn_w_in, ffn_w_out, mixer, mixer_params, bias_nat,
           rel_bias):
    xf = _ffn(xf, g[0], ffn_w_in, ffn_w_out, (layer, 0), g[1])
    project = functools.partial(_proj, xf, g[2])
    o_list, w_mix = mixer(project, bsz, seq, *mixer_params, bias_nat, rel_bias)
    return _ffn(xf, g[4], ffn_w_in, ffn_w_out, (layer, 1), g[5], mix=(o_list, w_mix, g[3]))


def kernel(x, rel_bias, norm_g, ffn_w_in, ffn_w_out, a_w_in, a_w_out, a_sinks, b_w_in, b_w_out,
           b_cmp_pos, b_cmp_w1, b_cmp_w2, c_w_in, c_w_out):
    bsz, seq, d = x.shape
    assert d == D_MODEL and seq % TQ == 0 and (bsz * seq) % max(TM, TM_PROJ) == 0
    depth = norm_g.shape[0]
    bias_nat = _bias_tiles(rel_bias, seq // TK)
    ffn_w_in, ffn_w_out = ffn_w_in.astype(BF16), ffn_w_out.astype(BF16)
    xf = x.reshape(bsz * seq, d)
    for i in range(depth):
        kind, j = i % N_MIXERS, i // N_MIXERS
        if kind == 0:
            mixer, params = _mixer_a, (a_w_in[j], a_w_out[j], a_sinks[j])
        elif kind == 1:
            mixer, params = _mixer_b, (b_w_in[j], b_w_out[j], b_cmp_pos[j], b_cmp_w1[j],
                                       b_cmp_w2[j])
        else:
            mixer, params = _mixer_c, (c_w_in[j], c_w_out[j])
        xf = _layer(xf, bsz, seq, i, norm_g[i], ffn_w_in, ffn_w_out, mixer, params,
                    bias_nat, rel_bias)
    return xf.reshape(bsz, seq, d)
```

```python
import dataclasses
import functools
import math

import numpy as np
import jax
import jax.numpy as jnp
from jax import lax
from jax.experimental import pallas as pl
from jax.experimental.pallas import tpu as pltpu

F32 = jnp.float32
BF16 = jnp.bfloat16

D_MODEL = 1024
HEAD_DIM = 64
N_HEADS = D_MODEL // HEAD_DIM
D_FF = 2816
NORM_EPS = 1e-6
N_MIXERS = 3
REL_BUCKETS = 32
REL_MAX_DIST = 1024
A_KV_HEADS = 2
A_WINDOW = 128
B_KV_GROUPS = 4
CMP_LEN = 32
CMP_STRIDE = 16
CMP_HIDDEN = 256
SLC_BLK = 64
SLC_TOPK = 16
B_WINDOW = 512
MOBA_BLK = 256
MOBA_TOPK = 3
NEG = -1e30
TINY = 1e-30

LANES = 128
SUBLANES = 8
MXU_DIM = 256
VMEM_LIMIT_BYTES = 56 * 1024 * 1024

SLOTS = 4
HB_W = SLOTS * HEAD_DIM
assert HB_W == MXU_DIM
N_HB = N_HEADS // SLOTS
TQ = 256
TK = 256
TM = 512
TM_PROJ = 1024
FF_CHUNKS = ((0, 1536), (1536, D_FF))
GATE_ROWS = 16
SEL_BIG = float(2 ** 30)
SCALE = HEAD_DIM ** -0.5
LOG2E = math.log2(math.e)

_NT = (((1,), (1,)), ((), ()))


def _params(n_axes):
    return pltpu.CompilerParams(
        dimension_semantics=("arbitrary",) * n_axes,
        vmem_limit_bytes=VMEM_LIMIT_BYTES)


def _rms(x, g):
    return x * lax.rsqrt(jnp.mean(x * x, axis=-1, keepdims=True) + NORM_EPS) * g


def _sigmoid(x):
    return 1.0 / (1.0 + jnp.exp(-x))


def _div_pow2(x, d):
    assert d & (d - 1) == 0
    return jnp.right_shift(x, d.bit_length() - 1)


def _resident(shape):
    zeros = (0,) * len(shape)
    return pl.BlockSpec(shape, lambda *_: zeros, pipeline_mode=pl.Buffered(1))


def _ffn_kernel(*refs, n_mix):
    mix_refs, refs = refs[:n_mix], refs[n_mix:]
    if n_mix:
        wmix_ref, gmix_ref = refs[:2]
        refs = refs[2:]
    x_ref, gpre_ref, win_ref, wout_ref, gpost_ref, o_ref = refs
    x = x_ref[...]
    if n_mix:
        o = mix_refs[0][...]
        for r in mix_refs[1:]:
            o = o.astype(F32) + r[...].astype(F32)
        mixed = jnp.dot(o.astype(BF16), wmix_ref[...], preferred_element_type=F32)
        x = x + _rms(mixed, gmix_ref[...])
    hn = _rms(x, gpre_ref[...]).astype(BF16)
    y = None
    for lo, hi in FF_CHUNKS:
        gate = jnp.dot(hn, win_ref[:, lo:hi], preferred_element_type=F32)
        up = jnp.dot(hn, win_ref[:, D_FF + lo:D_FF + hi], preferred_element_type=F32)
        act = (gate * _sigmoid(gate) * up).astype(BF16)
        part = jnp.dot(act, wout_ref[lo:hi, :], preferred_element_type=F32)
        y = part if y is None else y + part
    o_ref[...] = x + 0.5 * _rms(y, gpost_ref[...])


def _ffn(xf, g_pre, w_in_all, w_out_all, which, g_post, mix=None):
    n, d = xf.shape
    tile = pl.BlockSpec((TM, d), lambda i: (i, 0))
    mix_specs, mix_args = [], []
    if mix is not None:
        o_list, w_mix, g_mix = mix
        mix_specs = [tile] * len(o_list) + [_resident(w_mix.shape), _resident((1, d))]
        mix_args = list(o_list) + [w_mix.astype(BF16), g_mix.reshape(1, d)]

    def picked(w_all):
        return pl.BlockSpec((None, None) + w_all.shape[2:], lambda i: which + (0, 0),
                            pipeline_mode=pl.Buffered(1))

    return pl.pallas_call(
        functools.partial(_ffn_kernel, n_mix=len(mix_args) - 2 if mix_args else 0),
        grid=(n // TM,),
        in_specs=mix_specs + [
            tile,
            _resident((1, d)),
            picked(w_in_all),
            picked(w_out_all),
            _resident((1, d)),
        ],
        out_specs=tile,
        out_shape=jax.ShapeDtypeStruct((n, d), F32),
        compiler_params=_params(1),
        name="ffn",
    )(*mix_args, xf, g_pre.reshape(1, d), w_in_all, w_out_all, g_post.reshape(1, d))


def _proj_kernel(x_ref, g_ref, wr_ref, wct_ref, rows_ref, cols_ref):
    hn = _rms(x_ref[...], g_ref[...]).astype(BF16)
    rows_ref[...] = jnp.dot(hn, wr_ref[...], preferred_element_type=F32)
    cols_ref[...] = lax.dot_general(wct_ref[...], hn, _NT,
                                    preferred_element_type=F32).astype(cols_ref.dtype)


def _proj(xf, g, w_rows, w_cols):
    n, d = xf.shape
    cr, cc = w_rows.shape[1], w_cols.shape[1]
    return pl.pallas_call(
        _proj_kernel,
        grid=(n // TM_PROJ,),
        in_specs=[
            pl.BlockSpec((TM_PROJ, d), lambda i: (i, 0)),
            _resident((1, d)),
            _resident((d, cr)),
            _resident((cc, d)),
        ],
        out_specs=[pl.BlockSpec((TM_PROJ, cr), lambda i: (i, 0)),
                   pl.BlockSpec((cc, TM_PROJ), lambda i: (0, i))],
        out_shape=[jax.ShapeDtypeStruct((n, cr), F32), jax.ShapeDtypeStruct((cc, n), BF16)],
        compiler_params=_params(1),
        name="proj",
    )(xf, g.reshape(1, d), w_rows.astype(BF16), w_cols.T.astype(BF16))


def _rel_bucket(dist):
    dist = jnp.maximum(dist, 0)
    exact = REL_BUCKETS // 2
    d = jnp.maximum(dist, 1).astype(jnp.float32)
    log_b = exact + (jnp.log(d / exact) / math.log(REL_MAX_DIST / exact)
                     * (REL_BUCKETS - exact)).astype(jnp.int32)
    return jnp.where(dist < exact, dist, jnp.minimum(log_b, REL_BUCKETS - 1))


def _tile_distance(n_delta, xp):
    delta = xp.arange(n_delta)[:, None, None]
    return delta * TK + xp.arange(TQ)[None, None, :] - xp.arange(TK)[None, :, None]


def _bias_of_distance(rel_bias, dist):
    bucket = _rel_bucket(dist).reshape(1, -1)
    onehot = (jnp.arange(REL_BUCKETS)[:, None] == bucket).astype(F32)
    tiles = jnp.dot(rel_bias.astype(F32).T, onehot, precision=lax.Precision.HIGHEST)
    return tiles.reshape((N_HEADS,) + dist.shape)


def _bias_tiles(rel_bias, n_delta):
    return _bias_of_distance(rel_bias, _tile_distance(n_delta, jnp))


def _band_distance(band_rows, xp):
    back = band_rows - TQ
    return back + xp.arange(TQ)[None, :] - xp.arange(band_rows + back)[:, None]


def _band_bias(rel_bias, n_groups, band_rows, window):
    bias = _bias_of_distance(rel_bias, _band_distance(band_rows, jnp))
    bias = _interleave_heads(bias, n_groups, 0)
    dist = _band_distance(band_rows, np)
    bias = jnp.where((dist >= 0) & (dist < window), bias * LOG2E, NEG)
    return bias.reshape((N_HB, SLOTS) + bias.shape[1:])


def _interleave_heads(w, n_groups, axis):
    r = N_HEADS // n_groups
    shape = w.shape
    w = w.reshape(shape[:axis] + (n_groups, r) + shape[axis + 1:])
    w = jnp.swapaxes(w, axis, axis + 1)
    return w.reshape(shape)


def _block_bias(bias_nat, n_groups, n_delta):
    bias = bias_nat[:, :n_delta]
    if n_groups > 1:
        bias = _interleave_heads(bias, n_groups, 0)
    bias = jnp.where(_tile_distance(n_delta, np) >= 0, bias * LOG2E, NEG)
    return bias.reshape((N_HB, SLOTS) + bias.shape[1:])


@dataclasses.dataclass(frozen=True)
class _AttnCfg:
    kv_width: int
    band_rows: int | None = None
    blocks_per_step: int = 1
    kv_per_block: bool = False
    sel_blk: int | None = None
    sel_from_input: bool = False
    gate_branch: int | None = None
    sink: bool = False

    @property
    def selecting(self):
        return self.sel_blk is not None

    @property
    def moba(self):
        return self.selecting and not self.sel_from_input


def _rank_rows(val):
    width = val.shape[1]
    row = lax.broadcasted_iota(jnp.int32, (1, width, 1), 1)
    rank = jnp.zeros(val.shape, F32)
    for m in range(width):
        vm = val[:, m:m + 1, :]
        rank = rank + jnp.where(row > m, jnp.where(vm >= val, 1.0, 0.0),
                                jnp.where(vm > val, 1.0, 0.0))
    return rank


def _u_chan0(hh, n_blk):
    return hh * n_blk + (LANES if hh < SLOTS // 2 else 0)


def _fold_rows(x, op):
    return op(x.reshape(x.shape[0] // SUBLANES, SUBLANES, x.shape[1]), axis=0)


def _attn_kernel(*refs, cfg: _AttnCfg):
    it = iter(refs)
    qt_ref, k_ref, vt_ref, bias_ref = next(it), next(it), next(it), next(it)
    sel_ref = next(it) if cfg.sel_from_input else None
    gate_ref = next(it) if cfg.gate_branch is not None else None
    sink_ref = next(it) if cfg.sink else None
    o_ref = next(it)
    kb_ref, vtb_ref = next(it), next(it)
    scratch_refs = ((next(it), next(it), next(it), next(it))
                    if cfg.band_rows is None else ())
    kmx_ref = next(it) if cfg.moba else None
    vt_tile = LANES if cfg.band_rows is not None else TK

    qt = pl.program_id(2)
    seq = k_ref.shape[1]
    n_tiles = seq // TK
    lane = lax.broadcasted_iota(jnp.int32, (1, HB_W), 1)
    slot_of_lane = _div_pow2(lane, HEAD_DIM)
    n_sel_blk = seq // cfg.sel_blk if cfg.selecting else 0

    n_kv = kb_ref.shape[0]
    kw = k_ref.shape[2] // n_kv

    @pl.when(qt == 0)
    def _prepare():
        if cfg.moba:
            kmx_ref[...] = jnp.zeros(kmx_ref.shape, F32)
        for kv, c in [(kv, c) for kv in range(n_kv) for c in range(n_tiles)]:
            rows = pl.ds(c * TK, TK)
            k = k_ref[0, rows, kv * kw:(kv + 1) * kw]
            vt = vt_ref[kv * kw:(kv + 1) * kw, c * TK:(c + 1) * TK]
            if kw == LANES:
                k = jnp.concatenate([k, k], axis=1)
                vt = jnp.concatenate([vt, vt], axis=0)
            for i in range(TK // vt_tile):
                vtb_ref[kv, c * (TK // vt_tile) + i] = (
                    vt[:, i * vt_tile:(i + 1) * vt_tile].astype(BF16))
            if cfg.selecting:
                if cfg.sel_blk % TK == 0:
                    key_blk = (c * TK) // cfg.sel_blk
                else:
                    key_blk = _div_pow2(
                        c * TK + lax.broadcasted_iota(jnp.int32, (TK, 1), 0), cfg.sel_blk)
                for hh in range(SLOTS):
                    onehot = (lane - _u_chan0(hh, n_sel_blk)) == key_blk
                    kb_ref[kv, hh, rows, :] = jnp.where(
                        slot_of_lane == hh, k, jnp.where(onehot, 1.0, 0.0)).astype(BF16)
            else:
                kb_ref[kv, 0, rows, :] = k.astype(BF16)
            if cfg.moba:
                k_mean = jnp.mean(k, axis=0, keepdims=True)
                for hh in range(SLOTS):
                    kmx_ref[kv, pl.ds(hh * n_sel_blk + c, 1), :] = jnp.where(
                        slot_of_lane == hh, k_mean, 0.0)

    for blk in range(qt_ref.shape[0] // HB_W):
        kv = blk if n_kv > 1 else 0
        refs_of_block = (qt_ref, bias_ref, sel_ref, gate_ref, sink_ref, o_ref, kb_ref.at[kv],
                         vtb_ref.at[kv], scratch_refs,
                         kmx_ref.at[kv] if cfg.moba else None)
        _attend_block(cfg, blk, qt, n_sel_blk, refs_of_block)


def _attend_block(cfg, blk, qt, n_sel_blk, refs):
    (qt_ref, bias_ref, sel_ref, gate_ref, sink_ref, o_ref, kb_ref, vtb_ref, scratch_refs,
     kmx_ref) = refs
    bias_blk = bias_ref.at[blk]
    q_t = qt_ref[blk * HB_W:(blk + 1) * HB_W, :].astype(F32)

    sel_rows = None
    if cfg.sel_from_input:
        sel_rows = sel_ref[0]
    elif cfg.moba:
        gate = jnp.dot(kmx_ref[...].astype(BF16), q_t.astype(BF16),
                       preferred_element_type=F32)
        gate = gate.reshape(SLOTS, n_sel_blk, TQ)
        key_blk = lax.broadcasted_iota(jnp.int32, (1, n_sel_blk, 1), 1)
        val = jnp.where(key_blk < qt, gate, NEG)
        rank = _rank_rows(val)
        chosen = jnp.where(key_blk < qt, jnp.where(rank < MOBA_TOPK, 1.0, 0.0), 0.0)
        chosen = jnp.where(key_blk == qt, 1.0, chosen)
        sel_rows = chosen.reshape(SLOTS * n_sel_blk, TQ)

    q_scaled = q_t * (SCALE * LOG2E)
    qms = []
    for hh in range(SLOTS):
        pieces = {hh * HEAD_DIM: q_scaled[hh * HEAD_DIM:(hh + 1) * HEAD_DIM]}
        if cfg.selecting:
            pieces[_u_chan0(hh, n_sel_blk)] = (
                sel_rows[hh * n_sel_blk:(hh + 1) * n_sel_blk] - 1.0) * SEL_BIG
        segs, row = [], 0
        for start in sorted(pieces):
            if start > row:
                segs.append(jnp.zeros((start - row, TQ), F32))
            segs.append(pieces[start])
            row = start + pieces[start].shape[0]
        if row < HB_W:
            segs.append(jnp.zeros((HB_W - row, TQ), F32))
        qms.append(jnp.concatenate(segs, axis=0).astype(BF16))
    q_stack = None if cfg.selecting else jnp.concatenate(qms, axis=1)

    sinks = ([sink_ref[blk, :, hh:hh + 1] * LOG2E for hh in range(SLOTS)]
             if cfg.sink else None)
    if cfg.band_rows is not None:
        ms, sums, accs = _band_softmax(cfg, qt, q_stack, kb_ref, vtb_ref, bias_blk, sinks)
    else:
        ms, sums, accs = _tiled_softmax(cfg, qt, qms, q_stack, kb_ref, vtb_ref, bias_blk,
                                        sinks, *scratch_refs)

    gsig = None
    if cfg.gate_branch is not None:
        gsig = _sigmoid(
            gate_ref[blk * GATE_ROWS:(blk + 1) * GATE_ROWS, :].astype(F32))
    outs = []
    for hh in range(SLOTS):
        denom = jnp.sum(sums[hh], axis=0, keepdims=True)
        if cfg.sink:
            denom = denom + jnp.exp2(sinks[hh] - ms[hh])
        o_h = accs[hh] / jnp.maximum(denom, TINY)
        if cfg.gate_branch is not None:
            row = cfg.gate_branch * SLOTS + hh
            o_h = o_h * gsig[row:row + 1, :]
        outs.append(o_h)
    o_ref[0, :, blk * HB_W:(blk + 1) * HB_W] = (
        jnp.concatenate(outs, axis=0).T.astype(o_ref.dtype))


def _band_softmax(cfg, qt, q_stack, kb_ref, vtb_ref, bias_ref, sinks):
    band = cfg.band_rows
    back = band - TQ
    start = pl.multiple_of(jnp.maximum(qt * TQ - back, 0), LANES)
    s_all = jnp.dot(kb_ref[0, pl.ds(start, band), :], q_stack, preferred_element_type=F32)
    bias_rows = pl.ds(pl.multiple_of(back - (qt * TQ - start), LANES), band)
    tile0 = _div_pow2(start, LANES)
    ms, sums, accs = [], [], []
    for hh in range(SLOTS):
        head = pl.ds(hh * HEAD_DIM, HEAD_DIM)
        sc = s_all[:, hh * TQ:(hh + 1) * TQ] + bias_ref[hh, bias_rows, :]
        m = jnp.max(_fold_rows(sc, jnp.max), axis=0, keepdims=True)
        if cfg.sink:
            m = jnp.maximum(m, sinks[hh])
        p = jnp.exp2(sc - m)
        vt = jnp.concatenate([vtb_ref[tile0 + c, head, :] for c in range(band // LANES)],
                             axis=1)
        ms.append(m)
        sums.append(_fold_rows(p, jnp.sum))
        accs.append(jnp.dot(vt, p.astype(BF16), preferred_element_type=F32))
    return ms, sums, accs


def _tiled_softmax(cfg, qt, qms, q_stack, kb_ref, vtb_ref, bias_ref, sinks,
                   sc_ref, m_ref, l_ref, acc_ref):
    m_ref[...] = jnp.full(m_ref.shape, NEG, F32)
    l_ref[...] = jnp.zeros(l_ref.shape, F32)
    acc_ref[...] = jnp.zeros(acc_ref.shape, F32)

    def pass1(kt, width):
        rows = pl.ds(pl.multiple_of(kt * TK, TK), width * TK)
        if cfg.selecting:
            s = [jnp.dot(kb_ref[hh, rows, :], qms[hh], preferred_element_type=F32)
                 for hh in range(SLOTS)]
        else:
            s_all = jnp.dot(kb_ref[0, rows, :], q_stack, preferred_element_type=F32)
            s = [s_all[:, hh * TQ:(hh + 1) * TQ] for hh in range(SLOTS)]
        for hh in range(SLOTS):
            m_new = m_ref[hh]
            for w in range(width):
                tile = s[hh][w * TK:(w + 1) * TK] + bias_ref[hh, qt - kt - w]
                sc_ref[hh, kt + w] = tile
                m_new = jnp.maximum(m_new, _fold_rows(tile, jnp.max))
            m_ref[hh] = m_new

    def pass2(kt, width, ms):
        for hh in range(SLOTS):
            head = pl.ds(hh * HEAD_DIM, HEAD_DIM)
            l_new = l_ref[hh]
            o_new = acc_ref[head, :]
            for w in range(width):
                p = jnp.exp2(sc_ref[hh, kt + w] - ms[hh])
                l_new = l_new + _fold_rows(p, jnp.sum)
                o_new = o_new + jnp.dot(vtb_ref[kt + w, head, :], p.astype(BF16),
                                        preferred_element_type=F32)
            l_ref[hh] = l_new
            acc_ref[head, :] = o_new

    def over_tiles(body):
        n = qt + 1
        start = 0
        for width in (1, 2):
            has = jnp.bitwise_and(n, width)

            @pl.when(has != 0)
            def _part(start=start, width=width):
                body(start, width)

            start = start + has

        def group(i, carry):
            body(start + 4 * i, 4)
            return carry

        lax.fori_loop(0, jnp.right_shift(n, 2), group, 0)

    over_tiles(pass1)
    ms = []
    for hh in range(SLOTS):
        m = jnp.max(m_ref[hh], axis=0, keepdims=True)
        ms.append(jnp.maximum(m, sinks[hh]) if cfg.sink else m)
    over_tiles(functools.partial(pass2, ms=ms))
    return (ms, [l_ref[hh] for hh in range(SLOTS)],
            [acc_ref[hh * HEAD_DIM:(hh + 1) * HEAD_DIM, :] for hh in range(SLOTS)])


def _attention(cfg, rows, cols, bias, *, k_blk, v_rblk, sel=None, gate_rblk0=None, sinks=None):
    bsz, seq, _ = rows.shape
    n_delta = bias.shape[2]
    n_q = seq // TQ
    nb = cfg.blocks_per_step
    n_kv = nb if cfg.kv_per_block else 1
    kw = n_kv * cfg.kv_width
    bias_tail = (0,) * (bias.ndim - 1)
    in_specs = [
        pl.BlockSpec((nb * HB_W, TQ), lambda hb, b, qt: (hb, b * n_q + qt)),
        pl.BlockSpec((1, seq, kw), lambda hb, b, qt: (b, 0, k_blk(hb))),
        pl.BlockSpec((kw, seq), lambda hb, b, qt: (v_rblk(hb), b)),
        pl.BlockSpec((nb,) + bias.shape[1:], lambda hb, b, qt: (hb,) + bias_tail,
                     pipeline_mode=pl.Buffered(1)),
    ]
    args = [cols, rows, cols, bias]
    if cfg.sel_from_input:
        in_specs.append(pl.BlockSpec((1, LANES, TQ), lambda hb, b, qt: (b, 0, qt)))
        args.append(sel)
    if cfg.gate_branch is not None:
        assert gate_rblk0 % nb == 0
        in_specs.append(pl.BlockSpec(
            (nb * GATE_ROWS, TQ), lambda hb, b, qt: (gate_rblk0 // nb + hb, b * n_q + qt)))
        args.append(cols)
    if cfg.sink:
        in_specs.append(pl.BlockSpec((nb, 1, LANES), lambda hb, b, qt: (hb, 0, 0)))
        args.append(sinks)
    vt_tile = LANES if cfg.band_rows is not None else TK
    scratch = [
        pltpu.VMEM((n_kv, SLOTS if cfg.selecting else 1, seq, HB_W), BF16),
        pltpu.VMEM((n_kv, seq // vt_tile, HB_W, vt_tile), BF16),
    ]
    if cfg.band_rows is None:
        scratch += [
            pltpu.VMEM((SLOTS, n_delta, TK, TQ), F32),
            pltpu.VMEM((SLOTS, SUBLANES, TQ), F32),
            pltpu.VMEM((SLOTS, SUBLANES, TQ), F32),
            pltpu.VMEM((HB_W, TQ), F32),
        ]
    if cfg.moba:
        assert SLOTS * (seq // cfg.sel_blk) <= LANES
        scratch.append(
            pltpu.VMEM((n_kv, SLOTS * (seq // cfg.sel_blk), HB_W), F32))
    return pl.pallas_call(
        functools.partial(_attn_kernel, cfg=cfg),
        grid=(N_HB // nb, bsz, n_q),
        in_specs=in_specs,
        out_specs=pl.BlockSpec((1, TQ, nb * HB_W), lambda hb, b, qt: (b, qt, hb)),
        out_shape=jax.ShapeDtypeStruct((bsz, seq, D_MODEL), BF16),
        scratch_shapes=scratch,
        compiler_params=_params(3),
        name="attention",
    )(*args)


def _gelu_tanh(x):
    cdf = 0.5 * (1.0 + jnp.tanh(math.sqrt(2.0 / math.pi) * (x + 0.044715 * (x * x * x))))
    return x * cdf


def _compress_kernel(kc_lo_ref, kc_hi_ref, vc_lo_ref, vc_hi_ref, pos_ref, w1_ref, w2k_ref,
                     w2vt_ref, kcb_ref, vcbt_ref):
    half = CMP_LEN // 2
    n_chunk = kc_lo_ref.shape[1] // CMP_STRIDE
    flat_lane = lax.broadcasted_iota(jnp.int32, (1, half * HB_W), 1)
    group_of_lane = _div_pow2(jnp.bitwise_and(flat_lane, HB_W - 1), HEAD_DIM)
    for kind, (lo_ref, hi_ref) in enumerate(((kc_lo_ref, kc_hi_ref), (vc_lo_ref, vc_hi_ref))):
        chunks = [jnp.concatenate(
            [lo_ref[0, pl.ds(l, n_chunk, stride=CMP_STRIDE), :],
             hi_ref[0, pl.ds(l, n_chunk, stride=CMP_STRIDE), :]], axis=1) for l in range(half)]
        parts = []
        for h in range(2):
            pieces = [chunks[l] + pos_ref[kind, pl.ds(h * half + l, 1), :] for l in range(half)]
            flat = jnp.concatenate(pieces, axis=1)
            stacked = jnp.concatenate(
                [jnp.where(group_of_lane == g, flat, 0.0)
                 for g in range(B_KV_GROUPS)], axis=0).astype(BF16)
            parts.append(jnp.dot(stacked, w1_ref[kind, h], preferred_element_type=F32))
        rows = B_KV_GROUPS * n_chunk
        hidden = parts[0] + pltpu.roll(parts[1], rows - 1, axis=0)
        act = _gelu_tanh(hidden).astype(BF16)
        out = None
        for g in range(B_KV_GROUPS):
            act_g = act[g * n_chunk:(g + 1) * n_chunk]
            if kind == 0:
                o = jnp.dot(act_g, w2k_ref[g], preferred_element_type=F32)
            else:
                o = lax.dot_general(w2vt_ref[g], act_g, _NT, preferred_element_type=F32)
            out = o if out is None else out + o
        if kind == 0:
            kcb_ref[0] = out
        else:
            vcbt_ref[0] = out


def _compress(rows, kc_blk, vc_blk, pos, w1, w2k, w2vt):
    bsz, seq, _ = rows.shape
    n_chunk = seq // CMP_STRIDE
    return pl.pallas_call(
        _compress_kernel,
        grid=(bsz,),
        in_specs=[
            pl.BlockSpec((1, seq, LANES), lambda b: (b, 0, 2 * kc_blk)),
            pl.BlockSpec((1, seq, LANES), lambda b: (b, 0, 2 * kc_blk + 1)),
            pl.BlockSpec((1, seq, LANES), lambda b: (b, 0, 2 * vc_blk)),
            pl.BlockSpec((1, seq, LANES), lambda b: (b, 0, 2 * vc_blk + 1)),
            _resident(pos.shape),
            _resident(w1.shape),
            _resident(w2k.shape),
            _resident(w2vt.shape),
        ],
        out_specs=[pl.BlockSpec((1, n_chunk, HB_W), lambda b: (b, 0, 0)),
                   pl.BlockSpec((1, HB_W, n_chunk), lambda b: (b, 0, 0))],
        out_shape=[jax.ShapeDtypeStruct((bsz, n_chunk, HB_W), F32),
                   jax.ShapeDtypeStruct((bsz, HB_W, n_chunk), F32)],
        compiler_params=_params(1),
        name="nsa_compress",
    )(rows, rows, rows, rows, pos, w1, w2k, w2vt)


def _cmp_kernel(qt_ref, kcb_ref, vcbt_ref, gate_ref, ovt_ref, o_ref, sel_ref):
    qt = pl.program_id(1)
    n_cmp_pad = kcb_ref.shape[1]
    n_slc = LANES // B_KV_GROUPS
    chan = lax.broadcasted_iota(jnp.int32, (HB_W, 1), 0)
    slot_of_chan = _div_pow2(chan, HEAD_DIM)
    t = qt * TQ + lax.broadcasted_iota(jnp.int32, (1, TQ), 1)
    cmp_end = (lax.broadcasted_iota(jnp.int32, (n_cmp_pad, 1), 0) * CMP_STRIDE + CMP_LEN - 1)
    ok = cmp_end <= jnp.concatenate([t] * N_HB, axis=1)
    kc = kcb_ref[0].astype(BF16)
    vct = vcbt_ref[0].astype(BF16)
    gsig = _sigmoid(gate_ref[...].astype(F32))
    outs = [[None] * B_KV_GROUPS for _ in range(N_HB)]
    imp = None
    for g in range(B_KV_GROUPS):
        own = jnp.concatenate(
            [qt_ref[r * HB_W + g * HEAD_DIM:r * HB_W + (g + 1) * HEAD_DIM, :].astype(F32)
             for r in range(N_HB)], axis=1) * SCALE
        qm = jnp.concatenate(
            [jnp.zeros((g * HEAD_DIM, N_HB * TQ), F32), own,
             jnp.zeros(((B_KV_GROUPS - 1 - g) * HEAD_DIM, N_HB * TQ), F32)][
                 (1 if g == 0 else 0):(2 if g == B_KV_GROUPS - 1 else 3)], axis=0)
        s = jnp.dot(kc, qm.astype(BF16), preferred_element_type=F32)
        s = jnp.where(ok, s, NEG)
        m = jnp.max(s, axis=0, keepdims=True)
        e = jnp.where(ok, jnp.exp(s - m), 0.0)
        p = e / jnp.maximum(jnp.sum(e, axis=0, keepdims=True), TINY)
        o = jnp.dot(vct[g * HEAD_DIM:(g + 1) * HEAD_DIM], p.astype(BF16),
                    preferred_element_type=F32)
        p_group = None
        for r in range(N_HB):
            p_r = p[:, r * TQ:(r + 1) * TQ]
            p_group = p_r if p_group is None else p_group + p_r
            row = r * GATE_ROWS + g
            outs[r][g] = o[:, r * TQ:(r + 1) * TQ] * gsig[row:row + 1, :]
        part = jnp.dot(ovt_ref[g], p_group, precision=lax.Precision.HIGHEST,
                       preferred_element_type=F32)
        imp = part if imp is None else imp + part
    for r in range(N_HB):
        o_ref[0, :, r * HB_W:(r + 1) * HB_W] = (
            jnp.concatenate(outs[r], axis=0).T.astype(o_ref.dtype))
    imp = imp.reshape(B_KV_GROUPS, n_slc, TQ)
    j = lax.broadcasted_iota(jnp.int32, (1, n_slc, 1), 1)
    cur = _div_pow2(t, SLC_BLK).reshape(1, 1, TQ)
    val = jnp.where(j == cur - 1, -NEG, imp)
    val = jnp.where(j == cur, -NEG, val)
    val = jnp.where(j == 0, -NEG, val)
    val = jnp.where(j > cur, NEG, val)
    chosen = jnp.where(_rank_rows(val) < SLC_TOPK, 1.0, 0.0)
    sel_ref[0] = chosen.reshape(LANES, TQ)


def _cmp_attention(cols, kcb, vcbt, gate_rblk, ovt, bsz, seq):
    n_cmp_pad = kcb.shape[1]
    n_q = seq // TQ
    return pl.pallas_call(
        _cmp_kernel,
        grid=(bsz, n_q),
        in_specs=[
            pl.BlockSpec((D_MODEL, TQ), lambda b, qt: (0, b * n_q + qt)),
            pl.BlockSpec((1, n_cmp_pad, HB_W), lambda b, qt: (b, 0, 0)),
            pl.BlockSpec((1, HB_W, n_cmp_pad), lambda b, qt: (b, 0, 0)),
            pl.BlockSpec((N_HB * GATE_ROWS, TQ), lambda b, qt: (gate_rblk, b * n_q + qt)),
            _resident(ovt.shape),
        ],
        out_specs=[
            pl.BlockSpec((1, TQ, D_MODEL), lambda b, qt: (b, qt, 0)),
            pl.BlockSpec((1, LANES, TQ), lambda b, qt: (b, 0, qt)),
        ],
        out_shape=[
            jax.ShapeDtypeStruct((bsz, seq, D_MODEL), BF16),
            jax.ShapeDtypeStruct((bsz, LANES, seq), F32),
        ],
        compiler_params=_params(2),
        name="nsa_cmp_select",
    )(cols, kcb, vcbt, cols, ovt)


def _mixer_a(project, bsz, seq, w_in, w_out, sinks, bias_nat, rel_bias):
    qw, kvw = N_HEADS * HEAD_DIM, A_KV_HEADS * HEAD_DIM
    w_q = _interleave_heads(w_in[:, :qw].reshape(D_MODEL, N_HEADS, HEAD_DIM), A_KV_HEADS, 1)
    w_cols = jnp.concatenate([w_q.reshape(D_MODEL, qw), w_in[:, qw + kvw:]], axis=1)
    rows, cols = project(w_in[:, qw:qw + kvw], w_cols)
    rows = rows.reshape(bsz, seq, kvw)
    band_rows = TQ + -(-(A_WINDOW - 1) // LANES) * LANES
    assert band_rows <= seq
    bias = _band_bias(rel_bias, A_KV_HEADS, band_rows, A_WINDOW)
    sink_rows = _interleave_heads(sinks.astype(F32), A_KV_HEADS, 0).reshape(N_HB, 1, SLOTS)
    sink_rows = jnp.pad(sink_rows, ((0, 0), (0, 0), (0, LANES - SLOTS)))
    cfg = _AttnCfg(kv_width=kvw, band_rows=band_rows, sink=True, blocks_per_step=N_HB)
    o = _attention(cfg, rows, cols, bias, k_blk=lambda hb: 0, v_rblk=lambda hb: qw // kvw,
                   sinks=sink_rows)
    w_o = _interleave_heads(w_out.reshape(N_HEADS, HEAD_DIM, D_MODEL), A_KV_HEADS, 0)
    return [o.reshape(bsz * seq, -1)], w_o.reshape(qw, D_MODEL)


def _nsa_overlap_t(seq):
    n_chunk = seq // CMP_STRIDE
    n_cmp = (seq - CMP_LEN) // CMP_STRIDE + 1
    n_slc = seq // SLC_BLK
    c_start = np.arange(n_cmp)[:, None] * CMP_STRIDE
    s_start = np.arange(n_slc)[None, :] * SLC_BLK
    overlap = ((c_start < s_start + SLC_BLK) & (c_start + CMP_LEN > s_start)).astype(np.float32)
    ovt = np.zeros((B_KV_GROUPS, LANES, n_chunk), np.float32)
    for g in range(B_KV_GROUPS):
        ovt[g, g * n_slc:(g + 1) * n_slc, :n_cmp] = overlap.T
    return jnp.asarray(ovt)


def _mixer_b(project, bsz, seq, w_in, w_out, cmp_pos, cmp_w1, cmp_w2, bias_nat, rel_bias):
    assert seq // SLC_BLK * B_KV_GROUPS == LANES and seq // CMP_STRIDE == LANES
    grp, dh = B_KV_GROUPS, HEAD_DIM
    qw, kvw = N_HEADS * dh, B_KV_GROUPS * dh
    w_q = _interleave_heads(w_in[:, :qw].reshape(D_MODEL, N_HEADS, dh), grp, 1)
    kv = [w_in[:, qw + i * kvw:qw + (i + 1) * kvw] for i in range(6)]
    w_g = w_in[:, qw + 6 * kvw:].reshape(D_MODEL, grp, N_HB, 3)
    w_g = w_g.transpose(0, 2, 3, 1).reshape(D_MODEL, N_HB, 3 * grp)
    w_g = jnp.pad(w_g, ((0, 0), (0, 0), (0, GATE_ROWS - 3 * grp)))
    w_rows = jnp.concatenate([kv[0], kv[1], kv[2], kv[4]], axis=1)
    w_cols = jnp.concatenate([w_q.reshape(D_MODEL, qw), kv[3], kv[5],
                              w_g.reshape(D_MODEL, N_HB * GATE_ROWS)], axis=1)
    rows, cols = project(w_rows, w_cols)
    rows = rows.reshape(bsz, seq, 4 * kvw)
    gate_row0 = qw + 2 * kvw

    half = CMP_LEN // 2
    pos = jnp.tile(cmp_pos.astype(F32), (1, 1, grp))
    w1 = cmp_w1.reshape(2, 2, half, 1, dh, CMP_HIDDEN)
    w1 = jnp.broadcast_to(w1, (2, 2, half, grp, dh, CMP_HIDDEN))
    w1 = w1.reshape(2, 2, half * HB_W, CMP_HIDDEN).astype(BF16)
    w2 = jnp.zeros((2, grp, CMP_HIDDEN, grp, dh), F32)
    for g in range(grp):
        w2 = w2.at[:, g, :, g, :].set(cmp_w2)
    w2 = w2.reshape(2, grp, CMP_HIDDEN, HB_W).astype(BF16)
    kcb, vcbt = _compress(rows, 0, 1, pos, w1, w2[0], jnp.swapaxes(w2[1], 1, 2))

    o_cmp, sel = _cmp_attention(cols, kcb, vcbt, gate_row0 // (N_HB * GATE_ROWS),
                                _nsa_overlap_t(seq), bsz, seq)

    gate_rblk0 = gate_row0 // GATE_ROWS
    bias = _block_bias(bias_nat, grp, seq // TK)
    cfg_slc = _AttnCfg(kv_width=kvw, sel_blk=SLC_BLK, sel_from_input=True, gate_branch=1,
                       blocks_per_step=N_HB)
    o_slc = _attention(cfg_slc, rows, cols, bias, k_blk=lambda hb: 2,
                       v_rblk=lambda hb: qw // kvw, sel=sel, gate_rblk0=gate_rblk0)
    band_rows = TQ + -(-(B_WINDOW - 1) // LANES) * LANES
    assert band_rows <= seq
    cfg_win = _AttnCfg(kv_width=kvw, band_rows=band_rows, gate_branch=2, blocks_per_step=N_HB)
    bias_win = _band_bias(rel_bias, grp, band_rows, B_WINDOW)
    o_win = _attention(cfg_win, rows, cols, bias_win, k_blk=lambda hb: 3,
                       v_rblk=lambda hb: qw // kvw + 1, gate_rblk0=gate_rblk0)
    w_o = _interleave_heads(w_out.reshape(N_HEADS, dh, D_MODEL), grp, 0).reshape(qw, D_MODEL)
    n = bsz * seq
    return [o_cmp.reshape(n, -1), o_slc.reshape(n, -1), o_win.reshape(n, -1)], w_o


def _mixer_c(project, bsz, seq, w_in, w_out, bias_nat, rel_bias):
    assert seq % MOBA_BLK == 0 and MOBA_BLK == TK
    qw = N_HEADS * HEAD_DIM
    w_cols = jnp.concatenate([w_in[:, :qw], w_in[:, 2 * qw:]], axis=1)
    rows, cols = project(w_in[:, qw:2 * qw], w_cols)
    rows = rows.reshape(bsz, seq, qw)
    bias = _block_bias(bias_nat, 1, seq // TK)
    nb = 2
    cfg = _AttnCfg(kv_width=HB_W, sel_blk=MOBA_BLK, blocks_per_step=nb, kv_per_block=True)
    o = _attention(cfg, rows, cols, bias, k_blk=lambda hb: hb,
                   v_rblk=lambda hb: N_HB // nb + hb)
    return [o.reshape(bsz * seq, -1)], w_out


def _layer(xf, bsz, seq, layer, g, ffn_w_in, ffn_w_out, mixer, mixer_params, bias_nat,
           rel_bias):
    xf = _ffn(xf, g[0], ffn_w_in, ffn_w_out, (layer, 0), g[1])
    project = functools.partial(_proj, xf, g[2])
    o_list, w_mix = mixer(project, bsz, seq, *mixer_params, bias_nat, rel_bias)
    return _ffn(xf, g[4], ffn_w_in, ffn_w_out, (layer, 1), g[5], mix=(o_list, w_mix, g[3]))


def kernel(x, rel_bias, norm_g, ffn_w_in, ffn_w_out, a_w_in, a_w_out, a_sinks, b_w_in, b_w_out,
           b_cmp_pos, b_cmp_w1, b_cmp_w2, c_w_in, c_w_out):
    bsz, seq, d = x.shape
    assert d == D_MODEL and seq % TQ == 0 and (bsz * seq) % max(TM, TM_PROJ) == 0
    depth = norm_g.shape[0]
    bias_nat = _bias_tiles(rel_bias, seq // TK)
    ffn_w_in, ffn_w_out = ffn_w_in.astype(BF16), ffn_w_out.astype(BF16)
    xf = x.reshape(bsz * seq, d)
    for i in range(depth):
        kind, j = i % N_MIXERS, i // N_MIXERS
        if kind == 0:
            mixer, params = _mixer_a, (a_w_in[j], a_w_out[j], a_sinks[j])
        elif kind == 1:
            mixer, params = _mixer_b, (b_w_in[j], b_w_out[j], b_cmp_pos[j], b_cmp_w1[j],
                                       b_cmp_w2[j])
        else:
            mixer, params = _mixer_c, (c_w_in[j], c_w_out[j])
        xf = _layer(xf, bsz, seq, i, norm_g[i], ffn_w_in, ffn_w_out, mixer, params,
                    bias_nat, rel_bias)
    return xf.reshape(bsz, seq, d)
```
